```python
import numpy as np
import jax
import jax.numpy as jnp
from jax import lax

D_MODEL = 1024
BATCH = 8
SEQ = 4096
DEPTH = 1

PLE_DIM = 256
N_HEADS = 8
N_KV = 2
HPG = N_HEADS // N_KV
HEAD_DIM = 64
CMP_BLOCK = 32
CMP_STRIDE = 16
CMP_HIDDEN = 128
SEL_BLOCK = 64
N_SEL = 8
WINDOW = 512
Q_BLOCK = 128
SEL_Q_BLOCK = 32
CONV_CH = 512
CONV_WIDTH = 31
N_GROUPS = 4
EXP_PER_GROUP = 8
N_EXPERTS = N_GROUPS * EXP_PER_GROUP
EXPERT_FF = 512
TOP_K = 2
MOE_BLOCK = 128
EPS = 1e-6
NEG = -1e30
BIG = 1e4
Q_COLS = N_HEADS * HEAD_DIM
KV_COLS = N_KV * HEAD_DIM
OFF_KV = Q_COLS
OFF_GATE = OFF_KV + 6 * KV_COLS
OFF_CONV = OFF_GATE + 3 * N_HEADS
OFF_MERGE = OFF_CONV + 2 * CONV_CH
N_IN = OFF_MERGE + 2 * D_MODEL

kernel_name = "hybrid_nsa_conformer_hmoe_block"


def rmsnorm(x, g):
    xf = x.astype(jnp.float32)
    y = xf * lax.rsqrt(jnp.mean(xf * xf, axis=-1, keepdims=True) + EPS)
    return (y * g.astype(jnp.float32)).astype(x.dtype)


def masked_softmax(s, mask):
    p = jax.nn.softmax(jnp.where(mask, s, NEG), axis=-1)
    return jnp.where(mask, p, 0.0)


def alibi_slopes():
    m = 2.0 ** (-8.0 * np.arange(1, N_HEADS + 1) / N_HEADS)
    return jnp.asarray(m.reshape(N_KV, HPG), jnp.float32)


def compress(kv_raw, pe, w1, w2):
    b, s, g, dh = kv_raw.shape
    nc = (s - CMP_BLOCK) // CMP_STRIDE + 1
    idx = np.arange(nc)[:, None] * CMP_STRIDE + np.arange(CMP_BLOCK)[None, :]
    blk = kv_raw[:, idx] + pe[None, None, :, None, :]
    blk = blk.transpose(0, 1, 3, 2, 4).reshape(b, nc, g, CMP_BLOCK * dh)
    return jax.nn.gelu(blk @ w1) @ w2


def nsa(q, kc, vc, ks, vs, kw, vw, gates, slopes):
    b, s = q.shape[0], q.shape[1]
    scale = HEAD_DIM ** -0.5
    nc = kc.shape[1]
    nsb = s // SEL_BLOCK
    n_sel = min(N_SEL, nsb)
    c_end = jnp.arange(nc, dtype=jnp.int32) * CMP_STRIDE + CMP_BLOCK - 1
    c_start_np = np.arange(nc) * CMP_STRIDE
    s_start_np = np.arange(nsb) * SEL_BLOCK
    overlap = jnp.asarray((c_start_np[:, None] < s_start_np[None, :] + SEL_BLOCK)
                          & (c_start_np[:, None] + CMP_BLOCK > s_start_np[None, :]), jnp.float32)
    blk_ids = jnp.arange(nsb, dtype=jnp.int32)

    def cmp_block(j):
        q0 = j * Q_BLOCK
        t = q0 + jnp.arange(Q_BLOCK, dtype=jnp.int32)
        qb = lax.dynamic_slice_in_dim(q, q0, Q_BLOCK, axis=1)
        sc = jnp.einsum('bqgrd,bngd->bgrqn', qb, kc).astype(jnp.float32) * scale
        dist = (t[:, None] - c_end[None, :]).astype(jnp.float32)
        sc = sc - slopes[None, :, :, None, None] * dist
        pc = masked_softmax(sc, (c_end[None, :] <= t[:, None]))
        oc = jnp.einsum('bgrqn,bngd->bqgrd', pc.astype(vc.dtype), vc)
        imp = jnp.einsum('bgrqn,nj->bgqj', pc, overlap)
        valid = (blk_ids[None, :] * SEL_BLOCK) <= t[:, None]
        cur = t[:, None] // SEL_BLOCK
        forced = (blk_ids[None, :] == 0) | (blk_ids[None, :] == cur) | (blk_ids[None, :] == cur - 1)
        score = jnp.where(valid, imp + jnp.where(forced, BIG, 0.0), -BIG)
        top, idx = lax.top_k(score, n_sel)
        ok = top > -0.5 * BIG
        return oc, idx.transpose(0, 2, 1, 3), ok.transpose(0, 2, 1, 3)

    o_c, sel_idx, sel_ok = lax.map(cmp_block, jnp.arange(s // Q_BLOCK, dtype=jnp.int32))
    o_c = jnp.moveaxis(o_c, 0, 1).reshape(b, s, N_KV, HPG, HEAD_DIM)
    sel_idx = jnp.moveaxis(sel_idx, 0, 1).reshape(b, s, N_KV, n_sel)
    sel_ok = jnp.moveaxis(sel_ok, 0, 1).reshape(b, s, N_KV, n_sel)

    ksb = ks.reshape(b, nsb, SEL_BLOCK, N_KV, HEAD_DIM).transpose(0, 3, 1, 2, 4)
    vsb = vs.reshape(b, nsb, SEL_BLOCK, N_KV, HEAD_DIM).transpose(0, 3, 1, 2, 4)
    b_ix = jnp.arange(b)[:, None, None, None]
    g_ix = jnp.arange(N_KV)[None, None, :, None]

    def sel_block(j):
        q0 = j * SEL_Q_BLOCK
        t = q0 + jnp.arange(SEL_Q_BLOCK, dtype=jnp.int32)
        qb = lax.dynamic_slice_in_dim(q, q0, SEL_Q_BLOCK, axis=1)
        idx = lax.dynamic_slice_in_dim(sel_idx, q0, SEL_Q_BLOCK, axis=1)
        ok = lax.dynamic_slice_in_dim(sel_ok, q0, SEL_Q_BLOCK, axis=1)
        kg = ksb[b_ix, g_ix, idx]
        vg = vsb[b_ix, g_ix, idx]
        kpos = idx[..., None] * SEL_BLOCK + jnp.arange(SEL_BLOCK, dtype=jnp.int32)
        tt = t[None, :, None, None, None]
        mask = ((kpos <= tt) & ok[..., None])[:, :, :, None]
        dist = (tt - kpos).astype(jnp.float32)[:, :, :, None]
        ss = jnp.einsum('bqgrd,bqgnkd->bqgrnk', qb, kg).astype(jnp.float32) * scale
        ss = ss - slopes[None, None, :, :, None, None] * dist
        m = n_sel * SEL_BLOCK
        ps = masked_softmax(ss.reshape(b, SEL_Q_BLOCK, N_KV, HPG, m), mask.reshape(b, SEL_Q_BLOCK, N_KV, 1, m))
        return jnp.einsum('bqgrm,bqgmd->bqgrd', ps.astype(vg.dtype), vg.reshape(b, SEL_Q_BLOCK, N_KV, m, HEAD_DIM))

    o_s = lax.map(sel_block, jnp.arange(s // SEL_Q_BLOCK, dtype=jnp.int32))
    o_s = jnp.moveaxis(o_s, 0, 1).reshape(b, s, N_KV, HPG, HEAD_DIM)

    kwp = jnp.pad(kw, ((0, 0), (WINDOW, 0), (0, 0), (0, 0)))
    vwp = jnp.pad(vw, ((0, 0), (WINDOW, 0), (0, 0), (0, 0)))

    def win_block(j):
        q0 = j * Q_BLOCK
        t = q0 + jnp.arange(Q_BLOCK, dtype=jnp.int32)
        qb = lax.dynamic_slice_in_dim(q, q0, Q_BLOCK, axis=1)
        kb = lax.dynamic_slice_in_dim(kwp, q0, WINDOW + Q_BLOCK, axis=1)
        vb = lax.dynamic_slice_in_dim(vwp, q0, WINDOW + Q_BLOCK, axis=1)
        kpos = q0 - WINDOW + jnp.arange(WINDOW + Q_BLOCK, dtype=jnp.int32)
        d = t[:, None] - kpos[None, :]
        mask = (d >= 0) & (d < WINDOW) & (kpos[None, :] >= 0)
        sw = jnp.einsum('bqgrd,bkgd->bgrqk', qb, kb).astype(jnp.float32) * scale
        sw = sw - slopes[None, :, :, None, None] * d.astype(jnp.float32)
        pw = masked_softmax(sw, mask)
        return jnp.einsum('bgrqk,bkgd->bqgrd', pw.astype(vb.dtype), vb)

    o_w = lax.map(win_block, jnp.arange(s // Q_BLOCK, dtype=jnp.int32))
    o_w = jnp.moveaxis(o_w, 0, 1).reshape(b, s, N_KV, HPG, HEAD_DIM)

    o = gates[..., 0:1] * o_c + gates[..., 1:2] * o_s + gates[..., 2:3] * o_w
    return o.reshape(b, s, N_HEADS * HEAD_DIM)


def conformer_conv(u, conv_w, conv_b, ln_g, ln_b, w_out):
    a, g = jnp.split(u, 2, axis=-1)
    z = a * jax.nn.sigmoid(g)
    zp = jnp.pad(z, ((0, 0), (CONV_WIDTH - 1, 0), (0, 0)))
    z = lax.conv_general_dilated(zp, conv_w[:, None, :].astype(zp.dtype), (1,), 'VALID',
                                 dimension_numbers=('NWC', 'WIO', 'NWC'),
                                 feature_group_count=CONV_CH) + conv_b
    zf = z.astype(jnp.float32)
    mu = jnp.mean(zf, axis=-1, keepdims=True)
    var = jnp.mean(jnp.square(zf - mu), axis=-1, keepdims=True)
    z = ((zf - mu) * lax.rsqrt(var + EPS) * ln_g.astype(jnp.float32) + ln_b.astype(jnp.float32)).astype(u.dtype)
    return jax.nn.silu(z) @ w_out


def hier_moe(h, w_rg, b_rg, w_re, b_re, w_g, w_u, w_d):
    b, s, d = h.shape
    t_n = b * s
    hf = h.reshape(t_n, d)
    lg = (hf @ w_rg).astype(jnp.float32) + b_rg.astype(jnp.float32)
    pg = jax.nn.softmax(lg, axis=-1)
    gsel = jnp.argmax(lg, axis=-1).astype(jnp.int32)
    pg_sel = jnp.take_along_axis(pg, gsel[:, None], axis=1)
    le = ((hf @ w_re).astype(jnp.float32) + b_re.astype(jnp.float32)).reshape(t_n, N_GROUPS, EXP_PER_GROUP)
    le_g = jnp.take_along_axis(le, gsel[:, None, None], axis=1)[:, 0]
    top, loc = lax.top_k(jax.nn.softmax(le_g, axis=-1), TOP_K)
    wts = pg_sel * top / jnp.sum(top, axis=-1, keepdims=True)
    eid = gsel[:, None] * EXP_PER_GROUP + loc.astype(jnp.int32)
    tk = t_n * TOP_K
    flat_e = eid.reshape(tk)
    flat_tok = jnp.repeat(jnp.arange(t_n, dtype=jnp.int32), TOP_K)
    flat_w = wts.reshape(tk)
    order = jnp.argsort(flat_e)
    se, stok, sw = flat_e[order], flat_tok[order], flat_w[order]
    counts = jnp.bincount(flat_e, length=N_EXPERTS).astype(jnp.int32)
    starts = jnp.cumsum(counts) - counts
    padded = (counts + MOE_BLOCK - 1) // MOE_BLOCK * MOE_BLOCK
    pends = jnp.cumsum(padded)
    pstarts = pends - padded
    dest = pstarts[se] + jnp.arange(tk, dtype=jnp.int32) - starts[se]
    n_blk = -(-(tk + N_EXPERTS * (MOE_BLOCK - 1)) // MOE_BLOCK)
    p_rows = n_blk * MOE_BLOCK
    buf_tok = jnp.full((p_rows,), t_n, jnp.int32).at[dest].set(stok)
    buf_w = jnp.zeros((p_rows,), jnp.float32).at[dest].set(sw)
    blk_e = jnp.minimum(jnp.searchsorted(pends, jnp.arange(n_blk, dtype=jnp.int32) * MOE_BLOCK, side='right'),
                        N_EXPERTS - 1).astype(jnp.int32)
    hp = jnp.concatenate([hf, jnp.zeros((1, d), hf.dtype)], axis=0)

    def expert_block(args):
        tok, w, e = args
        xb = hp[tok]
        y = (jax.nn.silu(xb @ w_g[e]) * (xb @ w_u[e])) @ w_d[e]
        return y * w[:, None].astype(y.dtype)

    yb = lax.map(expert_block, (buf_tok.reshape(n_blk, MOE_BLOCK), buf_w.reshape(n_blk, MOE_BLOCK), blk_e))
    y = jnp.zeros((t_n + 1, d), yb.dtype).at[buf_tok].add(yb.reshape(p_rows, d))[:t_n]
    return y.reshape(b, s, d).astype(h.dtype)


def setup_inputs(seed: int = 0) -> dict:
    key = jax.random.key(seed)
    ks = jax.random.split(key, 32)
    f32 = jnp.float32

    def nrm(k, shape, scale):
        return jax.random.normal(k, shape, f32) * scale

    def gain(k, shape):
        return 1.0 + 0.02 * jax.random.normal(k, shape, f32)

    L = DEPTH
    return {
        "x": nrm(ks[0], (BATCH, SEQ, D_MODEL), 1.0),
        "p": nrm(ks[1], (DEPTH, BATCH, SEQ, PLE_DIM), 1.0),
        "attn_norm_g": gain(ks[2], (L, D_MODEL)),
        "w_in": nrm(ks[3], (L, D_MODEL, N_IN), D_MODEL ** -0.5),
        "q_norm_g": gain(ks[4], (L, HEAD_DIM)),
        "k_norm_g": gain(ks[5], (L, 3, HEAD_DIM)),
        "cmp_pos": nrm(ks[6], (L, 2, CMP_BLOCK, HEAD_DIM), 0.1),
        "cmp_w1": nrm(ks[7], (L, 2, CMP_BLOCK * HEAD_DIM, CMP_HIDDEN), (CMP_BLOCK * HEAD_DIM) ** -0.5),
        "cmp_w2": nrm(ks[8], (L, 2, CMP_HIDDEN, HEAD_DIM), CMP_HIDDEN ** -0.5),
        "w_attn_out": nrm(ks[9], (L, N_HEADS * HEAD_DIM, D_MODEL), (N_HEADS * HEAD_DIM) ** -0.5),
        "conv_w": nrm(ks[10], (L, CONV_WIDTH, CONV_CH), CONV_WIDTH ** -0.5),
        "conv_b": nrm(ks[11], (L, CONV_CH), 0.02),
        "conv_norm_g": gain(ks[12], (L, CONV_CH)),
        "conv_norm_b": nrm(ks[13], (L, CONV_CH), 0.02),
        "w_conv_out": nrm(ks[14], (L, CONV_CH, D_MODEL), CONV_CH ** -0.5),
        "merge_b": nrm(ks[15], (L, 2 * D_MODEL), 0.02),
        "w_out": nrm(ks[16], (L, D_MODEL, D_MODEL), D_MODEL ** -0.5),
        "ffn_norm_g": gain(ks[17], (L, D_MODEL)),
        "w_router_group": nrm(ks[18], (L, D_MODEL, N_GROUPS), D_MODEL ** -0.5),
        "b_router_group": nrm(ks[19], (L, N_GROUPS), 0.01),
        "w_router_expert": nrm(ks[20], (L, D_MODEL, N_EXPERTS), D_MODEL ** -0.5),
        "b_router_expert": nrm(ks[21], (L, N_EXPERTS), 0.01),
        "w_exp_gate": nrm(ks[22], (L, N_EXPERTS, D_MODEL, EXPERT_FF), D_MODEL ** -0.5),
        "w_exp_up": nrm(ks[23], (L, N_EXPERTS, D_MODEL, EXPERT_FF), D_MODEL ** -0.5),
        "w_exp_down": nrm(ks[24], (L, N_EXPERTS, EXPERT_FF, D_MODEL), EXPERT_FF ** -0.5),
        "ple_norm_g": gain(ks[25], (L, D_MODEL)),
        "w_ple_gate": nrm(ks[26], (L, D_MODEL, D_MODEL), D_MODEL ** -0.5),
        "b_ple_gate": nrm(ks[27], (L, D_MODEL), 0.02),
        "w_ple_proj": nrm(ks[28], (L, PLE_DIM, D_MODEL), PLE_DIM ** -0.5),
    }


def reference(x, p, attn_norm_g, w_in, q_norm_g, k_norm_g, cmp_pos, cmp_w1, cmp_w2, w_attn_out,
              conv_w, conv_b, conv_norm_g, conv_norm_b, w_conv_out, merge_b, w_out,
              ffn_norm_g, w_router_group, b_router_group, w_router_expert, b_router_expert,
              w_exp_gate, w_exp_up, w_exp_down, ple_norm_g, w_ple_gate, b_ple_gate, w_ple_proj):
    b, s, _ = x.shape
    slopes = alibi_slopes()
    for i in range(DEPTH):
        h = rmsnorm(x, attn_norm_g[i])
        z = h @ w_in[i]
        q = rmsnorm(z[..., :Q_COLS].reshape(b, s, N_KV, HPG, HEAD_DIM), q_norm_g[i])
        kv = z[..., OFF_KV:OFF_GATE].reshape(b, s, 6, N_KV, HEAD_DIM)
        kc = rmsnorm(compress(kv[:, :, 0], cmp_pos[i, 0], cmp_w1[i, 0], cmp_w2[i, 0]), k_norm_g[i, 0])
        vc = compress(kv[:, :, 1], cmp_pos[i, 1], cmp_w1[i, 1], cmp_w2[i, 1])
        k_sel = rmsnorm(kv[:, :, 2], k_norm_g[i, 1])
        k_win = rmsnorm(kv[:, :, 4], k_norm_g[i, 2])
        nsa_gates = jax.nn.sigmoid(z[..., OFF_GATE:OFF_CONV].reshape(b, s, N_KV, HPG, 3))
        attn = nsa(q, kc, vc, k_sel, kv[:, :, 3], k_win, kv[:, :, 5], nsa_gates, slopes)
        y_a = attn @ w_attn_out[i]
        y_c = conformer_conv(z[..., OFF_CONV:OFF_MERGE], conv_w[i], conv_b[i], conv_norm_g[i],
                             conv_norm_b[i], w_conv_out[i])
        mg = jax.nn.sigmoid(z[..., OFF_MERGE:] + merge_b[i])
        x = x + (mg[..., :D_MODEL] * y_a + mg[..., D_MODEL:] * y_c) @ w_out[i]
        x = x + hier_moe(rmsnorm(x, ffn_norm_g[i]), w_router_group[i], b_router_group[i],
                         w_router_expert[i], b_router_expert[i], w_exp_gate[i], w_exp_up[i], w_exp_down[i])
        gate = jax.nn.sigmoid(rmsnorm(x, ple_norm_g[i]) @ w_ple_gate[i] + b_ple_gate[i])
        x = x + gate * (p[i] @ w_ple_proj[i])
    return x
```

```python
import functools

import numpy as np
import jax
import jax.numpy as jnp
from jax import lax
from jax.experimental import pallas as pl
from jax.experimental.pallas import tpu as pltpu

F32 = jnp.float32
BF16 = jnp.bfloat16
I32 = jnp.int32

D_MODEL = 1024
PLE_DIM = 256
N_HEADS = 8
N_KV = 2
HPG = N_HEADS // N_KV
HEAD_DIM = 64
CMP_BLOCK = 32
CMP_STRIDE = 16
CMP_HIDDEN = 128
SEL_BLOCK = 64
N_SEL = 8
WINDOW = 512
CONV_CH = 512
CONV_WIDTH = 31
N_GROUPS = 4
EXP_PER_GROUP = 8
N_EXPERTS = N_GROUPS * EXP_PER_GROUP
EXPERT_FF = 512
EPS = 1e-6
NEG = -1e30
BIG = 1e4
Q_COLS = N_HEADS * HEAD_DIM
KV_COLS = N_KV * HEAD_DIM
OFF_KV = Q_COLS
OFF_GATE = OFF_KV + 6 * KV_COLS
OFF_CONV = OFF_GATE + 3 * N_HEADS
OFF_MERGE = OFF_CONV + 2 * CONV_CH

LANES = 128
SUBLANES = 8
VMEM_LIMIT = 56 * 1024 * 1024

TM_PROJ = 256
QB = 128
KT = 256
TS_CONV = 512
HALO = 32
TM_MERGE = 256
TD = 512
BM = 256
TC = 512

P_Q = 0
P_KV = P_Q + Q_COLS
P_GATE = P_KV + 6 * KV_COLS
P_CONV = P_GATE + LANES
P_MERGE = P_CONV + 2 * CONV_CH
P_END = P_MERGE + 2 * D_MODEL

R_E1, R_E2, R_W1, R_W2, R_K1, R_K2 = 0, 1, 2, 3, 4, 5
GROUP_LANE0 = N_EXPERTS


def _cparams(sem, vmem=VMEM_LIMIT):
    return pltpu.CompilerParams(dimension_semantics=sem, vmem_limit_bytes=vmem)


def _dot(a, b):
    return jnp.dot(a, b, preferred_element_type=F32)


def _dot_nt(a, b):
    return lax.dot_general(a, b, (((1,), (1,)), ((), ())), preferred_element_type=F32)


def _split(x):
    hi = x.astype(BF16)
    lo = (x - hi.astype(F32)).astype(BF16)
    return hi, lo


def _seg_rmsnorm(z, bd):
    hi, lo = _split(z * z)
    ss = _dot(hi, bd) + _dot(lo, bd)
    return z * lax.rsqrt(ss * (1.0 / HEAD_DIM) + EPS)


def _dup_halves(k):
    lane = lax.broadcasted_iota(I32, k.shape, 1)
    r = pltpu.roll(k, HEAD_DIM, 1)
    return jnp.concatenate([jnp.where(lane < HEAD_DIM, k, r), jnp.where(lane >= HEAD_DIM, k, r)], axis=1)


def _in_proj_kernel(x_ref, g_ref, w_ref, qg_ref, kg_ref, bd_ref, mb_ref,
                    q_o, kc_o, vc_o, ks_o, vs_o, kw_o, vw_o, gt_o, glu_o, mg_o):
    x = x_ref[...]
    h = (x * lax.rsqrt(jnp.mean(x * x, axis=-1, keepdims=True) + EPS) * g_ref[...]).astype(BF16)
    bd = bd_ref[...]

    def proj(a, b):
        return _dot(h, w_ref[:, a:b])

    qg = qg_ref[...]
    for c in range(Q_COLS // LANES):
        zq = proj(P_Q + c * LANES, P_Q + (c + 1) * LANES)
        q_o[:, c * LANES:(c + 1) * LANES] = (_seg_rmsnorm(zq, bd) * qg * (HEAD_DIM ** -0.5)).astype(BF16)
    kc_o[...] = proj(P_KV, P_KV + LANES)
    vc_o[...] = proj(P_KV + LANES, P_KV + 2 * LANES)
    zk = proj(P_KV + 2 * LANES, P_KV + 3 * LANES)
    ks_o[...] = _dup_halves(_seg_rmsnorm(zk, bd) * kg_ref[1:2, :]).astype(BF16)
    vs_o[...] = _dup_halves(proj(P_KV + 3 * LANES, P_KV + 4 * LANES)).astype(BF16)
    zk = proj(P_KV + 4 * LANES, P_KV + 5 * LANES)
    kw_o[...] = _dup_halves(_seg_rmsnorm(zk, bd) * kg_ref[2:3, :]).astype(BF16)
    vw_o[...] = _dup_halves(proj(P_KV + 5 * LANES, P_KV + 6 * LANES)).astype(BF16)
    gt_o[...] = jax.nn.sigmoid(proj(P_GATE, P_GATE + LANES))
    za = proj(P_CONV, P_CONV + CONV_CH)
    zg = proj(P_CONV + CONV_CH, P_CONV + 2 * CONV_CH)
    glu_o[...] = za * jax.nn.sigmoid(zg)
    for c in range(2):
        zm = proj(P_MERGE + c * D_MODEL, P_MERGE + (c + 1) * D_MODEL)
        mg_o[:, c * D_MODEL:(c + 1) * D_MODEL] = jax.nn.sigmoid(
            zm + mb_ref[:, c * D_MODEL:(c + 1) * D_MODEL]).astype(BF16)


def _in_proj(x2, norm_g, w_p, qg, kg, bd, mb):
    t = x2.shape[0]
    tm = TM_PROJ
    row = lambda n: pl.BlockSpec((tm, n), lambda i: (i, 0))
    full = lambda a: pl.BlockSpec(a.shape, lambda i: (0,) * a.ndim)
    outs = [(Q_COLS, BF16), (LANES, F32), (LANES, F32), (2 * LANES, BF16), (2 * LANES, BF16),
            (2 * LANES, BF16), (2 * LANES, BF16), (LANES, F32), (CONV_CH, F32), (2 * D_MODEL, BF16)]
    return pl.pallas_call(
        _in_proj_kernel,
        grid=(t // tm,),
        in_specs=[row(D_MODEL), full(norm_g), full(w_p), full(qg), full(kg), full(bd), full(mb)],
        out_specs=[row(n) for n, _ in outs],
        out_shape=[jax.ShapeDtypeStruct((t, n), d) for n, d in outs],
        compiler_params=_cparams(("parallel",)),
        name="in_proj",
    )(x2, norm_g, w_p, qg, kg, bd, mb)


def _compress_kernel(ck_ref, cv_ref, pe_ref, wa_ref, wb_ref, w2_ref, bd_ref, kg_ref, kc_o, vc_o):
    nrow = ck_ref.shape[1]
    for idx, (c_ref, o_ref) in enumerate(((ck_ref, kc_o), (cv_ref, vc_o))):
        c = c_ref[0]
        ya = _dot((c + pe_ref[idx, 0:1, :]).astype(BF16), wa_ref[idx])
        yb = _dot((c + pe_ref[idx, 1:2, :]).astype(BF16), wb_ref[idx])
        hid = ya + pltpu.roll(yb, nrow - 1, 0)
        out = _dot(jax.nn.gelu(hid).astype(BF16), w2_ref[idx])
        if idx == 0:
            out = _seg_rmsnorm(out, bd_ref[...]) * kg_ref[0:1, :]
        o_ref[0] = _dup_halves(out).astype(BF16)


def _compress(kc_raw, vc_raw, pe_x, wa, wb, w2x, bd, kg):
    b, nrow, width = kc_raw.shape
    batch = lambda n: pl.BlockSpec((1, nrow, n), lambda i: (i, 0, 0))
    full = lambda a: pl.BlockSpec(a.shape, lambda i: (0,) * a.ndim)
    return pl.pallas_call(
        _compress_kernel,
        grid=(b,),
        in_specs=[batch(width), batch(width), full(pe_x), full(wa), full(wb), full(w2x), full(bd), full(kg)],
        out_specs=[batch(2 * LANES), batch(2 * LANES)],
        out_shape=[jax.ShapeDtypeStruct((b, nrow, 2 * LANES), BF16)] * 2,
        compiler_params=_cparams(("parallel",)),
        name="compress",
    )(kc_raw, vc_raw, pe_x, wa, wb, w2x, bd, kg)


def _softmax_tile(s, mask, slopes, dist, m, l):
    ps, alphas, ms, ls = [], [], [], []
    for r in range(HPG):
        rows = slice(r * QB, (r + 1) * QB)
        sr = jnp.where(mask, s[rows] - slopes[r] * dist, NEG)
        m_new = jnp.maximum(m[rows], jnp.max(sr, axis=-1, keepdims=True))
        e = jnp.where(mask, jnp.exp(sr - m_new), 0.0)
        alpha = jnp.exp(m[rows] - m_new)
        ps.append(e)
        alphas.append(alpha)
        ms.append(m_new)
        ls.append(alpha * l[rows] + jnp.sum(e, axis=-1, keepdims=True))
    return ps, jnp.concatenate(alphas, 0), jnp.concatenate(ms, 0), jnp.concatenate(ls, 0)


def _safe_inv(l):
    pos = l > 0.0
    return jnp.where(pos, 1.0 / jnp.where(pos, l, 1.0), 0.0)


def _attn_kernel(q_ref, gt_ref, kc_ref, vc_ref, ks_ref, vs_ref, kw_ref, vw_ref, ovt_ref, o_ref):
    i = pl.program_id(1)
    q0 = i * QB
    nc_rows = kc_ref.shape[1]
    n_blk = ovt_ref.shape[0]
    lane = lax.broadcasted_iota(I32, (QB, LANES), 1)
    lo_half = lane < HEAD_DIM
    rel = (lax.broadcasted_iota(I32, (QB, KT), 0) - lax.broadcasted_iota(I32, (QB, KT), 1)).astype(F32)
    blk_of_key = lax.broadcasted_iota(I32, (LANES, KT), 0) - lax.broadcasted_iota(I32, (LANES, KT), 1) // SEL_BLOCK
    gt = gt_ref[...]
    m0 = jnp.full((HPG * QB, 1), NEG, F32)
    l0 = jnp.zeros((HPG * QB, 1), F32)
    acc0 = jnp.zeros((HPG * QB, LANES), F32)

    for g in range(N_KV):
        gl = slice(g * LANES, (g + 1) * LANES)
        slopes = [2.0 ** -(g * HPG + r + 1) for r in range(HPG)]
        parts = []
        for pr in range(HPG // 2):
            qp = q_ref[:, (g * 2 + pr) * LANES:(g * 2 + pr + 1) * LANES]
            parts += [jnp.where(lo_half, qp, 0), jnp.where(lo_half, 0, qp)]
        qx = jnp.concatenate(parts, axis=0)

        s = _dot_nt(qx, kc_ref[0, :, gl])
        cidx = lax.broadcasted_iota(I32, (QB, nc_rows), 1)
        c_end = cidx * CMP_STRIDE + (CMP_BLOCK - 1)
        tq = q0 + lax.broadcasted_iota(I32, (QB, nc_rows), 0)
        cmask = (c_end <= tq) & (cidx < nc_rows - 1)
        cdist = (tq - c_end).astype(F32)
        es, _, _, lc = _softmax_tile(s, cmask, slopes, cdist, m0, l0)
        inv = _safe_inv(lc)
        pc = [es[r] * inv[r * QB:(r + 1) * QB] for r in range(HPG)]
        o_c = _dot(jnp.concatenate(pc, 0).astype(BF16), vc_ref[0, :, gl])
        ph, plo = _split(pc[0] + pc[1] + pc[2] + pc[3])
        imp_t = _dot_nt(ovt_ref[...], ph) + _dot_nt(ovt_ref[...], plo)

        blk = lax.broadcasted_iota(I32, (n_blk, QB), 0)
        tl = q0 + lax.broadcasted_iota(I32, (n_blk, QB), 1)
        valid = blk * SEL_BLOCK <= tl
        cur = tl // SEL_BLOCK
        forced = (blk == 0) | (blk == cur) | (blk == cur - 1)
        score = jnp.where(valid, imp_t + jnp.where(forced, BIG, 0.0), -BIG)
        sel = jnp.zeros((n_blk, QB), jnp.bool_)
        for _ in range(min(N_SEL, n_blk)):
            mx = jnp.max(score, axis=0, keepdims=True)
            first = jnp.min(jnp.where(score == mx, blk, n_blk), axis=0, keepdims=True)
            pick = blk == first
            sel = sel | pick
            score = jnp.where(pick, -jnp.inf, score)
        sel_t = jnp.where(sel & valid, 1.0, 0.0)
        if n_blk < LANES:
            sel_t = jnp.concatenate([sel_t, jnp.zeros((LANES - n_blk, QB), F32)], axis=0)
        sel_q = sel_t.T.astype(BF16)

        def sel_body(j, carry):
            m, l, acc = carry
            k0 = pl.multiple_of(j * KT, KT)
            s = _dot_nt(qx, ks_ref[0, pl.ds(k0, KT), gl])
            dist = rel + (q0 - k0).astype(F32)
            expand = (blk_of_key == j * (KT // SEL_BLOCK)).astype(BF16)
            chosen = _dot(sel_q, expand) > 0.5
            ps, alpha, m, l = _softmax_tile(s, chosen & (dist >= 0.0), slopes, dist, m, l)
            pv = _dot(jnp.concatenate(ps, 0).astype(BF16), vs_ref[0, pl.ds(k0, KT), gl])
            return m, l, alpha * acc + pv

        _, l_s, acc_s = lax.fori_loop(0, (q0 + QB - 1) // KT + 1, sel_body, (m0, l0, acc0))
        o_s = acc_s * _safe_inv(l_s)

        def win_body(j, carry):
            m, l, acc = carry
            k0 = pl.multiple_of(j * KT, KT)
            s = _dot_nt(qx, kw_ref[0, pl.ds(k0, KT), gl])
            dist = rel + (q0 - k0).astype(F32)
            ps, alpha, m, l = _softmax_tile(s, (dist >= 0.0) & (dist < float(WINDOW)), slopes, dist, m, l)
            pv = _dot(jnp.concatenate(ps, 0).astype(BF16), vw_ref[0, pl.ds(k0, KT), gl])
            return m, l, alpha * acc + pv

        j_lo = jnp.maximum(q0 - (WINDOW - 1), 0) // KT
        _, l_w, acc_w = lax.fori_loop(j_lo, (q0 + QB - 1) // KT + 1, win_body, (m0, l0, acc0))
        o_w = acc_w * _safe_inv(l_w)

        mixed = []
        for r in range(HPG):
            rows = slice(r * QB, (r + 1) * QB)
            c0 = 3 * (g * HPG + r)
            mixed.append(gt[:, c0:c0 + 1] * o_c[rows] + gt[:, c0 + 1:c0 + 2] * o_s[rows]
                         + gt[:, c0 + 2:c0 + 3] * o_w[rows])
        for pr in range(HPG // 2):
            col = (g * 2 + pr) * LANES
            o_ref[:, col:col + LANES] = jnp.where(lo_half, mixed[2 * pr], mixed[2 * pr + 1]).astype(BF16)


def _attention(q, gt, kc, vc, ks, vs, kw, vw, ovt, b, s):
    per_q = lambda n: pl.BlockSpec((QB, n), lambda bi, i: (bi * (s // QB) + i, 0))
    per_b = lambda a: pl.BlockSpec((1,) + a.shape[1:], lambda bi, i: (bi, 0, 0))
    return pl.pallas_call(
        _attn_kernel,
        grid=(b, s // QB),
        in_specs=[per_q(Q_COLS), per_q(LANES), per_b(kc), per_b(vc), per_b(ks), per_b(vs), per_b(kw), per_b(vw),
                  pl.BlockSpec(ovt.shape, lambda bi, i: (0, 0))],
        out_specs=per_q(Q_COLS),
        out_shape=jax.ShapeDtypeStruct((b * s, Q_COLS), BF16),
        compiler_params=_cparams(("parallel", "parallel")),
        name="nsa_attention",
    )(q, gt, kc, vc, ks, vs, kw, vw, ovt)


def _conv_kernel(z_ref, w_ref, b_ref, g_ref, beta_ref, o_ref, buf):
    ts = z_ref.shape[0]
    chunk = 64

    @pl.when(pl.program_id(1) == 0)
    def _():
        buf[0:HALO, :] = jnp.zeros((HALO, CONV_CH), F32)

    buf[HALO:HALO + ts, :] = z_ref[...]
    first = HALO - (CONV_WIDTH - 1)
    for c in range(ts // chunk):
        acc = jnp.zeros((chunk, CONV_CH), F32) + b_ref[...]
        for k in range(CONV_WIDTH):
            acc = acc + buf[first + k + c * chunk:first + k + (c + 1) * chunk, :] * w_ref[k:k + 1, :]
        mu = jnp.mean(acc, axis=-1, keepdims=True)
        d = acc - mu
        var = jnp.mean(d * d, axis=-1, keepdims=True)
        y = d * lax.rsqrt(var + EPS) * g_ref[...] + beta_ref[...]
        o_ref[c * chunk:(c + 1) * chunk, :] = (y * jax.nn.sigmoid(y)).astype(BF16)
    buf[0:HALO, :] = buf[ts:ts + HALO, :]


def _conformer_conv(glu, w, bias, g, beta, b, s):
    ts = min(TS_CONV, s)
    full = lambda a: pl.BlockSpec(a.shape, lambda bi, i: (0, 0))
    row = pl.BlockSpec((ts, CONV_CH), lambda bi, i: (bi * (s // ts) + i, 0))
    return pl.pallas_call(
        _conv_kernel,
        grid=(b, s // ts),
        in_specs=[row, full(w), full(bias), full(g), full(beta)],
        out_specs=row,
        out_shape=jax.ShapeDtypeStruct((b * s, CONV_CH), BF16),
        scratch_shapes=[pltpu.VMEM((HALO + ts, CONV_CH), F32)],
        compiler_params=_cparams(("arbitrary", "arbitrary")),
        name="conformer_conv",
    )(glu, w, bias, g, beta)


def _merge_router_kernel(x_ref, a_ref, c_ref, mg_ref, wa_ref, wc_ref, wo_ref, fg_ref, wr_ref, br_ref,
                         x1_o, h2_o, rt_o, cnt_o):
    tm = x_ref.shape[0]
    y_a = _dot(a_ref[...], wa_ref[...])
    y_c = _dot(c_ref[...], wc_ref[...])
    mix = mg_ref[:, 0:D_MODEL].astype(F32) * y_a + mg_ref[:, D_MODEL:2 * D_MODEL].astype(F32) * y_c
    x1 = x_ref[...] + _dot(mix.astype(BF16), wo_ref[...])
    x1_o[...] = x1
    h2 = x1 * lax.rsqrt(jnp.mean(x1 * x1, axis=-1, keepdims=True) + EPS) * fg_ref[...]
    h2_o[...] = h2

    hh, hl = _split(h2)
    logits = _dot(hh, wr_ref[0]) + _dot(hl, wr_ref[0]) + _dot(hh, wr_ref[1]) + br_ref[...]
    lane = lax.broadcasted_iota(I32, (tm, LANES), 1)
    is_g = (lane >= GROUP_LANE0) & (lane < GROUP_LANE0 + N_GROUPS)
    gmax = jnp.max(jnp.where(is_g, logits, -jnp.inf), axis=-1, keepdims=True)
    gsel = jnp.min(jnp.where(is_g & (logits == gmax), lane - GROUP_LANE0, N_GROUPS), axis=-1, keepdims=True)
    pg_sel = 1.0 / jnp.sum(jnp.where(is_g, jnp.exp(logits - gmax), 0.0), axis=-1, keepdims=True)
    in_grp = (lane < N_EXPERTS) & (lane // EXP_PER_GROUP == gsel)
    emax = jnp.max(jnp.where(in_grp, logits, -jnp.inf), axis=-1, keepdims=True)
    ee = jnp.where(in_grp, jnp.exp(logits - emax), 0.0)
    pe = jnp.where(in_grp, ee / jnp.sum(ee, axis=-1, keepdims=True), -1.0)
    p1 = jnp.max(pe, axis=-1, keepdims=True)
    i1 = jnp.min(jnp.where(pe == p1, lane, LANES), axis=-1, keepdims=True)
    pe2 = jnp.where(lane == i1, -1.0, pe)
    p2 = jnp.max(pe2, axis=-1, keepdims=True)
    i2 = jnp.min(jnp.where(pe2 == p2, lane, LANES), axis=-1, keepdims=True)
    w1 = pg_sel * p1 / (p1 + p2)
    w2 = pg_sel * p2 / (p1 + p2)

    hot1 = lane == i1
    hot2 = lane == i2
    ind = jnp.where(hot1 | hot2, 1.0, 0.0)
    tri = (lax.broadcasted_iota(I32, (tm, tm), 1) < lax.broadcasted_iota(I32, (tm, tm), 0)).astype(BF16)
    before = _dot(tri, ind.astype(BF16))
    k1 = jnp.sum(jnp.where(hot1, before, 0.0), axis=-1, keepdims=True)
    k2 = jnp.sum(jnp.where(hot2, before, 0.0), axis=-1, keepdims=True)
    rec = jnp.zeros((tm, LANES), F32)
    for slot, val in ((R_E1, i1.astype(F32)), (R_E2, i2.astype(F32)), (R_W1, w1), (R_W2, w2), (R_K1, k1), (R_K2, k2)):
        rec = jnp.where(lane == slot, val, rec)
    rt_o[...] = rec
    cnt_o[0] = jnp.broadcast_to(jnp.sum(ind, axis=0, keepdims=True), (SUBLANES, LANES))


def _merge_router(x2, attn, cact, mg, wa, wc, wo, fg, wr, br):
    t = x2.shape[0]
    tm = TM_MERGE
    row = lambda n: pl.BlockSpec((tm, n), lambda i: (i, 0))
    full = lambda a: pl.BlockSpec(a.shape, lambda i: (0,) * a.ndim)
    return pl.pallas_call(
        _merge_router_kernel,
        grid=(t // tm,),
        in_specs=[row(D_MODEL), row(Q_COLS), row(CONV_CH), row(2 * D_MODEL), full(wa), full(wc), full(wo),
                  full(fg), full(wr), full(br)],
        out_specs=[row(D_MODEL), row(D_MODEL), row(LANES), pl.BlockSpec((1, SUBLANES, LANES), lambda i: (i, 0, 0))],
        out_shape=[jax.ShapeDtypeStruct((t, D_MODEL), F32), jax.ShapeDtypeStruct((t, D_MODEL), F32),
                   jax.ShapeDtypeStruct((t, LANES), F32), jax.ShapeDtypeStruct((t // tm, SUBLANES, LANES), F32)],
        compiler_params=_cparams(("parallel",)),
        name="merge_router",
    )(x2, attn, cact, mg, wa, wc, wo, fg, wr, br)


def _dispatch_kernel(dest_ref, h_ref, zeros_ref, xg_ref, sem):
    del zeros_ref
    td = h_ref.shape[0]

    def row_copy(t, slot):
        return pltpu.make_async_copy(h_ref.at[pl.ds(t, 1), :], xg_ref.at[pl.ds(slot, 1), :], sem)

    def issue(t, c):
        row_copy(t, dest_ref[2 * t]).start()
        row_copy(t, dest_ref[2 * t + 1]).start()
        return c

    lax.fori_loop(0, td, issue, 0)

    def drain(t, c):
        row_copy(0, 0).wait()
        return c

    lax.fori_loop(0, 2 * td, drain, 0)


def _dispatch(dest_flat, h2, p_rows):
    t = h2.shape[0]
    td = min(TD, t)
    zeros = jnp.zeros((p_rows, D_MODEL), F32)
    return pl.pallas_call(
        _dispatch_kernel,
        grid=(t // td,),
        in_specs=[pl.BlockSpec((2 * td,), lambda i: (i,), memory_space=pltpu.SMEM),
                  pl.BlockSpec((td, D_MODEL), lambda i: (i, 0)),
                  pl.BlockSpec(memory_space=pl.ANY)],
        out_specs=pl.BlockSpec(memory_space=pl.ANY),
        out_shape=jax.ShapeDtypeStruct((p_rows, D_MODEL), F32),
        scratch_shapes=[pltpu.SemaphoreType.DMA(())],
        input_output_aliases={2: 0},
        compiler_params=_cparams(("arbitrary",)),
        name="moe_dispatch",
    )(dest_flat, h2, zeros)


def _expert_kernel(blk_e_ref, nused_ref, x_ref, wg_ref, wu_ref, wd_ref, y_ref):
    del blk_e_ref
    used = pl.program_id(0) < nused_ref[0]

    @pl.when(used)
    def _():
        xb = x_ref[...].astype(BF16)
        a = _dot(xb, wg_ref[0])
        u = _dot(xb, wu_ref[0])
        y_ref[...] = _dot((a * jax.nn.sigmoid(a) * u).astype(BF16), wd_ref[0])

    @pl.when(jnp.logical_not(used))
    def _():
        y_ref[...] = jnp.zeros(y_ref.shape, F32)


def _experts(blk_e, nused, xg, wg, wu, wd):
    p_rows = xg.shape[0]
    return pl.pallas_call(
        _expert_kernel,
        grid_spec=pltpu.PrefetchScalarGridSpec(
            num_scalar_prefetch=2,
            grid=(p_rows // BM,),
            in_specs=[pl.BlockSpec((BM, D_MODEL), lambda i, be, nu: (i, 0)),
                      pl.BlockSpec((1, D_MODEL, EXPERT_FF), lambda i, be, nu: (be[i], 0, 0)),
                      pl.BlockSpec((1, D_MODEL, EXPERT_FF), lambda i, be, nu: (be[i], 0, 0)),
                      pl.BlockSpec((1, EXPERT_FF, D_MODEL), lambda i, be, nu: (be[i], 0, 0))],
            out_specs=pl.BlockSpec((BM, D_MODEL), lambda i, be, nu: (i, 0)),
        ),
        out_shape=jax.ShapeDtypeStruct((p_rows, D_MODEL), F32),
        compiler_params=_cparams(("arbitrary",)),
        name="moe_experts",
    )(blk_e, nused, xg, wg, wu, wd)


def _combine_ple_kernel(dest_ref, x1_ref, rt_ref, p_ref, pg_ref, wpg_ref, bpg_ref, wpp_ref, yg_ref, o_ref,
                        y0, y1, sem):
    tc = x1_ref.shape[0]

    def row_copy(slot, dst, t):
        return pltpu.make_async_copy(yg_ref.at[pl.ds(slot, 1), :], dst.at[pl.ds(t, 1), :], sem)

    def issue(t, c):
        row_copy(dest_ref[2 * t], y0, t).start()
        row_copy(dest_ref[2 * t + 1], y1, t).start()
        return c

    lax.fori_loop(0, tc, issue, 0)

    def drain(t, c):
        row_copy(0, y0, 0).wait()
        return c

    lax.fori_loop(0, 2 * tc, drain, 0)

    rt = rt_ref[...]
    x2 = x1_ref[...] + rt[:, R_W1:R_W1 + 1] * y0[...] + rt[:, R_W2:R_W2 + 1] * y1[...]
    h3 = (x2 * lax.rsqrt(jnp.mean(x2 * x2, axis=-1, keepdims=True) + EPS) * pg_ref[...]).astype(BF16)
    gate = jax.nn.sigmoid(_dot(h3, wpg_ref[...]) + bpg_ref[...])
    o_ref[...] = x2 + gate * _dot(p_ref[...].astype(BF16), wpp_ref[...])


def _combine_ple(dest_flat, x1, route, p2, pg, wpg, bpg, wpp, yg):
    t = x1.shape[0]
    tc = TC
    row = lambda n: pl.BlockSpec((tc, n), lambda i: (i, 0))
    full = lambda a: pl.BlockSpec(a.shape, lambda i: (0,) * a.ndim)
    return pl.pallas_call(
        _combine_ple_kernel,
        grid=(t // tc,),
        in_specs=[pl.BlockSpec((2 * tc,), lambda i: (i,), memory_space=pltpu.SMEM),
                  row(D_MODEL), row(LANES), row(PLE_DIM), full(pg), full(wpg), full(bpg), full(wpp),
                  pl.BlockSpec(memory_space=pl.ANY)],
        out_specs=row(D_MODEL),
        out_shape=jax.ShapeDtypeStruct((t, D_MODEL), F32),
        scratch_shapes=[pltpu.VMEM((tc, D_MODEL), F32), pltpu.VMEM((tc, D_MODEL), F32),
                        pltpu.SemaphoreType.DMA(())],
        compiler_params=_cparams(("arbitrary",)),
        name="moe_combine_ple",
    )(dest_flat, x1, route, p2, pg, wpg, bpg, wpp, yg)


def _seg_sum_matrix():
    seg = np.arange(LANES) // HEAD_DIM
    return jnp.asarray(seg[:, None] == seg[None, :], BF16)


def _overlap_t(nc_rows, n_blk):
    nc = np.arange(nc_rows)[None, :] * CMP_STRIDE
    sb = np.arange(n_blk)[:, None] * SEL_BLOCK
    ov = (nc < sb + SEL_BLOCK) & (nc + CMP_BLOCK > sb) & (np.arange(nc_rows)[None, :] < nc_rows - 1)
    return jnp.asarray(ov, BF16)


def _compress_weights(pos, w1, w2):
    half = CMP_BLOCK // 2
    pe = jnp.broadcast_to(pos[:, None, :], (CMP_BLOCK, N_KV, HEAD_DIM)).reshape(2, half * N_KV * HEAD_DIM)
    w1r = w1.reshape(2, half, HEAD_DIM, CMP_HIDDEN)
    eye = jnp.eye(N_KV, dtype=w1.dtype)
    wx = jnp.einsum('hldj,gk->hlgdkj', w1r, eye).reshape(2, half * N_KV * HEAD_DIM, N_KV * CMP_HIDDEN)
    w2x = jnp.einsum('jd,gk->gjkd', w2, eye).reshape(N_KV * CMP_HIDDEN, N_KV * HEAD_DIM)
    return pe, wx[0].astype(BF16), wx[1].astype(BF16), w2x.astype(BF16)


def _layer(x, p_i, prm):
    b, s, _ = x.shape
    t = b * s
    x2 = x.reshape(t, D_MODEL)
    bd = _seg_sum_matrix()

    w = prm["w_in"]
    w_p = jnp.concatenate([w[:, :OFF_GATE], jnp.pad(w[:, OFF_GATE:OFF_CONV], ((0, 0), (0, LANES - 3 * N_HEADS))),
                           w[:, OFF_CONV:]], axis=1).astype(BF16)
    qg = jnp.tile(prm["q_norm_g"], LANES // HEAD_DIM)[None, :]
    kg = jnp.tile(prm["k_norm_g"], (1, LANES // HEAD_DIM))
    q, kc_raw, vc_raw, ks, vs, kw, vw, gt, glu, mg = _in_proj(
        x2, prm["attn_norm_g"][None, :], w_p, qg, kg, bd, prm["merge_b"][None, :])

    nrow = s // CMP_STRIDE
    cw = [_compress_weights(prm["cmp_pos"][j], prm["cmp_w1"][j], prm["cmp_w2"][j]) for j in range(2)]
    pe_x, wa, wb, w2x = (jnp.stack([cw[0][j], cw[1][j]]) for j in range(4))
    kc, vc = _compress(kc_raw.reshape(b, nrow, CMP_STRIDE * LANES), vc_raw.reshape(b, nrow, CMP_STRIDE * LANES),
                       pe_x, wa, wb, w2x, bd, kg)

    r3 = lambda a: a.reshape(b, s, 2 * LANES)
    attn = _attention(q, gt, kc, vc, r3(ks), r3(vs), r3(kw), r3(vw), _overlap_t(nrow, s // SEL_BLOCK), b, s)

    cact = _conformer_conv(glu, prm["conv_w"], prm["conv_b"][None, :], prm["conv_norm_g"][None, :],
                           prm["conv_norm_b"][None, :], b, s)

    wr_full = jnp.zeros((D_MODEL, LANES), F32)
    wr_full = wr_full.at[:, :N_EXPERTS].set(prm["w_router_expert"])
    wr_full = wr_full.at[:, GROUP_LANE0:GROUP_LANE0 + N_GROUPS].set(prm["w_router_group"])
    wr_hi = wr_full.astype(BF16)
    wr = jnp.stack([wr_hi, (wr_full - wr_hi.astype(F32)).astype(BF16)])
    br = jnp.zeros((1, LANES), F32)
    br = br.at[0, :N_EXPERTS].set(prm["b_router_expert"])
    br = br.at[0, GROUP_LANE0:GROUP_LANE0 + N_GROUPS].set(prm["b_router_group"])
    x1, h2, route, cnt = _merge_router(
        x2, attn, cact, mg, prm["w_attn_out"].astype(BF16), prm["w_conv_out"].astype(BF16),
        prm["w_out"].astype(BF16), prm["ffn_norm_g"][None, :], wr, br)

    n_tiles = t // TM_MERGE
    counts = cnt[:, 0, :N_EXPERTS].astype(I32)
    total = jnp.sum(counts, axis=0)
    padded = (total + BM - 1) // BM * BM
    pends = jnp.cumsum(padded)
    tile_base = (pends - padded)[None, :] + jnp.cumsum(counts, axis=0) - counts
    eid = route[:, R_E1:R_E2 + 1].astype(I32).reshape(n_tiles, TM_MERGE * 2)
    rank = route[:, R_K1:R_K2 + 1].astype(I32).reshape(n_tiles, TM_MERGE * 2)
    dest = (jnp.take_along_axis(tile_base, eid, axis=1) + rank).reshape(2 * t)
    n_blk = -(-(2 * t + N_EXPERTS * (BM - 1)) // BM)
    blk_e = jnp.minimum(jnp.searchsorted(pends, jnp.arange(n_blk, dtype=I32) * BM, side="right"),
                        N_EXPERTS - 1).astype(I32)
    nused = (pends[-1:] // BM).astype(I32)

    xg = _dispatch(dest, h2, n_blk * BM)
    yg = _experts(blk_e, nused, xg, prm["w_exp_gate"].astype(BF16), prm["w_exp_up"].astype(BF16),
                  prm["w_exp_down"].astype(BF16))
    out = _combine_ple(dest, x1, route, p_i.reshape(t, PLE_DIM), prm["ple_norm_g"][None, :],
                       prm["w_ple_gate"].astype(BF16), prm["b_ple_gate"][None, :], prm["w_ple_proj"].astype(BF16), yg)
    return out.reshape(b, s, D_MODEL)


def kernel(x, p, attn_norm_g, w_in, q_norm_g, k_norm_g, cmp_pos, cmp_w1, cmp_w2, w_attn_out, conv_w, conv_b, conv_norm_g, conv_norm_b, w_conv_out, merge_b, w_out, ffn_norm_g, w_router_group, b_router_group, w_router_expert, b_router_expert, w_exp_gate, w_exp_up, w_exp_down, ple_norm_g, w_ple_gate, b_ple_gate, w_ple_proj):
    stacked = dict(attn_norm_g=attn_norm_g, w_in=w_in, q_norm_g=q_norm_g, k_norm_g=k_norm_g, cmp_pos=cmp_pos,
                   cmp_w1=cmp_w1, cmp_w2=cmp_w2, w_attn_out=w_attn_out, conv_w=conv_w, conv_b=conv_b,
                   conv_norm_g=conv_norm_g, conv_norm_b=conv_norm_b, w_conv_out=w_conv_out, merge_b=merge_b,
                   w_out=w_out, ffn_norm_g=ffn_norm_g, w_router_group=w_router_group,
                   b_router_group=b_router_group, w_router_expert=w_router_expert,
                   b_router_expert=b_router_expert, w_exp_gate=w_exp_gate, w_exp_up=w_exp_up,
                   w_exp_down=w_exp_down, ple_norm_g=ple_norm_g, w_ple_gate=w_ple_gate, b_ple_gate=b_ple_gate,
                   w_ple_proj=w_ple_proj)
    for i in range(w_in.shape[0]):
        x = _layer(x, p[i], {k: v[i] for k, v in stacked.items()})
    return x
```

```python
import functools

import numpy as np
import jax
import jax.numpy as jnp
from jax import lax
from jax.experimental import pallas as pl
from jax.experimental.pallas import tpu as pltpu

F32 = jnp.float32
BF16 = jnp.bfloat16
I32 = jnp.int32

D_MODEL = 1024
PLE_DIM = 256
N_HEADS = 8
N_KV = 2
HPG = N_HEADS // N_KV
HEAD_DIM = 64
CMP_BLOCK = 32
CMP_STRIDE = 16
CMP_HIDDEN = 128
SEL_BLOCK = 64
N_SEL = 8
WINDOW = 512
CONV_CH = 512
CONV_WIDTH = 31
N_GROUPS = 4
EXP_PER_GROUP = 8
N_EXPERTS = N_GROUPS * EXP_PER_GROUP
EXPERT_FF = 512
EPS = 1e-6
NEG = -1e30
BIG = 1e4
Q_COLS = N_HEADS * HEAD_DIM
KV_COLS = N_KV * HEAD_DIM
OFF_KV = Q_COLS
OFF_GATE = OFF_KV + 6 * KV_COLS
OFF_CONV = OFF_GATE + 3 * N_HEADS
OFF_MERGE = OFF_CONV + 2 * CONV_CH

LANES = 128
SUBLANES = 8
VMEM_LIMIT = 56 * 1024 * 1024

TM_PROJ = 256
QB = 128
KT = 256
TS_CONV = 512
HALO = 32
TM_MERGE = 256
TD = 512
BM = 256
TC = 512

P_Q = 0
P_KV = P_Q + Q_COLS
P_GATE = P_KV + 6 * KV_COLS
P_CONV = P_GATE + LANES
P_MERGE = P_CONV + 2 * CONV_CH
P_END = P_MERGE + 2 * D_MODEL

POS_HI_LANE = HEAD_DIM
POS_LO_LANE = HEAD_DIM + 1
UNSELECTED = -(2.0 ** 40)

R_E1, R_E2, R_W1, R_W2, R_K1, R_K2 = 0, 1, 2, 3, 4, 5
GROUP_LANE0 = N_EXPERTS


def _cparams(sem, vmem=VMEM_LIMIT):
    return pltpu.CompilerParams(dimension_semantics=sem, vmem_limit_bytes=vmem)


def _dot(a, b):
    return jnp.dot(a, b, preferred_element_type=F32)


def _dot_nt(a, b):
    return lax.dot_general(a, b, (((1,), (1,)), ((), ())), preferred_element_type=F32)


def _split(x):
    hi = x.astype(BF16)
    lo = (x - hi.astype(F32)).astype(BF16)
    return hi, lo


def _seg_rmsnorm(z, bd):
    hi, lo = _split(z * z)
    ss = _dot(hi, bd) + _dot(lo, bd)
    return z * lax.rsqrt(ss * (1.0 / HEAD_DIM) + EPS)


def _dup_halves(k):
    lane = lax.broadcasted_iota(I32, k.shape, 1)
    r = pltpu.roll(k, HEAD_DIM, 1)
    return jnp.concatenate([jnp.where(lane < HEAD_DIM, k, r), jnp.where(lane >= HEAD_DIM, k, r)], axis=1)


def _in_proj_kernel(x_ref, g_ref, w_ref, qg_ref, kg_ref, bd_ref, mb_ref,
                    q_o, kc_o, vc_o, ks_o, vs_o, kw_o, vw_o, gt_o, glu_o, mg_o, *, seq_len):
    tm = x_ref.shape[0]
    x = x_ref[...]
    h = (x * lax.rsqrt(jnp.mean(x * x, axis=-1, keepdims=True) + EPS) * g_ref[...]).astype(BF16)
    bd = bd_ref[...]
    lane = lax.broadcasted_iota(I32, (tm, LANES), 1)
    lo_half = lane < HEAD_DIM

    def proj(a, b):
        return _dot(h, w_ref[:, a:b])

    qg = qg_ref[...]
    for c in range(Q_COLS // LANES):
        zq = proj(P_Q + c * LANES, P_Q + (c + 1) * LANES)
        qn = _seg_rmsnorm(zq, bd) * qg * (HEAD_DIM ** -0.5)
        q_o[:, 2 * c * LANES:(2 * c + 1) * LANES] = jnp.where(lo_half, qn, 0.0).astype(BF16)
        q_o[:, (2 * c + 1) * LANES:(2 * c + 2) * LANES] = jnp.where(
            lo_half, pltpu.roll(qn, HEAD_DIM, 1), 0.0).astype(BF16)
    kc_o[...] = proj(P_KV, P_KV + LANES)
    vc_o[...] = proj(P_KV + LANES, P_KV + 2 * LANES)

    pos = (pl.program_id(0) * tm) % seq_len + lax.broadcasted_iota(I32, (tm, LANES), 0)
    blk = pos // SEL_BLOCK
    pos_lanes = jnp.where(lane == POS_HI_LANE, (blk * SEL_BLOCK).astype(F32),
                          jnp.where(lane == POS_LO_LANE, (pos - blk * SEL_BLOCK).astype(F32), 0.0))
    onehot = jnp.where(lane == blk, 1.0, 0.0).astype(BF16)

    def key_tiles(zk, gain):
        kn = _seg_rmsnorm(zk, bd) * gain
        return (jnp.where(lo_half, kn, pos_lanes).astype(BF16),
                jnp.where(lo_half, pltpu.roll(kn, HEAD_DIM, 1), pos_lanes).astype(BF16))

    k0, k1 = key_tiles(proj(P_KV + 2 * LANES, P_KV + 3 * LANES), kg_ref[1:2, :])
    ks_o[...] = jnp.concatenate([k0, onehot, k1, onehot], axis=1)
    vs_o[...] = _dup_halves(proj(P_KV + 3 * LANES, P_KV + 4 * LANES)).astype(BF16)
    k0, k1 = key_tiles(proj(P_KV + 4 * LANES, P_KV + 5 * LANES), kg_ref[2:3, :])
    kw_o[...] = jnp.concatenate([k0, k1], axis=1)
    vw_o[...] = _dup_halves(proj(P_KV + 5 * LANES, P_KV + 6 * LANES)).astype(BF16)
    gt_o[...] = jax.nn.sigmoid(proj(P_GATE, P_GATE + LANES))
    za = proj(P_CONV, P_CONV + CONV_CH)
    zg = proj(P_CONV + CONV_CH, P_CONV + 2 * CONV_CH)
    glu_o[...] = za * jax.nn.sigmoid(zg)
    for c in range(2):
        zm = proj(P_MERGE + c * D_MODEL, P_MERGE + (c + 1) * D_MODEL)
        mg_o[:, c * D_MODEL:(c + 1) * D_MODEL] = jax.nn.sigmoid(
            zm + mb_ref[:, c * D_MODEL:(c + 1) * D_MODEL]).astype(BF16)


def _in_proj(x2, norm_g, w_p, qg, kg, bd, mb, seq_len):
    t = x2.shape[0]
    tm = TM_PROJ
    assert seq_len % tm == 0 and seq_len // SEL_BLOCK <= HEAD_DIM
    row = lambda n: pl.BlockSpec((tm, n), lambda i: (i, 0))
    full = lambda a: pl.BlockSpec(a.shape, lambda i: (0,) * a.ndim)
    outs = [(N_HEADS * LANES, BF16), (LANES, F32), (LANES, F32), (4 * LANES, BF16), (2 * LANES, BF16),
            (2 * LANES, BF16), (2 * LANES, BF16), (LANES, F32), (CONV_CH, F32), (2 * D_MODEL, BF16)]
    return pl.pallas_call(
        functools.partial(_in_proj_kernel, seq_len=seq_len),
        grid=(t // tm,),
        in_specs=[row(D_MODEL), full(norm_g), full(w_p), full(qg), full(kg), full(bd), full(mb)],
        out_specs=[row(n) for n, _ in outs],
        out_shape=[jax.ShapeDtypeStruct((t, n), d) for n, d in outs],
        compiler_params=_cparams(("parallel",)),
        name="in_proj",
    )(x2, norm_g, w_p, qg, kg, bd, mb)


def _compress_kernel(ck_ref, cv_ref, pe_ref, wa_ref, wb_ref, w2_ref, bd_ref, kg_ref, kc_o, vc_o):
    nrow = ck_ref.shape[1]
    for idx, (c_ref, o_ref) in enumerate(((ck_ref, kc_o), (cv_ref, vc_o))):
        c = c_ref[0]
        ya = _dot((c + pe_ref[idx, 0:1, :]).astype(BF16), wa_ref[idx])
        yb = _dot((c + pe_ref[idx, 1:2, :]).astype(BF16), wb_ref[idx])
        hid = ya + pltpu.roll(yb, nrow - 1, 0)
        out = _dot(jax.nn.gelu(hid).astype(BF16), w2_ref[idx])
        if idx == 0:
            out = _seg_rmsnorm(out, bd_ref[...]) * kg_ref[0:1, :]
        o_ref[0] = _dup_halves(out).astype(BF16)


def _compress(kc_raw, vc_raw, pe_x, wa, wb, w2x, bd, kg):
    b, nrow, width = kc_raw.shape
    batch = lambda n: pl.BlockSpec((1, nrow, n), lambda i: (i, 0, 0))
    full = lambda a: pl.BlockSpec(a.shape, lambda i: (0,) * a.ndim)
    return pl.pallas_call(
        _compress_kernel,
        grid=(b,),
        in_specs=[batch(width), batch(width), full(pe_x), full(wa), full(wb), full(w2x), full(bd), full(kg)],
        out_specs=[batch(2 * LANES), batch(2 * LANES)],
        out_shape=[jax.ShapeDtypeStruct((b, nrow, 2 * LANES), BF16)] * 2,
        compiler_params=_cparams(("parallel",)),
        name="compress",
    )(kc_raw, vc_raw, pe_x, wa, wb, w2x, bd, kg)


def _softmax_tile(s, mask, slopes, dist, m, l):
    ps, alphas, ms, ls = [], [], [], []
    for r in range(HPG):
        rows = slice(r * QB, (r + 1) * QB)
        sr = jnp.where(mask, s[rows] - slopes[r] * dist, NEG)
        m_new = jnp.maximum(m[rows], jnp.max(sr, axis=-1, keepdims=True))
        e = jnp.where(mask, jnp.exp(sr - m_new), 0.0)
        alpha = jnp.exp(m[rows] - m_new)
        ps.append(e)
        alphas.append(alpha)
        ms.append(m_new)
        ls.append(alpha * l[rows] + jnp.sum(e, axis=-1, keepdims=True))
    return ps, jnp.concatenate(alphas, 0), jnp.concatenate(ms, 0), jnp.concatenate(ls, 0)


def _safe_inv(l):
    pos = l > 0.0
    return jnp.where(pos, 1.0 / jnp.where(pos, l, 1.0), 0.0)


def _attn_kernel(q_ref, gt_ref, kc_ref, vc_ref, ks_ref, vs_ref, kw_ref, vw_ref, ovt_ref, o_ref):
    i = pl.program_id(1)
    q0 = i * QB
    nc_rows = kc_ref.shape[1]
    n_blk = ovt_ref.shape[0]
    lane = lax.broadcasted_iota(I32, (QB, LANES), 1)
    lo_half = lane < HEAD_DIM
    rel4 = ((lax.broadcasted_iota(I32, (HPG * QB, KT), 0) & (QB - 1))
            - lax.broadcasted_iota(I32, (HPG * QB, KT), 1)).astype(F32)
    gt = gt_ref[...]
    mc0 = jnp.full((HPG * QB, 1), NEG, F32)
    lc0 = jnp.zeros((HPG * QB, 1), F32)

    pos_lane = (lane == POS_HI_LANE) | (lane == POS_LO_LANE)
    o_cmp, q_sel, q_win = [], [], []
    for g in range(N_KV):
        gl = slice(g * LANES, (g + 1) * LANES)
        slopes = [2.0 ** -(g * HPG + r + 1) for r in range(HPG)]
        heads = [q_ref[:, (g * HPG + r) * LANES:(g * HPG + r + 1) * LANES] for r in range(HPG)]
        qc = jnp.concatenate(heads, axis=0)

        s = _dot_nt(qc, kc_ref[0, :, gl])
        cidx = lax.broadcasted_iota(I32, (QB, nc_rows), 1)
        c_end = cidx * CMP_STRIDE + (CMP_BLOCK - 1)
        tq = q0 + lax.broadcasted_iota(I32, (QB, nc_rows), 0)
        cmask = (c_end <= tq) & (cidx < nc_rows - 1)
        cdist = (tq - c_end).astype(F32)
        es, _, _, lc = _softmax_tile(s, cmask, slopes, cdist, mc0, lc0)
        inv = _safe_inv(lc)
        pc = [es[r] * inv[r * QB:(r + 1) * QB] for r in range(HPG)]
        o_c = _dot(jnp.concatenate(pc, 0).astype(BF16), vc_ref[0, :, gl])
        ph, plo = _split(pc[0] + pc[1] + pc[2] + pc[3])
        imp_t = _dot_nt(ovt_ref[...], ph) + _dot_nt(ovt_ref[...], plo)

        blk = lax.broadcasted_iota(I32, (n_blk, QB), 0)
        tl = q0 + lax.broadcasted_iota(I32, (n_blk, QB), 1)
        valid = blk * SEL_BLOCK <= tl
        cur = tl // SEL_BLOCK
        forced = (blk == 0) | (blk == cur) | (blk == cur - 1)
        score = jnp.where(valid, imp_t + jnp.where(forced, BIG, 0.0), -BIG)
        sel = jnp.zeros((n_blk, QB), jnp.bool_)
        for _ in range(min(N_SEL, n_blk)):
            mx = jnp.max(score, axis=0, keepdims=True)
            first = jnp.min(jnp.where(score == mx, blk, n_blk), axis=0, keepdims=True)
            pick = blk == first
            sel = sel | pick
            score = jnp.where(pick, -jnp.inf, score)
        sel_t = jnp.where(sel & valid, 1.0, 0.0)
        if n_blk < LANES:
            sel_t = jnp.concatenate([sel_t, jnp.zeros((LANES - n_blk, QB), F32)], axis=0)
        unsel = ((sel_t.T - 1.0) * -UNSELECTED).astype(BF16)

        qa = jnp.concatenate([jnp.where(pos_lane, slopes[r], heads[r]).astype(BF16) for r in range(HPG)], axis=0)
        o_cmp.append(o_c)
        q_win.append(qa)
        q_sel.append(jnp.concatenate([qa, jnp.concatenate([unsel] * HPG, axis=0)], axis=1))

    def update(s, v, m, l, acc):
        m_new = jnp.maximum(m, jnp.max(s, axis=-1, keepdims=True))
        p = jnp.exp(s - jnp.concatenate([m_new] * (KT // LANES), axis=1))
        alpha = jnp.exp(m - m_new)
        l = alpha * l + jnp.sum(p, axis=-1, keepdims=True)
        return m_new, l, alpha * acc + _dot(p.astype(BF16), v)

    def key_tile(j, carry, q_list, k_ref, v_ref, k_width, mask_fn):
        k0 = pl.multiple_of(j * KT, KT)
        dist = rel4 + (q0 - k0).astype(F32)
        out = []
        for g in range(N_KV):
            s = _dot_nt(q_list[g], k_ref[0, pl.ds(k0, KT), g * k_width:(g + 1) * k_width])
            if mask_fn is not None:
                s = jnp.where(mask_fn(dist), s, NEG)
            out += update(s, v_ref[0, pl.ds(k0, KT), g * LANES:(g + 1) * LANES], *carry[3 * g:3 * g + 3])
        return tuple(out)

    m0 = jnp.full((HPG * QB, LANES), NEG, F32)
    z0 = jnp.zeros((HPG * QB, LANES), F32)
    init = (m0, z0, z0) * N_KV
    causal = lambda d: d >= 0.0
    in_window = lambda d: (d >= 0.0) & (d < float(WINDOW))
    j_diag = (q0 + QB - 1) // KT
    carry = lax.fori_loop(0, j_diag, lambda j, c: key_tile(j, c, q_sel, ks_ref, vs_ref, 2 * LANES, None), init)
    sel_out = key_tile(j_diag, carry, q_sel, ks_ref, vs_ref, 2 * LANES, causal)
    j_lo = jnp.maximum(q0 - (WINDOW - 1), 0) // KT
    win_out = lax.fori_loop(j_lo, j_diag + 1,
                            lambda j, c: key_tile(j, c, q_win, kw_ref, vw_ref, LANES, in_window), init)

    for g in range(N_KV):
        o_s = sel_out[3 * g + 2] / sel_out[3 * g + 1]
        o_w = win_out[3 * g + 2] / win_out[3 * g + 1]
        mixed = []
        for r in range(HPG):
            rows = slice(r * QB, (r + 1) * QB)
            c0 = 3 * (g * HPG + r)
            mixed.append(gt[:, c0:c0 + 1] * o_cmp[g][rows] + gt[:, c0 + 1:c0 + 2] * o_s[rows]
                         + gt[:, c0 + 2:c0 + 3] * o_w[rows])
        for pr in range(HPG // 2):
            col = (g * 2 + pr) * LANES
            o_ref[:, col:col + LANES] = jnp.where(lo_half, mixed[2 * pr], mixed[2 * pr + 1]).astype(BF16)


def _attention(q, gt, kc, vc, ks, vs, kw, vw, ovt, b, s):
    per_q = lambda n: pl.BlockSpec((QB, n), lambda bi, i: (bi * (s // QB) + i, 0))
    per_b = lambda a: pl.BlockSpec((1,) + a.shape[1:], lambda bi, i: (bi, 0, 0))
    return pl.pallas_call(
        _attn_kernel,
        grid=(b, s // QB),
        in_specs=[per_q(N_HEADS * LANES), per_q(LANES), per_b(kc), per_b(vc), per_b(ks), per_b(vs), per_b(kw),
                  per_b(vw),
                  pl.BlockSpec(ovt.shape, lambda bi, i: (0, 0))],
        out_specs=per_q(Q_COLS),
        out_shape=jax.ShapeDtypeStruct((b * s, Q_COLS), BF16),
        compiler_params=_cparams(("parallel", "parallel")),
        name="nsa_attention",
    )(q, gt, kc, vc, ks, vs, kw, vw, ovt)


def _conv_kernel(z_ref, w_ref, b_ref, g_ref, beta_ref, o_ref, buf):
    ts = z_ref.shape[0]
    chunk = 64

    @pl.when(pl.program_id(1) == 0)
    def _():
        buf[0:HALO, :] = jnp.zeros((HALO, CONV_CH), F32)

    buf[HALO:HALO + ts, :] = z_ref[...]
    first = HALO - (CONV_WIDTH - 1)
    for c in range(ts // chunk):
        acc = jnp.zeros((chunk, CONV_CH), F32) + b_ref[...]
        for k in range(CONV_WIDTH):
            acc = acc + buf[first + k + c * chunk:first + k + (c + 1) * chunk, :] * w_ref[k:k + 1, :]
        mu = jnp.mean(acc, axis=-1, keepdims=True)
        d = acc - mu
        var = jnp.mean(d * d, axis=-1, keepdims=True)
        y = d * lax.rsqrt(var + EPS) * g_ref[...] + beta_ref[...]
        o_ref[c * chunk:(c + 1) * chunk, :] = (y * jax.nn.sigmoid(y)).astype(BF16)
    buf[0:HALO, :] = buf[ts:ts + HALO, :]


def _conformer_conv(glu, w, bias, g, beta, b, s):
    ts = min(TS_CONV, s)
    full = lambda a: pl.BlockSpec(a.shape, lambda bi, i: (0, 0))
    row = pl.BlockSpec((ts, CONV_CH), lambda bi, i: (bi * (s // ts) + i, 0))
    return pl.pallas_call(
        _conv_kernel,
        grid=(b, s // ts),
        in_specs=[row, full(w), full(bias), full(g), full(beta)],
        out_specs=row,
        out_shape=jax.ShapeDtypeStruct((b * s, CONV_CH), BF16),
        scratch_shapes=[pltpu.VMEM((HALO + ts, CONV_CH), F32)],
        compiler_params=_cparams(("arbitrary", "arbitrary")),
        name="conformer_conv",
    )(glu, w, bias, g, beta)


def _merge_router_kernel(x_ref, a_ref, c_ref, mg_ref, wa_ref, wc_ref, wo_ref, fg_ref, wr_ref, br_ref,
                         x1_o, h2_o, rt_o, cnt_o):
    tm = x_ref.shape[0]
    y_a = _dot(a_ref[...], wa_ref[...])
    y_c = _dot(c_ref[...], wc_ref[...])
    mix = mg_ref[:, 0:D_MODEL].astype(F32) * y_a + mg_ref[:, D_MODEL:2 * D_MODEL].astype(F32) * y_c
    x1 = x_ref[...] + _dot(mix.astype(BF16), wo_ref[...])
    x1_o[...] = x1
    h2 = x1 * lax.rsqrt(jnp.mean(x1 * x1, axis=-1, keepdims=True) + EPS) * fg_ref[...]
    h2_o[...] = h2

    hh, hl = _split(h2)
    logits = _dot(hh, wr_ref[0]) + _dot(hl, wr_ref[0]) + _dot(hh, wr_ref[1]) + br_ref[...]
    lane = lax.broadcasted_iota(I32, (tm, LANES), 1)
    is_g = (lane >= GROUP_LANE0) & (lane < GROUP_LANE0 + N_GROUPS)
    gmax = jnp.max(jnp.where(is_g, logits, -jnp.inf), axis=-1, keepdims=True)
    gsel = jnp.min(jnp.where(is_g & (logits == gmax), lane - GROUP_LANE0, N_GROUPS), axis=-1, keepdims=True)
    pg_sel = 1.0 / jnp.sum(jnp.where(is_g, jnp.exp(logits - gmax), 0.0), axis=-1, keepdims=True)
    in_grp = (lane < N_EXPERTS) & (lane // EXP_PER_GROUP == gsel)
    emax = jnp.max(jnp.where(in_grp, logits, -jnp.inf), axis=-1, keepdims=True)
    ee = jnp.where(in_grp, jnp.exp(logits - emax), 0.0)
    pe = jnp.where(in_grp, ee / jnp.sum(ee, axis=-1, keepdims=True), -1.0)
    p1 = jnp.max(pe, axis=-1, keepdims=True)
    i1 = jnp.min(jnp.where(pe == p1, lane, LANES), axis=-1, keepdims=True)
    pe2 = jnp.where(lane == i1, -1.0, pe)
    p2 = jnp.max(pe2, axis=-1, keepdims=True)
    i2 = jnp.min(jnp.where(pe2 == p2, lane, LANES), axis=-1, keepdims=True)
    w1 = pg_sel * p1 / (p1 + p2)
    w2 = pg_sel * p2 / (p1 + p2)

    hot1 = lane == i1
    hot2 = lane == i2
    ind = jnp.where(hot1 | hot2, 1.0, 0.0)
    tri = (lax.broadcasted_iota(I32, (tm, tm), 1) < lax.broadcasted_iota(I32, (tm, tm), 0)).astype(BF16)
    before = _dot(tri, ind.astype(BF16))
    k1 = jnp.sum(jnp.where(hot1, before, 0.0), axis=-1, keepdims=True)
    k2 = jnp.sum(jnp.where(hot2, before, 0.0), axis=-1, keepdims=True)
    rec = jnp.zeros((tm, LANES), F32)
    for slot, val in ((R_E1, i1.astype(F32)), (R_E2, i2.astype(F32)), (R_W1, w1), (R_W2, w2), (R_K1, k1), (R_K2, k2)):
        rec = jnp.where(lane == slot, val, rec)
    rt_o[...] = rec
    cnt_o[0] = jnp.broadcast_to(jnp.sum(ind, axis=0, keepdims=True), (SUBLANES, LANES))


def _merge_router(x2, attn, cact, mg, wa, wc, wo, fg, wr, br):
    t = x2.shape[0]
    tm = TM_MERGE
    row = lambda n: pl.BlockSpec((tm, n), lambda i: (i, 0))
    full = lambda a: pl.BlockSpec(a.shape, lambda i: (0,) * a.ndim)
    return pl.pallas_call(
        _merge_router_kernel,
        grid=(t // tm,),
        in_specs=[row(D_MODEL), row(Q_COLS), row(CONV_CH), row(2 * D_MODEL), full(wa), full(wc), full(wo),
                  full(fg), full(wr), full(br)],
        out_specs=[row(D_MODEL), row(D_MODEL), row(LANES), pl.BlockSpec((1, SUBLANES, LANES), lambda i: (i, 0, 0))],
        out_shape=[jax.ShapeDtypeStruct((t, D_MODEL), F32), jax.ShapeDtypeStruct((t, D_MODEL), F32),
                   jax.ShapeDtypeStruct((t, LANES), F32), jax.ShapeDtypeStruct((t // tm, SUBLANES, LANES), F32)],
        compiler_params=_cparams(("parallel",)),
        name="merge_router",
    )(x2, attn, cact, mg, wa, wc, wo, fg, wr, br)


def _dispatch_kernel(dest_ref, h_ref, zeros_ref, xg_ref, sem):
    del zeros_ref
    td = h_ref.shape[0]

    def row_copy(t, slot):
        return pltpu.make_async_copy(h_ref.at[pl.ds(t, 1), :], xg_ref.at[pl.ds(slot, 1), :], sem)

    def issue(t, c):
        row_copy(t, dest_ref[2 * t]).start()
        row_copy(t, dest_ref[2 * t + 1]).start()
        return c

    lax.fori_loop(0, td, issue, 0)

    def drain(t, c):
        row_copy(0, 0).wait()
        return c

    lax.fori_loop(0, 2 * td, drain, 0)


def _dispatch(dest_flat, h2, p_rows):
    t = h2.shape[0]
    td = min(TD, t)
    zeros = jnp.zeros((p_rows, D_MODEL), F32)
    return pl.pallas_call(
        _dispatch_kernel,
        grid=(t // td,),
        in_specs=[pl.BlockSpec((2 * td,), lambda i: (i,), memory_space=pltpu.SMEM),
                  pl.BlockSpec((td, D_MODEL), lambda i: (i, 0)),
                  pl.BlockSpec(memory_space=pl.ANY)],
        out_specs=pl.BlockSpec(memory_space=pl.ANY),
        out_shape=jax.ShapeDtypeStruct((p_rows, D_MODEL), F32),
        scratch_shapes=[pltpu.SemaphoreType.DMA(())],
        input_output_aliases={2: 0},
        compiler_params=_cparams(("arbitrary",)),
        name="moe_dispatch",
    )(dest_flat, h2, zeros)


def _expert_kernel(blk_e_ref, nused_ref, x_ref, wg_ref, wu_ref, wd_ref, y_ref):
    del blk_e_ref
    used = pl.program_id(0) < nused_ref[0]

    @pl.when(used)
    def _():
        xb = x_ref[...].astype(BF16)
        a = _dot(xb, wg_ref[0])
        u = _dot(xb, wu_ref[0])
        y_ref[...] = _dot((a * jax.nn.sigmoid(a) * u).astype(BF16), wd_ref[0])

    @pl.when(jnp.logical_not(used))
    def _():
        y_ref[...] = jnp.zeros(y_ref.shape, F32)


def _experts(blk_e, nused, xg, wg, wu, wd):
    p_rows = xg.shape[0]
    return pl.pallas_call(
        _expert_kernel,
        grid_spec=pltpu.PrefetchScalarGridSpec(
            num_scalar_prefetch=2,
            grid=(p_rows // BM,),
            in_specs=[pl.BlockSpec((BM, D_MODEL), lambda i, be, nu: (i, 0)),
                      pl.BlockSpec((1, D_MODEL, EXPERT_FF), lambda i, be, nu: (be[i], 0, 0)),
                      pl.BlockSpec((1, D_MODEL, EXPERT_FF), lambda i, be, nu: (be[i], 0, 0)),
                      pl.BlockSpec((1, EXPERT_FF, D_MODEL), lambda i, be, nu: (be[i], 0, 0))],
            out_specs=pl.BlockSpec((BM, D_MODEL), lambda i, be, nu: (i, 0)),
        ),
        out_shape=jax.ShapeDtypeStruct((p_rows, D_MODEL), F32),
        compiler_params=_cparams(("arbitrary",)),
        name="moe_experts",
    )(blk_e, nused, xg, wg, wu, wd)


def _combine_ple_kernel(dest_ref, x1_ref, rt_ref, p_ref, pg_ref, wpg_ref, bpg_ref, wpp_ref, yg_ref, o_ref,
                        y0, y1, sem):
    tc = x1_ref.shape[0]

    def row_copy(slot, dst, t):
        return pltpu.make_async_copy(yg_ref.at[pl.ds(slot, 1), :], dst.at[pl.ds(t, 1), :], sem)

    def issue(t, c):
        row_copy(dest_ref[2 * t], y0, t).start()
        row_copy(dest_ref[2 * t + 1], y1, t).start()
        return c

    lax.fori_loop(0, tc, issue, 0)

    def drain(t, c):
        row_copy(0, y0, 0).wait()
        return c

    lax.fori_loop(0, 2 * tc, drain, 0)

    rt = rt_ref[...]
    x2 = x1_ref[...] + rt[:, R_W1:R_W1 + 1] * y0[...] + rt[:, R_W2:R_W2 + 1] * y1[...]
    h3 = (x2 * lax.rsqrt(jnp.mean(x2 * x2, axis=-1, keepdims=True) + EPS) * pg_ref[...]).astype(BF16)
    gate = jax.nn.sigmoid(_dot(h3, wpg_ref[...]) + bpg_ref[...])
    o_ref[...] = x2 + gate * _dot(p_ref[...].astype(BF16), wpp_ref[...])


def _combine_ple(dest_flat, x1, route, p2, pg, wpg, bpg, wpp, yg):
    t = x1.shape[0]
    tc = TC
    row = lambda n: pl.BlockSpec((tc, n), lambda i: (i, 0))
    full = lambda a: pl.BlockSpec(a.shape, lambda i: (0,) * a.ndim)
    return pl.pallas_call(
        _combine_ple_kernel,
        grid=(t // tc,),
        in_specs=[pl.BlockSpec((2 * tc,), lambda i: (i,), memory_space=pltpu.SMEM),
                  row(D_MODEL), row(LANES), row(PLE_DIM), full(pg), full(wpg), full(bpg), full(wpp),
                  pl.BlockSpec(memory_space=pl.ANY)],
        out_specs=row(D_MODEL),
        out_shape=jax.ShapeDtypeStruct((t, D_MODEL), F32),
        scratch_shapes=[pltpu.VMEM((tc, D_MODEL), F32), pltpu.VMEM((tc, D_MODEL), F32),
                        pltpu.SemaphoreType.DMA(())],
        compiler_params=_cparams(("arbitrary",)),
        name="moe_combine_ple",
    )(dest_flat, x1, route, p2, pg, wpg, bpg, wpp, yg)


def _seg_sum_matrix():
    seg = np.arange(LANES) // HEAD_DIM
    return jnp.asarray(seg[:, None] == seg[None, :], BF16)


def _overlap_t(nc_rows, n_blk):
    nc = np.arange(nc_rows)[None, :] * CMP_STRIDE
    sb = np.arange(n_blk)[:, None] * SEL_BLOCK
    ov = (nc < sb + SEL_BLOCK) & (nc + CMP_BLOCK > sb) & (np.arange(nc_rows)[None, :] < nc_rows - 1)
    return jnp.asarray(ov, BF16)


def _compress_weights(pos, w1, w2):
    half = CMP_BLOCK // 2
    pe = jnp.broadcast_to(pos[:, None, :], (CMP_BLOCK, N_KV, HEAD_DIM)).reshape(2, half * N_KV * HEAD_DIM)
    w1r = w1.reshape(2, half, HEAD_DIM, CMP_HIDDEN)
    eye = jnp.eye(N_KV, dtype=w1.dtype)
    wx = jnp.einsum('hldj,gk->hlgdkj', w1r, eye).reshape(2, half * N_KV * HEAD_DIM, N_KV * CMP_HIDDEN)
    w2x = jnp.einsum('jd,gk->gjkd', w2, eye).reshape(N_KV * CMP_HIDDEN, N_KV * HEAD_DIM)
    return pe, wx[0].astype(BF16), wx[1].astype(BF16), w2x.astype(BF16)


def _layer(x, p_i, prm):
    b, s, _ = x.shape
    t = b * s
    x2 = x.reshape(t, D_MODEL)
    bd = _seg_sum_matrix()

    w = prm["w_in"]
    w_p = jnp.concatenate([w[:, :OFF_GATE], jnp.pad(w[:, OFF_GATE:OFF_CONV], ((0, 0), (0, LANES - 3 * N_HEADS))),
                           w[:, OFF_CONV:]], axis=1).astype(BF16)
    qg = jnp.tile(prm["q_norm_g"], LANES // HEAD_DIM)[None, :]
    kg = jnp.tile(prm["k_norm_g"], (1, LANES // HEAD_DIM))
    q, kc_raw, vc_raw, ks, vs, kw, vw, gt, glu, mg = _in_proj(
        x2, prm["attn_norm_g"][None, :], w_p, qg, kg, bd, prm["merge_b"][None, :], s)

    nrow = s // CMP_STRIDE
    cw = [_compress_weights(prm["cmp_pos"][j], prm["cmp_w1"][j], prm["cmp_w2"][j]) for j in range(2)]
    pe_x, wa, wb, w2x = (jnp.stack([cw[0][j], cw[1][j]]) for j in range(4))
    kc, vc = _compress(kc_raw.reshape(b, nrow, CMP_STRIDE * LANES), vc_raw.reshape(b, nrow, CMP_STRIDE * LANES),
                       pe_x, wa, wb, w2x, bd, kg)

    r3 = lambda a: a.reshape(b, s, a.shape[-1])
    attn = _attention(q, gt, kc, vc, r3(ks), r3(vs), r3(kw), r3(vw), _overlap_t(nrow, s // SEL_BLOCK), b, s)

    cact = _conformer_conv(glu, prm["conv_w"], prm["conv_b"][None, :], prm["conv_norm_g"][None, :],
                           prm["conv_norm_b"][None, :], b, s)

    wr_full = jnp.zeros((D_MODEL, LANES), F32)
    wr_full = wr_full.at[:, :N_EXPERTS].set(prm["w_router_expert"])
    wr_full = wr_full.at[:, GROUP_LANE0:GROUP_LANE0 + N_GROUPS].set(prm["w_router_group"])
    wr_hi = wr_full.astype(BF16)
    wr = jnp.stack([wr_hi, (wr_full - wr_hi.astype(F32)).astype(BF16)])
    br = jnp.zeros((1, LANES), F32)
    br = br.at[0, :N_EXPERTS].set(prm["b_router_expert"])
    br = br.at[0, GROUP_LANE0:GROUP_LANE0 + N_GROUPS].set(prm["b_router_group"])
    x1, h2, route, cnt = _merge_router(
        x2, attn, cact, mg, prm["w_attn_out"].astype(BF16), prm["w_conv_out"].astype(BF16),
        prm["w_out"].astype(BF16), prm["ffn_norm_g"][None, :], wr, br)

    n_tiles = t // TM_MERGE
    counts = cnt[:, 0, :N_EXPERTS].astype(I32)
    total = jnp.sum(counts, axis=0)
    padded = (total + BM - 1) // BM * BM
    pends = jnp.cumsum(padded)
    tile_base = (pends - padded)[None, :] + jnp.cumsum(counts, axis=0) - counts
    eid = route[:, R_E1:R_E2 + 1].astype(I32).reshape(n_tiles, TM_MERGE * 2)
    rank = route[:, R_K1:R_K2 + 1].astype(I32).reshape(n_tiles, TM_MERGE * 2)
    dest = (jnp.take_along_axis(tile_base, eid, axis=1) + rank).reshape(2 * t)
    n_blk = -(-(2 * t + N_EXPERTS * (BM - 1)) // BM)
    blk_e = jnp.minimum(jnp.searchsorted(pends, jnp.arange(n_blk, dtype=I32) * BM, side="right"),
                        N_EXPERTS - 1).astype(I32)
    nused = (pends[-1:] // BM).astype(I32)

    xg = _dispatch(dest, h2, n_blk * BM)
    yg = _experts(blk_e, nused, xg, prm["w_exp_gate"].astype(BF16), prm["w_exp_up"].astype(BF16),
                  prm["w_exp_down"].astype(BF16))
    out = _combine_ple(dest, x1, route, p_i.reshape(t, PLE_DIM), prm["ple_norm_g"][None, :],
                       prm["w_ple_gate"].astype(BF16), prm["b_ple_gate"][None, :], prm["w_ple_proj"].astype(BF16), yg)
    return out.reshape(b, s, D_MODEL)


def kernel(x, p, attn_norm_g, w_in, q_norm_g, k_norm_g, cmp_pos, cmp_w1, cmp_w2, w_attn_out, conv_w, conv_b, conv_norm_g, conv_norm_b, w_conv_out, merge_b, w_out, ffn_norm_g, w_router_group, b_router_group, w_router_expert, b_router_expert, w_exp_gate, w_exp_up, w_exp_down, ple_norm_g, w_ple_gate, b_ple_gate, w_ple_proj):
    stacked = dict(attn_norm_g=attn_norm_g, w_in=w_in, q_norm_g=q_norm_g, k_norm_g=k_norm_g, cmp_pos=cmp_pos,
                   cmp_w1=cmp_w1, cmp_w2=cmp_w2, w_attn_out=w_attn_out, conv_w=conv_w, conv_b=conv_b,
                   conv_norm_g=conv_norm_g, conv_norm_b=conv_norm_b, w_conv_out=w_conv_out, merge_b=merge_b,
                   w_out=w_out, ffn_norm_g=ffn_norm_g, w_router_group=w_router_group,
                   b_router_group=b_router_group, w_router_expert=w_router_expert,
                   b_router_expert=b_router_expert, w_exp_gate=w_exp_gate, w_exp_up=w_exp_up,
                   w_exp_down=w_exp_down, ple_norm_g=ple_norm_g, w_ple_gate=w_ple_gate, b_ple_gate=b_ple_gate,
                   w_ple_proj=w_ple_proj)
    for i in range(w_in.shape[0]):
        x = _layer(x, p[i], {k: v[i] for k, v in stacked.items()})
    return x
```

```python
import functools

import numpy as np
import jax
import jax.numpy as jnp
from jax import lax
from jax.experimental import pallas as pl
from jax.experimental.pallas import tpu as pltpu

F32 = jnp.float32
BF16 = jnp.bfloat16
I32 = jnp.int32

D_MODEL = 1024
PLE_DIM = 256
N_HEADS = 8
N_KV = 2
HPG = N_HEADS // N_KV
HEAD_DIM = 64
CMP_BLOCK = 32
CMP_STRIDE = 16
CMP_HIDDEN = 128
SEL_BLOCK = 64
N_SEL = 8
WINDOW = 512
CONV_CH = 512
CONV_WIDTH = 31
N_GROUPS = 4
EXP_PER_GROUP = 8
N_EXPERTS = N_GROUPS * EXP_PER_GROUP
EXPERT_FF = 512
EPS = 1e-6
NEG = -1e30
BIG = 1e4
Q_COLS = N_HEADS * HEAD_DIM
KV_COLS = N_KV * HEAD_DIM
OFF_KV = Q_COLS
OFF_GATE = OFF_KV + 6 * KV_COLS
OFF_CONV = OFF_GATE + 3 * N_HEADS
OFF_MERGE = OFF_CONV + 2 * CONV_CH

LANES = 128
SUBLANES = 8
VMEM_LIMIT = 56 * 1024 * 1024

TM_PROJ = 256
QB = 128
KT = 256
TS_CONV = 512
HALO = 32
TR = 512
BM = 256
CHUNK = SUBLANES
R_LOCAL = 2 * TR + N_EXPERTS * CHUNK
NCH = R_LOCAL // CHUNK

P_Q = 0
P_KV = P_Q + Q_COLS
P_GATE = P_KV + 6 * KV_COLS
P_CONV = P_GATE + LANES
P_MERGE = P_CONV + 2 * CONV_CH
P_END = P_MERGE + 2 * D_MODEL

POS_HI_LANE = HEAD_DIM
POS_LO_LANE = HEAD_DIM + 1
UNSELECTED = -(2.0 ** 40)

R_E1, R_E2, R_W1, R_W2, R_K1, R_K2 = 0, 1, 2, 3, 4, 5
GROUP_LANE0 = N_EXPERTS


def _cparams(sem, vmem=VMEM_LIMIT):
    return pltpu.CompilerParams(dimension_semantics=sem, vmem_limit_bytes=vmem)


def _dot(a, b):
    return jnp.dot(a, b, preferred_element_type=F32)


def _dot_nt(a, b):
    return lax.dot_general(a, b, (((1,), (1,)), ((), ())), preferred_element_type=F32)


def _split(x):
    hi = x.astype(BF16)
    lo = (x - hi.astype(F32)).astype(BF16)
    return hi, lo


def _seg_rmsnorm(z, bd):
    hi, lo = _split(z * z)
    ss = _dot(hi, bd) + _dot(lo, bd)
    return z * lax.rsqrt(ss * (1.0 / HEAD_DIM) + EPS)


def _dup_halves(k):
    lane = lax.broadcasted_iota(I32, k.shape, 1)
    r = pltpu.roll(k, HEAD_DIM, 1)
    return jnp.concatenate([jnp.where(lane < HEAD_DIM, k, r), jnp.where(lane >= HEAD_DIM, k, r)], axis=1)


def _in_proj_kernel(x_ref, g_ref, w_ref, qg_ref, kg_ref, bd_ref, mb_ref,
                    q_o, kc_o, vc_o, ks_o, vs_o, kw_o, vw_o, gt_o, glu_o, mg_o, *, seq_len):
    tm = x_ref.shape[0]
    x = x_ref[...]
    h = (x * lax.rsqrt(jnp.mean(x * x, axis=-1, keepdims=True) + EPS) * g_ref[...]).astype(BF16)
    bd = bd_ref[...]
    lane = lax.broadcasted_iota(I32, (tm, LANES), 1)
    lo_half = lane < HEAD_DIM

    def proj(a, b):
        return _dot(h, w_ref[:, a:b])

    qg = qg_ref[...]
    for c in range(Q_COLS // LANES):
        zq = proj(P_Q + c * LANES, P_Q + (c + 1) * LANES)
        qn = _seg_rmsnorm(zq, bd) * qg * (HEAD_DIM ** -0.5)
        q_o[:, 2 * c * LANES:(2 * c + 1) * LANES] = jnp.where(lo_half, qn, 0.0).astype(BF16)
        q_o[:, (2 * c + 1) * LANES:(2 * c + 2) * LANES] = jnp.where(
            lo_half, pltpu.roll(qn, HEAD_DIM, 1), 0.0).astype(BF16)
    kc_o[...] = proj(P_KV, P_KV + LANES)
    vc_o[...] = proj(P_KV + LANES, P_KV + 2 * LANES)

    pos = (pl.program_id(0) * tm) % seq_len + lax.broadcasted_iota(I32, (tm, LANES), 0)
    blk = pos // SEL_BLOCK
    pos_lanes = jnp.where(lane == POS_HI_LANE, (blk * SEL_BLOCK).astype(F32),
                          jnp.where(lane == POS_LO_LANE, (pos - blk * SEL_BLOCK).astype(F32), 0.0))
    onehot = jnp.where(lane == blk, 1.0, 0.0).astype(BF16)

    def key_tiles(zk, gain):
        kn = _seg_rmsnorm(zk, bd) * gain
        return (jnp.where(lo_half, kn, pos_lanes).astype(BF16),
                jnp.where(lo_half, pltpu.roll(kn, HEAD_DIM, 1), pos_lanes).astype(BF16))

    k0, k1 = key_tiles(proj(P_KV + 2 * LANES, P_KV + 3 * LANES), kg_ref[1:2, :])
    ks_o[...] = jnp.concatenate([k0, onehot, k1, onehot], axis=1)
    vs_o[...] = _dup_halves(proj(P_KV + 3 * LANES, P_KV + 4 * LANES)).astype(BF16)
    k0, k1 = key_tiles(proj(P_KV + 4 * LANES, P_KV + 5 * LANES), kg_ref[2:3, :])
    kw_o[...] = jnp.concatenate([k0, k1], axis=1)
    vw_o[...] = _dup_halves(proj(P_KV + 5 * LANES, P_KV + 6 * LANES)).astype(BF16)
    gt_o[...] = jax.nn.sigmoid(proj(P_GATE, P_GATE + LANES))
    za = proj(P_CONV, P_CONV + CONV_CH)
    zg = proj(P_CONV + CONV_CH, P_CONV + 2 * CONV_CH)
    glu_o[...] = za * jax.nn.sigmoid(zg)
    for c in range(2):
        zm = proj(P_MERGE + c * D_MODEL, P_MERGE + (c + 1) * D_MODEL)
        mg_o[:, c * D_MODEL:(c + 1) * D_MODEL] = jax.nn.sigmoid(
            zm + mb_ref[:, c * D_MODEL:(c + 1) * D_MODEL]).astype(BF16)


def _in_proj(x2, norm_g, w_p, qg, kg, bd, mb, seq_len):
    t = x2.shape[0]
    tm = TM_PROJ
    assert seq_len % tm == 0 and seq_len // SEL_BLOCK <= HEAD_DIM
    row = lambda n: pl.BlockSpec((tm, n), lambda i: (i, 0))
    full = lambda a: pl.BlockSpec(a.shape, lambda i: (0,) * a.ndim)
    outs = [(N_HEADS * LANES, BF16), (LANES, F32), (LANES, F32), (4 * LANES, BF16), (2 * LANES, BF16),
            (2 * LANES, BF16), (2 * LANES, BF16), (LANES, F32), (CONV_CH, F32), (2 * D_MODEL, BF16)]
    return pl.pallas_call(
        functools.partial(_in_proj_kernel, seq_len=seq_len),
        grid=(t // tm,),
        in_specs=[row(D_MODEL), full(norm_g), full(w_p), full(qg), full(kg), full(bd), full(mb)],
        out_specs=[row(n) for n, _ in outs],
        out_shape=[jax.ShapeDtypeStruct((t, n), d) for n, d in outs],
        compiler_params=_cparams(("parallel",)),
        name="in_proj",
    )(x2, norm_g, w_p, qg, kg, bd, mb)


def _compress_kernel(ck_ref, cv_ref, pe_ref, wa_ref, wb_ref, w2_ref, bd_ref, kg_ref, kc_o, vc_o):
    nrow = ck_ref.shape[1]
    for idx, (c_ref, o_ref) in enumerate(((ck_ref, kc_o), (cv_ref, vc_o))):
        c = c_ref[0]
        ya = _dot((c + pe_ref[idx, 0:1, :]).astype(BF16), wa_ref[idx])
        yb = _dot((c + pe_ref[idx, 1:2, :]).astype(BF16), wb_ref[idx])
        hid = ya + pltpu.roll(yb, nrow - 1, 0)
        out = _dot(jax.nn.gelu(hid).astype(BF16), w2_ref[idx])
        if idx == 0:
            out = _seg_rmsnorm(out, bd_ref[...]) * kg_ref[0:1, :]
        o_ref[0] = _dup_halves(out).astype(BF16)


def _compress(kc_raw, vc_raw, pe_x, wa, wb, w2x, bd, kg):
    b, nrow, width = kc_raw.shape
    batch = lambda n: pl.BlockSpec((1, nrow, n), lambda i: (i, 0, 0))
    full = lambda a: pl.BlockSpec(a.shape, lambda i: (0,) * a.ndim)
    return pl.pallas_call(
        _compress_kernel,
        grid=(b,),
        in_specs=[batch(width), batch(width), full(pe_x), full(wa), full(wb), full(w2x), full(bd), full(kg)],
        out_specs=[batch(2 * LANES), batch(2 * LANES)],
        out_shape=[jax.ShapeDtypeStruct((b, nrow, 2 * LANES), BF16)] * 2,
        compiler_params=_cparams(("parallel",)),
        name="compress",
    )(kc_raw, vc_raw, pe_x, wa, wb, w2x, bd, kg)


def _softmax_tile(s, mask, slopes, dist, m, l):
    ps, alphas, ms, ls = [], [], [], []
    for r in range(HPG):
        rows = slice(r * QB, (r + 1) * QB)
        sr = jnp.where(mask, s[rows] - slopes[r] * dist, NEG)
        m_new = jnp.maximum(m[rows], jnp.max(sr, axis=-1, keepdims=True))
        e = jnp.where(mask, jnp.exp(sr - m_new), 0.0)
        alpha = jnp.exp(m[rows] - m_new)
        ps.append(e)
        alphas.append(alpha)
        ms.append(m_new)
        ls.append(alpha * l[rows] + jnp.sum(e, axis=-1, keepdims=True))
    return ps, jnp.concatenate(alphas, 0), jnp.concatenate(ms, 0), jnp.concatenate(ls, 0)


def _safe_inv(l):
    pos = l > 0.0
    return jnp.where(pos, 1.0 / jnp.where(pos, l, 1.0), 0.0)


def _attn_kernel(q_ref, gt_ref, kc_ref, vc_ref, ks_ref, vs_ref, kw_ref, vw_ref, ovt_ref, o_ref):
    i = pl.program_id(1)
    q0 = i * QB
    nc_rows = kc_ref.shape[1]
    n_blk = ovt_ref.shape[0]
    lane = lax.broadcasted_iota(I32, (QB, LANES), 1)
    lo_half = lane < HEAD_DIM
    rel4 = ((lax.broadcasted_iota(I32, (HPG * QB, KT), 0) & (QB - 1))
            - lax.broadcasted_iota(I32, (HPG * QB, KT), 1)).astype(F32)
    gt = gt_ref[...]
    mc0 = jnp.full((HPG * QB, 1), NEG, F32)
    lc0 = jnp.zeros((HPG * QB, 1), F32)

    pos_lane = (lane == POS_HI_LANE) | (lane == POS_LO_LANE)
    o_cmp, q_sel, q_win = [], [], []
    for g in range(N_KV):
        gl = slice(g * LANES, (g + 1) * LANES)
        slopes = [2.0 ** -(g * HPG + r + 1) for r in range(HPG)]
        heads = [q_ref[:, (g * HPG + r) * LANES:(g * HPG + r + 1) * LANES] for r in range(HPG)]
        qc = jnp.concatenate(heads, axis=0)

        s = _dot_nt(qc, kc_ref[0, :, gl])
        cidx = lax.broadcasted_iota(I32, (QB, nc_rows), 1)
        c_end = cidx * CMP_STRIDE + (CMP_BLOCK - 1)
        tq = q0 + lax.broadcasted_iota(I32, (QB, nc_rows), 0)
        cmask = (c_end <= tq) & (cidx < nc_rows - 1)
        cdist = (tq - c_end).astype(F32)
        es, _, _, lc = _softmax_tile(s, cmask, slopes, cdist, mc0, lc0)
        inv = _safe_inv(lc)
        pc = [es[r] * inv[r * QB:(r + 1) * QB] for r in range(HPG)]
        o_c = _dot(jnp.concatenate(pc, 0).astype(BF16), vc_ref[0, :, gl])
        ph, plo = _split(pc[0] + pc[1] + pc[2] + pc[3])
        imp_t = _dot_nt(ovt_ref[...], ph) + _dot_nt(ovt_ref[...], plo)

        blk = lax.broadcasted_iota(I32, (n_blk, QB), 0)
        tl = q0 + lax.broadcasted_iota(I32, (n_blk, QB), 1)
        valid = blk * SEL_BLOCK <= tl
        cur = tl // SEL_BLOCK
        forced = (blk == 0) | (blk == cur) | (blk == cur - 1)
        score = jnp.where(valid, imp_t + jnp.where(forced, BIG, 0.0), -BIG)
        sel = jnp.zeros((n_blk, QB), jnp.bool_)
        for _ in range(min(N_SEL, n_blk)):
            mx = jnp.max(score, axis=0, keepdims=True)
            first = jnp.min(jnp.where(score == mx, blk, n_blk), axis=0, keepdims=True)
            pick = blk == first
            sel = sel | pick
            score = jnp.where(pick, -jnp.inf, score)
        sel_t = jnp.where(sel & valid, 1.0, 0.0)
        if n_blk < LANES:
            sel_t = jnp.concatenate([sel_t, jnp.zeros((LANES - n_blk, QB), F32)], axis=0)
        unsel = ((sel_t.T - 1.0) * -UNSELECTED).astype(BF16)

        qa = jnp.concatenate([jnp.where(pos_lane, slopes[r], heads[r]).astype(BF16) for r in range(HPG)], axis=0)
        o_cmp.append(o_c)
        q_win.append(qa)
        q_sel.append(jnp.concatenate([qa, jnp.concatenate([unsel] * HPG, axis=0)], axis=1))

    def update(s, v, m, l, acc):
        m_new = jnp.maximum(m, jnp.max(s, axis=-1, keepdims=True))
        p = jnp.exp(s - jnp.concatenate([m_new] * (KT // LANES), axis=1))
        alpha = jnp.exp(m - m_new)
        l = alpha * l + jnp.sum(p, axis=-1, keepdims=True)
        return m_new, l, alpha * acc + _dot(p.astype(BF16), v)

    def key_tile(j, carry, q_list, k_ref, v_ref, k_width, mask_fn):
        k0 = pl.multiple_of(j * KT, KT)
        dist = rel4 + (q0 - k0).astype(F32)
        out = []
        for g in range(N_KV):
            s = _dot_nt(q_list[g], k_ref[0, pl.ds(k0, KT), g * k_width:(g + 1) * k_width])
            if mask_fn is not None:
                s = jnp.where(mask_fn(dist), s, NEG)
            out += update(s, v_ref[0, pl.ds(k0, KT), g * LANES:(g + 1) * LANES], *carry[3 * g:3 * g + 3])
        return tuple(out)

    m0 = jnp.full((HPG * QB, LANES), NEG, F32)
    z0 = jnp.zeros((HPG * QB, LANES), F32)
    init = (m0, z0, z0) * N_KV
    causal = lambda d: d >= 0.0
    in_window = lambda d: (d >= 0.0) & (d < float(WINDOW))
    j_diag = (q0 + QB - 1) // KT
    carry = lax.fori_loop(0, j_diag, lambda j, c: key_tile(j, c, q_sel, ks_ref, vs_ref, 2 * LANES, None), init)
    sel_out = key_tile(j_diag, carry, q_sel, ks_ref, vs_ref, 2 * LANES, causal)
    j_lo = jnp.maximum(q0 - (WINDOW - 1), 0) // KT
    win_out = lax.fori_loop(j_lo, j_diag + 1,
                            lambda j, c: key_tile(j, c, q_win, kw_ref, vw_ref, LANES, in_window), init)

    for g in range(N_KV):
        o_s = sel_out[3 * g + 2] / sel_out[3 * g + 1]
        o_w = win_out[3 * g + 2] / win_out[3 * g + 1]
        mixed = []
        for r in range(HPG):
            rows = slice(r * QB, (r + 1) * QB)
            c0 = 3 * (g * HPG + r)
            mixed.append(gt[:, c0:c0 + 1] * o_cmp[g][rows] + gt[:, c0 + 1:c0 + 2] * o_s[rows]
                         + gt[:, c0 + 2:c0 + 3] * o_w[rows])
        for pr in range(HPG // 2):
            col = (g * 2 + pr) * LANES
            o_ref[:, col:col + LANES] = jnp.where(lo_half, mixed[2 * pr], mixed[2 * pr + 1]).astype(BF16)


def _attention(q, gt, kc, vc, ks, vs, kw, vw, ovt, b, s):
    per_q = lambda n: pl.BlockSpec((QB, n), lambda bi, i: (bi * (s // QB) + i, 0))
    per_b = lambda a: pl.BlockSpec((1,) + a.shape[1:], lambda bi, i: (bi, 0, 0))
    return pl.pallas_call(
        _attn_kernel,
        grid=(b, s // QB),
        in_specs=[per_q(N_HEADS * LANES), per_q(LANES), per_b(kc), per_b(vc), per_b(ks), per_b(vs), per_b(kw),
                  per_b(vw),
                  pl.BlockSpec(ovt.shape, lambda bi, i: (0, 0))],
        out_specs=per_q(Q_COLS),
        out_shape=jax.ShapeDtypeStruct((b * s, Q_COLS), BF16),
        compiler_params=_cparams(("parallel", "parallel")),
        name="nsa_attention",
    )(q, gt, kc, vc, ks, vs, kw, vw, ovt)


def _conv_kernel(z_ref, w_ref, b_ref, g_ref, beta_ref, o_ref, buf):
    ts = z_ref.shape[0]
    chunk = 64

    @pl.when(pl.program_id(1) == 0)
    def _():
        buf[0:HALO, :] = jnp.zeros((HALO, CONV_CH), F32)

    buf[HALO:HALO + ts, :] = z_ref[...]
    first = HALO - (CONV_WIDTH - 1)
    for c in range(ts // chunk):
        acc = jnp.zeros((chunk, CONV_CH), F32) + b_ref[...]
        for k in range(CONV_WIDTH):
            acc = acc + buf[first + k + c * chunk:first + k + (c + 1) * chunk, :] * w_ref[k:k + 1, :]
        mu = jnp.mean(acc, axis=-1, keepdims=True)
        d = acc - mu
        var = jnp.mean(d * d, axis=-1, keepdims=True)
        y = d * lax.rsqrt(var + EPS) * g_ref[...] + beta_ref[...]
        o_ref[c * chunk:(c + 1) * chunk, :] = (y * jax.nn.sigmoid(y)).astype(BF16)
    buf[0:HALO, :] = buf[ts:ts + HALO, :]


def _conformer_conv(glu, w, bias, g, beta, b, s):
    ts = min(TS_CONV, s)
    full = lambda a: pl.BlockSpec(a.shape, lambda bi, i: (0, 0))
    row = pl.BlockSpec((ts, CONV_CH), lambda bi, i: (bi * (s // ts) + i, 0))
    return pl.pallas_call(
        _conv_kernel,
        grid=(b, s // ts),
        in_specs=[row, full(w), full(bias), full(g), full(beta)],
        out_specs=row,
        out_shape=jax.ShapeDtypeStruct((b * s, CONV_CH), BF16),
        scratch_shapes=[pltpu.VMEM((HALO + ts, CONV_CH), F32)],
        compiler_params=_cparams(("arbitrary", "arbitrary")),
        name="conformer_conv",
    )(glu, w, bias, g, beta)


def _merge_router_kernel(x_ref, a_ref, c_ref, mg_ref, wa_ref, wc_ref, wo_ref, fg_ref, wr_ref, br_ref,
                         x1_o, h2_o, rt_o, cnt_o):
    tm = x_ref.shape[0]
    y_a = _dot(a_ref[...], wa_ref[...])
    y_c = _dot(c_ref[...], wc_ref[...])
    mix = mg_ref[:, 0:D_MODEL].astype(F32) * y_a + mg_ref[:, D_MODEL:2 * D_MODEL].astype(F32) * y_c
    x1 = x_ref[...] + _dot(mix.astype(BF16), wo_ref[...])
    x1_o[...] = x1
    h2 = x1 * lax.rsqrt(jnp.mean(x1 * x1, axis=-1, keepdims=True) + EPS) * fg_ref[...]
    h2_o[...] = h2

    hh, hl = _split(h2)
    logits = _dot(hh, wr_ref[0]) + _dot(hl, wr_ref[0]) + _dot(hh, wr_ref[1]) + br_ref[...]
    lane = lax.broadcasted_iota(I32, (tm, LANES), 1)
    is_g = (lane >= GROUP_LANE0) & (lane < GROUP_LANE0 + N_GROUPS)
    gmax = jnp.max(jnp.where(is_g, logits, -jnp.inf), axis=-1, keepdims=True)
    gsel = jnp.min(jnp.where(is_g & (logits == gmax), lane - GROUP_LANE0, N_GROUPS), axis=-1, keepdims=True)
    pg_sel = 1.0 / jnp.sum(jnp.where(is_g, jnp.exp(logits - gmax), 0.0), axis=-1, keepdims=True)
    in_grp = (lane < N_EXPERTS) & (lane // EXP_PER_GROUP == gsel)
    emax = jnp.max(jnp.where(in_grp, logits, -jnp.inf), axis=-1, keepdims=True)
    ee = jnp.where(in_grp, jnp.exp(logits - emax), 0.0)
    pe = jnp.where(in_grp, ee / jnp.sum(ee, axis=-1, keepdims=True), -1.0)
    p1 = jnp.max(pe, axis=-1, keepdims=True)
    i1 = jnp.min(jnp.where(pe == p1, lane, LANES), axis=-1, keepdims=True)
    pe2 = jnp.where(lane == i1, -1.0, pe)
    p2 = jnp.max(pe2, axis=-1, keepdims=True)
    i2 = jnp.min(jnp.where(pe2 == p2, lane, LANES), axis=-1, keepdims=True)
    w1 = pg_sel * p1 / (p1 + p2)
    w2 = pg_sel * p2 / (p1 + p2)

    hot1 = lane == i1
    hot2 = lane == i2
    ind = jnp.where(hot1 | hot2, 1.0, 0.0)
    tri = (lax.broadcasted_iota(I32, (tm, tm), 1) < lax.broadcasted_iota(I32, (tm, tm), 0)).astype(BF16)
    before = _dot(tri, ind.astype(BF16))
    k1 = jnp.sum(jnp.where(hot1, before, 0.0), axis=-1, keepdims=True)
    k2 = jnp.sum(jnp.where(hot2, before, 0.0), axis=-1, keepdims=True)
    rec = jnp.zeros((tm, LANES), F32)
    for slot, val in ((R_E1, i1.astype(F32)), (R_E2, i2.astype(F32)), (R_W1, w1), (R_W2, w2), (R_K1, k1), (R_K2, k2)):
        rec = jnp.where(lane == slot, val, rec)
    rt_o[...] = rec
    cnt_o[0] = jnp.broadcast_to(jnp.sum(ind, axis=0, keepdims=True), (SUBLANES, LANES))


def _merge_router(x2, attn, cact, mg, wa, wc, wo, fg, wr, br):
    t = x2.shape[0]
    tm = TR
    row = lambda n: pl.BlockSpec((tm, n), lambda i: (i, 0))
    full = lambda a: pl.BlockSpec(a.shape, lambda i: (0,) * a.ndim)
    return pl.pallas_call(
        _merge_router_kernel,
        grid=(t // tm,),
        in_specs=[row(D_MODEL), row(Q_COLS), row(CONV_CH), row(2 * D_MODEL), full(wa), full(wc), full(wo),
                  full(fg), full(wr), full(br)],
        out_specs=[row(D_MODEL), row(D_MODEL), row(LANES), pl.BlockSpec((1, SUBLANES, LANES), lambda i: (i, 0, 0))],
        out_shape=[jax.ShapeDtypeStruct((t, D_MODEL), F32), jax.ShapeDtypeStruct((t, D_MODEL), F32),
                   jax.ShapeDtypeStruct((t, LANES), F32), jax.ShapeDtypeStruct((t // tm, SUBLANES, LANES), F32)],
        compiler_params=_cparams(("parallel",)),
        name="merge_router",
    )(x2, attn, cact, mg, wa, wc, wo, fg, wr, br)


def _chunk_copies(dst8_ref, n_chunks, make_copy):
    base = pl.program_id(0) * NCH

    def issue(c, carry):
        far = pl.multiple_of(dst8_ref[base + c] * CHUNK, CHUNK)
        make_copy(pl.multiple_of(c * CHUNK, CHUNK), far).start()
        return carry

    lax.fori_loop(0, n_chunks, issue, 0)

    def drain():
        def wait_one(c, carry):
            make_copy(0, 0).wait()
            return carry
        lax.fori_loop(0, n_chunks, wait_one, 0)

    return drain


def _dispatch_kernel(dst8_ref, nct_ref, h_ref, rt_ref, lo_ref, zeros_ref, xg_ref, slot_o, xs, sem):
    del zeros_ref
    tr = h_ref.shape[0]
    rt = rt_ref[...]
    lane = lax.broadcasted_iota(I32, (tr, LANES), 1)
    lane_f = lane.astype(F32)
    lo_row = lo_ref[0, 0:1, :]
    slots = []
    for e_lane, k_lane in ((R_E1, R_K1), (R_E2, R_K2)):
        hot = lane_f == rt[:, e_lane:e_lane + 1]
        slots.append(jnp.sum(jnp.where(hot, lo_row, 0.0), axis=-1, keepdims=True) + rt[:, k_lane:k_lane + 1])
    rec = jnp.where(lane == 0, slots[0], jnp.where(lane == 1, slots[1], 0.0))
    slot_o[...] = rec
    rec_t = jnp.concatenate([rec[c * LANES:(c + 1) * LANES].T for c in range(tr // LANES)], axis=1)
    row_id = lax.broadcasted_iota(I32, (R_LOCAL, tr), 0).astype(F32)
    pick = (row_id == rec_t[0:1, :]) | (row_id == rec_t[1:2, :])
    xs[...] = _dot(jnp.where(pick, 1.0, 0.0).astype(BF16), h_ref[...].astype(BF16))

    def make_copy(near, far):
        return pltpu.make_async_copy(xs.at[pl.ds(near, CHUNK), :], xg_ref.at[pl.ds(far, CHUNK), :], sem)

    _chunk_copies(dst8_ref, nct_ref[pl.program_id(0)], make_copy)()


def _dispatch(dst8, nct, h2, route, lo_rows, p_rows):
    t = h2.shape[0]
    zeros = jnp.zeros((p_rows, D_MODEL), F32)
    return pl.pallas_call(
        _dispatch_kernel,
        grid_spec=pltpu.PrefetchScalarGridSpec(
            num_scalar_prefetch=2,
            grid=(t // TR,),
            in_specs=[pl.BlockSpec((TR, D_MODEL), lambda i, d, n: (i, 0)),
                      pl.BlockSpec((TR, LANES), lambda i, d, n: (i, 0)),
                      pl.BlockSpec((1, SUBLANES, LANES), lambda i, d, n: (i, 0, 0)),
                      pl.BlockSpec(memory_space=pl.ANY)],
            out_specs=[pl.BlockSpec(memory_space=pl.ANY),
                       pl.BlockSpec((TR, LANES), lambda i, d, n: (i, 0))],
            scratch_shapes=[pltpu.VMEM((R_LOCAL, D_MODEL), F32), pltpu.SemaphoreType.DMA(())],
        ),
        out_shape=[jax.ShapeDtypeStruct((p_rows, D_MODEL), F32), jax.ShapeDtypeStruct((t, LANES), F32)],
        input_output_aliases={5: 0},
        compiler_params=_cparams(("arbitrary",)),
        name="moe_dispatch",
    )(dst8, nct, h2, route, lo_rows, zeros)


def _expert_kernel(blk_e_ref, nused_ref, x_ref, wg_ref, wu_ref, wd_ref, y_ref):
    del blk_e_ref
    used = pl.program_id(0) < nused_ref[0]

    @pl.when(used)
    def _():
        xb = x_ref[...].astype(BF16)
        a = _dot(xb, wg_ref[0])
        u = _dot(xb, wu_ref[0])
        y_ref[...] = _dot((a * jax.nn.sigmoid(a) * u).astype(BF16), wd_ref[0])

    @pl.when(jnp.logical_not(used))
    def _():
        y_ref[...] = jnp.zeros(y_ref.shape, F32)


def _experts(blk_e, nused, xg, wg, wu, wd):
    p_rows = xg.shape[0]
    return pl.pallas_call(
        _expert_kernel,
        grid_spec=pltpu.PrefetchScalarGridSpec(
            num_scalar_prefetch=2,
            grid=(p_rows // BM,),
            in_specs=[pl.BlockSpec((BM, D_MODEL), lambda i, be, nu: (i, 0)),
                      pl.BlockSpec((1, D_MODEL, EXPERT_FF), lambda i, be, nu: (be[i], 0, 0)),
                      pl.BlockSpec((1, D_MODEL, EXPERT_FF), lambda i, be, nu: (be[i], 0, 0)),
                      pl.BlockSpec((1, EXPERT_FF, D_MODEL), lambda i, be, nu: (be[i], 0, 0))],
            out_specs=pl.BlockSpec((BM, D_MODEL), lambda i, be, nu: (i, 0)),
        ),
        out_shape=jax.ShapeDtypeStruct((p_rows, D_MODEL), F32),
        compiler_params=_cparams(("arbitrary",)),
        name="moe_experts",
    )(blk_e, nused, xg, wg, wu, wd)


def _combine_ple_kernel(dst8_ref, nct_ref, x1_ref, rt_ref, slot_ref, p_ref, pg_ref, wpg_ref, bpg_ref, wpp_ref,
                        yg_ref, o_ref, ys, sem):
    tr = x1_ref.shape[0]
    n_chunks = nct_ref[pl.program_id(0)]

    def make_copy(near, far):
        return pltpu.make_async_copy(yg_ref.at[pl.ds(far, CHUNK), :], ys.at[pl.ds(near, CHUNK), :], sem)

    drain = _chunk_copies(dst8_ref, n_chunks, make_copy)

    def clear(c, carry):
        ys[pl.ds(pl.multiple_of(c * CHUNK, CHUNK), CHUNK), :] = jnp.zeros((CHUNK, D_MODEL), F32)
        return carry

    lax.fori_loop(n_chunks, NCH, clear, 0)
    drain()

    rt = rt_ref[...]
    sl = slot_ref[...]
    col_id = lax.broadcasted_iota(I32, (tr, R_LOCAL), 1).astype(F32)
    yb = ys[...].astype(BF16)
    y0 = _dot(jnp.where(col_id == sl[:, 0:1], 1.0, 0.0).astype(BF16), yb)
    y1 = _dot(jnp.where(col_id == sl[:, 1:2], 1.0, 0.0).astype(BF16), yb)
    x2 = x1_ref[...] + rt[:, R_W1:R_W1 + 1] * y0 + rt[:, R_W2:R_W2 + 1] * y1
    h3 = (x2 * lax.rsqrt(jnp.mean(x2 * x2, axis=-1, keepdims=True) + EPS) * pg_ref[...]).astype(BF16)
    gate = jax.nn.sigmoid(_dot(h3, wpg_ref[...]) + bpg_ref[...])
    o_ref[...] = x2 + gate * _dot(p_ref[...].astype(BF16), wpp_ref[...])


def _combine_ple(dst8, nct, x1, route, slots, p2, pg, wpg, bpg, wpp, yg):
    t = x1.shape[0]
    row = lambda n: pl.BlockSpec((TR, n), lambda i, d, c: (i, 0))
    full = lambda a: pl.BlockSpec(a.shape, lambda i, d, c: (0,) * a.ndim)
    return pl.pallas_call(
        _combine_ple_kernel,
        grid_spec=pltpu.PrefetchScalarGridSpec(
            num_scalar_prefetch=2,
            grid=(t // TR,),
            in_specs=[row(D_MODEL), row(LANES), row(LANES), row(PLE_DIM), full(pg), full(wpg), full(bpg),
                      full(wpp), pl.BlockSpec(memory_space=pl.ANY)],
            out_specs=row(D_MODEL),
            scratch_shapes=[pltpu.VMEM((R_LOCAL, D_MODEL), F32), pltpu.SemaphoreType.DMA(())],
        ),
        out_shape=jax.ShapeDtypeStruct((t, D_MODEL), F32),
        compiler_params=_cparams(("arbitrary",)),
        name="moe_combine_ple",
    )(dst8, nct, x1, route, slots, p2, pg, wpg, bpg, wpp, yg)


def _seg_sum_matrix():
    seg = np.arange(LANES) // HEAD_DIM
    return jnp.asarray(seg[:, None] == seg[None, :], BF16)


def _overlap_t(nc_rows, n_blk):
    nc = np.arange(nc_rows)[None, :] * CMP_STRIDE
    sb = np.arange(n_blk)[:, None] * SEL_BLOCK
    ov = (nc < sb + SEL_BLOCK) & (nc + CMP_BLOCK > sb) & (np.arange(nc_rows)[None, :] < nc_rows - 1)
    return jnp.asarray(ov, BF16)


def _compress_weights(pos, w1, w2):
    half = CMP_BLOCK // 2
    pe = jnp.broadcast_to(pos[:, None, :], (CMP_BLOCK, N_KV, HEAD_DIM)).reshape(2, half * N_KV * HEAD_DIM)
    w1r = w1.reshape(2, half, HEAD_DIM, CMP_HIDDEN)
    eye = jnp.eye(N_KV, dtype=w1.dtype)
    wx = jnp.einsum('hldj,gk->hlgdkj', w1r, eye).reshape(2, half * N_KV * HEAD_DIM, N_KV * CMP_HIDDEN)
    w2x = jnp.einsum('jd,gk->gjkd', w2, eye).reshape(N_KV * CMP_HIDDEN, N_KV * HEAD_DIM)
    return pe, wx[0].astype(BF16), wx[1].astype(BF16), w2x.astype(BF16)


def _layer(x, p_i, prm):
    b, s, _ = x.shape
    t = b * s
    x2 = x.reshape(t, D_MODEL)
    bd = _seg_sum_matrix()

    w = prm["w_in"]
    w_p = jnp.concatenate([w[:, :OFF_GATE], jnp.pad(w[:, OFF_GATE:OFF_CONV], ((0, 0), (0, LANES - 3 * N_HEADS))),
                           w[:, OFF_CONV:]], axis=1).astype(BF16)
    qg = jnp.tile(prm["q_norm_g"], LANES // HEAD_DIM)[None, :]
    kg = jnp.tile(prm["k_norm_g"], (1, LANES // HEAD_DIM))
    q, kc_raw, vc_raw, ks, vs, kw, vw, gt, glu, mg = _in_proj(
        x2, prm["attn_norm_g"][None, :], w_p, qg, kg, bd, prm["merge_b"][None, :], s)

    nrow = s // CMP_STRIDE
    cw = [_compress_weights(prm["cmp_pos"][j], prm["cmp_w1"][j], prm["cmp_w2"][j]) for j in range(2)]
    pe_x, wa, wb, w2x = (jnp.stack([cw[0][j], cw[1][j]]) for j in range(4))
    kc, vc = _compress(kc_raw.reshape(b, nrow, CMP_STRIDE * LANES), vc_raw.reshape(b, nrow, CMP_STRIDE * LANES),
                       pe_x, wa, wb, w2x, bd, kg)

    r3 = lambda a: a.reshape(b, s, a.shape[-1])
    attn = _attention(q, gt, kc, vc, r3(ks), r3(vs), r3(kw), r3(vw), _overlap_t(nrow, s // SEL_BLOCK), b, s)

    cact = _conformer_conv(glu, prm["conv_w"], prm["conv_b"][None, :], prm["conv_norm_g"][None, :],
                           prm["conv_norm_b"][None, :], b, s)

    wr_full = jnp.zeros((D_MODEL, LANES), F32)
    wr_full = wr_full.at[:, :N_EXPERTS].set(prm["w_router_expert"])
    wr_full = wr_full.at[:, GROUP_LANE0:GROUP_LANE0 + N_GROUPS].set(prm["w_router_group"])
    wr_hi = wr_full.astype(BF16)
    wr = jnp.stack([wr_hi, (wr_full - wr_hi.astype(F32)).astype(BF16)])
    br = jnp.zeros((1, LANES), F32)
    br = br.at[0, :N_EXPERTS].set(prm["b_router_expert"])
    br = br.at[0, GROUP_LANE0:GROUP_LANE0 + N_GROUPS].set(prm["b_router_group"])
    x1, h2, route, cnt = _merge_router(
        x2, attn, cact, mg, prm["w_attn_out"].astype(BF16), prm["w_conv_out"].astype(BF16),
        prm["w_out"].astype(BF16), prm["ffn_norm_g"][None, :], wr, br)

    n_tiles = t // TR
    counts = cnt[:, 0, :N_EXPERTS].astype(I32)
    seg = (counts + CHUNK - 1) // CHUNK * CHUNK
    local_off = jnp.cumsum(seg, axis=1) - seg
    nct = (jnp.sum(seg, axis=1) // CHUNK).astype(I32)
    total = jnp.sum(seg, axis=0)
    padded = (total + BM - 1) // BM * BM
    pends = jnp.cumsum(padded)
    far_off = (pends - padded)[None, :] + jnp.cumsum(seg, axis=0) - seg
    chunk = jnp.arange(NCH, dtype=I32)[None, :, None]
    lo8 = (local_off // CHUNK)[:, None, :]
    in_seg = (chunk >= lo8) & (chunk < ((local_off + seg) // CHUNK)[:, None, :])
    dst8 = (jnp.sum(jnp.where(in_seg, (far_off // CHUNK)[:, None, :] - lo8, 0), axis=-1)
            + chunk[:, :, 0]).astype(I32).reshape(n_tiles * NCH)
    lo_rows = jnp.broadcast_to(jnp.pad(local_off.astype(F32), ((0, 0), (0, LANES - N_EXPERTS)))[:, None, :],
                               (n_tiles, SUBLANES, LANES))
    n_blk = -(-(2 * t + n_tiles * N_EXPERTS * (CHUNK - 1) + N_EXPERTS * (BM - 1)) // BM)
    blk_e = jnp.minimum(jnp.sum(jnp.arange(n_blk, dtype=I32)[:, None] * BM >= pends[None, :], axis=1),
                        N_EXPERTS - 1).astype(I32)
    nused = (pends[-1:] // BM).astype(I32)

    xg, slots = _dispatch(dst8, nct, h2, route, lo_rows, n_blk * BM)
    yg = _experts(blk_e, nused, xg, prm["w_exp_gate"].astype(BF16), prm["w_exp_up"].astype(BF16),
                  prm["w_exp_down"].astype(BF16))
    out = _combine_ple(dst8, nct, x1, route, slots, p_i.reshape(t, PLE_DIM), prm["ple_norm_g"][None, :],
                       prm["w_ple_gate"].astype(BF16), prm["b_ple_gate"][None, :], prm["w_ple_proj"].astype(BF16), yg)
    return out.reshape(b, s, D_MODEL)


def kernel(x, p, attn_norm_g, w_in, q_norm_g, k_norm_g, cmp_pos, cmp_w1, cmp_w2, w_attn_out, conv_w, conv_b, conv_norm_g, conv_norm_b, w_conv_out, merge_b, w_out, ffn_norm_g, w_router_group, b_router_group, w_router_expert, b_router_expert, w_exp_gate, w_exp_up, w_exp_down, ple_norm_g, w_ple_gate, b_ple_gate, w_ple_proj):
    stacked = dict(attn_norm_g=attn_norm_g, w_in=w_in, q_norm_g=q_norm_g, k_norm_g=k_norm_g, cmp_pos=cmp_pos,
                   cmp_w1=cmp_w1, cmp_w2=cmp_w2, w_attn_out=w_attn_out, conv_w=conv_w, conv_b=conv_b,
                   conv_norm_g=conv_norm_g, conv_norm_b=conv_norm_b, w_conv_out=w_conv_out, merge_b=merge_b,
                   w_out=w_out, ffn_norm_g=ffn_norm_g, w_router_group=w_router_group,
                   b_router_group=b_router_group, w_router_expert=w_router_expert,
                   b_router_expert=b_router_expert, w_exp_gate=w_exp_gate, w_exp_up=w_exp_up,
                   w_exp_down=w_exp_down, ple_norm_g=ple_norm_g, w_ple_gate=w_ple_gate, b_ple_gate=b_ple_gate,
                   w_ple_proj=w_ple_proj)
    for i in range(w_in.shape[0]):
        x = _layer(x, p[i], {k: v[i] for k, v in stacked.items()})
    return x
```

```python
import functools

import numpy as np
import jax
import jax.numpy as jnp
from jax import lax
from jax.experimental import pallas as pl
from jax.experimental.pallas import tpu as pltpu

F32 = jnp.float32
BF16 = jnp.bfloat16
I32 = jnp.int32

D_MODEL = 1024
PLE_DIM = 256
N_HEADS = 8
N_KV = 2
HPG = N_HEADS // N_KV
HEAD_DIM = 64
CMP_BLOCK = 32
CMP_STRIDE = 16
CMP_HIDDEN = 128
SEL_BLOCK = 64
N_SEL = 8
WINDOW = 512
CONV_CH = 512
CONV_WIDTH = 31
N_GROUPS = 4
EXP_PER_GROUP = 8
N_EXPERTS = N_GROUPS * EXP_PER_GROUP
EXPERT_FF = 512
EPS = 1e-6
NEG = -1e30
BIG = 1e4
Q_COLS = N_HEADS * HEAD_DIM
KV_COLS = N_KV * HEAD_DIM
OFF_KV = Q_COLS
OFF_GATE = OFF_KV + 6 * KV_COLS
OFF_CONV = OFF_GATE + 3 * N_HEADS
OFF_MERGE = OFF_CONV + 2 * CONV_CH

LANES = 128
SUBLANES = 8
VMEM_LIMIT = 56 * 1024 * 1024

TM_PROJ = 256
QB = 128
KT = 256
TS_CONV = 512
HALO = 32
TR = 512
BM = 256
CHUNK = SUBLANES
R_LOCAL = 2 * TR + N_EXPERTS * CHUNK
NCH = R_LOCAL // CHUNK

N_KC = 0
N_VC = N_KC + LANES
N_KS = N_VC + LANES
N_KW = N_KS + LANES
N_CONV = N_KW + LANES
N_MERGE = N_CONV + 2 * CONV_CH
T_Q = 0
T_VS = T_Q + Q_COLS
T_VW = T_VS + LANES
T_GATE = T_VW + LANES

POS_HI_LANE = HEAD_DIM
POS_LO_LANE = HEAD_DIM + 1
UNSELECTED = -(2.0 ** 40)

R_E1, R_E2, R_W1, R_W2, R_K1, R_K2 = 0, 1, 2, 3, 4, 5
GROUP_LANE0 = N_EXPERTS


def _cparams(sem, vmem=VMEM_LIMIT):
    return pltpu.CompilerParams(dimension_semantics=sem, vmem_limit_bytes=vmem)


def _dot(a, b):
    return jnp.dot(a, b, preferred_element_type=F32)


def _dot_nt(a, b):
    return lax.dot_general(a, b, (((1,), (1,)), ((), ())), preferred_element_type=F32)


def _split(x):
    hi = x.astype(BF16)
    lo = (x - hi.astype(F32)).astype(BF16)
    return hi, lo


def _seg_rmsnorm(z, bd):
    hi, lo = _split(z * z)
    ss = _dot(hi, bd) + _dot(lo, bd)
    return z * lax.rsqrt(ss * (1.0 / HEAD_DIM) + EPS)


def _with_pos_lanes(kn, pos):
    lane = lax.broadcasted_iota(I32, kn.shape, 1)
    hi = pos // SEL_BLOCK * SEL_BLOCK
    pos_lanes = jnp.where(lane == POS_HI_LANE, hi.astype(F32),
                          jnp.where(lane == POS_LO_LANE, (pos - hi).astype(F32), 0.0))
    lo_half = lane < HEAD_DIM
    return (jnp.where(lo_half, kn, pos_lanes).astype(BF16),
            jnp.where(lo_half, pltpu.roll(kn, HEAD_DIM, 1), pos_lanes).astype(BF16))


def _in_proj_kernel(x_ref, g_ref, w_ref, wt_ref, qg_ref, kg_ref, bd_ref, mb_ref,
                    qt_o, kc_o, vc_o, ks_o, vst_o, kw_o, vwt_o, gtt_o, glu_o, mg_o, *, seq_len):
    tm = x_ref.shape[0]
    x = x_ref[...]
    h = (x * lax.rsqrt(jnp.mean(x * x, axis=-1, keepdims=True) + EPS) * g_ref[...]).astype(BF16)
    bd = bd_ref[...]
    lane = lax.broadcasted_iota(I32, (tm, LANES), 1)

    def proj(a, b):
        return _dot(h, w_ref[:, a:b])

    zt = _dot_nt(wt_ref[...], h)
    for hd in range(N_HEADS):
        zh = zt[T_Q + hd * HEAD_DIM:T_Q + (hd + 1) * HEAD_DIM, :]
        ms = jnp.sum(zh * zh, axis=0, keepdims=True) * (1.0 / HEAD_DIM)
        qt_o[hd * HEAD_DIM:(hd + 1) * HEAD_DIM, :] = (
            zh * lax.rsqrt(ms + EPS) * qg_ref[...] * (HEAD_DIM ** -0.5)).astype(BF16)
    vst_o[0] = zt[T_VS:T_VS + LANES, :].astype(BF16)
    vwt_o[0] = zt[T_VW:T_VW + LANES, :].astype(BF16)
    gtt_o[...] = jax.nn.sigmoid(zt[T_GATE:T_GATE + LANES, :])

    kc_o[...] = proj(N_KC, N_KC + LANES)
    vc_o[...] = proj(N_VC, N_VC + LANES)

    pos = (pl.program_id(0) * tm) % seq_len + lax.broadcasted_iota(I32, (tm, LANES), 0)
    onehot = jnp.where(lane == pos // SEL_BLOCK, 1.0, 0.0).astype(BF16)

    def key_tiles(zk, gain):
        return _with_pos_lanes(_seg_rmsnorm(zk, bd) * gain, pos)

    k0, k1 = key_tiles(proj(N_KS, N_KS + LANES), kg_ref[1:2, :])
    ks_o[...] = jnp.concatenate([k0, onehot, k1, onehot], axis=1)
    k0, k1 = key_tiles(proj(N_KW, N_KW + LANES), kg_ref[2:3, :])
    kw_o[...] = jnp.concatenate([k0, k1], axis=1)
    za = proj(N_CONV, N_CONV + CONV_CH)
    zg = proj(N_CONV + CONV_CH, N_CONV + 2 * CONV_CH)
    glu_o[...] = za * jax.nn.sigmoid(zg)
    for c in range(2):
        zm = proj(N_MERGE + c * D_MODEL, N_MERGE + (c + 1) * D_MODEL)
        mg_o[:, c * D_MODEL:(c + 1) * D_MODEL] = jax.nn.sigmoid(
            zm + mb_ref[:, c * D_MODEL:(c + 1) * D_MODEL]).astype(BF16)


def _in_proj(x2, norm_g, w_nat, w_t, qg, kg, bd, mb, seq_len):
    t = x2.shape[0]
    tm = TM_PROJ
    assert tm == KT and seq_len % tm == 0 and seq_len // SEL_BLOCK <= HEAD_DIM
    row = lambda n: pl.BlockSpec((tm, n), lambda i: (i, 0))
    col = lambda n: pl.BlockSpec((n, tm), lambda i: (0, i))
    vt = pl.BlockSpec((1, LANES, tm), lambda i: (i, 0, 0))
    full = lambda a: pl.BlockSpec(a.shape, lambda i: (0,) * a.ndim)
    sds = jax.ShapeDtypeStruct
    return pl.pallas_call(
        functools.partial(_in_proj_kernel, seq_len=seq_len),
        grid=(t // tm,),
        in_specs=[row(D_MODEL), full(norm_g), full(w_nat), full(w_t), full(qg), full(kg), full(bd), full(mb)],
        out_specs=[col(Q_COLS), row(LANES), row(LANES), row(4 * LANES), vt, row(2 * LANES), vt, col(LANES),
                   row(CONV_CH), row(2 * D_MODEL)],
        out_shape=[sds((Q_COLS, t), BF16), sds((t, LANES), F32), sds((t, LANES), F32), sds((t, 4 * LANES), BF16),
                   sds((t // tm, LANES, tm), BF16), sds((t, 2 * LANES), BF16), sds((t // tm, LANES, tm), BF16),
                   sds((LANES, t), F32), sds((t, CONV_CH), F32), sds((t, 2 * D_MODEL), BF16)],
        compiler_params=_cparams(("parallel",)),
        name="in_proj",
    )(x2, norm_g, w_nat, w_t, qg, kg, bd, mb)


def _compress_kernel(ck_ref, cv_ref, pe_ref, wa_ref, wb_ref, w2k_ref, w2vt_ref, bd_ref, kg_ref, kc_o, vct_o):
    nrow = ck_ref.shape[1]

    def hidden(idx, c_ref):
        c = c_ref[0]
        ya = _dot((c + pe_ref[idx, 0:1, :]).astype(BF16), wa_ref[idx])
        yb = _dot((c + pe_ref[idx, 1:2, :]).astype(BF16), wb_ref[idx])
        return jax.nn.gelu(ya + pltpu.roll(yb, nrow - 1, 0)).astype(BF16)

    kn = _seg_rmsnorm(_dot(hidden(0, ck_ref), w2k_ref[...]), bd_ref[...]) * kg_ref[0:1, :]
    c_end = lax.broadcasted_iota(I32, (nrow, LANES), 0) * CMP_STRIDE + (CMP_BLOCK - 1)
    kc_o[0] = jnp.concatenate(_with_pos_lanes(kn, c_end), axis=1)
    vct_o[0] = _dot_nt(w2vt_ref[...], hidden(1, cv_ref)).astype(BF16)


def _compress(kc_raw, vc_raw, pe_x, wa, wb, w2k, w2vt, bd, kg):
    b, nrow, width = kc_raw.shape
    batch = lambda r, n: pl.BlockSpec((1, r, n), lambda i: (i, 0, 0))
    full = lambda a: pl.BlockSpec(a.shape, lambda i: (0,) * a.ndim)
    return pl.pallas_call(
        _compress_kernel,
        grid=(b,),
        in_specs=[batch(nrow, width), batch(nrow, width), full(pe_x), full(wa), full(wb), full(w2k), full(w2vt),
                  full(bd), full(kg)],
        out_specs=[batch(nrow, 2 * LANES), batch(LANES, nrow)],
        out_shape=[jax.ShapeDtypeStruct((b, nrow, 2 * LANES), BF16), jax.ShapeDtypeStruct((b, LANES, nrow), BF16)],
        compiler_params=_cparams(("parallel",)),
        name="compress",
    )(kc_raw, vc_raw, pe_x, wa, wb, w2k, w2vt, bd, kg)


def _safe_inv(l):
    pos = l > 0.0
    return jnp.where(pos, 1.0 / jnp.where(pos, l, 1.0), 0.0)


def _halve_rows(x, op):
    rows = x.shape[0]
    while rows > SUBLANES:
        rows //= 2
        x = op(x[:rows], x[rows:])
    return x


def _attn_kernel(qt_ref, gtt_ref, kc_ref, vct_ref, ks_ref, vst_ref, kw_ref, vwt_ref, ovt_ref, o_ref):
    i = pl.program_id(1)
    q0 = i * QB
    nc_rows = kc_ref.shape[1]
    n_blk = ovt_ref.shape[0]
    pw = 2 * QB
    n_pairs = N_KV * HPG // 2
    rel = ((lax.broadcasted_iota(I32, (KT, pw), 1) & (QB - 1))
           - lax.broadcasted_iota(I32, (KT, pw), 0)).astype(F32)
    slope_rows = lax.broadcasted_iota(I32, (HEAD_DIM, QB), 0) < 2
    gtt = gtt_ref[...]

    cidx = lax.broadcasted_iota(I32, (nc_rows, pw), 0)
    c_end = cidx * CMP_STRIDE + (CMP_BLOCK - 1)
    cmask = (c_end <= q0 + (lax.broadcasted_iota(I32, (nc_rows, pw), 1) & (QB - 1))) & (cidx < nc_rows - 1)

    o_cmp, q_sel, q_win = [], [], []
    for g in range(N_KV):
        q_pos = []
        for r in range(HPG):
            hd = g * HPG + r
            slope = jnp.where(slope_rows, 2.0 ** -(hd + 1), 0.0).astype(BF16)
            q_pos.append(jnp.concatenate([qt_ref[hd * HEAD_DIM:(hd + 1) * HEAD_DIM, :], slope], axis=0))
        pairs = [jnp.concatenate(q_pos[2 * pr:2 * pr + 2], axis=1) for pr in range(HPG // 2)]

        p_sum = jnp.zeros((nc_rows, QB), F32)
        for pr in range(HPG // 2):
            s = jnp.where(cmask, _dot(kc_ref[0, :, g * LANES:(g + 1) * LANES], pairs[pr]), NEG)
            e = jnp.where(cmask, jnp.exp(s - jnp.max(_halve_rows(s, jnp.maximum), axis=0, keepdims=True)), 0.0)
            p = e * _safe_inv(jnp.sum(_halve_rows(e, jnp.add), axis=0, keepdims=True))
            o_cmp.append(_dot(vct_ref[0, g * HEAD_DIM:(g + 1) * HEAD_DIM, :], p.astype(BF16)))
            p_sum = p_sum + p[:, :QB] + p[:, QB:]
        ph, plo = _split(p_sum)
        imp_t = _dot(ovt_ref[...], ph) + _dot(ovt_ref[...], plo)

        blk = lax.broadcasted_iota(I32, (n_blk, QB), 0)
        tl = q0 + lax.broadcasted_iota(I32, (n_blk, QB), 1)
        valid = blk * SEL_BLOCK <= tl
        cur = tl // SEL_BLOCK
        forced = (blk == 0) | (blk == cur) | (blk == cur - 1)
        score = jnp.where(valid, imp_t + jnp.where(forced, BIG, 0.0), -BIG)
        sel = jnp.zeros((n_blk, QB), jnp.bool_)
        for _ in range(min(N_SEL, n_blk)):
            mx = jnp.max(score, axis=0, keepdims=True)
            first = jnp.min(jnp.where(score == mx, blk, n_blk), axis=0, keepdims=True)
            pick = blk == first
            sel = sel | pick
            score = jnp.where(pick, -jnp.inf, score)
        unsel = jnp.where(sel & valid, 0.0, UNSELECTED).astype(BF16)
        bias = jnp.concatenate([unsel, jnp.zeros((2 * HEAD_DIM - n_blk, QB), BF16)], axis=0)
        for pr in range(HPG // 2):
            q_win.append(pairs[pr])
            q_sel.append(jnp.concatenate([pairs[pr], jnp.concatenate([bias, bias], axis=1)], axis=0))

    def update(s, vt, m, l, acc):
        m_new = jnp.maximum(m, jnp.max(_halve_rows(s, jnp.maximum), axis=0, keepdims=True))
        p = jnp.exp(s - m_new)
        alpha = jnp.exp(m - m_new)
        l_new = alpha * l + jnp.sum(_halve_rows(p, jnp.add), axis=0, keepdims=True)
        return m_new, l_new, alpha * acc + _dot(vt, p.astype(BF16))

    def scores(j, q_list, k_ref, k_width):
        k0 = pl.multiple_of(j * KT, KT)
        return tuple(_dot(k_ref[0, pl.ds(k0, KT), (c // (HPG // 2)) * k_width:(c // (HPG // 2) + 1) * k_width],
                          q_list[c]) for c in range(n_pairs))

    def consume(j, s_list, state, vt_ref, mask_fn):
        keep = None if mask_fn is None else mask_fn(rel + (q0 - j * KT).astype(F32))
        out = []
        for c in range(n_pairs):
            g = c // (HPG // 2)
            s = s_list[c] if keep is None else jnp.where(keep, s_list[c], NEG)
            out += update(s, vt_ref[j, g * HEAD_DIM:(g + 1) * HEAD_DIM, :], *state[3 * c:3 * c + 3])
        return tuple(out)

    j_diag = (q0 + QB - 1) // KT
    init = (jnp.full((1, pw), NEG, F32), jnp.zeros((1, pw), F32), jnp.zeros((HEAD_DIM, pw), F32)) * n_pairs

    def branch(j_first, q_list, k_ref, vt_ref, k_width, body_mask, last_mask):
        def body(j, carry):
            s_next = scores(j + 1, q_list, k_ref, k_width)
            return s_next + consume(j, carry[:n_pairs], carry[n_pairs:], vt_ref, body_mask)

        carry = lax.fori_loop(j_first, j_diag, body, scores(j_first, q_list, k_ref, k_width) + init)
        return consume(j_diag, carry[:n_pairs], carry[n_pairs:], vt_ref, last_mask)

    causal = lambda d: d >= 0.0
    in_window = lambda d: (d >= 0.0) & (d < float(WINDOW))
    sel_out = branch(0, q_sel, ks_ref, vst_ref, 2 * LANES, None, causal)
    win_out = branch(jnp.maximum(q0 - (WINDOW - 1), 0) // KT, q_win, kw_ref, vwt_ref, LANES, in_window, in_window)

    for c in range(n_pairs):
        o_s = sel_out[3 * c + 2] / sel_out[3 * c + 1]
        o_w = win_out[3 * c + 2] / win_out[3 * c + 1]
        halves = []
        for k in range(2):
            g0 = 3 * (2 * c + k)
            lanes = slice(k * QB, (k + 1) * QB)
            halves.append(gtt[g0:g0 + 1, :] * o_cmp[c][:, lanes] + gtt[g0 + 1:g0 + 2, :] * o_s[:, lanes]
                          + gtt[g0 + 2:g0 + 3, :] * o_w[:, lanes])
        o_ref[:, c * LANES:(c + 1) * LANES] = jnp.concatenate(halves, axis=0).T.astype(BF16)


def _attention(qt, gtt, kc, vct, ks, vst, kw, vwt, ovt, b, s):
    n_q = s // QB
    per_q = lambda n: pl.BlockSpec((n, QB), lambda bi, i: (0, bi * n_q + i))
    per_b = lambda a: pl.BlockSpec((1,) + a.shape[1:], lambda bi, i: (bi, 0, 0))
    vt = pl.BlockSpec((s // KT, LANES, KT), lambda bi, i: (bi, 0, 0))
    return pl.pallas_call(
        _attn_kernel,
        grid=(b, n_q),
        in_specs=[per_q(Q_COLS), per_q(LANES), per_b(kc), per_b(vct), per_b(ks), vt, per_b(kw), vt,
                  pl.BlockSpec(ovt.shape, lambda bi, i: (0, 0))],
        out_specs=pl.BlockSpec((QB, Q_COLS), lambda bi, i: (bi * n_q + i, 0)),
        out_shape=jax.ShapeDtypeStruct((b * s, Q_COLS), BF16),
        compiler_params=_cparams(("parallel", "parallel")),
        name="nsa_attention",
    )(qt, gtt, kc, vct, ks, vst, kw, vwt, ovt)


def _conv_kernel(z_ref, w_ref, b_ref, g_ref, beta_ref, o_ref, buf):
    ts = z_ref.shape[0]
    chunk = 64

    @pl.when(pl.program_id(1) == 0)
    def _():
        buf[0:HALO, :] = jnp.zeros((HALO, CONV_CH), F32)

    buf[HALO:HALO + ts, :] = z_ref[...]
    first = HALO - (CONV_WIDTH - 1)
    for c in range(ts // chunk):
        acc = jnp.zeros((chunk, CONV_CH), F32) + b_ref[...]
        for k in range(CONV_WIDTH):
            acc = acc + buf[first + k + c * chunk:first + k + (c + 1) * chunk, :] * w_ref[k:k + 1, :]
        mu = jnp.mean(acc, axis=-1, keepdims=True)
        d = acc - mu
        var = jnp.mean(d * d, axis=-1, keepdims=True)
        y = d * lax.rsqrt(var + EPS) * g_ref[...] + beta_ref[...]
        o_ref[c * chunk:(c + 1) * chunk, :] = (y * jax.nn.sigmoid(y)).astype(BF16)
    buf[0:HALO, :] = buf[ts:ts + HALO, :]


def _conformer_conv(glu, w, bias, g, beta, b, s):
    ts = min(TS_CONV, s)
    full = lambda a: pl.BlockSpec(a.shape, lambda bi, i: (0, 0))
    row = pl.BlockSpec((ts, CONV_CH), lambda bi, i: (bi * (s // ts) + i, 0))
    return pl.pallas_call(
        _conv_kernel,
        grid=(b, s // ts),
        in_specs=[row, full(w), full(bias), full(g), full(beta)],
        out_specs=row,
        out_shape=jax.ShapeDtypeStruct((b * s, CONV_CH), BF16),
        scratch_shapes=[pltpu.VMEM((HALO + ts, CONV_CH), F32)],
        compiler_params=_cparams(("arbitrary", "arbitrary")),
        name="conformer_conv",
    )(glu, w, bias, g, beta)


def _merge_router_kernel(x_ref, a_ref, c_ref, mg_ref, wa_ref, wc_ref, wo_ref, fg_ref, wr_ref, br_ref,
                         x1_o, h2_o, rt_o, cnt_o):
    tm = x_ref.shape[0]
    y_a = _dot(a_ref[...], wa_ref[...])
    y_c = _dot(c_ref[...], wc_ref[...])
    mix = mg_ref[:, 0:D_MODEL].astype(F32) * y_a + mg_ref[:, D_MODEL:2 * D_MODEL].astype(F32) * y_c
    x1 = x_ref[...] + _dot(mix.astype(BF16), wo_ref[...])
    x1_o[...] = x1
    h2 = x1 * lax.rsqrt(jnp.mean(x1 * x1, axis=-1, keepdims=True) + EPS) * fg_ref[...]
    h2_o[...] = h2

    hh, hl = _split(h2)
    logits = _dot(hh, wr_ref[0]) + _dot(hl, wr_ref[0]) + _dot(hh, wr_ref[1]) + br_ref[...]
    lane = lax.broadcasted_iota(I32, (tm, LANES), 1)
    is_g = (lane >= GROUP_LANE0) & (lane < GROUP_LANE0 + N_GROUPS)
    gmax = jnp.max(jnp.where(is_g, logits, -jnp.inf), axis=-1, keepdims=True)
    gsel = jnp.min(jnp.where(is_g & (logits == gmax), lane - GROUP_LANE0, N_GROUPS), axis=-1, keepdims=True)
    pg_sel = 1.0 / jnp.sum(jnp.where(is_g, jnp.exp(logits - gmax), 0.0), axis=-1, keepdims=True)
    in_grp = (lane < N_EXPERTS) & (lane // EXP_PER_GROUP == gsel)
    emax = jnp.max(jnp.where(in_grp, logits, -jnp.inf), axis=-1, keepdims=True)
    ee = jnp.where(in_grp, jnp.exp(logits - emax), 0.0)
    pe = jnp.where(in_grp, ee / jnp.sum(ee, axis=-1, keepdims=True), -1.0)
    p1 = jnp.max(pe, axis=-1, keepdims=True)
    i1 = jnp.min(jnp.where(pe == p1, lane, LANES), axis=-1, keepdims=True)
    pe2 = jnp.where(lane == i1, -1.0, pe)
    p2 = jnp.max(pe2, axis=-1, keepdims=True)
    i2 = jnp.min(jnp.where(pe2 == p2, lane, LANES), axis=-1, keepdims=True)
    w1 = pg_sel * p1 / (p1 + p2)
    w2 = pg_sel * p2 / (p1 + p2)

    hot1 = lane == i1
    hot2 = lane == i2
    ind = jnp.where(hot1 | hot2, 1.0, 0.0)
    tri = (lax.broadcasted_iota(I32, (tm, tm), 1) < lax.broadcasted_iota(I32, (tm, tm), 0)).astype(BF16)
    before = _dot(tri, ind.astype(BF16))
    k1 = jnp.sum(jnp.where(hot1, before, 0.0), axis=-1, keepdims=True)
    k2 = jnp.sum(jnp.where(hot2, before, 0.0), axis=-1, keepdims=True)
    rec = jnp.zeros((tm, LANES), F32)
    for slot, val in ((R_E1, i1.astype(F32)), (R_E2, i2.astype(F32)), (R_W1, w1), (R_W2, w2), (R_K1, k1), (R_K2, k2)):
        rec = jnp.where(lane == slot, val, rec)
    rt_o[...] = rec
    cnt_o[0] = jnp.broadcast_to(jnp.sum(ind, axis=0, keepdims=True), (SUBLANES, LANES))


def _merge_router(x2, attn, cact, mg, wa, wc, wo, fg, wr, br):
    t = x2.shape[0]
    tm = TR
    row = lambda n: pl.BlockSpec((tm, n), lambda i: (i, 0))
    full = lambda a: pl.BlockSpec(a.shape, lambda i: (0,) * a.ndim)
    return pl.pallas_call(
        _merge_router_kernel,
        grid=(t // tm,),
        in_specs=[row(D_MODEL), row(Q_COLS), row(CONV_CH), row(2 * D_MODEL), full(wa), full(wc), full(wo),
                  full(fg), full(wr), full(br)],
        out_specs=[row(D_MODEL), row(D_MODEL), row(LANES), pl.BlockSpec((1, SUBLANES, LANES), lambda i: (i, 0, 0))],
        out_shape=[jax.ShapeDtypeStruct((t, D_MODEL), F32), jax.ShapeDtypeStruct((t, D_MODEL), F32),
                   jax.ShapeDtypeStruct((t, LANES), F32), jax.ShapeDtypeStruct((t // tm, SUBLANES, LANES), F32)],
        compiler_params=_cparams(("parallel",)),
        name="merge_router",
    )(x2, attn, cact, mg, wa, wc, wo, fg, wr, br)


def _chunk_copies(dst8_ref, n_chunks, make_copy):
    base = pl.program_id(0) * NCH

    def issue(c, carry):
        far = pl.multiple_of(dst8_ref[base + c] * CHUNK, CHUNK)
        make_copy(pl.multiple_of(c * CHUNK, CHUNK), far).start()
        return carry

    lax.fori_loop(0, n_chunks, issue, 0)

    def drain():
        def wait_one(c, carry):
            make_copy(0, 0).wait()
            return carry
        lax.fori_loop(0, n_chunks, wait_one, 0)

    return drain


def _dispatch_kernel(dst8_ref, nct_ref, h_ref, rt_ref, lo_ref, zeros_ref, xg_ref, slot_o, xs, sem):
    del zeros_ref
    tr = h_ref.shape[0]
    rt = rt_ref[...]
    lane = lax.broadcasted_iota(I32, (tr, LANES), 1)
    lane_f = lane.astype(F32)
    lo_row = lo_ref[0, 0:1, :]
    slots = []
    for e_lane, k_lane in ((R_E1, R_K1), (R_E2, R_K2)):
        hot = lane_f == rt[:, e_lane:e_lane + 1]
        slots.append(jnp.sum(jnp.where(hot, lo_row, 0.0), axis=-1, keepdims=True) + rt[:, k_lane:k_lane + 1])
    rec = jnp.where(lane == 0, slots[0], jnp.where(lane == 1, slots[1], 0.0))
    slot_o[...] = rec
    rec_t = jnp.concatenate([rec[c * LANES:(c + 1) * LANES].T for c in range(tr // LANES)], axis=1)
    row_id = lax.broadcasted_iota(I32, (R_LOCAL, tr), 0).astype(F32)
    pick = (row_id == rec_t[0:1, :]) | (row_id == rec_t[1:2, :])
    xs[...] = _dot(jnp.where(pick, 1.0, 0.0).astype(BF16), h_ref[...].astype(BF16))

    def make_copy(near, far):
        return pltpu.make_async_copy(xs.at[pl.ds(near, CHUNK), :], xg_ref.at[pl.ds(far, CHUNK), :], sem)

    _chunk_copies(dst8_ref, nct_ref[pl.program_id(0)], make_copy)()


def _dispatch(dst8, nct, h2, route, lo_rows, p_rows):
    t = h2.shape[0]
    zeros = jnp.zeros((p_rows, D_MODEL), F32)
    return pl.pallas_call(
        _dispatch_kernel,
        grid_spec=pltpu.PrefetchScalarGridSpec(
            num_scalar_prefetch=2,
            grid=(t // TR,),
            in_specs=[pl.BlockSpec((TR, D_MODEL), lambda i, d, n: (i, 0)),
                      pl.BlockSpec((TR, LANES), lambda i, d, n: (i, 0)),
                      pl.BlockSpec((1, SUBLANES, LANES), lambda i, d, n: (i, 0, 0)),
                      pl.BlockSpec(memory_space=pl.ANY)],
            out_specs=[pl.BlockSpec(memory_space=pl.ANY),
                       pl.BlockSpec((TR, LANES), lambda i, d, n: (i, 0))],
            scratch_shapes=[pltpu.VMEM((R_LOCAL, D_MODEL), F32), pltpu.SemaphoreType.DMA(())],
        ),
        out_shape=[jax.ShapeDtypeStruct((p_rows, D_MODEL), F32), jax.ShapeDtypeStruct((t, LANES), F32)],
        input_output_aliases={5: 0},
        compiler_params=_cparams(("arbitrary",)),
        name="moe_dispatch",
    )(dst8, nct, h2, route, lo_rows, zeros)


def _expert_kernel(blk_e_ref, nused_ref, x_ref, wg_ref, wu_ref, wd_ref, y_ref):
    del blk_e_ref
    used = pl.program_id(0) < nused_ref[0]

    @pl.when(used)
    def _():
        xb = x_ref[...].astype(BF16)
        a = _dot(xb, wg_ref[0])
        u = _dot(xb, wu_ref[0])
        y_ref[...] = _dot((a * jax.nn.sigmoid(a) * u).astype(BF16), wd_ref[0])

    @pl.when(jnp.logical_not(used))
    def _():
        y_ref[...] = jnp.zeros(y_ref.shape, F32)


def _experts(blk_e, nused, xg, wg, wu, wd):
    p_rows = xg.shape[0]
    return pl.pallas_call(
        _expert_kernel,
        grid_spec=pltpu.PrefetchScalarGridSpec(
            num_scalar_prefetch=2,
            grid=(p_rows // BM,),
            in_specs=[pl.BlockSpec((BM, D_MODEL), lambda i, be, nu: (i, 0)),
                      pl.BlockSpec((1, D_MODEL, EXPERT_FF), lambda i, be, nu: (be[i], 0, 0)),
                      pl.BlockSpec((1, D_MODEL, EXPERT_FF), lambda i, be, nu: (be[i], 0, 0)),
                      pl.BlockSpec((1, EXPERT_FF, D_MODEL), lambda i, be, nu: (be[i], 0, 0))],
            out_specs=pl.BlockSpec((BM, D_MODEL), lambda i, be, nu: (i, 0)),
        ),
        out_shape=jax.ShapeDtypeStruct((p_rows, D_MODEL), F32),
        compiler_params=_cparams(("arbitrary",)),
        name="moe_experts",
    )(blk_e, nused, xg, wg, wu, wd)


def _combine_ple_kernel(dst8_ref, nct_ref, x1_ref, rt_ref, slot_ref, p_ref, pg_ref, wpg_ref, bpg_ref, wpp_ref,
                        yg_ref, o_ref, ys, sem):
    tr = x1_ref.shape[0]
    n_chunks = nct_ref[pl.program_id(0)]

    def make_copy(near, far):
        return pltpu.make_async_copy(yg_ref.at[pl.ds(far, CHUNK), :], ys.at[pl.ds(near, CHUNK), :], sem)

    drain = _chunk_copies(dst8_ref, n_chunks, make_copy)

    def clear(c, carry):
        ys[pl.ds(pl.multiple_of(c * CHUNK, CHUNK), CHUNK), :] = jnp.zeros((CHUNK, D_MODEL), F32)
        return carry

    lax.fori_loop(n_chunks, NCH, clear, 0)
    drain()

    rt = rt_ref[...]
    sl = slot_ref[...]
    col_id = lax.broadcasted_iota(I32, (tr, R_LOCAL), 1).astype(F32)
    yb = ys[...].astype(BF16)
    y0 = _dot(jnp.where(col_id == sl[:, 0:1], 1.0, 0.0).astype(BF16), yb)
    y1 = _dot(jnp.where(col_id == sl[:, 1:2], 1.0, 0.0).astype(BF16), yb)
    x2 = x1_ref[...] + rt[:, R_W1:R_W1 + 1] * y0 + rt[:, R_W2:R_W2 + 1] * y1
    h3 = (x2 * lax.rsqrt(jnp.mean(x2 * x2, axis=-1, keepdims=True) + EPS) * pg_ref[...]).astype(BF16)
    gate = jax.nn.sigmoid(_dot(h3, wpg_ref[...]) + bpg_ref[...])
    o_ref[...] = x2 + gate * _dot(p_ref[...].astype(BF16), wpp_ref[...])


def _combine_ple(dst8, nct, x1, route, slots, p2, pg, wpg, bpg, wpp, yg):
    t = x1.shape[0]
    row = lambda n: pl.BlockSpec((TR, n), lambda i, d, c: (i, 0))
    full = lambda a: pl.BlockSpec(a.shape, lambda i, d, c: (0,) * a.ndim)
    return pl.pallas_call(
        _combine_ple_kernel,
        grid_spec=pltpu.PrefetchScalarGridSpec(
            num_scalar_prefetch=2,
            grid=(t // TR,),
            in_specs=[row(D_MODEL), row(LANES), row(LANES), row(PLE_DIM), full(pg), full(wpg), full(bpg),
                      full(wpp), pl.BlockSpec(memory_space=pl.ANY)],
            out_specs=row(D_MODEL),
            scratch_shapes=[pltpu.VMEM((R_LOCAL, D_MODEL), F32), pltpu.SemaphoreType.DMA(())],
        ),
        out_shape=jax.ShapeDtypeStruct((t, D_MODEL), F32),
        compiler_params=_cparams(("arbitrary",)),
        name="moe_combine_ple",
    )(dst8, nct, x1, route, slots, p2, pg, wpg, bpg, wpp, yg)


def _seg_sum_matrix():
    seg = np.arange(LANES) // HEAD_DIM
    return jnp.asarray(seg[:, None] == seg[None, :], BF16)


def _overlap_t(nc_rows, n_blk):
    nc = np.arange(nc_rows)[None, :] * CMP_STRIDE
    sb = np.arange(n_blk)[:, None] * SEL_BLOCK
    ov = (nc < sb + SEL_BLOCK) & (nc + CMP_BLOCK > sb) & (np.arange(nc_rows)[None, :] < nc_rows - 1)
    return jnp.asarray(ov, BF16)


def _compress_weights(pos, w1, w2):
    half = CMP_BLOCK // 2
    pe = jnp.broadcast_to(pos[:, None, :], (CMP_BLOCK, N_KV, HEAD_DIM)).reshape(2, half * N_KV * HEAD_DIM)
    w1r = w1.reshape(2, half, HEAD_DIM, CMP_HIDDEN)
    eye = jnp.eye(N_KV, dtype=w1.dtype)
    wx = jnp.einsum('hldj,gk->hlgdkj', w1r, eye).reshape(2, half * N_KV * HEAD_DIM, N_KV * CMP_HIDDEN)
    w2x = jnp.einsum('jd,gk->gjkd', w2, eye).reshape(N_KV * CMP_HIDDEN, N_KV * HEAD_DIM)
    return pe, wx[0].astype(BF16), wx[1].astype(BF16), w2x.astype(BF16)


def _layer(x, p_i, prm):
    b, s, _ = x.shape
    t = b * s
    x2 = x.reshape(t, D_MODEL)
    bd = _seg_sum_matrix()

    w = prm["w_in"]
    kv = lambda j: w[:, OFF_KV + j * KV_COLS:OFF_KV + (j + 1) * KV_COLS]
    w_nat = jnp.concatenate([kv(0), kv(1), kv(2), kv(4), w[:, OFF_CONV:]], axis=1).astype(BF16)
    w_t = jnp.concatenate([w[:, :Q_COLS], kv(3), kv(5),
                           jnp.pad(w[:, OFF_GATE:OFF_CONV], ((0, 0), (0, LANES - 3 * N_HEADS)))], axis=1).T.astype(BF16)
    qg = jnp.broadcast_to(prm["q_norm_g"][:, None], (HEAD_DIM, TM_PROJ))
    kg = jnp.tile(prm["k_norm_g"], (1, LANES // HEAD_DIM))
    qt, kc_raw, vc_raw, ks, vst, kw, vwt, gtt, glu, mg = _in_proj(
        x2, prm["attn_norm_g"][None, :], w_nat, w_t, qg, kg, bd, prm["merge_b"][None, :], s)

    nrow = s // CMP_STRIDE
    cw = [_compress_weights(prm["cmp_pos"][j], prm["cmp_w1"][j], prm["cmp_w2"][j]) for j in range(2)]
    pe_x, wa, wb = (jnp.stack([cw[0][j], cw[1][j]]) for j in range(3))
    kc, vct = _compress(kc_raw.reshape(b, nrow, CMP_STRIDE * LANES), vc_raw.reshape(b, nrow, CMP_STRIDE * LANES),
                        pe_x, wa, wb, cw[0][3], cw[1][3].T, bd, kg)

    attn = _attention(qt, gtt, kc, vct, ks.reshape(b, s, 4 * LANES), vst, kw.reshape(b, s, 2 * LANES), vwt,
                      _overlap_t(nrow, s // SEL_BLOCK), b, s)

    cact = _conformer_conv(glu, prm["conv_w"], prm["conv_b"][None, :], prm["conv_norm_g"][None, :],
                           prm["conv_norm_b"][None, :], b, s)

    wr_full = jnp.zeros((D_MODEL, LANES), F32)
    wr_full = wr_full.at[:, :N_EXPERTS].set(prm["w_router_expert"])
    wr_full = wr_full.at[:, GROUP_LANE0:GROUP_LANE0 + N_GROUPS].set(prm["w_router_group"])
    wr_hi = wr_full.astype(BF16)
    wr = jnp.stack([wr_hi, (wr_full - wr_hi.astype(F32)).astype(BF16)])
    br = jnp.zeros((1, LANES), F32)
    br = br.at[0, :N_EXPERTS].set(prm["b_router_expert"])
    br = br.at[0, GROUP_LANE0:GROUP_LANE0 + N_GROUPS].set(prm["b_router_group"])
    x1, h2, route, cnt = _merge_router(
        x2, attn, cact, mg, prm["w_attn_out"].astype(BF16), prm["w_conv_out"].astype(BF16),
        prm["w_out"].astype(BF16), prm["ffn_norm_g"][None, :], wr, br)

    n_tiles = t // TR
    counts = cnt[:, 0, :N_EXPERTS].astype(I32)
    seg = (counts + CHUNK - 1) // CHUNK * CHUNK
    local_off = jnp.cumsum(seg, axis=1) - seg
    nct = (jnp.sum(seg, axis=1) // CHUNK).astype(I32)
    total = jnp.sum(seg, axis=0)
    padded = (total + BM - 1) // BM * BM
    pends = jnp.cumsum(padded)
    far_off = (pends - padded)[None, :] + jnp.cumsum(seg, axis=0) - seg
    chunk = jnp.arange(NCH, dtype=I32)[None, :, None]
    lo8 = (local_off // CHUNK)[:, None, :]
    in_seg = (chunk >= lo8) & (chunk < ((local_off + seg) // CHUNK)[:, None, :])
    dst8 = (jnp.sum(jnp.where(in_seg, (far_off // CHUNK)[:, None, :] - lo8, 0), axis=-1)
            + chunk[:, :, 0]).astype(I32).reshape(n_tiles * NCH)
    lo_rows = jnp.broadcast_to(jnp.pad(local_off.astype(F32), ((0, 0), (0, LANES - N_EXPERTS)))[:, None, :],
                               (n_tiles, SUBLANES, LANES))
    n_blk = -(-(2 * t + n_tiles * N_EXPERTS * (CHUNK - 1) + N_EXPERTS * (BM - 1)) // BM)
    blk_e = jnp.minimum(jnp.sum(jnp.arange(n_blk, dtype=I32)[:, None] * BM >= pends[None, :], axis=1),
                        N_EXPERTS - 1).astype(I32)
    nused = (pends[-1:] // BM).astype(I32)

    xg, slots = _dispatch(dst8, nct, h2, route, lo_rows, n_blk * BM)
    yg = _experts(blk_e, nused, xg, prm["w_exp_gate"].astype(BF16), prm["w_exp_up"].astype(BF16),
                  prm["w_exp_down"].astype(BF16))
    out = _combine_ple(dst8, nct, x1, route, slots, p_i.reshape(t, PLE_DIM), prm["ple_norm_g"][None, :],
                       prm["w_ple_gate"].astype(BF16), prm["b_ple_gate"][None, :], prm["w_ple_proj"].astype(BF16), yg)
    return out.reshape(b, s, D_MODEL)


def kernel(x, p, attn_norm_g, w_in, q_norm_g, k_norm_g, cmp_pos, cmp_w1, cmp_w2, w_attn_out, conv_w, conv_b, conv_norm_g, conv_norm_b, w_conv_out, merge_b, w_out, ffn_norm_g, w_router_group, b_router_group, w_router_expert, b_router_expert, w_exp_gate, w_exp_up, w_exp_down, ple_norm_g, w_ple_gate, b_ple_gate, w_ple_proj):
    stacked = dict(attn_norm_g=attn_norm_g, w_in=w_in, q_norm_g=q_norm_g, k_norm_g=k_norm_g, cmp_pos=cmp_pos,
                   cmp_w1=cmp_w1, cmp_w2=cmp_w2, w_attn_out=w_attn_out, conv_w=conv_w, conv_b=conv_b,
                   conv_norm_g=conv_norm_g, conv_norm_b=conv_norm_b, w_conv_out=w_conv_out, merge_b=merge_b,
                   w_out=w_out, ffn_norm_g=ffn_norm_g, w_router_group=w_router_group,
                   b_router_group=b_router_group, w_router_expert=w_router_expert,
                   b_router_expert=b_router_expert, w_exp_gate=w_exp_gate, w_exp_up=w_exp_up,
                   w_exp_down=w_exp_down, ple_norm_g=ple_norm_g, w_ple_gate=w_ple_gate, b_ple_gate=b_ple_gate,
                   w_ple_proj=w_ple_proj)
    for i in range(w_in.shape[0]):
        x = _layer(x, p[i], {k: v[i] for k, v in stacked.items()})
    return x
```

```python
import functools

import numpy as np
import jax
import jax.numpy as jnp
from jax import lax
from jax.experimental import pallas as pl
from jax.experimental.pallas import tpu as pltpu

F32 = jnp.float32
BF16 = jnp.bfloat16
I32 = jnp.int32

D_MODEL = 1024
PLE_DIM = 256
N_HEADS = 8
N_KV = 2
HPG = N_HEADS // N_KV
HEAD_DIM = 64
CMP_BLOCK = 32
CMP_STRIDE = 16
CMP_HIDDEN = 128
SEL_BLOCK = 64
N_SEL = 8
WINDOW = 512
CONV_CH = 512
CONV_WIDTH = 31
N_GROUPS = 4
EXP_PER_GROUP = 8
N_EXPERTS = N_GROUPS * EXP_PER_GROUP
EXPERT_FF = 512
EPS = 1e-6
NEG = -1e30
BIG = 1e4
Q_COLS = N_HEADS * HEAD_DIM
KV_COLS = N_KV * HEAD_DIM
OFF_KV = Q_COLS
OFF_GATE = OFF_KV + 6 * KV_COLS
OFF_CONV = OFF_GATE + 3 * N_HEADS
OFF_MERGE = OFF_CONV + 2 * CONV_CH

LANES = 128
SUBLANES = 8
VMEM_LIMIT = 56 * 1024 * 1024

TM_PROJ = 256
QB = 128
KT = 256
TS_CONV = 512
HALO = 32
TR = 512
BM = 256
CHUNK = SUBLANES
R_LOCAL = 2 * TR + N_EXPERTS * CHUNK
NCH = R_LOCAL // CHUNK

N_KC = 0
N_VC = N_KC + LANES
N_KS = N_VC + LANES
N_KW = N_KS + LANES
N_CONV = N_KW + LANES
N_MERGE = N_CONV + 2 * CONV_CH
T_Q = 0
T_VS = T_Q + Q_COLS
T_VW = T_VS + LANES
T_GATE = T_VW + LANES

POS_HI_LANE = HEAD_DIM
POS_LO_LANE = HEAD_DIM + 1
UNSELECTED = -(2.0 ** 40)

R_E1, R_E2, R_W1, R_W2, R_K1, R_K2 = 0, 1, 2, 3, 4, 5
GROUP_LANE0 = N_EXPERTS


def _cparams(sem, vmem=VMEM_LIMIT):
    return pltpu.CompilerParams(dimension_semantics=sem, vmem_limit_bytes=vmem)


def _dot(a, b):
    return jnp.dot(a, b, preferred_element_type=F32)


def _dot_nt(a, b):
    return lax.dot_general(a, b, (((1,), (1,)), ((), ())), preferred_element_type=F32)


def _split(x):
    hi = x.astype(BF16)
    lo = (x - hi.astype(F32)).astype(BF16)
    return hi, lo


def _seg_rmsnorm(z, bd):
    hi, lo = _split(z * z)
    ss = _dot(hi, bd) + _dot(lo, bd)
    return z * lax.rsqrt(ss * (1.0 / HEAD_DIM) + EPS)


def _with_pos_lanes(kn, pos):
    lane = lax.broadcasted_iota(I32, kn.shape, 1)
    hi = pos // SEL_BLOCK * SEL_BLOCK
    pos_lanes = jnp.where(lane == POS_HI_LANE, hi.astype(F32),
                          jnp.where(lane == POS_LO_LANE, (pos - hi).astype(F32), 0.0))
    lo_half = lane < HEAD_DIM
    return (jnp.where(lo_half, kn, pos_lanes).astype(BF16),
            jnp.where(lo_half, pltpu.roll(kn, HEAD_DIM, 1), pos_lanes).astype(BF16))


def _in_proj_kernel(x_ref, g_ref, w_ref, wt_ref, qg_ref, kg_ref, bd_ref, mb_ref,
                    qt_o, kc_o, vc_o, ks_o, vst_o, kw_o, vwt_o, gtt_o, glu_o, mg_o, *, seq_len):
    tm = x_ref.shape[0]
    x = x_ref[...]
    h = (x * lax.rsqrt(jnp.mean(x * x, axis=-1, keepdims=True) + EPS) * g_ref[...]).astype(BF16)
    bd = bd_ref[...]
    lane = lax.broadcasted_iota(I32, (tm, LANES), 1)

    def proj(a, b):
        return _dot(h, w_ref[:, a:b])

    zt = _dot_nt(wt_ref[...], h)
    for hd in range(N_HEADS):
        zh = zt[T_Q + hd * HEAD_DIM:T_Q + (hd + 1) * HEAD_DIM, :]
        ms = jnp.sum(zh * zh, axis=0, keepdims=True) * (1.0 / HEAD_DIM)
        qt_o[hd * HEAD_DIM:(hd + 1) * HEAD_DIM, :] = (
            zh * lax.rsqrt(ms + EPS) * qg_ref[...] * (HEAD_DIM ** -0.5)).astype(BF16)
    vst_o[0] = zt[T_VS:T_VS + LANES, :].astype(BF16)
    vwt_o[0] = zt[T_VW:T_VW + LANES, :].astype(BF16)
    gtt_o[...] = jax.nn.sigmoid(zt[T_GATE:T_GATE + LANES, :])

    kc_o[...] = proj(N_KC, N_KC + LANES)
    vc_o[...] = proj(N_VC, N_VC + LANES)

    pos = (pl.program_id(0) * tm) % seq_len + lax.broadcasted_iota(I32, (tm, LANES), 0)
    onehot = jnp.where(lane == pos // SEL_BLOCK, 1.0, 0.0).astype(BF16)

    def key_tiles(zk, gain):
        return _with_pos_lanes(_seg_rmsnorm(zk, bd) * gain, pos)

    k0, k1 = key_tiles(proj(N_KS, N_KS + LANES), kg_ref[1:2, :])
    ks_o[...] = jnp.concatenate([k0, onehot, k1, onehot], axis=1)
    k0, k1 = key_tiles(proj(N_KW, N_KW + LANES), kg_ref[2:3, :])
    kw_o[...] = jnp.concatenate([k0, k1], axis=1)
    za = proj(N_CONV, N_CONV + CONV_CH)
    zg = proj(N_CONV + CONV_CH, N_CONV + 2 * CONV_CH)
    glu_o[...] = za * jax.nn.sigmoid(zg)
    for c in range(2):
        zm = proj(N_MERGE + c * D_MODEL, N_MERGE + (c + 1) * D_MODEL)
        mg_o[:, c * D_MODEL:(c + 1) * D_MODEL] = jax.nn.sigmoid(
            zm + mb_ref[:, c * D_MODEL:(c + 1) * D_MODEL]).astype(BF16)


def _in_proj(x2, norm_g, w_nat, w_t, qg, kg, bd, mb, seq_len):
    t = x2.shape[0]
    tm = TM_PROJ
    assert tm == KT and seq_len % tm == 0 and seq_len // SEL_BLOCK <= HEAD_DIM
    row = lambda n: pl.BlockSpec((tm, n), lambda i: (i, 0))
    col = lambda n: pl.BlockSpec((n, tm), lambda i: (0, i))
    vt = pl.BlockSpec((1, LANES, tm), lambda i: (i, 0, 0))
    full = lambda a: pl.BlockSpec(a.shape, lambda i: (0,) * a.ndim)
    sds = jax.ShapeDtypeStruct
    return pl.pallas_call(
        functools.partial(_in_proj_kernel, seq_len=seq_len),
        grid=(t // tm,),
        in_specs=[row(D_MODEL), full(norm_g), full(w_nat), full(w_t), full(qg), full(kg), full(bd), full(mb)],
        out_specs=[col(Q_COLS), row(LANES), row(LANES), row(4 * LANES), vt, row(2 * LANES), vt, col(LANES),
                   row(CONV_CH), row(2 * D_MODEL)],
        out_shape=[sds((Q_COLS, t), BF16), sds((t, LANES), F32), sds((t, LANES), F32), sds((t, 4 * LANES), BF16),
                   sds((t // tm, LANES, tm), BF16), sds((t, 2 * LANES), BF16), sds((t // tm, LANES, tm), BF16),
                   sds((LANES, t), F32), sds((t, CONV_CH), F32), sds((t, 2 * D_MODEL), BF16)],
        compiler_params=_cparams(("parallel",)),
        name="in_proj",
    )(x2, norm_g, w_nat, w_t, qg, kg, bd, mb)


def _compress_kernel(ck_ref, cv_ref, pe_ref, wa_ref, wb_ref, w2k_ref, w2vt_ref, bd_ref, kg_ref, kc_o, vct_o):
    nrow = ck_ref.shape[1]

    def hidden(idx, c_ref):
        c = c_ref[0]
        ya = _dot((c + pe_ref[idx, 0:1, :]).astype(BF16), wa_ref[idx])
        yb = _dot((c + pe_ref[idx, 1:2, :]).astype(BF16), wb_ref[idx])
        return jax.nn.gelu(ya + pltpu.roll(yb, nrow - 1, 0)).astype(BF16)

    kn = _seg_rmsnorm(_dot(hidden(0, ck_ref), w2k_ref[...]), bd_ref[...]) * kg_ref[0:1, :]
    c_end = lax.broadcasted_iota(I32, (nrow, LANES), 0) * CMP_STRIDE + (CMP_BLOCK - 1)
    kc_o[0] = jnp.concatenate(_with_pos_lanes(kn, c_end), axis=1)
    vct_o[0] = _dot_nt(w2vt_ref[...], hidden(1, cv_ref)).astype(BF16)


def _compress(kc_raw, vc_raw, pe_x, wa, wb, w2k, w2vt, bd, kg):
    b, nrow, width = kc_raw.shape
    batch = lambda r, n: pl.BlockSpec((1, r, n), lambda i: (i, 0, 0))
    full = lambda a: pl.BlockSpec(a.shape, lambda i: (0,) * a.ndim)
    return pl.pallas_call(
        _compress_kernel,
        grid=(b,),
        in_specs=[batch(nrow, width), batch(nrow, width), full(pe_x), full(wa), full(wb), full(w2k), full(w2vt),
                  full(bd), full(kg)],
        out_specs=[batch(nrow, 2 * LANES), batch(LANES, nrow)],
        out_shape=[jax.ShapeDtypeStruct((b, nrow, 2 * LANES), BF16), jax.ShapeDtypeStruct((b, LANES, nrow), BF16)],
        compiler_params=_cparams(("parallel",)),
        name="compress",
    )(kc_raw, vc_raw, pe_x, wa, wb, w2k, w2vt, bd, kg)


def _safe_inv(l):
    pos = l > 0.0
    return jnp.where(pos, 1.0 / jnp.where(pos, l, 1.0), 0.0)


def _halve_rows(x, op):
    rows = x.shape[0]
    while rows > SUBLANES:
        rows //= 2
        x = op(x[:rows], x[rows:])
    return x


def _attn_kernel(qt_ref, gtt_ref, kc_ref, vct_ref, ks_ref, vst_ref, kw_ref, vwt_ref, ovt_ref, o_ref, tiles_ref):
    i = pl.program_id(1)
    q0 = i * QB
    nc_rows = kc_ref.shape[1]
    n_blk = ovt_ref.shape[0]
    pw = 2 * QB
    n_pairs = N_KV * HPG // 2
    rel = ((lax.broadcasted_iota(I32, (KT, pw), 1) & (QB - 1))
           - lax.broadcasted_iota(I32, (KT, pw), 0)).astype(F32)
    slope_rows = lax.broadcasted_iota(I32, (HEAD_DIM, QB), 0) < 2
    gtt = gtt_ref[...]

    cidx = lax.broadcasted_iota(I32, (nc_rows, pw), 0)
    c_end = cidx * CMP_STRIDE + (CMP_BLOCK - 1)
    cmask = (c_end <= q0 + (lax.broadcasted_iota(I32, (nc_rows, pw), 1) & (QB - 1))) & (cidx < nc_rows - 1)

    o_cmp, q_sel, q_win = [], [], []
    any_sel = jnp.zeros((n_blk, QB), F32)
    for g in range(N_KV):
        q_pos = []
        for r in range(HPG):
            hd = g * HPG + r
            slope = jnp.where(slope_rows, 2.0 ** -(hd + 1), 0.0).astype(BF16)
            q_pos.append(jnp.concatenate([qt_ref[hd * HEAD_DIM:(hd + 1) * HEAD_DIM, :], slope], axis=0))
        pairs = [jnp.concatenate(q_pos[2 * pr:2 * pr + 2], axis=1) for pr in range(HPG // 2)]

        p_sum = jnp.zeros((nc_rows, QB), F32)
        for pr in range(HPG // 2):
            s = jnp.where(cmask, _dot(kc_ref[0, :, g * LANES:(g + 1) * LANES], pairs[pr]), NEG)
            e = jnp.where(cmask, jnp.exp(s - jnp.max(_halve_rows(s, jnp.maximum), axis=0, keepdims=True)), 0.0)
            p = e * _safe_inv(jnp.sum(_halve_rows(e, jnp.add), axis=0, keepdims=True))
            o_cmp.append(_dot(vct_ref[0, g * HEAD_DIM:(g + 1) * HEAD_DIM, :], p.astype(BF16)))
            p_sum = p_sum + p[:, :QB] + p[:, QB:]
        ph, plo = _split(p_sum)
        imp_t = _dot(ovt_ref[...], ph) + _dot(ovt_ref[...], plo)

        blk = lax.broadcasted_iota(I32, (n_blk, QB), 0)
        tl = q0 + lax.broadcasted_iota(I32, (n_blk, QB), 1)
        valid = blk * SEL_BLOCK <= tl
        cur = tl // SEL_BLOCK
        forced = (blk == 0) | (blk == cur) | (blk == cur - 1)
        score = jnp.where(valid, imp_t + jnp.where(forced, BIG, 0.0), -BIG)
        sel = jnp.zeros((n_blk, QB), jnp.bool_)
        for _ in range(min(N_SEL, n_blk)):
            mx = jnp.max(score, axis=0, keepdims=True)
            first = jnp.min(jnp.where(score == mx, blk, n_blk), axis=0, keepdims=True)
            pick = blk == first
            sel = sel | pick
            score = jnp.where(pick, -jnp.inf, score)
        any_sel = any_sel + jnp.where(sel & valid, 1.0, 0.0)
        unsel = jnp.where(sel & valid, 0.0, UNSELECTED).astype(BF16)
        bias = jnp.concatenate([unsel, jnp.zeros((2 * HEAD_DIM - n_blk, QB), BF16)], axis=0)
        for pr in range(HPG // 2):
            q_win.append(pairs[pr])
            q_sel.append(jnp.concatenate([pairs[pr], jnp.concatenate([bias, bias], axis=1)], axis=0))

    def update(s, vt, m, l, acc):
        m_new = jnp.maximum(m, jnp.max(_halve_rows(s, jnp.maximum), axis=0, keepdims=True))
        p = jnp.exp(s - m_new)
        alpha = jnp.exp(m - m_new)
        l_new = alpha * l + jnp.sum(_halve_rows(p, jnp.add), axis=0, keepdims=True)
        return m_new, l_new, alpha * acc + _dot(vt, p.astype(BF16))

    def scores(j, q_list, k_ref, k_width):
        k0 = pl.multiple_of(j * KT, KT)
        return tuple(_dot(k_ref[0, pl.ds(k0, KT), (c // (HPG // 2)) * k_width:(c // (HPG // 2) + 1) * k_width],
                          q_list[c]) for c in range(n_pairs))

    def consume(j, s_list, state, vt_ref, mask_fn):
        keep = None if mask_fn is None else mask_fn(rel + (q0 - j * KT).astype(F32))
        out = []
        for c in range(n_pairs):
            g = c // (HPG // 2)
            s = s_list[c] if keep is None else jnp.where(keep, s_list[c], NEG)
            out += update(s, vt_ref[j, g * HEAD_DIM:(g + 1) * HEAD_DIM, :], *state[3 * c:3 * c + 3])
        return tuple(out)

    j_diag = (q0 + QB - 1) // KT
    init = (jnp.full((1, pw), NEG, F32), jnp.zeros((1, pw), F32), jnp.zeros((HEAD_DIM, pw), F32)) * n_pairs

    def branch(n_first, n_last, tile_of, q_list, k_ref, vt_ref, k_width, body_mask, last_mask):
        def body(n, carry):
            s_next = scores(tile_of(n + 1), q_list, k_ref, k_width)
            return s_next + consume(tile_of(n), carry[:n_pairs], carry[n_pairs:], vt_ref, body_mask)

        carry = lax.fori_loop(n_first, n_last, body, scores(tile_of(n_first), q_list, k_ref, k_width) + init)
        return consume(tile_of(n_last), carry[:n_pairs], carry[n_pairs:], vt_ref, last_mask)

    per_blk = jnp.sum(any_sel, axis=1, keepdims=True)
    n_act = jnp.int32(0)
    for jt in range(n_blk * SEL_BLOCK // KT):
        blocks = slice(jt * (KT // SEL_BLOCK), (jt + 1) * (KT // SEL_BLOCK))
        tiles_ref[n_act] = jnp.int32(jt)
        n_act = n_act + ((jnp.sum(per_blk[blocks]) > 0.0) & (jt < j_diag)).astype(I32)
    tiles_ref[n_act] = j_diag

    causal = lambda d: d >= 0.0
    in_window = lambda d: (d >= 0.0) & (d < float(WINDOW))
    sel_out = branch(0, n_act, lambda n: tiles_ref[n], q_sel, ks_ref, vst_ref, 2 * LANES, None, causal)
    win_out = branch(jnp.maximum(q0 - (WINDOW - 1), 0) // KT, j_diag, lambda n: n, q_win, kw_ref, vwt_ref, LANES,
                     in_window, in_window)

    for c in range(n_pairs):
        o_s = sel_out[3 * c + 2] / sel_out[3 * c + 1]
        o_w = win_out[3 * c + 2] / win_out[3 * c + 1]
        halves = []
        for k in range(2):
            g0 = 3 * (2 * c + k)
            lanes = slice(k * QB, (k + 1) * QB)
            halves.append(gtt[g0:g0 + 1, :] * o_cmp[c][:, lanes] + gtt[g0 + 1:g0 + 2, :] * o_s[:, lanes]
                          + gtt[g0 + 2:g0 + 3, :] * o_w[:, lanes])
        o_ref[:, c * LANES:(c + 1) * LANES] = jnp.concatenate(halves, axis=0).T.astype(BF16)


def _attention(qt, gtt, kc, vct, ks, vst, kw, vwt, ovt, b, s):
    n_q = s // QB
    per_q = lambda n: pl.BlockSpec((n, QB), lambda bi, i: (0, bi * n_q + i))
    per_b = lambda a: pl.BlockSpec((1,) + a.shape[1:], lambda bi, i: (bi, 0, 0))
    vt = pl.BlockSpec((s // KT, LANES, KT), lambda bi, i: (bi, 0, 0))
    return pl.pallas_call(
        _attn_kernel,
        grid=(b, n_q),
        in_specs=[per_q(Q_COLS), per_q(LANES), per_b(kc), per_b(vct), per_b(ks), vt, per_b(kw), vt,
                  pl.BlockSpec(ovt.shape, lambda bi, i: (0, 0))],
        out_specs=pl.BlockSpec((QB, Q_COLS), lambda bi, i: (bi * n_q + i, 0)),
        out_shape=jax.ShapeDtypeStruct((b * s, Q_COLS), BF16),
        scratch_shapes=[pltpu.SMEM((s // KT + 1,), I32)],
        compiler_params=_cparams(("parallel", "parallel")),
        name="nsa_attention",
    )(qt, gtt, kc, vct, ks, vst, kw, vwt, ovt)


def _conv_kernel(z_ref, w_ref, b_ref, g_ref, beta_ref, o_ref, buf):
    ts = z_ref.shape[0]
    chunk = 64

    @pl.when(pl.program_id(1) == 0)
    def _():
        buf[0:HALO, :] = jnp.zeros((HALO, CONV_CH), F32)

    buf[HALO:HALO + ts, :] = z_ref[...]
    first = HALO - (CONV_WIDTH - 1)
    for c in range(ts // chunk):
        acc = jnp.zeros((chunk, CONV_CH), F32) + b_ref[...]
        for k in range(CONV_WIDTH):
            acc = acc + buf[first + k + c * chunk:first + k + (c + 1) * chunk, :] * w_ref[k:k + 1, :]
        mu = jnp.mean(acc, axis=-1, keepdims=True)
        d = acc - mu
        var = jnp.mean(d * d, axis=-1, keepdims=True)
        y = d * lax.rsqrt(var + EPS) * g_ref[...] + beta_ref[...]
        o_ref[c * chunk:(c + 1) * chunk, :] = (y * jax.nn.sigmoid(y)).astype(BF16)
    buf[0:HALO, :] = buf[ts:ts + HALO, :]


def _conformer_conv(glu, w, bias, g, beta, b, s):
    ts = min(TS_CONV, s)
    full = lambda a: pl.BlockSpec(a.shape, lambda bi, i: (0, 0))
    row = pl.BlockSpec((ts, CONV_CH), lambda bi, i: (bi * (s // ts) + i, 0))
    return pl.pallas_call(
        _conv_kernel,
        grid=(b, s // ts),
        in_specs=[row, full(w), full(bias), full(g), full(beta)],
        out_specs=row,
        out_shape=jax.ShapeDtypeStruct((b * s, CONV_CH), BF16),
        scratch_shapes=[pltpu.VMEM((HALO + ts, CONV_CH), F32)],
        compiler_params=_cparams(("arbitrary", "arbitrary")),
        name="conformer_conv",
    )(glu, w, bias, g, beta)


def _merge_router_kernel(x_ref, a_ref, c_ref, mg_ref, wa_ref, wc_ref, wo_ref, fg_ref, wr_ref, br_ref,
                         x1_o, h2_o, rt_o, cnt_o):
    tm = x_ref.shape[0]
    y_a = _dot(a_ref[...], wa_ref[...])
    y_c = _dot(c_ref[...], wc_ref[...])
    mix = mg_ref[:, 0:D_MODEL].astype(F32) * y_a + mg_ref[:, D_MODEL:2 * D_MODEL].astype(F32) * y_c
    x1 = x_ref[...] + _dot(mix.astype(BF16), wo_ref[...])
    x1_o[...] = x1
    h2 = x1 * lax.rsqrt(jnp.mean(x1 * x1, axis=-1, keepdims=True) + EPS) * fg_ref[...]
    h2_o[...] = h2

    hh, hl = _split(h2)
    logits = _dot(hh, wr_ref[0]) + _dot(hl, wr_ref[0]) + _dot(hh, wr_ref[1]) + br_ref[...]
    lane = lax.broadcasted_iota(I32, (tm, LANES), 1)
    is_g = (lane >= GROUP_LANE0) & (lane < GROUP_LANE0 + N_GROUPS)
    gmax = jnp.max(jnp.where(is_g, logits, -jnp.inf), axis=-1, keepdims=True)
    gsel = jnp.min(jnp.where(is_g & (logits == gmax), lane - GROUP_LANE0, N_GROUPS), axis=-1, keepdims=True)
    pg_sel = 1.0 / jnp.sum(jnp.where(is_g, jnp.exp(logits - gmax), 0.0), axis=-1, keepdims=True)
    in_grp = (lane < N_EXPERTS) & (lane // EXP_PER_GROUP == gsel)
    emax = jnp.max(jnp.where(in_grp, logits, -jnp.inf), axis=-1, keepdims=True)
    ee = jnp.where(in_grp, jnp.exp(logits - emax), 0.0)
    pe = jnp.where(in_grp, ee / jnp.sum(ee, axis=-1, keepdims=True), -1.0)
    p1 = jnp.max(pe, axis=-1, keepdims=True)
    i1 = jnp.min(jnp.where(pe == p1, lane, LANES), axis=-1, keepdims=True)
    pe2 = jnp.where(lane == i1, -1.0, pe)
    p2 = jnp.max(pe2, axis=-1, keepdims=True)
    i2 = jnp.min(jnp.where(pe2 == p2, lane, LANES), axis=-1, keepdims=True)
    w1 = pg_sel * p1 / (p1 + p2)
    w2 = pg_sel * p2 / (p1 + p2)

    hot1 = lane == i1
    hot2 = lane == i2
    ind = jnp.where(hot1 | hot2, 1.0, 0.0)
    tri = (lax.broadcasted_iota(I32, (tm, tm), 1) < lax.broadcasted_iota(I32, (tm, tm), 0)).astype(BF16)
    before = _dot(tri, ind.astype(BF16))
    k1 = jnp.sum(jnp.where(hot1, before, 0.0), axis=-1, keepdims=True)
    k2 = jnp.sum(jnp.where(hot2, before, 0.0), axis=-1, keepdims=True)
    rec = jnp.zeros((tm, LANES), F32)
    for slot, val in ((R_E1, i1.astype(F32)), (R_E2, i2.astype(F32)), (R_W1, w1), (R_W2, w2), (R_K1, k1), (R_K2, k2)):
        rec = jnp.where(lane == slot, val, rec)
    rt_o[...] = rec
    cnt_o[0] = jnp.broadcast_to(jnp.sum(ind, axis=0, keepdims=True), (SUBLANES, LANES))


def _merge_router(x2, attn, cact, mg, wa, wc, wo, fg, wr, br):
    t = x2.shape[0]
    tm = TR
    row = lambda n: pl.BlockSpec((tm, n), lambda i: (i, 0))
    full = lambda a: pl.BlockSpec(a.shape, lambda i: (0,) * a.ndim)
    return pl.pallas_call(
        _merge_router_kernel,
        grid=(t // tm,),
        in_specs=[row(D_MODEL), row(Q_COLS), row(CONV_CH), row(2 * D_MODEL), full(wa), full(wc), full(wo),
                  full(fg), full(wr), full(br)],
        out_specs=[row(D_MODEL), row(D_MODEL), row(LANES), pl.BlockSpec((1, SUBLANES, LANES), lambda i: (i, 0, 0))],
        out_shape=[jax.ShapeDtypeStruct((t, D_MODEL), F32), jax.ShapeDtypeStruct((t, D_MODEL), F32),
                   jax.ShapeDtypeStruct((t, LANES), F32), jax.ShapeDtypeStruct((t // tm, SUBLANES, LANES), F32)],
        compiler_params=_cparams(("parallel",)),
        name="merge_router",
    )(x2, attn, cact, mg, wa, wc, wo, fg, wr, br)


def _chunk_copies(dst8_ref, n_chunks, make_copy):
    base = pl.program_id(0) * NCH

    def issue(c, carry):
        far = pl.multiple_of(dst8_ref[base + c] * CHUNK, CHUNK)
        make_copy(pl.multiple_of(c * CHUNK, CHUNK), far).start()
        return carry

    lax.fori_loop(0, n_chunks, issue, 0)

    def drain():
        def wait_one(c, carry):
            make_copy(0, 0).wait()
            return carry
        lax.fori_loop(0, n_chunks, wait_one, 0)

    return drain


def _dispatch_kernel(dst8_ref, nct_ref, pad8_ref, npad_ref, nused_ref, h_ref, rt_ref, lo_ref, xg_ref, slot_o, xs,
                     zbuf, sem, zsem):
    tr = h_ref.shape[0]

    @pl.when(pl.program_id(0) == 0)
    def _():
        zbuf[...] = jnp.zeros(zbuf.shape, F32)

        def zero_chunk(far):
            return pltpu.make_async_copy(zbuf.at[pl.ds(0, CHUNK), :], xg_ref.at[pl.ds(far, CHUNK), :], zsem)

        def zero_block(blk):
            return pltpu.make_async_copy(zbuf, xg_ref.at[pl.ds(pl.multiple_of(blk * BM, BM), BM), :], sem)

        def per_expert(e, carry, wait):
            def one(c, inner):
                if wait:
                    zero_chunk(0).wait()
                else:
                    zero_chunk(pl.multiple_of((pad8_ref[e] + c) * CHUNK, CHUNK)).start()
                return inner
            return lax.fori_loop(0, npad_ref[e], one, carry)

        def tail(blk, carry, wait):
            if wait:
                zero_block(0).wait()
            else:
                zero_block(blk).start()
            return carry

        n_blocks = xg_ref.shape[0] // BM
        lax.fori_loop(0, N_EXPERTS, functools.partial(per_expert, wait=False), 0)
        lax.fori_loop(nused_ref[0], n_blocks, functools.partial(tail, wait=False), 0)
        lax.fori_loop(0, N_EXPERTS, functools.partial(per_expert, wait=True), 0)
        lax.fori_loop(nused_ref[0], n_blocks, functools.partial(tail, wait=True), 0)

    rt = rt_ref[...]
    lane = lax.broadcasted_iota(I32, (tr, LANES), 1)
    lane_f = lane.astype(F32)
    lo_row = lo_ref[0, 0:1, :]
    slots = []
    for e_lane, k_lane in ((R_E1, R_K1), (R_E2, R_K2)):
        hot = lane_f == rt[:, e_lane:e_lane + 1]
        slots.append(jnp.sum(jnp.where(hot, lo_row, 0.0), axis=-1, keepdims=True) + rt[:, k_lane:k_lane + 1])
    rec = jnp.where(lane == 0, slots[0], jnp.where(lane == 1, slots[1], 0.0))
    slot_o[...] = rec
    rec_t = jnp.concatenate([rec[c * LANES:(c + 1) * LANES].T for c in range(tr // LANES)], axis=1)
    row_id = lax.broadcasted_iota(I32, (R_LOCAL, tr), 0).astype(F32)
    pick = (row_id == rec_t[0:1, :]) | (row_id == rec_t[1:2, :])
    xs[...] = _dot(jnp.where(pick, 1.0, 0.0).astype(BF16), h_ref[...].astype(BF16))

    def make_copy(near, far):
        return pltpu.make_async_copy(xs.at[pl.ds(near, CHUNK), :], xg_ref.at[pl.ds(far, CHUNK), :], sem)

    _chunk_copies(dst8_ref, nct_ref[pl.program_id(0)], make_copy)()


def _dispatch(dst8, nct, pad8, npad, nused, h2, route, lo_rows, p_rows):
    t = h2.shape[0]
    return pl.pallas_call(
        _dispatch_kernel,
        grid_spec=pltpu.PrefetchScalarGridSpec(
            num_scalar_prefetch=5,
            grid=(t // TR,),
            in_specs=[pl.BlockSpec((TR, D_MODEL), lambda i, *_: (i, 0)),
                      pl.BlockSpec((TR, LANES), lambda i, *_: (i, 0)),
                      pl.BlockSpec((1, SUBLANES, LANES), lambda i, *_: (i, 0, 0))],
            out_specs=[pl.BlockSpec(memory_space=pl.ANY),
                       pl.BlockSpec((TR, LANES), lambda i, *_: (i, 0))],
            scratch_shapes=[pltpu.VMEM((R_LOCAL, D_MODEL), F32), pltpu.VMEM((BM, D_MODEL), F32),
                            pltpu.SemaphoreType.DMA(()), pltpu.SemaphoreType.DMA(())],
        ),
        out_shape=[jax.ShapeDtypeStruct((p_rows, D_MODEL), F32), jax.ShapeDtypeStruct((t, LANES), F32)],
        compiler_params=_cparams(("arbitrary",)),
        name="moe_dispatch",
    )(dst8, nct, pad8, npad, nused, h2, route, lo_rows)


def _expert_kernel(blk_e_ref, nused_ref, x_ref, wg_ref, wu_ref, wd_ref, y_ref, wg_b, wu_b, wd_b):
    i = pl.program_id(0)
    used = i < nused_ref[0]

    @pl.when((i == 0) | (blk_e_ref[i] != blk_e_ref[jnp.maximum(i - 1, 0)]))
    def _():
        wg_b[...] = wg_ref[0].astype(BF16)
        wu_b[...] = wu_ref[0].astype(BF16)
        wd_b[...] = wd_ref[0].astype(BF16)

    @pl.when(used)
    def _():
        xb = x_ref[...].astype(BF16)
        a = _dot(xb, wg_b[...])
        u = _dot(xb, wu_b[...])
        y_ref[...] = _dot((a * jax.nn.sigmoid(a) * u).astype(BF16), wd_b[...])

    @pl.when(jnp.logical_not(used))
    def _():
        y_ref[...] = jnp.zeros(y_ref.shape, F32)


def _experts(blk_e, nused, xg, wg, wu, wd):
    p_rows = xg.shape[0]
    x_blk = lambda i, be, nu: (jnp.minimum(i, nu[0] - 1), 0)
    return pl.pallas_call(
        _expert_kernel,
        grid_spec=pltpu.PrefetchScalarGridSpec(
            num_scalar_prefetch=2,
            grid=(p_rows // BM,),
            in_specs=[pl.BlockSpec((BM, D_MODEL), x_blk),
                      pl.BlockSpec((1, D_MODEL, EXPERT_FF), lambda i, be, nu: (be[i], 0, 0)),
                      pl.BlockSpec((1, D_MODEL, EXPERT_FF), lambda i, be, nu: (be[i], 0, 0)),
                      pl.BlockSpec((1, EXPERT_FF, D_MODEL), lambda i, be, nu: (be[i], 0, 0))],
            out_specs=pl.BlockSpec((BM, D_MODEL), lambda i, be, nu: (i, 0)),
            scratch_shapes=[pltpu.VMEM((D_MODEL, EXPERT_FF), BF16), pltpu.VMEM((D_MODEL, EXPERT_FF), BF16),
                            pltpu.VMEM((EXPERT_FF, D_MODEL), BF16)],
        ),
        out_shape=jax.ShapeDtypeStruct((p_rows, D_MODEL), F32),
        compiler_params=_cparams(("arbitrary",)),
        name="moe_experts",
    )(blk_e, nused, xg, wg, wu, wd)


def _combine_ple_kernel(dst8_ref, nct_ref, x1_ref, rt_ref, slot_ref, p_ref, pg_ref, wpg_ref, bpg_ref, wpp_ref,
                        yg_ref, o_ref, ys, sem):
    tr = x1_ref.shape[0]
    n_chunks = nct_ref[pl.program_id(0)]

    def make_copy(near, far):
        return pltpu.make_async_copy(yg_ref.at[pl.ds(far, CHUNK), :], ys.at[pl.ds(near, CHUNK), :], sem)

    drain = _chunk_copies(dst8_ref, n_chunks, make_copy)

    def clear(c, carry):
        ys[pl.ds(pl.multiple_of(c * CHUNK, CHUNK), CHUNK), :] = jnp.zeros((CHUNK, D_MODEL), F32)
        return carry

    lax.fori_loop(n_chunks, NCH, clear, 0)
    drain()

    rt = rt_ref[...]
    sl = slot_ref[...]
    col_id = lax.broadcasted_iota(I32, (tr, R_LOCAL), 1).astype(F32)
    yb = ys[...].astype(BF16)
    y0 = _dot(jnp.where(col_id == sl[:, 0:1], 1.0, 0.0).astype(BF16), yb)
    y1 = _dot(jnp.where(col_id == sl[:, 1:2], 1.0, 0.0).astype(BF16), yb)
    x2 = x1_ref[...] + rt[:, R_W1:R_W1 + 1] * y0 + rt[:, R_W2:R_W2 + 1] * y1
    h3 = (x2 * lax.rsqrt(jnp.mean(x2 * x2, axis=-1, keepdims=True) + EPS) * pg_ref[...]).astype(BF16)
    gate = jax.nn.sigmoid(_dot(h3, wpg_ref[...]) + bpg_ref[...])
    o_ref[...] = x2 + gate * _dot(p_ref[...].astype(BF16), wpp_ref[...])


def _combine_ple(dst8, nct, x1, route, slots, p2, pg, wpg, bpg, wpp, yg):
    t = x1.shape[0]
    row = lambda n: pl.BlockSpec((TR, n), lambda i, d, c: (i, 0))
    full = lambda a: pl.BlockSpec(a.shape, lambda i, d, c: (0,) * a.ndim)
    return pl.pallas_call(
        _combine_ple_kernel,
        grid_spec=pltpu.PrefetchScalarGridSpec(
            num_scalar_prefetch=2,
            grid=(t // TR,),
            in_specs=[row(D_MODEL), row(LANES), row(LANES), row(PLE_DIM), full(pg), full(wpg), full(bpg),
                      full(wpp), pl.BlockSpec(memory_space=pl.ANY)],
            out_specs=row(D_MODEL),
            scratch_shapes=[pltpu.VMEM((R_LOCAL, D_MODEL), F32), pltpu.SemaphoreType.DMA(())],
        ),
        out_shape=jax.ShapeDtypeStruct((t, D_MODEL), F32),
        compiler_params=_cparams(("arbitrary",)),
        name="moe_combine_ple",
    )(dst8, nct, x1, route, slots, p2, pg, wpg, bpg, wpp, yg)


def _seg_sum_matrix():
    seg = np.arange(LANES) // HEAD_DIM
    return jnp.asarray(seg[:, None] == seg[None, :], BF16)


def _overlap_t(nc_rows, n_blk):
    nc = np.arange(nc_rows)[None, :] * CMP_STRIDE
    sb = np.arange(n_blk)[:, None] * SEL_BLOCK
    ov = (nc < sb + SEL_BLOCK) & (nc + CMP_BLOCK > sb) & (np.arange(nc_rows)[None, :] < nc_rows - 1)
    return jnp.asarray(ov, BF16)


def _compress_weights(pos, w1, w2):
    half = CMP_BLOCK // 2
    pe = jnp.broadcast_to(pos[:, None, :], (CMP_BLOCK, N_KV, HEAD_DIM)).reshape(2, half * N_KV * HEAD_DIM)
    w1r = w1.reshape(2, half, HEAD_DIM, CMP_HIDDEN)
    eye = jnp.eye(N_KV, dtype=w1.dtype)
    wx = jnp.einsum('hldj,gk->hlgdkj', w1r, eye).reshape(2, half * N_KV * HEAD_DIM, N_KV * CMP_HIDDEN)
    w2x = jnp.einsum('jd,gk->gjkd', w2, eye).reshape(N_KV * CMP_HIDDEN, N_KV * HEAD_DIM)
    return pe, wx[0].astype(BF16), wx[1].astype(BF16), w2x.astype(BF16)


def _layer(x, p_i, prm):
    b, s, _ = x.shape
    t = b * s
    x2 = x.reshape(t, D_MODEL)
    bd = _seg_sum_matrix()

    w = prm["w_in"]
    kv = lambda j: w[:, OFF_KV + j * KV_COLS:OFF_KV + (j + 1) * KV_COLS]
    w_nat = jnp.concatenate([kv(0), kv(1), kv(2), kv(4), w[:, OFF_CONV:]], axis=1).astype(BF16)
    w_t = jnp.concatenate([w[:, :Q_COLS], kv(3), kv(5),
                           jnp.pad(w[:, OFF_GATE:OFF_CONV], ((0, 0), (0, LANES - 3 * N_HEADS)))], axis=1).T.astype(BF16)
    qg = jnp.broadcast_to(prm["q_norm_g"][:, None], (HEAD_DIM, TM_PROJ))
    kg = jnp.tile(prm["k_norm_g"], (1, LANES // HEAD_DIM))
    qt, kc_raw, vc_raw, ks, vst, kw, vwt, gtt, glu, mg = _in_proj(
        x2, prm["attn_norm_g"][None, :], w_nat, w_t, qg, kg, bd, prm["merge_b"][None, :], s)

    nrow = s // CMP_STRIDE
    cw = [_compress_weights(prm["cmp_pos"][j], prm["cmp_w1"][j], prm["cmp_w2"][j]) for j in range(2)]
    pe_x, wa, wb = (jnp.stack([cw[0][j], cw[1][j]]) for j in range(3))
    kc, vct = _compress(kc_raw.reshape(b, nrow, CMP_STRIDE * LANES), vc_raw.reshape(b, nrow, CMP_STRIDE * LANES),
                        pe_x, wa, wb, cw[0][3], cw[1][3].T, bd, kg)

    attn = _attention(qt, gtt, kc, vct, ks.reshape(b, s, 4 * LANES), vst, kw.reshape(b, s, 2 * LANES), vwt,
                      _overlap_t(nrow, s // SEL_BLOCK), b, s)

    cact = _conformer_conv(glu, prm["conv_w"], prm["conv_b"][None, :], prm["conv_norm_g"][None, :],
                           prm["conv_norm_b"][None, :], b, s)

    wr_full = jnp.zeros((D_MODEL, LANES), F32)
    wr_full = wr_full.at[:, :N_EXPERTS].set(prm["w_router_expert"])
    wr_full = wr_full.at[:, GROUP_LANE0:GROUP_LANE0 + N_GROUPS].set(prm["w_router_group"])
    wr_hi = wr_full.astype(BF16)
    wr = jnp.stack([wr_hi, (wr_full - wr_hi.astype(F32)).astype(BF16)])
    br = jnp.zeros((1, LANES), F32)
    br = br.at[0, :N_EXPERTS].set(prm["b_router_expert"])
    br = br.at[0, GROUP_LANE0:GROUP_LANE0 + N_GROUPS].set(prm["b_router_group"])
    x1, h2, route, cnt = _merge_router(
        x2, attn, cact, mg, prm["w_attn_out"].astype(BF16), prm["w_conv_out"].astype(BF16),
        prm["w_out"].astype(BF16), prm["ffn_norm_g"][None, :], wr, br)

    n_tiles = t // TR
    counts = cnt[:, 0, :N_EXPERTS].astype(I32)
    seg = (counts + CHUNK - 1) // CHUNK * CHUNK
    local_off = jnp.cumsum(seg, axis=1) - seg
    nct = (jnp.sum(seg, axis=1) // CHUNK).astype(I32)
    total = jnp.sum(seg, axis=0)
    padded = (total + BM - 1) // BM * BM
    pends = jnp.cumsum(padded)
    far_off = (pends - padded)[None, :] + jnp.cumsum(seg, axis=0) - seg
    chunk = jnp.arange(NCH, dtype=I32)[None, :, None]
    lo8 = (local_off // CHUNK)[:, None, :]
    in_seg = (chunk >= lo8) & (chunk < ((local_off + seg) // CHUNK)[:, None, :])
    dst8 = (jnp.sum(jnp.where(in_seg, (far_off // CHUNK)[:, None, :] - lo8, 0), axis=-1)
            + chunk[:, :, 0]).astype(I32).reshape(n_tiles * NCH)
    lo_rows = jnp.broadcast_to(jnp.pad(local_off.astype(F32), ((0, 0), (0, LANES - N_EXPERTS)))[:, None, :],
                               (n_tiles, SUBLANES, LANES))
    n_blk = -(-(2 * t + n_tiles * N_EXPERTS * (CHUNK - 1) + N_EXPERTS * (BM - 1)) // BM)
    blk_e = jnp.minimum(jnp.sum(jnp.arange(n_blk, dtype=I32)[:, None] * BM >= pends[None, :], axis=1),
                        N_EXPERTS - 1).astype(I32)
    nused = (pends[-1:] // BM).astype(I32)
    pad8 = ((pends - padded + total) // CHUNK).astype(I32)
    npad = ((padded - total) // CHUNK).astype(I32)

    xg, slots = _dispatch(dst8, nct, pad8, npad, nused, h2, route, lo_rows, n_blk * BM)
    yg = _experts(blk_e, nused, xg, prm["w_exp_gate"], prm["w_exp_up"], prm["w_exp_down"])
    out = _combine_ple(dst8, nct, x1, route, slots, p_i.reshape(t, PLE_DIM), prm["ple_norm_g"][None, :],
                       prm["w_ple_gate"].astype(BF16), prm["b_ple_gate"][None, :], prm["w_ple_proj"].astype(BF16), yg)
    return out.reshape(b, s, D_MODEL)


def kernel(x, p, attn_norm_g, w_in, q_norm_g, k_norm_g, cmp_pos, cmp_w1, cmp_w2, w_attn_out, conv_w, conv_b, conv_norm_g, conv_norm_b, w_conv_out, merge_b, w_out, ffn_norm_g, w_router_group, b_router_group, w_router_expert, b_router_expert, w_exp_gate, w_exp_up, w_exp_down, ple_norm_g, w_ple_gate, b_ple_gate, w_ple_proj):
    stacked = dict(attn_norm_g=attn_norm_g, w_in=w_in, q_norm_g=q_norm_g, k_norm_g=k_norm_g, cmp_pos=cmp_pos,
                   cmp_w1=cmp_w1, cmp_w2=cmp_w2, w_attn_out=w_attn_out, conv_w=conv_w, conv_b=conv_b,
                   conv_norm_g=conv_norm_g, conv_norm_b=conv_norm_b, w_conv_out=w_conv_out, merge_b=merge_b,
                   w_out=w_out, ffn_norm_g=ffn_norm_g, w_router_group=w_router_group,
                   b_router_group=b_router_group, w_router_expert=w_router_expert,
                   b_router_expert=b_router_expert, w_exp_gate=w_exp_gate, w_exp_up=w_exp_up,
                   w_exp_down=w_exp_down, ple_norm_g=ple_norm_g, w_ple_gate=w_ple_gate, b_ple_gate=b_ple_gate,
                   w_ple_proj=w_ple_proj)
    for i in range(w_in.shape[0]):
        x = _layer(x, p[i], {k: v[i] for k, v in stacked.items()})
    return x
```

```python
import functools

import numpy as np
import jax
import jax.numpy as jnp
from jax import lax
from jax.experimental import pallas as pl
from jax.experimental.pallas import tpu as pltpu

F32 = jnp.float32
BF16 = jnp.bfloat16
I32 = jnp.int32

D_MODEL = 1024
PLE_DIM = 256
N_HEADS = 8
N_KV = 2
HPG = N_HEADS // N_KV
HEAD_DIM = 64
CMP_BLOCK = 32
CMP_STRIDE = 16
CMP_HIDDEN = 128
SEL_BLOCK = 64
N_SEL = 8
WINDOW = 512
CONV_CH = 512
CONV_WIDTH = 31
N_GROUPS = 4
EXP_PER_GROUP = 8
N_EXPERTS = N_GROUPS * EXP_PER_GROUP
EXPERT_FF = 512
EPS = 1e-6
NEG = -1e30
BIG = 1e4
Q_COLS = N_HEADS * HEAD_DIM
KV_COLS = N_KV * HEAD_DIM
OFF_KV = Q_COLS
OFF_GATE = OFF_KV + 6 * KV_COLS
OFF_CONV = OFF_GATE + 3 * N_HEADS
OFF_MERGE = OFF_CONV + 2 * CONV_CH

LANES = 128
SUBLANES = 8
VMEM_LIMIT = 56 * 1024 * 1024

TM_PROJ = 256
QB = 128
KT = 256
TS_CONV = 512
HALO = 32
TR = 512
BM = 256
CHUNK = SUBLANES
R_LOCAL = 2 * TR + N_EXPERTS * CHUNK
NCH = R_LOCAL // CHUNK

N_KC = 0
N_VC = N_KC + LANES
N_KS = N_VC + LANES
N_KW = N_KS + LANES
N_CONV = N_KW + LANES
N_MERGE = N_CONV + 2 * CONV_CH
T_Q = 0
T_VS = T_Q + Q_COLS
T_VW = T_VS + LANES
T_GATE = T_VW + LANES

POS_HI_LANE = HEAD_DIM
POS_LO_LANE = HEAD_DIM + 1
UNSELECTED = -(2.0 ** 40)

R_E1, R_E2, R_W1, R_W2, R_K1, R_K2 = 0, 1, 2, 3, 4, 5
GROUP_LANE0 = N_EXPERTS


def _cparams(sem, vmem=VMEM_LIMIT):
    return pltpu.CompilerParams(dimension_semantics=sem, vmem_limit_bytes=vmem)


def _dot(a, b):
    return jnp.dot(a, b, preferred_element_type=F32)


def _dot_nt(a, b):
    return lax.dot_general(a, b, (((1,), (1,)), ((), ())), preferred_element_type=F32)


def _split(x):
    hi = x.astype(BF16)
    lo = (x - hi.astype(F32)).astype(BF16)
    return hi, lo


def _seg_rmsnorm(z, bd):
    hi, lo = _split(z * z)
    ss = _dot(hi, bd) + _dot(lo, bd)
    return z * lax.rsqrt(ss * (1.0 / HEAD_DIM) + EPS)


def _with_pos_lanes(kn, pos):
    lane = lax.broadcasted_iota(I32, kn.shape, 1)
    hi = pos // SEL_BLOCK * SEL_BLOCK
    pos_lanes = jnp.where(lane == POS_HI_LANE, hi.astype(F32),
                          jnp.where(lane == POS_LO_LANE, (pos - hi).astype(F32), 0.0))
    lo_half = lane < HEAD_DIM
    return (jnp.where(lo_half, kn, pos_lanes).astype(BF16),
            jnp.where(lo_half, pltpu.roll(kn, HEAD_DIM, 1), pos_lanes).astype(BF16))


def _in_proj_kernel(x_ref, g_ref, w_ref, wt_ref, qg_ref, kg_ref, bd_ref, mb_ref,
                    qt_o, kc_o, vc_o, ks_o, vst_o, kw_o, vwt_o, gtt_o, glu_o, mg_o, *, seq_len):
    tm = x_ref.shape[0]
    x = x_ref[...]
    h = (x * lax.rsqrt(jnp.mean(x * x, axis=-1, keepdims=True) + EPS) * g_ref[...]).astype(BF16)
    bd = bd_ref[...]
    lane = lax.broadcasted_iota(I32, (tm, LANES), 1)

    def proj(a, b):
        return _dot(h, w_ref[:, a:b])

    zt = _dot_nt(wt_ref[...], h)
    for hd in range(N_HEADS):
        zh = zt[T_Q + hd * HEAD_DIM:T_Q + (hd + 1) * HEAD_DIM, :]
        ms = jnp.sum(zh * zh, axis=0, keepdims=True) * (1.0 / HEAD_DIM)
        qt_o[hd * HEAD_DIM:(hd + 1) * HEAD_DIM, :] = (
            zh * lax.rsqrt(ms + EPS) * qg_ref[...] * (HEAD_DIM ** -0.5)).astype(BF16)
    vst_o[0] = zt[T_VS:T_VS + LANES, :].astype(BF16)
    vwt_o[0] = zt[T_VW:T_VW + LANES, :].astype(BF16)
    gtt_o[...] = jax.nn.sigmoid(zt[T_GATE:T_GATE + LANES, :])

    kc_o[...] = proj(N_KC, N_KC + LANES)
    vc_o[...] = proj(N_VC, N_VC + LANES)

    pos = (pl.program_id(0) * tm) % seq_len + lax.broadcasted_iota(I32, (tm, LANES), 0)
    onehot = jnp.where(lane == pos // SEL_BLOCK, 1.0, 0.0).astype(BF16)

    def key_tiles(zk, gain):
        return _with_pos_lanes(_seg_rmsnorm(zk, bd) * gain, pos)

    k0, k1 = key_tiles(proj(N_KS, N_KS + LANES), kg_ref[1:2, :])
    ks_o[...] = jnp.concatenate([k0, onehot, k1, onehot], axis=1)
    k0, k1 = key_tiles(proj(N_KW, N_KW + LANES), kg_ref[2:3, :])
    kw_o[...] = jnp.concatenate([k0, k1], axis=1)
    za = proj(N_CONV, N_CONV + CONV_CH)
    zg = proj(N_CONV + CONV_CH, N_CONV + 2 * CONV_CH)
    glu_o[...] = za * jax.nn.sigmoid(zg)
    for c in range(2):
        zm = proj(N_MERGE + c * D_MODEL, N_MERGE + (c + 1) * D_MODEL)
        mg_o[:, c * D_MODEL:(c + 1) * D_MODEL] = jax.nn.sigmoid(
            zm + mb_ref[:, c * D_MODEL:(c + 1) * D_MODEL]).astype(BF16)


def _in_proj(x2, norm_g, w_nat, w_t, qg, kg, bd, mb, seq_len):
    t = x2.shape[0]
    tm = TM_PROJ
    assert tm == KT and seq_len % tm == 0 and seq_len // SEL_BLOCK <= HEAD_DIM
    row = lambda n: pl.BlockSpec((tm, n), lambda i: (i, 0))
    col = lambda n: pl.BlockSpec((n, tm), lambda i: (0, i))
    vt = pl.BlockSpec((1, LANES, tm), lambda i: (i, 0, 0))
    full = lambda a: pl.BlockSpec(a.shape, lambda i: (0,) * a.ndim)
    sds = jax.ShapeDtypeStruct
    return pl.pallas_call(
        functools.partial(_in_proj_kernel, seq_len=seq_len),
        grid=(t // tm,),
        in_specs=[row(D_MODEL), full(norm_g), full(w_nat), full(w_t), full(qg), full(kg), full(bd), full(mb)],
        out_specs=[col(Q_COLS), row(LANES), row(LANES), row(4 * LANES), vt, row(2 * LANES), vt, col(LANES),
                   row(CONV_CH), row(2 * D_MODEL)],
        out_shape=[sds((Q_COLS, t), BF16), sds((t, LANES), F32), sds((t, LANES), F32), sds((t, 4 * LANES), BF16),
                   sds((t // tm, LANES, tm), BF16), sds((t, 2 * LANES), BF16), sds((t // tm, LANES, tm), BF16),
                   sds((LANES, t), F32), sds((t, CONV_CH), F32), sds((t, 2 * D_MODEL), BF16)],
        compiler_params=_cparams(("parallel",)),
        name="in_proj",
    )(x2, norm_g, w_nat, w_t, qg, kg, bd, mb)


def _compress_kernel(ck_ref, cv_ref, pe_ref, wa_ref, wb_ref, w2k_ref, w2vt_ref, bd_ref, kg_ref, kc_o, vct_o):
    nrow = ck_ref.shape[1]

    def hidden(idx, c_ref):
        c = c_ref[0]
        ya = _dot((c + pe_ref[idx, 0:1, :]).astype(BF16), wa_ref[idx])
        yb = _dot((c + pe_ref[idx, 1:2, :]).astype(BF16), wb_ref[idx])
        return jax.nn.gelu(ya + pltpu.roll(yb, nrow - 1, 0)).astype(BF16)

    kn = _seg_rmsnorm(_dot(hidden(0, ck_ref), w2k_ref[...]), bd_ref[...]) * kg_ref[0:1, :]
    c_end = lax.broadcasted_iota(I32, (nrow, LANES), 0) * CMP_STRIDE + (CMP_BLOCK - 1)
    kc_o[0] = jnp.concatenate(_with_pos_lanes(kn, c_end), axis=1)
    vct_o[0] = _dot_nt(w2vt_ref[...], hidden(1, cv_ref)).astype(BF16)


def _compress(kc_raw, vc_raw, pe_x, wa, wb, w2k, w2vt, bd, kg):
    b, nrow, width = kc_raw.shape
    batch = lambda r, n: pl.BlockSpec((1, r, n), lambda i: (i, 0, 0))
    full = lambda a: pl.BlockSpec(a.shape, lambda i: (0,) * a.ndim)
    return pl.pallas_call(
        _compress_kernel,
        grid=(b,),
        in_specs=[batch(nrow, width), batch(nrow, width), full(pe_x), full(wa), full(wb), full(w2k), full(w2vt),
                  full(bd), full(kg)],
        out_specs=[batch(nrow, 2 * LANES), batch(LANES, nrow)],
        out_shape=[jax.ShapeDtypeStruct((b, nrow, 2 * LANES), BF16), jax.ShapeDtypeStruct((b, LANES, nrow), BF16)],
        compiler_params=_cparams(("parallel",)),
        name="compress",
    )(kc_raw, vc_raw, pe_x, wa, wb, w2k, w2vt, bd, kg)


def _safe_inv(l):
    pos = l > 0.0
    return jnp.where(pos, 1.0 / jnp.where(pos, l, 1.0), 0.0)


def _halve_rows(x, op):
    rows = x.shape[0]
    while rows > SUBLANES:
        rows //= 2
        x = op(x[:rows], x[rows:])
    return x


def _attn_kernel(qt_ref, gtt_ref, kc_ref, vct_ref, ks_ref, vst_ref, kw_ref, vwt_ref, ovt_ref, o_ref, tiles_ref):
    i = pl.program_id(1)
    q0 = i * QB
    nc_rows = kc_ref.shape[1]
    n_blk = ovt_ref.shape[0]
    pw = 2 * QB
    n_pairs = N_KV * HPG // 2
    rel = ((lax.broadcasted_iota(I32, (KT, pw), 1) & (QB - 1))
           - lax.broadcasted_iota(I32, (KT, pw), 0)).astype(F32)
    slope_rows = lax.broadcasted_iota(I32, (HEAD_DIM, QB), 0) < 2
    gtt = gtt_ref[...]

    pairs = []
    for pr in range(n_pairs):
        q_pos = []
        for hd in (2 * pr, 2 * pr + 1):
            slope = jnp.where(slope_rows, 2.0 ** -(hd + 1), 0.0).astype(BF16)
            q_pos.append(jnp.concatenate([qt_ref[hd * HEAD_DIM:(hd + 1) * HEAD_DIM, :], slope], axis=0))
        pairs.append(jnp.concatenate(q_pos, axis=1))

    def update(s, vt, m, l, acc):
        m_new = jnp.maximum(m, jnp.max(_halve_rows(s, jnp.maximum), axis=0, keepdims=True))
        p = jnp.exp(s - m_new)
        alpha = jnp.exp(m - m_new)
        l_new = alpha * l + jnp.sum(_halve_rows(p, jnp.add), axis=0, keepdims=True)
        return m_new, l_new, alpha * acc + _dot(vt, p.astype(BF16))

    def scores(j, q_list, k_ref, k_width):
        k0 = pl.multiple_of(j * KT, KT)
        return tuple(_dot(k_ref[0, pl.ds(k0, KT), (c // (HPG // 2)) * k_width:(c // (HPG // 2) + 1) * k_width],
                          q_list[c]) for c in range(n_pairs))

    def consume(j, s_list, state, vt_ref, mask_fn, extra_dist=0.0):
        keep = None if mask_fn is None else mask_fn(rel + ((q0 - j * KT).astype(F32) + extra_dist))
        out = []
        for c in range(n_pairs):
            g = c // (HPG // 2)
            s = s_list[c] if keep is None else jnp.where(keep, s_list[c], NEG)
            out += update(s, vt_ref[j, g * HEAD_DIM:(g + 1) * HEAD_DIM, :], *state[3 * c:3 * c + 3])
        return tuple(out)

    j_diag = (q0 + QB - 1) // KT
    init = (jnp.full((1, pw), NEG, F32), jnp.zeros((1, pw), F32), jnp.zeros((HEAD_DIM, pw), F32)) * n_pairs

    def branch(n_first, n_last, tile_of, q_list, k_ref, vt_ref, k_width, body_mask, last_mask):
        def body(n, carry):
            s_next = scores(tile_of(n + 1), q_list, k_ref, k_width)
            return s_next + consume(tile_of(n), carry[:n_pairs], carry[n_pairs:], vt_ref, body_mask)

        carry = lax.fori_loop(n_first, n_last, body, scores(tile_of(n_first), q_list, k_ref, k_width) + init)
        return consume(tile_of(n_last), carry[:n_pairs], carry[n_pairs:], vt_ref, last_mask)

    in_window = lambda d: (d >= 0.0) & (d < float(WINDOW))
    win_out = init
    for w in range(WINDOW // KT + 1):
        jw = j_diag - WINDOW // KT + w
        before_start = jnp.where(jw < 0, float(WINDOW + KT), 0.0)
        jc = jnp.maximum(jw, 0)
        win_out = consume(jc, scores(jc, pairs, kw_ref, LANES), win_out, vwt_ref, in_window, before_start)

    cidx = lax.broadcasted_iota(I32, (nc_rows, pw), 0)
    c_end = cidx * CMP_STRIDE + (CMP_BLOCK - 1)
    cmask = (c_end <= q0 + (lax.broadcasted_iota(I32, (nc_rows, pw), 1) & (QB - 1))) & (cidx < nc_rows - 1)

    o_cmp, q_sel = [], []
    any_sel = jnp.zeros((n_blk, QB), F32)
    for g in range(N_KV):
        p_sum = jnp.zeros((nc_rows, QB), F32)
        for pr in range(g * (HPG // 2), (g + 1) * (HPG // 2)):
            s = jnp.where(cmask, _dot(kc_ref[0, :, g * LANES:(g + 1) * LANES], pairs[pr]), NEG)
            e = jnp.where(cmask, jnp.exp(s - jnp.max(_halve_rows(s, jnp.maximum), axis=0, keepdims=True)), 0.0)
            p = e * _safe_inv(jnp.sum(_halve_rows(e, jnp.add), axis=0, keepdims=True))
            o_cmp.append(_dot(vct_ref[0, g * HEAD_DIM:(g + 1) * HEAD_DIM, :], p.astype(BF16)))
            p_sum = p_sum + p[:, :QB] + p[:, QB:]
        ph, plo = _split(p_sum)
        imp_t = _dot(ovt_ref[...], ph) + _dot(ovt_ref[...], plo)

        blk = lax.broadcasted_iota(I32, (n_blk, QB), 0)
        tl = q0 + lax.broadcasted_iota(I32, (n_blk, QB), 1)
        valid = blk * SEL_BLOCK <= tl
        cur = tl // SEL_BLOCK
        forced = (blk == 0) | (blk == cur) | (blk == cur - 1)
        score = jnp.where(valid, imp_t + jnp.where(forced, BIG, 0.0), -BIG)
        sel = jnp.zeros((n_blk, QB), jnp.bool_)
        for _ in range(min(N_SEL, n_blk)):
            mx = jnp.max(score, axis=0, keepdims=True)
            first = jnp.min(jnp.where(score == mx, blk, n_blk), axis=0, keepdims=True)
            pick = blk == first
            sel = sel | pick
            score = jnp.where(pick, -jnp.inf, score)
        any_sel = any_sel + jnp.where(sel & valid, 1.0, 0.0)
        unsel = jnp.where(sel & valid, 0.0, UNSELECTED).astype(BF16)
        bias = jnp.concatenate([unsel, jnp.zeros((2 * HEAD_DIM - n_blk, QB), BF16)], axis=0)
        for pr in range(g * (HPG // 2), (g + 1) * (HPG // 2)):
            q_sel.append(jnp.concatenate([pairs[pr], jnp.concatenate([bias, bias], axis=1)], axis=0))

    per_blk = jnp.sum(any_sel, axis=1, keepdims=True)
    n_act = jnp.int32(0)
    for jt in range(n_blk * SEL_BLOCK // KT):
        blocks = slice(jt * (KT // SEL_BLOCK), (jt + 1) * (KT // SEL_BLOCK))
        tiles_ref[n_act] = jnp.int32(jt)
        n_act = n_act + ((jnp.sum(per_blk[blocks]) > 0.0) & (jt < j_diag)).astype(I32)
    tiles_ref[n_act] = j_diag

    sel_out = branch(0, n_act, lambda n: tiles_ref[n], q_sel, ks_ref, vst_ref, 2 * LANES, None, lambda d: d >= 0.0)

    for c in range(n_pairs):
        o_s = sel_out[3 * c + 2] / sel_out[3 * c + 1]
        o_w = win_out[3 * c + 2] / win_out[3 * c + 1]
        halves = []
        for k in range(2):
            g0 = 3 * (2 * c + k)
            lanes = slice(k * QB, (k + 1) * QB)
            halves.append(gtt[g0:g0 + 1, :] * o_cmp[c][:, lanes] + gtt[g0 + 1:g0 + 2, :] * o_s[:, lanes]
                          + gtt[g0 + 2:g0 + 3, :] * o_w[:, lanes])
        o_ref[:, c * LANES:(c + 1) * LANES] = jnp.concatenate(halves, axis=0).T.astype(BF16)


def _attention(qt, gtt, kc, vct, ks, vst, kw, vwt, ovt, b, s):
    n_q = s // QB
    per_q = lambda n: pl.BlockSpec((n, QB), lambda bi, i: (0, bi * n_q + i))
    per_b = lambda a: pl.BlockSpec((1,) + a.shape[1:], lambda bi, i: (bi, 0, 0))
    vt = pl.BlockSpec((s // KT, LANES, KT), lambda bi, i: (bi, 0, 0))
    return pl.pallas_call(
        _attn_kernel,
        grid=(b, n_q),
        in_specs=[per_q(Q_COLS), per_q(LANES), per_b(kc), per_b(vct), per_b(ks), vt, per_b(kw), vt,
                  pl.BlockSpec(ovt.shape, lambda bi, i: (0, 0))],
        out_specs=pl.BlockSpec((QB, Q_COLS), lambda bi, i: (bi * n_q + i, 0)),
        out_shape=jax.ShapeDtypeStruct((b * s, Q_COLS), BF16),
        scratch_shapes=[pltpu.SMEM((s // KT + 1,), I32)],
        compiler_params=_cparams(("parallel", "parallel")),
        name="nsa_attention",
    )(qt, gtt, kc, vct, ks, vst, kw, vwt, ovt)


def _conv_kernel(z_ref, w_ref, b_ref, g_ref, beta_ref, o_ref, buf, shifted):
    ts = z_ref.shape[0]
    chunk = 64

    @pl.when(pl.program_id(1) == 0)
    def _():
        buf[0:HALO, :] = jnp.zeros((HALO, CONV_CH), F32)

    buf[HALO:HALO + ts, :] = z_ref[...]
    span = ts + HALO - SUBLANES
    for r in range(1, SUBLANES):
        shifted[r - 1, 0:span, :] = buf[r:r + span, :]

    def tap_rows(k, c):
        start = HALO - (CONV_WIDTH - 1) + k
        r = start % SUBLANES
        src = buf if r == 0 else shifted.at[r - 1]
        return src[start - r + c * chunk:start - r + (c + 1) * chunk, :]

    for c in range(ts // chunk):
        acc = jnp.zeros((chunk, CONV_CH), F32) + b_ref[...]
        for k in range(CONV_WIDTH):
            acc = acc + tap_rows(k, c) * w_ref[k:k + 1, :]
        mu = jnp.mean(acc, axis=-1, keepdims=True)
        d = acc - mu
        var = jnp.mean(d * d, axis=-1, keepdims=True)
        y = d * lax.rsqrt(var + EPS) * g_ref[...] + beta_ref[...]
        o_ref[c * chunk:(c + 1) * chunk, :] = (y * jax.nn.sigmoid(y)).astype(BF16)
    buf[0:HALO, :] = buf[ts:ts + HALO, :]


def _conformer_conv(glu, w, bias, g, beta, b, s):
    ts = min(TS_CONV, s)
    full = lambda a: pl.BlockSpec(a.shape, lambda bi, i: (0, 0))
    row = pl.BlockSpec((ts, CONV_CH), lambda bi, i: (bi * (s // ts) + i, 0))
    return pl.pallas_call(
        _conv_kernel,
        grid=(b, s // ts),
        in_specs=[row, full(w), full(bias), full(g), full(beta)],
        out_specs=row,
        out_shape=jax.ShapeDtypeStruct((b * s, CONV_CH), BF16),
        scratch_shapes=[pltpu.VMEM((HALO + ts, CONV_CH), F32),
                        pltpu.VMEM((SUBLANES - 1, HALO + ts, CONV_CH), F32)],
        compiler_params=_cparams(("arbitrary", "arbitrary")),
        name="conformer_conv",
    )(glu, w, bias, g, beta)


def _merge_router_kernel(x_ref, a_ref, c_ref, mg_ref, wa_ref, wc_ref, wo_ref, fg_ref, wr_ref, br_ref,
                         x1_o, h2_o, rt_o, cnt_o):
    tm = x_ref.shape[0]
    y_a = _dot(a_ref[...], wa_ref[...])
    y_c = _dot(c_ref[...], wc_ref[...])
    mix = mg_ref[:, 0:D_MODEL].astype(F32) * y_a + mg_ref[:, D_MODEL:2 * D_MODEL].astype(F32) * y_c
    x1 = x_ref[...] + _dot(mix.astype(BF16), wo_ref[...])
    x1_o[...] = x1
    h2 = x1 * lax.rsqrt(jnp.mean(x1 * x1, axis=-1, keepdims=True) + EPS) * fg_ref[...]
    h2_o[...] = h2

    hh, hl = _split(h2)
    logits = _dot(hh, wr_ref[0]) + _dot(hl, wr_ref[0]) + _dot(hh, wr_ref[1]) + br_ref[...]
    lane = lax.broadcasted_iota(I32, (tm, LANES), 1)
    is_g = (lane >= GROUP_LANE0) & (lane < GROUP_LANE0 + N_GROUPS)
    gmax = jnp.max(jnp.where(is_g, logits, -jnp.inf), axis=-1, keepdims=True)
    gsel = jnp.min(jnp.where(is_g & (logits == gmax), lane - GROUP_LANE0, N_GROUPS), axis=-1, keepdims=True)
    pg_sel = 1.0 / jnp.sum(jnp.where(is_g, jnp.exp(logits - gmax), 0.0), axis=-1, keepdims=True)
    in_grp = (lane < N_EXPERTS) & (lane // EXP_PER_GROUP == gsel)
    emax = jnp.max(jnp.where(in_grp, logits, -jnp.inf), axis=-1, keepdims=True)
    ee = jnp.where(in_grp, jnp.exp(logits - emax), 0.0)
    pe = jnp.where(in_grp, ee / jnp.sum(ee, axis=-1, keepdims=True), -1.0)
    p1 = jnp.max(pe, axis=-1, keepdims=True)
    i1 = jnp.min(jnp.where(pe == p1, lane, LANES), axis=-1, keepdims=True)
    pe2 = jnp.where(lane == i1, -1.0, pe)
    p2 = jnp.max(pe2, axis=-1, keepdims=True)
    i2 = jnp.min(jnp.where(pe2 == p2, lane, LANES), axis=-1, keepdims=True)
    w1 = pg_sel * p1 / (p1 + p2)
    w2 = pg_sel * p2 / (p1 + p2)

    hot1 = lane == i1
    hot2 = lane == i2
    ind = jnp.where(hot1 | hot2, 1.0, 0.0)
    tri = (lax.broadcasted_iota(I32, (tm, tm), 1) < lax.broadcasted_iota(I32, (tm, tm), 0)).astype(BF16)
    before = _dot(tri, ind.astype(BF16))
    k1 = jnp.sum(jnp.where(hot1, before, 0.0), axis=-1, keepdims=True)
    k2 = jnp.sum(jnp.where(hot2, before, 0.0), axis=-1, keepdims=True)
    rec = jnp.zeros((tm, LANES), F32)
    for slot, val in ((R_E1, i1.astype(F32)), (R_E2, i2.astype(F32)), (R_W1, w1), (R_W2, w2), (R_K1, k1), (R_K2, k2)):
        rec = jnp.where(lane == slot, val, rec)
    rt_o[...] = rec
    cnt_o[0] = jnp.broadcast_to(jnp.sum(ind, axis=0, keepdims=True), (SUBLANES, LANES))


def _merge_router(x2, attn, cact, mg, wa, wc, wo, fg, wr, br):
    t = x2.shape[0]
    tm = TR
    row = lambda n: pl.BlockSpec((tm, n), lambda i: (i, 0))
    full = lambda a: pl.BlockSpec(a.shape, lambda i: (0,) * a.ndim)
    return pl.pallas_call(
        _merge_router_kernel,
        grid=(t // tm,),
        in_specs=[row(D_MODEL), row(Q_COLS), row(CONV_CH), row(2 * D_MODEL), full(wa), full(wc), full(wo),
                  full(fg), full(wr), full(br)],
        out_specs=[row(D_MODEL), row(D_MODEL), row(LANES), pl.BlockSpec((1, SUBLANES, LANES), lambda i: (i, 0, 0))],
        out_shape=[jax.ShapeDtypeStruct((t, D_MODEL), F32), jax.ShapeDtypeStruct((t, D_MODEL), F32),
                   jax.ShapeDtypeStruct((t, LANES), F32), jax.ShapeDtypeStruct((t // tm, SUBLANES, LANES), F32)],
        compiler_params=_cparams(("parallel",)),
        name="merge_router",
    )(x2, attn, cact, mg, wa, wc, wo, fg, wr, br)


def _chunk_copies(dst8_ref, n_chunks, make_copy):
    base = pl.program_id(0) * NCH

    def issue(c, carry):
        far = pl.multiple_of(dst8_ref[base + c] * CHUNK, CHUNK)
        make_copy(pl.multiple_of(c * CHUNK, CHUNK), far).start()
        return carry

    lax.fori_loop(0, n_chunks, issue, 0)

    def drain():
        def wait_one(c, carry):
            make_copy(0, 0).wait()
            return carry
        lax.fori_loop(0, n_chunks, wait_one, 0)

    return drain


def _dispatch_kernel(dst8_ref, nct_ref, pad8_ref, npad_ref, nused_ref, h_ref, rt_ref, lo_ref, xg_ref, slot_o, xs,
                     zbuf, sem, zsem):
    tr = h_ref.shape[0]

    @pl.when(pl.program_id(0) == 0)
    def _():
        zbuf[...] = jnp.zeros(zbuf.shape, F32)

        def zero_chunk(far):
            return pltpu.make_async_copy(zbuf.at[pl.ds(0, CHUNK), :], xg_ref.at[pl.ds(far, CHUNK), :], zsem)

        def zero_block(blk):
            return pltpu.make_async_copy(zbuf, xg_ref.at[pl.ds(pl.multiple_of(blk * BM, BM), BM), :], sem)

        def per_expert(e, carry, wait):
            def one(c, inner):
                if wait:
                    zero_chunk(0).wait()
                else:
                    zero_chunk(pl.multiple_of((pad8_ref[e] + c) * CHUNK, CHUNK)).start()
                return inner
            return lax.fori_loop(0, npad_ref[e], one, carry)

        def tail(blk, carry, wait):
            if wait:
                zero_block(0).wait()
            else:
                zero_block(blk).start()
            return carry

        n_blocks = xg_ref.shape[0] // BM
        lax.fori_loop(0, N_EXPERTS, functools.partial(per_expert, wait=False), 0)
        lax.fori_loop(nused_ref[0], n_blocks, functools.partial(tail, wait=False), 0)
        lax.fori_loop(0, N_EXPERTS, functools.partial(per_expert, wait=True), 0)
        lax.fori_loop(nused_ref[0], n_blocks, functools.partial(tail, wait=True), 0)

    rt = rt_ref[...]
    lane = lax.broadcasted_iota(I32, (tr, LANES), 1)
    lane_f = lane.astype(F32)
    lo_row = lo_ref[0, 0:1, :]
    slots = []
    for e_lane, k_lane in ((R_E1, R_K1), (R_E2, R_K2)):
        hot = lane_f == rt[:, e_lane:e_lane + 1]
        slots.append(jnp.sum(jnp.where(hot, lo_row, 0.0), axis=-1, keepdims=True) + rt[:, k_lane:k_lane + 1])
    rec = jnp.where(lane == 0, slots[0], jnp.where(lane == 1, slots[1], 0.0))
    slot_o[...] = rec
    rec_t = jnp.concatenate([rec[c * LANES:(c + 1) * LANES].T for c in range(tr // LANES)], axis=1)
    row_id = lax.broadcasted_iota(I32, (R_LOCAL, tr), 0).astype(F32)
    pick = (row_id == rec_t[0:1, :]) | (row_id == rec_t[1:2, :])
    xs[...] = _dot(jnp.where(pick, 1.0, 0.0).astype(BF16), h_ref[...].astype(BF16))

    def make_copy(near, far):
        return pltpu.make_async_copy(xs.at[pl.ds(near, CHUNK), :], xg_ref.at[pl.ds(far, CHUNK), :], sem)

    _chunk_copies(dst8_ref, nct_ref[pl.program_id(0)], make_copy)()


def _dispatch(dst8, nct, pad8, npad, nused, h2, route, lo_rows, p_rows):
    t = h2.shape[0]
    return pl.pallas_call(
        _dispatch_kernel,
        grid_spec=pltpu.PrefetchScalarGridSpec(
            num_scalar_prefetch=5,
            grid=(t // TR,),
            in_specs=[pl.BlockSpec((TR, D_MODEL), lambda i, *_: (i, 0)),
                      pl.BlockSpec((TR, LANES), lambda i, *_: (i, 0)),
                      pl.BlockSpec((1, SUBLANES, LANES), lambda i, *_: (i, 0, 0))],
            out_specs=[pl.BlockSpec(memory_space=pl.ANY),
                       pl.BlockSpec((TR, LANES), lambda i, *_: (i, 0))],
            scratch_shapes=[pltpu.VMEM((R_LOCAL, D_MODEL), F32), pltpu.VMEM((BM, D_MODEL), F32),
                            pltpu.SemaphoreType.DMA(()), pltpu.SemaphoreType.DMA(())],
        ),
        out_shape=[jax.ShapeDtypeStruct((p_rows, D_MODEL), F32), jax.ShapeDtypeStruct((t, LANES), F32)],
        compiler_params=_cparams(("arbitrary",)),
        name="moe_dispatch",
    )(dst8, nct, pad8, npad, nused, h2, route, lo_rows)


def _expert_kernel(blk_e_ref, nused_ref, x_ref, wg_ref, wu_ref, wd_ref, y_ref, wg_b, wu_b, wd_b):
    i = pl.program_id(0)
    used = i < nused_ref[0]

    @pl.when((i == 0) | (blk_e_ref[i] != blk_e_ref[jnp.maximum(i - 1, 0)]))
    def _():
        wg_b[...] = wg_ref[0].astype(BF16)
        wu_b[...] = wu_ref[0].astype(BF16)
        wd_b[...] = wd_ref[0].astype(BF16)

    @pl.when(used)
    def _():
        xb = x_ref[...].astype(BF16)
        a = _dot(xb, wg_b[...])
        u = _dot(xb, wu_b[...])
        y_ref[...] = _dot((a * jax.nn.sigmoid(a) * u).astype(BF16), wd_b[...])

    @pl.when(jnp.logical_not(used))
    def _():
        y_ref[...] = jnp.zeros(y_ref.shape, F32)


def _experts(blk_e, nused, xg, wg, wu, wd):
    p_rows = xg.shape[0]
    x_blk = lambda i, be, nu: (jnp.minimum(i, nu[0] - 1), 0)
    return pl.pallas_call(
        _expert_kernel,
        grid_spec=pltpu.PrefetchScalarGridSpec(
            num_scalar_prefetch=2,
            grid=(p_rows // BM,),
            in_specs=[pl.BlockSpec((BM, D_MODEL), x_blk),
                      pl.BlockSpec((1, D_MODEL, EXPERT_FF), lambda i, be, nu: (be[i], 0, 0)),
                      pl.BlockSpec((1, D_MODEL, EXPERT_FF), lambda i, be, nu: (be[i], 0, 0)),
                      pl.BlockSpec((1, EXPERT_FF, D_MODEL), lambda i, be, nu: (be[i], 0, 0))],
            out_specs=pl.BlockSpec((BM, D_MODEL), lambda i, be, nu: (i, 0)),
            scratch_shapes=[pltpu.VMEM((D_MODEL, EXPERT_FF), BF16), pltpu.VMEM((D_MODEL, EXPERT_FF), BF16),
                            pltpu.VMEM((EXPERT_FF, D_MODEL), BF16)],
        ),
        out_shape=jax.ShapeDtypeStruct((p_rows, D_MODEL), F32),
        compiler_params=_cparams(("arbitrary",)),
        name="moe_experts",
    )(blk_e, nused, xg, wg, wu, wd)


def _combine_ple_kernel(dst8_ref, nct_ref, x1_ref, rt_ref, slot_ref, p_ref, pg_ref, wpg_ref, bpg_ref, wpp_ref,
                        yg_ref, o_ref, ys, sem):
    tr = x1_ref.shape[0]
    n_chunks = nct_ref[pl.program_id(0)]

    def make_copy(near, far):
        return pltpu.make_async_copy(yg_ref.at[pl.ds(far, CHUNK), :], ys.at[pl.ds(near, CHUNK), :], sem)

    drain = _chunk_copies(dst8_ref, n_chunks, make_copy)

    def clear(c, carry):
        ys[pl.ds(pl.multiple_of(c * CHUNK, CHUNK), CHUNK), :] = jnp.zeros((CHUNK, D_MODEL), F32)
        return carry

    lax.fori_loop(n_chunks, NCH, clear, 0)
    drain()

    rt = rt_ref[...]
    sl = slot_ref[...]
    col_id = lax.broadcasted_iota(I32, (tr, R_LOCAL), 1).astype(F32)
    yb = ys[...].astype(BF16)
    y0 = _dot(jnp.where(col_id == sl[:, 0:1], 1.0, 0.0).astype(BF16), yb)
    y1 = _dot(jnp.where(col_id == sl[:, 1:2], 1.0, 0.0).astype(BF16), yb)
    x2 = x1_ref[...] + rt[:, R_W1:R_W1 + 1] * y0 + rt[:, R_W2:R_W2 + 1] * y1
    h3 = (x2 * lax.rsqrt(jnp.mean(x2 * x2, axis=-1, keepdims=True) + EPS) * pg_ref[...]).astype(BF16)
    gate = jax.nn.sigmoid(_dot(h3, wpg_ref[...]) + bpg_ref[...])
    o_ref[...] = x2 + gate * _dot(p_ref[...].astype(BF16), wpp_ref[...])


def _combine_ple(dst8, nct, x1, route, slots, p2, pg, wpg, bpg, wpp, yg):
    t = x1.shape[0]
    row = lambda n: pl.BlockSpec((TR, n), lambda i, d, c: (i, 0))
    full = lambda a: pl.BlockSpec(a.shape, lambda i, d, c: (0,) * a.ndim)
    return pl.pallas_call(
        _combine_ple_kernel,
        grid_spec=pltpu.PrefetchScalarGridSpec(
            num_scalar_prefetch=2,
            grid=(t // TR,),
            in_specs=[row(D_MODEL), row(LANES), row(LANES), row(PLE_DIM), full(pg), full(wpg), full(bpg),
                      full(wpp), pl.BlockSpec(memory_space=pl.ANY)],
            out_specs=row(D_MODEL),
            scratch_shapes=[pltpu.VMEM((R_LOCAL, D_MODEL), F32), pltpu.SemaphoreType.DMA(())],
        ),
        out_shape=jax.ShapeDtypeStruct((t, D_MODEL), F32),
        compiler_params=_cparams(("arbitrary",)),
        name="moe_combine_ple",
    )(dst8, nct, x1, route, slots, p2, pg, wpg, bpg, wpp, yg)


def _seg_sum_matrix():
    seg = np.arange(LANES) // HEAD_DIM
    return jnp.asarray(seg[:, None] == seg[None, :], BF16)


def _overlap_t(nc_rows, n_blk):
    nc = np.arange(nc_rows)[None, :] * CMP_STRIDE
    sb = np.arange(n_blk)[:, None] * SEL_BLOCK
    ov = (nc < sb + SEL_BLOCK) & (nc + CMP_BLOCK > sb) & (np.arange(nc_rows)[None, :] < nc_rows - 1)
    return jnp.asarray(ov, BF16)


def _compress_weights(pos, w1, w2):
    half = CMP_BLOCK // 2
    pe = jnp.broadcast_to(pos[:, None, :], (CMP_BLOCK, N_KV, HEAD_DIM)).reshape(2, half * N_KV * HEAD_DIM)
    w1r = w1.reshape(2, half, HEAD_DIM, CMP_HIDDEN)
    eye = jnp.eye(N_KV, dtype=w1.dtype)
    wx = jnp.einsum('hldj,gk->hlgdkj', w1r, eye).reshape(2, half * N_KV * HEAD_DIM, N_KV * CMP_HIDDEN)
    w2x = jnp.einsum('jd,gk->gjkd', w2, eye).reshape(N_KV * CMP_HIDDEN, N_KV * HEAD_DIM)
    return pe, wx[0].astype(BF16), wx[1].astype(BF16), w2x.astype(BF16)


def _layer(x, p_i, prm):
    b, s, _ = x.shape
    t = b * s
    x2 = x.reshape(t, D_MODEL)
    bd = _seg_sum_matrix()

    w = prm["w_in"]
    kv = lambda j: w[:, OFF_KV + j * KV_COLS:OFF_KV + (j + 1) * KV_COLS]
    w_nat = jnp.concatenate([kv(0), kv(1), kv(2), kv(4), w[:, OFF_CONV:]], axis=1).astype(BF16)
    w_t = jnp.concatenate([w[:, :Q_COLS], kv(3), kv(5),
                           jnp.pad(w[:, OFF_GATE:OFF_CONV], ((0, 0), (0, LANES - 3 * N_HEADS)))], axis=1).T.astype(BF16)
    qg = jnp.broadcast_to(prm["q_norm_g"][:, None], (HEAD_DIM, TM_PROJ))
    kg = jnp.tile(prm["k_norm_g"], (1, LANES // HEAD_DIM))
    qt, kc_raw, vc_raw, ks, vst, kw, vwt, gtt, glu, mg = _in_proj(
        x2, prm["attn_norm_g"][None, :], w_nat, w_t, qg, kg, bd, prm["merge_b"][None, :], s)

    nrow = s // CMP_STRIDE
    cw = [_compress_weights(prm["cmp_pos"][j], prm["cmp_w1"][j], prm["cmp_w2"][j]) for j in range(2)]
    pe_x, wa, wb = (jnp.stack([cw[0][j], cw[1][j]]) for j in range(3))
    kc, vct = _compress(kc_raw.reshape(b, nrow, CMP_STRIDE * LANES), vc_raw.reshape(b, nrow, CMP_STRIDE * LANES),
                        pe_x, wa, wb, cw[0][3], cw[1][3].T, bd, kg)

    attn = _attention(qt, gtt, kc, vct, ks.reshape(b, s, 4 * LANES), vst, kw.reshape(b, s, 2 * LANES), vwt,
                      _overlap_t(nrow, s // SEL_BLOCK), b, s)

    cact = _conformer_conv(glu, prm["conv_w"], prm["conv_b"][None, :], prm["conv_norm_g"][None, :],
                           prm["conv_norm_b"][None, :], b, s)

    wr_full = jnp.zeros((D_MODEL, LANES), F32)
    wr_full = wr_full.at[:, :N_EXPERTS].set(prm["w_router_expert"])
    wr_full = wr_full.at[:, GROUP_LANE0:GROUP_LANE0 + N_GROUPS].set(prm["w_router_group"])
    wr_hi = wr_full.astype(BF16)
    wr = jnp.stack([wr_hi, (wr_full - wr_hi.astype(F32)).astype(BF16)])
    br = jnp.zeros((1, LANES), F32)
    br = br.at[0, :N_EXPERTS].set(prm["b_router_expert"])
    br = br.at[0, GROUP_LANE0:GROUP_LANE0 + N_GROUPS].set(prm["b_router_group"])
    x1, h2, route, cnt = _merge_router(
        x2, attn, cact, mg, prm["w_attn_out"].astype(BF16), prm["w_conv_out"].astype(BF16),
        prm["w_out"].astype(BF16), prm["ffn_norm_g"][None, :], wr, br)

    n_tiles = t // TR
    counts = cnt[:, 0, :N_EXPERTS].astype(I32)
    seg = (counts + CHUNK - 1) // CHUNK * CHUNK
    local_off = jnp.cumsum(seg, axis=1) - seg
    nct = (jnp.sum(seg, axis=1) // CHUNK).astype(I32)
    total = jnp.sum(seg, axis=0)
    padded = (total + BM - 1) // BM * BM
    pends = jnp.cumsum(padded)
    far_off = (pends - padded)[None, :] + jnp.cumsum(seg, axis=0) - seg
    chunk = jnp.arange(NCH, dtype=I32)[None, :, None]
    lo8 = (local_off // CHUNK)[:, None, :]
    in_seg = (chunk >= lo8) & (chunk < ((local_off + seg) // CHUNK)[:, None, :])
    dst8 = (jnp.sum(jnp.where(in_seg, (far_off // CHUNK)[:, None, :] - lo8, 0), axis=-1)
            + chunk[:, :, 0]).astype(I32).reshape(n_tiles * NCH)
    lo_rows = jnp.broadcast_to(jnp.pad(local_off.astype(F32), ((0, 0), (0, LANES - N_EXPERTS)))[:, None, :],
                               (n_tiles, SUBLANES, LANES))
    n_blk = -(-(2 * t + n_tiles * N_EXPERTS * (CHUNK - 1) + N_EXPERTS * (BM - 1)) // BM)
    blk_e = jnp.minimum(jnp.sum(jnp.arange(n_blk, dtype=I32)[:, None] * BM >= pends[None, :], axis=1),
                        N_EXPERTS - 1).astype(I32)
    nused = (pends[-1:] // BM).astype(I32)
    pad8 = ((pends - padded + total) // CHUNK).astype(I32)
    npad = ((padded - total) // CHUNK).astype(I32)

    xg, slots = _dispatch(dst8, nct, pad8, npad, nused, h2, route, lo_rows, n_blk * BM)
    yg = _experts(blk_e, nused, xg, prm["w_exp_gate"], prm["w_exp_up"], prm["w_exp_down"])
    out = _combine_ple(dst8, nct, x1, route, slots, p_i.reshape(t, PLE_DIM), prm["ple_norm_g"][None, :],
                       prm["w_ple_gate"].astype(BF16), prm["b_ple_gate"][None, :], prm["w_ple_proj"].astype(BF16), yg)
    return out.reshape(b, s, D_MODEL)


def kernel(x, p, attn_norm_g, w_in, q_norm_g, k_norm_g, cmp_pos, cmp_w1, cmp_w2, w_attn_out, conv_w, conv_b, conv_norm_g, conv_norm_b, w_conv_out, merge_b, w_out, ffn_norm_g, w_router_group, b_router_group, w_router_expert, b_router_expert, w_exp_gate, w_exp_up, w_exp_down, ple_norm_g, w_ple_gate, b_ple_gate, w_ple_proj):
    stacked = dict(attn_norm_g=attn_norm_g, w_in=w_in, q_norm_g=q_norm_g, k_norm_g=k_norm_g, cmp_pos=cmp_pos,
                   cmp_w1=cmp_w1, cmp_w2=cmp_w2, w_attn_out=w_attn_out, conv_w=conv_w, conv_b=conv_b,
                   conv_norm_g=conv_norm_g, conv_norm_b=conv_norm_b, w_conv_out=w_conv_out, merge_b=merge_b,
                   w_out=w_out, ffn_norm_g=ffn_norm_g, w_router_group=w_router_group,
                   b_router_group=b_router_group, w_router_expert=w_router_expert,
                   b_router_expert=b_router_expert, w_exp_gate=w_exp_gate, w_exp_up=w_exp_up,
                   w_exp_down=w_exp_down, ple_norm_g=ple_norm_g, w_ple_gate=w_ple_gate, b_ple_gate=b_ple_gate,
                   w_ple_proj=w_ple_proj)
    for i in range(w_in.shape[0]):
        x = _layer(x, p[i], {k: v[i] for k, v in stacked.items()})
    return x
```

```python
import functools

import numpy as np
import jax
import jax.numpy as jnp
from jax import lax
from jax.experimental import pallas as pl
from jax.experimental.pallas import tpu as pltpu

F32 = jnp.float32
BF16 = jnp.bfloat16
I32 = jnp.int32

D_MODEL = 1024
PLE_DIM = 256
N_HEADS = 8
N_KV = 2
HPG = N_HEADS // N_KV
HEAD_DIM = 64
CMP_BLOCK = 32
CMP_STRIDE = 16
CMP_HIDDEN = 128
SEL_BLOCK = 64
N_SEL = 8
WINDOW = 512
CONV_CH = 512
CONV_WIDTH = 31
N_GROUPS = 4
EXP_PER_GROUP = 8
N_EXPERTS = N_GROUPS * EXP_PER_GROUP
EXPERT_FF = 512
EPS = 1e-6
NEG = -1e30
BIG = 1e4
Q_COLS = N_HEADS * HEAD_DIM
KV_COLS = N_KV * HEAD_DIM
OFF_KV = Q_COLS
OFF_GATE = OFF_KV + 6 * KV_COLS
OFF_CONV = OFF_GATE + 3 * N_HEADS
OFF_MERGE = OFF_CONV + 2 * CONV_CH

LANES = 128
SUBLANES = 8
VMEM_LIMIT = 56 * 1024 * 1024

TM_PROJ = 256
QB = 128
KT = 256
TS_CONV = 512
HALO = 32
TR = 512
BM = 256
CHUNK = SUBLANES
R_LOCAL = 2 * TR + N_EXPERTS * CHUNK
NCH = R_LOCAL // CHUNK

N_KC = 0
N_VC = N_KC + LANES
N_KS = N_VC + LANES
N_KW = N_KS + LANES
N_CONV = N_KW + LANES
N_MERGE = N_CONV + 2 * CONV_CH
T_Q = 0
T_VS = T_Q + Q_COLS
T_VW = T_VS + LANES
T_GATE = T_VW + LANES

POS_HI_LANE = HEAD_DIM
POS_LO_LANE = HEAD_DIM + 1
UNSELECTED = -(2.0 ** 40)

R_E1, R_E2, R_W1, R_W2, R_K1, R_K2 = 0, 1, 2, 3, 4, 5
GROUP_LANE0 = N_EXPERTS


def _cparams(sem, vmem=VMEM_LIMIT):
    return pltpu.CompilerParams(dimension_semantics=sem, vmem_limit_bytes=vmem)


def _dot(a, b):
    return jnp.dot(a, b, preferred_element_type=F32)


def _dot_nt(a, b):
    return lax.dot_general(a, b, (((1,), (1,)), ((), ())), preferred_element_type=F32)


def _split(x):
    hi = x.astype(BF16)
    lo = (x - hi.astype(F32)).astype(BF16)
    return hi, lo


def _seg_rmsnorm(z, bd):
    hi, lo = _split(z * z)
    ss = _dot(hi, bd) + _dot(lo, bd)
    return z * lax.rsqrt(ss * (1.0 / HEAD_DIM) + EPS)


def _with_pos_lanes(kn, pos):
    lane = lax.broadcasted_iota(I32, kn.shape, 1)
    hi = pos // SEL_BLOCK * SEL_BLOCK
    pos_lanes = jnp.where(lane == POS_HI_LANE, hi.astype(F32),
                          jnp.where(lane == POS_LO_LANE, (pos - hi).astype(F32), 0.0))
    lo_half = lane < HEAD_DIM
    return (jnp.where(lo_half, kn, pos_lanes).astype(BF16),
            jnp.where(lo_half, pltpu.roll(kn, HEAD_DIM, 1), pos_lanes).astype(BF16))


def _in_proj_kernel(x_ref, g_ref, w_ref, wt_ref, qg_ref, kg_ref, bd_ref, mb_ref,
                    qt_o, kc_o, vc_o, ks_o, vst_o, kw_o, vwt_o, gtt_o, glu_o, mg_o, *, seq_len):
    tm = x_ref.shape[0]
    x = x_ref[...]
    h = (x * lax.rsqrt(jnp.mean(x * x, axis=-1, keepdims=True) + EPS) * g_ref[...]).astype(BF16)
    bd = bd_ref[...]
    lane = lax.broadcasted_iota(I32, (tm, LANES), 1)

    def proj(a, b):
        return _dot(h, w_ref[:, a:b])

    zt = _dot_nt(wt_ref[...], h)
    for hd in range(N_HEADS):
        zh = zt[T_Q + hd * HEAD_DIM:T_Q + (hd + 1) * HEAD_DIM, :]
        ms = jnp.sum(zh * zh, axis=0, keepdims=True) * (1.0 / HEAD_DIM)
        qt_o[hd * HEAD_DIM:(hd + 1) * HEAD_DIM, :] = (
            zh * lax.rsqrt(ms + EPS) * qg_ref[...] * (HEAD_DIM ** -0.5)).astype(BF16)
    vst_o[0] = zt[T_VS:T_VS + LANES, :].astype(BF16)
    vwt_o[0] = zt[T_VW:T_VW + LANES, :].astype(BF16)
    gtt_o[...] = jax.nn.sigmoid(zt[T_GATE:T_GATE + LANES, :])

    kc_o[...] = proj(N_KC, N_KC + LANES)
    vc_o[...] = proj(N_VC, N_VC + LANES)

    pos = (pl.program_id(0) * tm) % seq_len + lax.broadcasted_iota(I32, (tm, LANES), 0)
    onehot = jnp.where(lane == pos // SEL_BLOCK, 1.0, 0.0).astype(BF16)

    def key_tiles(zk, gain):
        return _with_pos_lanes(_seg_rmsnorm(zk, bd) * gain, pos)

    k0, k1 = key_tiles(proj(N_KS, N_KS + LANES), kg_ref[1:2, :])
    ks_o[...] = jnp.concatenate([k0, onehot, k1, onehot], axis=1)
    k0, k1 = key_tiles(proj(N_KW, N_KW + LANES), kg_ref[2:3, :])
    kw_o[...] = jnp.concatenate([k0, k1], axis=1)
    za = proj(N_CONV, N_CONV + CONV_CH)
    zg = proj(N_CONV + CONV_CH, N_CONV + 2 * CONV_CH)
    glu_o[...] = za * jax.nn.sigmoid(zg)
    for c in range(2):
        zm = proj(N_MERGE + c * D_MODEL, N_MERGE + (c + 1) * D_MODEL)
        mg_o[:, c * D_MODEL:(c + 1) * D_MODEL] = jax.nn.sigmoid(
            zm + mb_ref[:, c * D_MODEL:(c + 1) * D_MODEL]).astype(BF16)


def _in_proj(x2, norm_g, w_nat, w_t, qg, kg, bd, mb, seq_len):
    t = x2.shape[0]
    tm = TM_PROJ
    assert tm == KT and seq_len % tm == 0 and seq_len // SEL_BLOCK <= HEAD_DIM
    row = lambda n: pl.BlockSpec((tm, n), lambda i: (i, 0))
    col = lambda n: pl.BlockSpec((n, tm), lambda i: (0, i))
    vt = pl.BlockSpec((1, LANES, tm), lambda i: (i, 0, 0))
    full = lambda a: pl.BlockSpec(a.shape, lambda i: (0,) * a.ndim)
    sds = jax.ShapeDtypeStruct
    return pl.pallas_call(
        functools.partial(_in_proj_kernel, seq_len=seq_len),
        grid=(t // tm,),
        in_specs=[row(D_MODEL), full(norm_g), full(w_nat), full(w_t), full(qg), full(kg), full(bd), full(mb)],
        out_specs=[col(Q_COLS), row(LANES), row(LANES), row(4 * LANES), vt, row(2 * LANES), vt, col(LANES),
                   row(CONV_CH), row(2 * D_MODEL)],
        out_shape=[sds((Q_COLS, t), BF16), sds((t, LANES), F32), sds((t, LANES), F32), sds((t, 4 * LANES), BF16),
                   sds((t // tm, LANES, tm), BF16), sds((t, 2 * LANES), BF16), sds((t // tm, LANES, tm), BF16),
                   sds((LANES, t), F32), sds((t, CONV_CH), F32), sds((t, 2 * D_MODEL), BF16)],
        compiler_params=_cparams(("parallel",)),
        name="in_proj",
    )(x2, norm_g, w_nat, w_t, qg, kg, bd, mb)


def _compress_kernel(ck_ref, cv_ref, pe_ref, wa_ref, wb_ref, w2k_ref, w2vt_ref, bd_ref, kg_ref, kc_o, vct_o):
    nrow = ck_ref.shape[1]

    def hidden(idx, c_ref):
        c = c_ref[0]
        ya = _dot((c + pe_ref[idx, 0:1, :]).astype(BF16), wa_ref[idx])
        yb = _dot((c + pe_ref[idx, 1:2, :]).astype(BF16), wb_ref[idx])
        return jax.nn.gelu(ya + pltpu.roll(yb, nrow - 1, 0)).astype(BF16)

    kn = _seg_rmsnorm(_dot(hidden(0, ck_ref), w2k_ref[...]), bd_ref[...]) * kg_ref[0:1, :]
    c_end = lax.broadcasted_iota(I32, (nrow, LANES), 0) * CMP_STRIDE + (CMP_BLOCK - 1)
    kc_o[0] = jnp.concatenate(_with_pos_lanes(kn, c_end), axis=1)
    vct_o[0] = _dot_nt(w2vt_ref[...], hidden(1, cv_ref)).astype(BF16)


def _compress(kc_raw, vc_raw, pe_x, wa, wb, w2k, w2vt, bd, kg):
    b, nrow, width = kc_raw.shape
    batch = lambda r, n: pl.BlockSpec((1, r, n), lambda i: (i, 0, 0))
    full = lambda a: pl.BlockSpec(a.shape, lambda i: (0,) * a.ndim)
    return pl.pallas_call(
        _compress_kernel,
        grid=(b,),
        in_specs=[batch(nrow, width), batch(nrow, width), full(pe_x), full(wa), full(wb), full(w2k), full(w2vt),
                  full(bd), full(kg)],
        out_specs=[batch(nrow, 2 * LANES), batch(LANES, nrow)],
        out_shape=[jax.ShapeDtypeStruct((b, nrow, 2 * LANES), BF16), jax.ShapeDtypeStruct((b, LANES, nrow), BF16)],
        compiler_params=_cparams(("parallel",)),
        name="compress",
    )(kc_raw, vc_raw, pe_x, wa, wb, w2k, w2vt, bd, kg)


def _safe_inv(l):
    pos = l > 0.0
    return jnp.where(pos, 1.0 / jnp.where(pos, l, 1.0), 0.0)


def _halve_rows(x, op):
    rows = x.shape[0]
    while rows > SUBLANES:
        rows //= 2
        x = op(x[:rows], x[rows:])
    return x


def _attn_kernel(qt_ref, gtt_ref, kc_ref, vct_ref, ks_ref, vst_ref, kw_ref, vwt_ref, ovt_ref, o_ref, tiles_ref):
    i = pl.program_id(1)
    q0 = i * QB
    nc_rows = kc_ref.shape[1]
    n_blk = ovt_ref.shape[0]
    pw = 2 * QB
    n_pairs = N_KV * HPG // 2
    rel = ((lax.broadcasted_iota(I32, (KT, pw), 1) & (QB - 1))
           - lax.broadcasted_iota(I32, (KT, pw), 0)).astype(F32)
    slope_rows = lax.broadcasted_iota(I32, (HEAD_DIM, QB), 0) < 2
    gtt = gtt_ref[...]

    pairs = []
    for pr in range(n_pairs):
        q_pos = []
        for hd in (2 * pr, 2 * pr + 1):
            slope = jnp.where(slope_rows, 2.0 ** -(hd + 1), 0.0).astype(BF16)
            q_pos.append(jnp.concatenate([qt_ref[hd * HEAD_DIM:(hd + 1) * HEAD_DIM, :], slope], axis=0))
        pairs.append(jnp.concatenate(q_pos, axis=1))

    def update(s, vt, m, l, acc):
        m_new = jnp.maximum(m, jnp.max(_halve_rows(s, jnp.maximum), axis=0, keepdims=True))
        p = jnp.exp(s - m_new)
        alpha = jnp.exp(m - m_new)
        l_new = alpha * l + jnp.sum(_halve_rows(p, jnp.add), axis=0, keepdims=True)
        return m_new, l_new, alpha * acc + _dot(vt, p.astype(BF16))

    def scores(j, q_list, k_ref, k_width):
        k0 = pl.multiple_of(j * KT, KT)
        return tuple(_dot(k_ref[0, pl.ds(k0, KT), (c // (HPG // 2)) * k_width:(c // (HPG // 2) + 1) * k_width],
                          q_list[c]) for c in range(n_pairs))

    def consume(j, s_list, state, vt_ref, mask_fn, extra_dist=0.0):
        keep = None if mask_fn is None else mask_fn(rel + ((q0 - j * KT).astype(F32) + extra_dist))
        out = []
        for c in range(n_pairs):
            g = c // (HPG // 2)
            s = s_list[c] if keep is None else jnp.where(keep, s_list[c], NEG)
            out += update(s, vt_ref[j, g * HEAD_DIM:(g + 1) * HEAD_DIM, :], *state[3 * c:3 * c + 3])
        return tuple(out)

    j_diag = (q0 + QB - 1) // KT
    init = (jnp.full((1, pw), NEG, F32), jnp.zeros((1, pw), F32), jnp.zeros((HEAD_DIM, pw), F32)) * n_pairs

    def branch(n_first, n_last, tile_of, q_list, k_ref, vt_ref, k_width, body_mask, last_mask):
        def body(n, carry):
            s_next = scores(tile_of(n + 1), q_list, k_ref, k_width)
            return s_next + consume(tile_of(n), carry[:n_pairs], carry[n_pairs:], vt_ref, body_mask)

        carry = lax.fori_loop(n_first, n_last, body, scores(tile_of(n_first), q_list, k_ref, k_width) + init)
        return consume(tile_of(n_last), carry[:n_pairs], carry[n_pairs:], vt_ref, last_mask)

    in_window = lambda d: (d >= 0.0) & (d < float(WINDOW))
    win_out = init
    for w in range(WINDOW // KT + 1):
        jw = j_diag - WINDOW // KT + w
        before_start = jnp.where(jw < 0, float(WINDOW + KT), 0.0)
        jc = jnp.maximum(jw, 0)
        win_out = consume(jc, scores(jc, pairs, kw_ref, LANES), win_out, vwt_ref, in_window, before_start)

    cidx = lax.broadcasted_iota(I32, (nc_rows, pw), 0)
    c_end = cidx * CMP_STRIDE + (CMP_BLOCK - 1)
    cmask = (c_end <= q0 + (lax.broadcasted_iota(I32, (nc_rows, pw), 1) & (QB - 1))) & (cidx < nc_rows - 1)

    o_cmp, q_sel = [], []
    any_sel = jnp.zeros((n_blk, QB), F32)
    for g in range(N_KV):
        p_sum = jnp.zeros((nc_rows, QB), F32)
        for pr in range(g * (HPG // 2), (g + 1) * (HPG // 2)):
            s = jnp.where(cmask, _dot(kc_ref[0, :, g * LANES:(g + 1) * LANES], pairs[pr]), NEG)
            e = jnp.where(cmask, jnp.exp(s - jnp.max(_halve_rows(s, jnp.maximum), axis=0, keepdims=True)), 0.0)
            p = e * _safe_inv(jnp.sum(_halve_rows(e, jnp.add), axis=0, keepdims=True))
            o_cmp.append(_dot(vct_ref[0, g * HEAD_DIM:(g + 1) * HEAD_DIM, :], p.astype(BF16)))
            p_sum = p_sum + p[:, :QB] + p[:, QB:]
        ph, plo = _split(p_sum)
        imp_t = _dot(ovt_ref[...], ph) + _dot(ovt_ref[...], plo)

        blk = lax.broadcasted_iota(I32, (n_blk, QB), 0)
        tl = q0 + lax.broadcasted_iota(I32, (n_blk, QB), 1)
        valid = blk * SEL_BLOCK <= tl
        cur = tl // SEL_BLOCK
        forced = (blk == 0) | (blk == cur) | (blk == cur - 1)
        score = jnp.where(valid, imp_t + jnp.where(forced, BIG, 0.0), -BIG)
        sel = jnp.zeros((n_blk, QB), jnp.bool_)
        for _ in range(min(N_SEL, n_blk)):
            mx = jnp.max(score, axis=0, keepdims=True)
            first = jnp.min(jnp.where(score == mx, blk, n_blk), axis=0, keepdims=True)
            pick = blk == first
            sel = sel | pick
            score = jnp.where(pick, -jnp.inf, score)
        any_sel = any_sel + jnp.where(sel & valid, 1.0, 0.0)
        unsel = jnp.where(sel & valid, 0.0, UNSELECTED).astype(BF16)
        bias = jnp.concatenate([unsel, jnp.zeros((2 * HEAD_DIM - n_blk, QB), BF16)], axis=0)
        for pr in range(g * (HPG // 2), (g + 1) * (HPG // 2)):
            q_sel.append(jnp.concatenate([pairs[pr], jnp.concatenate([bias, bias], axis=1)], axis=0))

    per_blk = jnp.sum(any_sel, axis=1, keepdims=True)
    n_act = jnp.int32(0)
    for jt in range(n_blk * SEL_BLOCK // KT):
        blocks = slice(jt * (KT // SEL_BLOCK), (jt + 1) * (KT // SEL_BLOCK))
        tiles_ref[n_act] = jnp.int32(jt)
        n_act = n_act + ((jnp.sum(per_blk[blocks]) > 0.0) & (jt < j_diag)).astype(I32)
    tiles_ref[n_act] = j_diag

    sel_out = branch(0, n_act, lambda n: tiles_ref[n], q_sel, ks_ref, vst_ref, 2 * LANES, None, lambda d: d >= 0.0)

    for c in range(n_pairs):
        o_s = sel_out[3 * c + 2] / sel_out[3 * c + 1]
        o_w = win_out[3 * c + 2] / win_out[3 * c + 1]
        halves = []
        for k in range(2):
            g0 = 3 * (2 * c + k)
            lanes = slice(k * QB, (k + 1) * QB)
            halves.append(gtt[g0:g0 + 1, :] * o_cmp[c][:, lanes] + gtt[g0 + 1:g0 + 2, :] * o_s[:, lanes]
                          + gtt[g0 + 2:g0 + 3, :] * o_w[:, lanes])
        o_ref[:, c * LANES:(c + 1) * LANES] = jnp.concatenate(halves, axis=0).T.astype(BF16)


def _attention(qt, gtt, kc, vct, ks, vst, kw, vwt, ovt, b, s):
    n_q = s // QB
    per_q = lambda n: pl.BlockSpec((n, QB), lambda bi, i: (0, bi * n_q + i))
    per_b = lambda a: pl.BlockSpec((1,) + a.shape[1:], lambda bi, i: (bi, 0, 0))
    vt = pl.BlockSpec((s // KT, LANES, KT), lambda bi, i: (bi, 0, 0))
    return pl.pallas_call(
        _attn_kernel,
        grid=(b, n_q),
        in_specs=[per_q(Q_COLS), per_q(LANES), per_b(kc), per_b(vct), per_b(ks), vt, per_b(kw), vt,
                  pl.BlockSpec(ovt.shape, lambda bi, i: (0, 0))],
        out_specs=pl.BlockSpec((QB, Q_COLS), lambda bi, i: (bi * n_q + i, 0)),
        out_shape=jax.ShapeDtypeStruct((b * s, Q_COLS), BF16),
        scratch_shapes=[pltpu.SMEM((s // KT + 1,), I32)],
        compiler_params=_cparams(("parallel", "parallel")),
        name="nsa_attention",
    )(qt, gtt, kc, vct, ks, vst, kw, vwt, ovt)


def _conv_kernel(z_ref, w_ref, b_ref, g_ref, beta_ref, o_ref, buf, shifted):
    ts = z_ref.shape[0]
    chunk = 64

    @pl.when(pl.program_id(1) == 0)
    def _():
        buf[0:HALO, :] = jnp.zeros((HALO, CONV_CH), F32)

    buf[HALO:HALO + ts, :] = z_ref[...]
    span = ts + HALO - SUBLANES
    for r in range(1, SUBLANES):
        shifted[r - 1, 0:span, :] = buf[r:r + span, :]

    def tap_rows(k, c):
        start = HALO - (CONV_WIDTH - 1) + k
        r = start % SUBLANES
        src = buf if r == 0 else shifted.at[r - 1]
        return src[start - r + c * chunk:start - r + (c + 1) * chunk, :]

    for c in range(ts // chunk):
        acc = jnp.zeros((chunk, CONV_CH), F32) + b_ref[...]
        for k in range(CONV_WIDTH):
            acc = acc + tap_rows(k, c) * w_ref[k:k + 1, :]
        mu = jnp.mean(acc, axis=-1, keepdims=True)
        d = acc - mu
        var = jnp.mean(d * d, axis=-1, keepdims=True)
        y = d * lax.rsqrt(var + EPS) * g_ref[...] + beta_ref[...]
        o_ref[c * chunk:(c + 1) * chunk, :] = (y * jax.nn.sigmoid(y)).astype(BF16)
    buf[0:HALO, :] = buf[ts:ts + HALO, :]


def _conformer_conv(glu, w, bias, g, beta, b, s):
    ts = min(TS_CONV, s)
    full = lambda a: pl.BlockSpec(a.shape, lambda bi, i: (0, 0))
    row = pl.BlockSpec((ts, CONV_CH), lambda bi, i: (bi * (s // ts) + i, 0))
    return pl.pallas_call(
        _conv_kernel,
        grid=(b, s // ts),
        in_specs=[row, full(w), full(bias), full(g), full(beta)],
        out_specs=row,
        out_shape=jax.ShapeDtypeStruct((b * s, CONV_CH), BF16),
        scratch_shapes=[pltpu.VMEM((HALO + ts, CONV_CH), F32),
                        pltpu.VMEM((SUBLANES - 1, HALO + ts, CONV_CH), F32)],
        compiler_params=_cparams(("arbitrary", "arbitrary")),
        name="conformer_conv",
    )(glu, w, bias, g, beta)


def _merge_router_kernel(x_ref, a_ref, c_ref, mg_ref, wa_ref, wc_ref, wo_ref, fg_ref, wr_ref, br_ref,
                         x1_o, h2_o, rt_o, cnt_o):
    tm = x_ref.shape[0]
    y_a = _dot(a_ref[...], wa_ref[...])
    y_c = _dot(c_ref[...], wc_ref[...])
    mix = mg_ref[:, 0:D_MODEL].astype(F32) * y_a + mg_ref[:, D_MODEL:2 * D_MODEL].astype(F32) * y_c
    x1 = x_ref[...] + _dot(mix.astype(BF16), wo_ref[...])
    x1_o[...] = x1
    h2 = x1 * lax.rsqrt(jnp.mean(x1 * x1, axis=-1, keepdims=True) + EPS) * fg_ref[...]
    h2_o[...] = h2

    hh, hl = _split(h2)
    logits = _dot(hh, wr_ref[0]) + _dot(hl, wr_ref[0]) + _dot(hh, wr_ref[1]) + br_ref[...]
    lane = lax.broadcasted_iota(I32, (tm, LANES), 1)
    is_g = (lane >= GROUP_LANE0) & (lane < GROUP_LANE0 + N_GROUPS)
    gmax = jnp.max(jnp.where(is_g, logits, -jnp.inf), axis=-1, keepdims=True)
    gsel = jnp.min(jnp.where(is_g & (logits == gmax), lane - GROUP_LANE0, N_GROUPS), axis=-1, keepdims=True)
    pg_sel = 1.0 / jnp.sum(jnp.where(is_g, jnp.exp(logits - gmax), 0.0), axis=-1, keepdims=True)
    in_grp = (lane < N_EXPERTS) & (lane // EXP_PER_GROUP == gsel)
    emax = jnp.max(jnp.where(in_grp, logits, -jnp.inf), axis=-1, keepdims=True)
    ee = jnp.where(in_grp, jnp.exp(logits - emax), 0.0)
    pe = jnp.where(in_grp, ee / jnp.sum(ee, axis=-1, keepdims=True), -1.0)
    p1 = jnp.max(pe, axis=-1, keepdims=True)
    i1 = jnp.min(jnp.where(pe == p1, lane, LANES), axis=-1, keepdims=True)
    pe2 = jnp.where(lane == i1, -1.0, pe)
    p2 = jnp.max(pe2, axis=-1, keepdims=True)
    i2 = jnp.min(jnp.where(pe2 == p2, lane, LANES), axis=-1, keepdims=True)
    w1 = pg_sel * p1 / (p1 + p2)
    w2 = pg_sel * p2 / (p1 + p2)

    hot1 = lane == i1
    hot2 = lane == i2
    ind = jnp.where(hot1 | hot2, 1.0, 0.0)
    tri = (lax.broadcasted_iota(I32, (tm, tm), 1) < lax.broadcasted_iota(I32, (tm, tm), 0)).astype(BF16)
    before = _dot(tri, ind.astype(BF16))
    k1 = jnp.sum(jnp.where(hot1, before, 0.0), axis=-1, keepdims=True)
    k2 = jnp.sum(jnp.where(hot2, before, 0.0), axis=-1, keepdims=True)
    rec = jnp.zeros((tm, LANES), F32)
    for slot, val in ((R_E1, i1.astype(F32)), (R_E2, i2.astype(F32)), (R_W1, w1), (R_W2, w2), (R_K1, k1), (R_K2, k2)):
        rec = jnp.where(lane == slot, val, rec)
    rt_o[...] = rec
    cnt_o[0] = jnp.broadcast_to(jnp.sum(ind, axis=0, keepdims=True), (SUBLANES, LANES))


def _merge_router(x2, attn, cact, mg, wa, wc, wo, fg, wr, br):
    t = x2.shape[0]
    tm = TR
    row = lambda n: pl.BlockSpec((tm, n), lambda i: (i, 0))
    full = lambda a: pl.BlockSpec(a.shape, lambda i: (0,) * a.ndim)
    return pl.pallas_call(
        _merge_router_kernel,
        grid=(t // tm,),
        in_specs=[row(D_MODEL), row(Q_COLS), row(CONV_CH), row(2 * D_MODEL), full(wa), full(wc), full(wo),
                  full(fg), full(wr), full(br)],
        out_specs=[row(D_MODEL), row(D_MODEL), row(LANES), pl.BlockSpec((1, SUBLANES, LANES), lambda i: (i, 0, 0))],
        out_shape=[jax.ShapeDtypeStruct((t, D_MODEL), F32), jax.ShapeDtypeStruct((t, D_MODEL), F32),
                   jax.ShapeDtypeStruct((t, LANES), F32), jax.ShapeDtypeStruct((t // tm, SUBLANES, LANES), F32)],
        compiler_params=_cparams(("parallel",)),
        name="merge_router",
    )(x2, attn, cact, mg, wa, wc, wo, fg, wr, br)


def _start_chunks(dst8_ref, tile, n_chunks, make_copy):
    def issue(c, carry):
        far = pl.multiple_of(dst8_ref[tile * NCH + c] * CHUNK, CHUNK)
        make_copy(pl.multiple_of(c * CHUNK, CHUNK), far).start()
        return carry

    lax.fori_loop(0, n_chunks, issue, 0)


def _wait_chunks(n_chunks, make_copy):
    def wait_one(c, carry):
        make_copy(0, 0).wait()
        return carry

    lax.fori_loop(0, n_chunks, wait_one, 0)


def _dispatch_kernel(dst8_ref, nct_ref, pad8_ref, npad_ref, nused_ref, h_ref, rt_ref, lo_ref, xg_ref, slot_o, xs,
                     zbuf, sem, zsem):
    tr = h_ref.shape[0]

    @pl.when(pl.program_id(0) == 0)
    def _():
        zbuf[...] = jnp.zeros(zbuf.shape, F32)

        def zero_chunk(far):
            return pltpu.make_async_copy(zbuf.at[pl.ds(0, CHUNK), :], xg_ref.at[pl.ds(far, CHUNK), :], zsem)

        def zero_block(blk):
            return pltpu.make_async_copy(zbuf, xg_ref.at[pl.ds(pl.multiple_of(blk * BM, BM), BM), :], sem.at[0])

        def per_expert(e, carry, wait):
            def one(c, inner):
                if wait:
                    zero_chunk(0).wait()
                else:
                    zero_chunk(pl.multiple_of((pad8_ref[e] + c) * CHUNK, CHUNK)).start()
                return inner
            return lax.fori_loop(0, npad_ref[e], one, carry)

        def tail(blk, carry, wait):
            if wait:
                zero_block(0).wait()
            else:
                zero_block(blk).start()
            return carry

        n_blocks = xg_ref.shape[0] // BM
        lax.fori_loop(0, N_EXPERTS, functools.partial(per_expert, wait=False), 0)
        lax.fori_loop(nused_ref[0], n_blocks, functools.partial(tail, wait=False), 0)
        lax.fori_loop(0, N_EXPERTS, functools.partial(per_expert, wait=True), 0)
        lax.fori_loop(nused_ref[0], n_blocks, functools.partial(tail, wait=True), 0)

    rt = rt_ref[...]
    lane = lax.broadcasted_iota(I32, (tr, LANES), 1)
    lane_f = lane.astype(F32)
    lo_row = lo_ref[0, 0:1, :]
    slots = []
    for e_lane, k_lane in ((R_E1, R_K1), (R_E2, R_K2)):
        hot = lane_f == rt[:, e_lane:e_lane + 1]
        slots.append(jnp.sum(jnp.where(hot, lo_row, 0.0), axis=-1, keepdims=True) + rt[:, k_lane:k_lane + 1])
    rec = jnp.where(lane == 0, slots[0], jnp.where(lane == 1, slots[1], 0.0))
    slot_o[...] = rec
    rec_t = jnp.concatenate([rec[c * LANES:(c + 1) * LANES].T for c in range(tr // LANES)], axis=1)
    row_id = lax.broadcasted_iota(I32, (R_LOCAL, tr), 0).astype(F32)
    pick = (row_id == rec_t[0:1, :]) | (row_id == rec_t[1:2, :])
    staged = _dot(jnp.where(pick, 1.0, 0.0).astype(BF16), h_ref[...].astype(BF16))

    i = pl.program_id(0)
    last = pl.num_programs(0) - 1
    slot = i % 2

    def copies_from(s):
        return lambda near, far: pltpu.make_async_copy(
            xs.at[s, pl.ds(near, CHUNK), :], xg_ref.at[pl.ds(far, CHUNK), :], sem.at[s])

    @pl.when(i >= 2)
    def _():
        _wait_chunks(nct_ref[i - 2], copies_from(slot))

    xs[slot] = staged
    _start_chunks(dst8_ref, i, nct_ref[i], copies_from(slot))

    @pl.when(i == last)
    def _():
        @pl.when(i >= 1)
        def _():
            _wait_chunks(nct_ref[i - 1], copies_from(1 - slot))
        _wait_chunks(nct_ref[i], copies_from(slot))


def _dispatch(dst8, nct, pad8, npad, nused, h2, route, lo_rows, p_rows):
    t = h2.shape[0]
    return pl.pallas_call(
        _dispatch_kernel,
        grid_spec=pltpu.PrefetchScalarGridSpec(
            num_scalar_prefetch=5,
            grid=(t // TR,),
            in_specs=[pl.BlockSpec((TR, D_MODEL), lambda i, *_: (i, 0)),
                      pl.BlockSpec((TR, LANES), lambda i, *_: (i, 0)),
                      pl.BlockSpec((1, SUBLANES, LANES), lambda i, *_: (i, 0, 0))],
            out_specs=[pl.BlockSpec(memory_space=pl.ANY),
                       pl.BlockSpec((TR, LANES), lambda i, *_: (i, 0))],
            scratch_shapes=[pltpu.VMEM((2, R_LOCAL, D_MODEL), F32), pltpu.VMEM((BM, D_MODEL), F32),
                            pltpu.SemaphoreType.DMA((2,)), pltpu.SemaphoreType.DMA(())],
        ),
        out_shape=[jax.ShapeDtypeStruct((p_rows, D_MODEL), F32), jax.ShapeDtypeStruct((t, LANES), F32)],
        compiler_params=_cparams(("arbitrary",)),
        name="moe_dispatch",
    )(dst8, nct, pad8, npad, nused, h2, route, lo_rows)


def _expert_kernel(blk_e_ref, nused_ref, x_ref, wg_ref, wu_ref, wd_ref, y_ref, wg_b, wu_b, wd_b):
    i = pl.program_id(0)
    used = i < nused_ref[0]

    @pl.when((i == 0) | (blk_e_ref[i] != blk_e_ref[jnp.maximum(i - 1, 0)]))
    def _():
        wg_b[...] = wg_ref[0].astype(BF16)
        wu_b[...] = wu_ref[0].astype(BF16)
        wd_b[...] = wd_ref[0].astype(BF16)

    @pl.when(used)
    def _():
        xb = x_ref[...].astype(BF16)
        a = _dot(xb, wg_b[...])
        u = _dot(xb, wu_b[...])
        y_ref[...] = _dot((a * jax.nn.sigmoid(a) * u).astype(BF16), wd_b[...])

    @pl.when(jnp.logical_not(used))
    def _():
        y_ref[...] = jnp.zeros(y_ref.shape, F32)


def _experts(blk_e, nused, xg, wg, wu, wd):
    p_rows = xg.shape[0]
    x_blk = lambda i, be, nu: (jnp.minimum(i, nu[0] - 1), 0)
    return pl.pallas_call(
        _expert_kernel,
        grid_spec=pltpu.PrefetchScalarGridSpec(
            num_scalar_prefetch=2,
            grid=(p_rows // BM,),
            in_specs=[pl.BlockSpec((BM, D_MODEL), x_blk),
                      pl.BlockSpec((1, D_MODEL, EXPERT_FF), lambda i, be, nu: (be[i], 0, 0)),
                      pl.BlockSpec((1, D_MODEL, EXPERT_FF), lambda i, be, nu: (be[i], 0, 0)),
                      pl.BlockSpec((1, EXPERT_FF, D_MODEL), lambda i, be, nu: (be[i], 0, 0))],
            out_specs=pl.BlockSpec((BM, D_MODEL), lambda i, be, nu: (i, 0)),
            scratch_shapes=[pltpu.VMEM((D_MODEL, EXPERT_FF), BF16), pltpu.VMEM((D_MODEL, EXPERT_FF), BF16),
                            pltpu.VMEM((EXPERT_FF, D_MODEL), BF16)],
        ),
        out_shape=jax.ShapeDtypeStruct((p_rows, D_MODEL), F32),
        compiler_params=_cparams(("arbitrary",)),
        name="moe_experts",
    )(blk_e, nused, xg, wg, wu, wd)


def _combine_ple_kernel(dst8_ref, nct_ref, x1_ref, rt_ref, slot_ref, p_ref, pg_ref, wpg_ref, bpg_ref, wpp_ref,
                        yg_ref, o_ref, ys, sem):
    tr = x1_ref.shape[0]
    i = pl.program_id(0)
    slot = i % 2

    def copies_into(s):
        return lambda near, far: pltpu.make_async_copy(
            yg_ref.at[pl.ds(far, CHUNK), :], ys.at[s, pl.ds(near, CHUNK), :], sem.at[s])

    def fetch(tile, s):
        _start_chunks(dst8_ref, tile, nct_ref[tile], copies_into(s))

        def clear(c, carry):
            ys[s, pl.ds(pl.multiple_of(c * CHUNK, CHUNK), CHUNK), :] = jnp.zeros((CHUNK, D_MODEL), F32)
            return carry

        lax.fori_loop(nct_ref[tile], NCH, clear, 0)

    @pl.when(i == 0)
    def _():
        fetch(0, 0)

    @pl.when(i + 1 < pl.num_programs(0))
    def _():
        fetch(i + 1, 1 - slot)

    _wait_chunks(nct_ref[i], copies_into(slot))

    rt = rt_ref[...]
    sl = slot_ref[...]
    col_id = lax.broadcasted_iota(I32, (tr, R_LOCAL), 1).astype(F32)
    yb = ys[slot].astype(BF16)
    y0 = _dot(jnp.where(col_id == sl[:, 0:1], 1.0, 0.0).astype(BF16), yb)
    y1 = _dot(jnp.where(col_id == sl[:, 1:2], 1.0, 0.0).astype(BF16), yb)
    x2 = x1_ref[...] + rt[:, R_W1:R_W1 + 1] * y0 + rt[:, R_W2:R_W2 + 1] * y1
    h3 = (x2 * lax.rsqrt(jnp.mean(x2 * x2, axis=-1, keepdims=True) + EPS) * pg_ref[...]).astype(BF16)
    gate = jax.nn.sigmoid(_dot(h3, wpg_ref[...]) + bpg_ref[...])
    o_ref[...] = x2 + gate * _dot(p_ref[...].astype(BF16), wpp_ref[...])


def _combine_ple(dst8, nct, x1, route, slots, p2, pg, wpg, bpg, wpp, yg):
    t = x1.shape[0]
    row = lambda n: pl.BlockSpec((TR, n), lambda i, d, c: (i, 0))
    full = lambda a: pl.BlockSpec(a.shape, lambda i, d, c: (0,) * a.ndim)
    return pl.pallas_call(
        _combine_ple_kernel,
        grid_spec=pltpu.PrefetchScalarGridSpec(
            num_scalar_prefetch=2,
            grid=(t // TR,),
            in_specs=[row(D_MODEL), row(LANES), row(LANES), row(PLE_DIM), full(pg), full(wpg), full(bpg),
                      full(wpp), pl.BlockSpec(memory_space=pl.ANY)],
            out_specs=row(D_MODEL),
            scratch_shapes=[pltpu.VMEM((2, R_LOCAL, D_MODEL), F32), pltpu.SemaphoreType.DMA((2,))],
        ),
        out_shape=jax.ShapeDtypeStruct((t, D_MODEL), F32),
        compiler_params=_cparams(("arbitrary",)),
        name="moe_combine_ple",
    )(dst8, nct, x1, route, slots, p2, pg, wpg, bpg, wpp, yg)


def _seg_sum_matrix():
    seg = np.arange(LANES) // HEAD_DIM
    return jnp.asarray(seg[:, None] == seg[None, :], BF16)


def _overlap_t(nc_rows, n_blk):
    nc = np.arange(nc_rows)[None, :] * CMP_STRIDE
    sb = np.arange(n_blk)[:, None] * SEL_BLOCK
    ov = (nc < sb + SEL_BLOCK) & (nc + CMP_BLOCK > sb) & (np.arange(nc_rows)[None, :] < nc_rows - 1)
    return jnp.asarray(ov, BF16)


def _compress_weights(pos, w1, w2):
    half = CMP_BLOCK // 2
    pe = jnp.broadcast_to(pos[:, None, :], (CMP_BLOCK, N_KV, HEAD_DIM)).reshape(2, half * N_KV * HEAD_DIM)
    w1r = w1.reshape(2, half, HEAD_DIM, CMP_HIDDEN)
    eye = jnp.eye(N_KV, dtype=w1.dtype)
    wx = jnp.einsum('hldj,gk->hlgdkj', w1r, eye).reshape(2, half * N_KV * HEAD_DIM, N_KV * CMP_HIDDEN)
    w2x = jnp.einsum('jd,gk->gjkd', w2, eye).reshape(N_KV * CMP_HIDDEN, N_KV * HEAD_DIM)
    return pe, wx[0].astype(BF16), wx[1].astype(BF16), w2x.astype(BF16)


def _layer(x, p_i, prm):
    b, s, _ = x.shape
    t = b * s
    x2 = x.reshape(t, D_MODEL)
    bd = _seg_sum_matrix()

    w = prm["w_in"]
    kv = lambda j: w[:, OFF_KV + j * KV_COLS:OFF_KV + (j + 1) * KV_COLS]
    w_nat = jnp.concatenate([kv(0), kv(1), kv(2), kv(4), w[:, OFF_CONV:]], axis=1).astype(BF16)
    w_t = jnp.concatenate([w[:, :Q_COLS], kv(3), kv(5),
                           jnp.pad(w[:, OFF_GATE:OFF_CONV], ((0, 0), (0, LANES - 3 * N_HEADS)))], axis=1).T.astype(BF16)
    qg = jnp.broadcast_to(prm["q_norm_g"][:, None], (HEAD_DIM, TM_PROJ))
    kg = jnp.tile(prm["k_norm_g"], (1, LANES // HEAD_DIM))
    qt, kc_raw, vc_raw, ks, vst, kw, vwt, gtt, glu, mg = _in_proj(
        x2, prm["attn_norm_g"][None, :], w_nat, w_t, qg, kg, bd, prm["merge_b"][None, :], s)

    nrow = s // CMP_STRIDE
    cw = [_compress_weights(prm["cmp_pos"][j], prm["cmp_w1"][j], prm["cmp_w2"][j]) for j in range(2)]
    pe_x, wa, wb = (jnp.stack([cw[0][j], cw[1][j]]) for j in range(3))
    kc, vct = _compress(kc_raw.reshape(b, nrow, CMP_STRIDE * LANES), vc_raw.reshape(b, nrow, CMP_STRIDE * LANES),
                        pe_x, wa, wb, cw[0][3], cw[1][3].T, bd, kg)

    attn = _attention(qt, gtt, kc, vct, ks.reshape(b, s, 4 * LANES), vst, kw.reshape(b, s, 2 * LANES), vwt,
                      _overlap_t(nrow, s // SEL_BLOCK), b, s)

    cact = _conformer_conv(glu, prm["conv_w"], prm["conv_b"][None, :], prm["conv_norm_g"][None, :],
                           prm["conv_norm_b"][None, :], b, s)

    wr_full = jnp.zeros((D_MODEL, LANES), F32)
    wr_full = wr_full.at[:, :N_EXPERTS].set(prm["w_router_expert"])
    wr_full = wr_full.at[:, GROUP_LANE0:GROUP_LANE0 + N_GROUPS].set(prm["w_router_group"])
    wr_hi = wr_full.astype(BF16)
    wr = jnp.stack([wr_hi, (wr_full - wr_hi.astype(F32)).astype(BF16)])
    br = jnp.zeros((1, LANES), F32)
    br = br.at[0, :N_EXPERTS].set(prm["b_router_expert"])
    br = br.at[0, GROUP_LANE0:GROUP_LANE0 + N_GROUPS].set(prm["b_router_group"])
    x1, h2, route, cnt = _merge_router(
        x2, attn, cact, mg, prm["w_attn_out"].astype(BF16), prm["w_conv_out"].astype(BF16),
        prm["w_out"].astype(BF16), prm["ffn_norm_g"][None, :], wr, br)

    n_tiles = t // TR
    counts = cnt[:, 0, :N_EXPERTS].astype(I32)
    seg = (counts + CHUNK - 1) // CHUNK * CHUNK
    local_off = jnp.cumsum(seg, axis=1) - seg
    nct = (jnp.sum(seg, axis=1) // CHUNK).astype(I32)
    total = jnp.sum(seg, axis=0)
    padded = (total + BM - 1) // BM * BM
    pends = jnp.cumsum(padded)
    far_off = (pends - padded)[None, :] + jnp.cumsum(seg, axis=0) - seg
    chunk = jnp.arange(NCH, dtype=I32)[None, :, None]
    lo8 = (local_off // CHUNK)[:, None, :]
    in_seg = (chunk >= lo8) & (chunk < ((local_off + seg) // CHUNK)[:, None, :])
    dst8 = (jnp.sum(jnp.where(in_seg, (far_off // CHUNK)[:, None, :] - lo8, 0), axis=-1)
            + chunk[:, :, 0]).astype(I32).reshape(n_tiles * NCH)
    lo_rows = jnp.broadcast_to(jnp.pad(local_off.astype(F32), ((0, 0), (0, LANES - N_EXPERTS)))[:, None, :],
                               (n_tiles, SUBLANES, LANES))
    n_blk = -(-(2 * t + n_tiles * N_EXPERTS * (CHUNK - 1) + N_EXPERTS * (BM - 1)) // BM)
    blk_e = jnp.minimum(jnp.sum(jnp.arange(n_blk, dtype=I32)[:, None] * BM >= pends[None, :], axis=1),
                        N_EXPERTS - 1).astype(I32)
    nused = (pends[-1:] // BM).astype(I32)
    pad8 = ((pends - padded + total) // CHUNK).astype(I32)
    npad = ((padded - total) // CHUNK).astype(I32)

    xg, slots = _dispatch(dst8, nct, pad8, npad, nused, h2, route, lo_rows, n_blk * BM)
    yg = _experts(blk_e, nused, xg, prm["w_exp_gate"], prm["w_exp_up"], prm["w_exp_down"])
    out = _combine_ple(dst8, nct, x1, route, slots, p_i.reshape(t, PLE_DIM), prm["ple_norm_g"][None, :],
                       prm["w_ple_gate"].astype(BF16), prm["b_ple_gate"][None, :], prm["w_ple_proj"].astype(BF16), yg)
    return out.reshape(b, s, D_MODEL)


def kernel(x, p, attn_norm_g, w_in, q_norm_g, k_norm_g, cmp_pos, cmp_w1, cmp_w2, w_attn_out, conv_w, conv_b, conv_norm_g, conv_norm_b, w_conv_out, merge_b, w_out, ffn_norm_g, w_router_group, b_router_group, w_router_expert, b_router_expert, w_exp_gate, w_exp_up, w_exp_down, ple_norm_g, w_ple_gate, b_ple_gate, w_ple_proj):
    stacked = dict(attn_norm_g=attn_norm_g, w_in=w_in, q_norm_g=q_norm_g, k_norm_g=k_norm_g, cmp_pos=cmp_pos,
                   cmp_w1=cmp_w1, cmp_w2=cmp_w2, w_attn_out=w_attn_out, conv_w=conv_w, conv_b=conv_b,
                   conv_norm_g=conv_norm_g, conv_norm_b=conv_norm_b, w_conv_out=w_conv_out, merge_b=merge_b,
                   w_out=w_out, ffn_norm_g=ffn_norm_g, w_router_group=w_router_group,
                   b_router_group=b_router_group, w_router_expert=w_router_expert,
                   b_router_expert=b_router_expert, w_exp_gate=w_exp_gate, w_exp_up=w_exp_up,
                   w_exp_down=w_exp_down, ple_norm_g=ple_norm_g, w_ple_gate=w_ple_gate, b_ple_gate=b_ple_gate,
                   w_ple_proj=w_ple_proj)
    for i in range(w_in.shape[0]):
        x = _layer(x, p[i], {k: v[i] for k, v in stacked.items()})
    return x
```

```python
import functools

import numpy as np
import jax
import jax.numpy as jnp
from jax import lax
from jax.experimental import pallas as pl
from jax.experimental.pallas import tpu as pltpu

F32 = jnp.float32
BF16 = jnp.bfloat16
I32 = jnp.int32

D_MODEL = 1024
PLE_DIM = 256
N_HEADS = 8
N_KV = 2
HPG = N_HEADS // N_KV
HEAD_DIM = 64
CMP_BLOCK = 32
CMP_STRIDE = 16
CMP_HIDDEN = 128
SEL_BLOCK = 64
N_SEL = 8
WINDOW = 512
CONV_CH = 512
CONV_WIDTH = 31
N_GROUPS = 4
EXP_PER_GROUP = 8
N_EXPERTS = N_GROUPS * EXP_PER_GROUP
EXPERT_FF = 512
EPS = 1e-6
NEG = -1e30
BIG = 1e4
Q_COLS = N_HEADS * HEAD_DIM
KV_COLS = N_KV * HEAD_DIM
OFF_KV = Q_COLS
OFF_GATE = OFF_KV + 6 * KV_COLS
OFF_CONV = OFF_GATE + 3 * N_HEADS
OFF_MERGE = OFF_CONV + 2 * CONV_CH

LANES = 128
SUBLANES = 8
VMEM_LIMIT = 56 * 1024 * 1024

TM_PROJ = 256
QB = 128
KT = 256
TS_CONV = 512
HALO = 32
TR = 512
BM = 256
CHUNK = SUBLANES
R_LOCAL = 2 * TR + N_EXPERTS * CHUNK
NCH = R_LOCAL // CHUNK

N_KC = 0
N_VC = N_KC + LANES
N_KS = N_VC + LANES
N_KW = N_KS + LANES
N_CONV = N_KW + LANES
N_MERGE = N_CONV + 2 * CONV_CH
T_Q = 0
T_VS = T_Q + Q_COLS
T_VW = T_VS + LANES
T_GATE = T_VW + LANES

POS_HI_LANE = HEAD_DIM
POS_LO_LANE = HEAD_DIM + 1
UNSELECTED = -(2.0 ** 40)

R_E1, R_E2, R_W1, R_W2, R_K1, R_K2 = 0, 1, 2, 3, 4, 5
GROUP_LANE0 = N_EXPERTS


def _cparams(sem, vmem=VMEM_LIMIT):
    return pltpu.CompilerParams(dimension_semantics=sem, vmem_limit_bytes=vmem)


def _dot(a, b):
    return jnp.dot(a, b, preferred_element_type=F32)


def _dot_nt(a, b):
    return lax.dot_general(a, b, (((1,), (1,)), ((), ())), preferred_element_type=F32)


def _split(x):
    hi = x.astype(BF16)
    lo = (x - hi.astype(F32)).astype(BF16)
    return hi, lo


def _seg_rmsnorm(z, bd):
    hi, lo = _split(z * z)
    ss = _dot(hi, bd) + _dot(lo, bd)
    return z * lax.rsqrt(ss * (1.0 / HEAD_DIM) + EPS)


def _with_pos_lanes(kn, pos):
    lane = lax.broadcasted_iota(I32, kn.shape, 1)
    hi = pos // SEL_BLOCK * SEL_BLOCK
    pos_lanes = jnp.where(lane == POS_HI_LANE, hi.astype(F32),
                          jnp.where(lane == POS_LO_LANE, (pos - hi).astype(F32), 0.0))
    lo_half = lane < HEAD_DIM
    return (jnp.where(lo_half, kn, pos_lanes).astype(BF16),
            jnp.where(lo_half, pltpu.roll(kn, HEAD_DIM, 1), pos_lanes).astype(BF16))


def _in_proj_kernel(x_ref, g_ref, w_ref, wt_ref, qg_ref, kg_ref, bd_ref, mb_ref,
                    qt_o, kc_o, vc_o, ks_o, vst_o, kw_o, vwt_o, gtt_o, glu_o, mg_o, *, seq_len):
    tm = x_ref.shape[0]
    x = x_ref[...]
    h = (x * lax.rsqrt(jnp.mean(x * x, axis=-1, keepdims=True) + EPS) * g_ref[...]).astype(BF16)
    bd = bd_ref[...]
    lane = lax.broadcasted_iota(I32, (tm, LANES), 1)

    def proj(a, b):
        return _dot(h, w_ref[:, a:b])

    zt = _dot_nt(wt_ref[...], h)
    for hd in range(N_HEADS):
        zh = zt[T_Q + hd * HEAD_DIM:T_Q + (hd + 1) * HEAD_DIM, :]
        ms = jnp.sum(zh * zh, axis=0, keepdims=True) * (1.0 / HEAD_DIM)
        qt_o[hd * HEAD_DIM:(hd + 1) * HEAD_DIM, :] = (
            zh * lax.rsqrt(ms + EPS) * qg_ref[...] * (HEAD_DIM ** -0.5)).astype(BF16)
    ones = jnp.ones((HEAD_DIM, tm), F32)
    for off, v_o in ((T_VS, vst_o), (T_VW, vwt_o)):
        v_o[0] = jnp.concatenate([zt[off:off + HEAD_DIM, :], ones, zt[off + HEAD_DIM:off + LANES, :], ones],
                                 axis=0).astype(BF16)
    gtt_o[...] = jax.nn.sigmoid(zt[T_GATE:T_GATE + LANES, :])

    kc_o[...] = proj(N_KC, N_KC + LANES)
    vc_o[...] = proj(N_VC, N_VC + LANES)

    pos = (pl.program_id(0) * tm) % seq_len + lax.broadcasted_iota(I32, (tm, LANES), 0)
    onehot = jnp.where(lane == pos // SEL_BLOCK, 1.0, 0.0).astype(BF16)

    def key_tiles(zk, gain):
        return _with_pos_lanes(_seg_rmsnorm(zk, bd) * gain, pos)

    k0, k1 = key_tiles(proj(N_KS, N_KS + LANES), kg_ref[1:2, :])
    ks_o[...] = jnp.concatenate([k0, onehot, k1, onehot], axis=1)
    k0, k1 = key_tiles(proj(N_KW, N_KW + LANES), kg_ref[2:3, :])
    kw_o[...] = jnp.concatenate([k0, k1], axis=1)
    za = proj(N_CONV, N_CONV + CONV_CH)
    zg = proj(N_CONV + CONV_CH, N_CONV + 2 * CONV_CH)
    glu_o[...] = za * jax.nn.sigmoid(zg)
    for c in range(2):
        zm = proj(N_MERGE + c * D_MODEL, N_MERGE + (c + 1) * D_MODEL)
        mg_o[:, c * D_MODEL:(c + 1) * D_MODEL] = jax.nn.sigmoid(
            zm + mb_ref[:, c * D_MODEL:(c + 1) * D_MODEL]).astype(BF16)


def _in_proj(x2, norm_g, w_nat, w_t, qg, kg, bd, mb, seq_len):
    t = x2.shape[0]
    tm = TM_PROJ
    assert tm == KT and seq_len % tm == 0 and seq_len // SEL_BLOCK <= HEAD_DIM
    row = lambda n: pl.BlockSpec((tm, n), lambda i: (i, 0))
    col = lambda n: pl.BlockSpec((n, tm), lambda i: (0, i))
    vt = pl.BlockSpec((1, 2 * LANES, tm), lambda i: (i, 0, 0))
    full = lambda a: pl.BlockSpec(a.shape, lambda i: (0,) * a.ndim)
    sds = jax.ShapeDtypeStruct
    return pl.pallas_call(
        functools.partial(_in_proj_kernel, seq_len=seq_len),
        grid=(t // tm,),
        in_specs=[row(D_MODEL), full(norm_g), full(w_nat), full(w_t), full(qg), full(kg), full(bd), full(mb)],
        out_specs=[col(Q_COLS), row(LANES), row(LANES), row(4 * LANES), vt, row(2 * LANES), vt, col(LANES),
                   row(CONV_CH), row(2 * D_MODEL)],
        out_shape=[sds((Q_COLS, t), BF16), sds((t, LANES), F32), sds((t, LANES), F32), sds((t, 4 * LANES), BF16),
                   sds((t // tm, 2 * LANES, tm), BF16), sds((t, 2 * LANES), BF16), sds((t // tm, 2 * LANES, tm), BF16),
                   sds((LANES, t), F32), sds((t, CONV_CH), F32), sds((t, 2 * D_MODEL), BF16)],
        compiler_params=_cparams(("parallel",)),
        name="in_proj",
    )(x2, norm_g, w_nat, w_t, qg, kg, bd, mb)


def _compress_kernel(ck_ref, cv_ref, pe_ref, wa_ref, wb_ref, w2k_ref, w2vt_ref, bd_ref, kg_ref, kc_o, vct_o):
    nrow = ck_ref.shape[1]

    def hidden(idx, c_ref):
        c = c_ref[0]
        ya = _dot((c + pe_ref[idx, 0:1, :]).astype(BF16), wa_ref[idx])
        yb = _dot((c + pe_ref[idx, 1:2, :]).astype(BF16), wb_ref[idx])
        return jax.nn.gelu(ya + pltpu.roll(yb, nrow - 1, 0)).astype(BF16)

    kn = _seg_rmsnorm(_dot(hidden(0, ck_ref), w2k_ref[...]), bd_ref[...]) * kg_ref[0:1, :]
    c_end = lax.broadcasted_iota(I32, (nrow, LANES), 0) * CMP_STRIDE + (CMP_BLOCK - 1)
    kc_o[0] = jnp.concatenate(_with_pos_lanes(kn, c_end), axis=1)
    vct_o[0] = _dot_nt(w2vt_ref[...], hidden(1, cv_ref)).astype(BF16)


def _compress(kc_raw, vc_raw, pe_x, wa, wb, w2k, w2vt, bd, kg):
    b, nrow, width = kc_raw.shape
    batch = lambda r, n: pl.BlockSpec((1, r, n), lambda i: (i, 0, 0))
    full = lambda a: pl.BlockSpec(a.shape, lambda i: (0,) * a.ndim)
    return pl.pallas_call(
        _compress_kernel,
        grid=(b,),
        in_specs=[batch(nrow, width), batch(nrow, width), full(pe_x), full(wa), full(wb), full(w2k), full(w2vt),
                  full(bd), full(kg)],
        out_specs=[batch(nrow, 2 * LANES), batch(LANES, nrow)],
        out_shape=[jax.ShapeDtypeStruct((b, nrow, 2 * LANES), BF16), jax.ShapeDtypeStruct((b, LANES, nrow), BF16)],
        compiler_params=_cparams(("parallel",)),
        name="compress",
    )(kc_raw, vc_raw, pe_x, wa, wb, w2k, w2vt, bd, kg)


def _safe_inv(l):
    pos = l > 0.0
    return jnp.where(pos, 1.0 / jnp.where(pos, l, 1.0), 0.0)


def _halve_rows(x, op):
    rows = x.shape[0]
    while rows > SUBLANES:
        rows //= 2
        x = op(x[:rows], x[rows:])
    return x


def _attn_kernel(qt_ref, gtt_ref, kc_ref, vct_ref, ks_ref, vst_ref, kw_ref, vwt_ref, ovt_ref, o_ref,
                 tiles_ref, sa_ref, sb_ref, m_ref, acc_ref):
    i = pl.program_id(1)
    q0 = i * QB
    nc_rows = kc_ref.shape[1]
    n_blk = ovt_ref.shape[0]
    pw = 2 * QB
    n_pairs = N_KV * HPG // 2
    rel = ((lax.broadcasted_iota(I32, (KT, pw), 1) & (QB - 1))
           - lax.broadcasted_iota(I32, (KT, pw), 0)).astype(F32)
    slope_rows = lax.broadcasted_iota(I32, (HEAD_DIM, QB), 0) < 2
    gtt = gtt_ref[...]

    pairs = []
    for pr in range(n_pairs):
        q_pos = []
        for hd in (2 * pr, 2 * pr + 1):
            slope = jnp.where(slope_rows, 2.0 ** -(hd + 1), 0.0).astype(BF16)
            q_pos.append(jnp.concatenate([qt_ref[hd * HEAD_DIM:(hd + 1) * HEAD_DIM, :], slope], axis=0))
        pairs.append(jnp.concatenate(q_pos, axis=1))

    def update(s, vt, m, acc):
        m_new = jnp.maximum(m, jnp.max(_halve_rows(s, jnp.maximum), axis=0, keepdims=True))
        p = jnp.exp(s - m_new).astype(BF16)
        return m_new, jnp.exp(m - m_new) * acc + _dot(vt, p)

    def scores(j, q_list, k_ref, k_width):
        k0 = pl.multiple_of(j * KT, KT)
        return tuple(_dot(k_ref[0, pl.ds(k0, KT), (c // (HPG // 2)) * k_width:(c // (HPG // 2) + 1) * k_width],
                          q_list[c]) for c in range(n_pairs))

    def consume(j, s_list, state, vt_ref, mask_fn, extra_dist=0.0):
        keep = None if mask_fn is None else mask_fn(rel + ((q0 - j * KT).astype(F32) + extra_dist))
        out = []
        for c in range(n_pairs):
            g = c // (HPG // 2)
            s = s_list[c] if keep is None else jnp.where(keep, s_list[c], NEG)
            out += update(s, vt_ref[j, g * LANES:(g + 1) * LANES, :], *state[2 * c:2 * c + 2])
        return tuple(out)

    j_diag = (q0 + QB - 1) // KT
    init = (jnp.full((1, pw), NEG, F32), jnp.zeros((LANES, pw), F32)) * n_pairs

    def selected_branch(q_list, n_act):
        def fill(buf, n):
            for c, s in enumerate(scores(tiles_ref[n], q_list, ks_ref, 2 * LANES)):
                buf[c] = s

        def use(buf, n, mask_fn):
            state = []
            for c in range(n_pairs):
                state += [m_ref[c, 0:1, :], acc_ref[c]]
            out = consume(tiles_ref[n], [buf[c] for c in range(n_pairs)], state, vst_ref, mask_fn)
            for c in range(n_pairs):
                m_ref[c, 0:1, :] = out[2 * c]
                acc_ref[c] = out[2 * c + 1]

        for c in range(n_pairs):
            m_ref[c] = jnp.full((SUBLANES, pw), NEG, F32)
            acc_ref[c] = jnp.zeros((LANES, pw), F32)
        fill(sa_ref, 0)

        def two_tiles(h, carry):
            n = 2 * h
            fill(sb_ref, n + 1)
            use(sa_ref, n, None)

            @pl.when(n + 1 < n_act)
            def _():
                fill(sa_ref, n + 2)
                use(sb_ref, n + 1, None)
            return carry

        lax.fori_loop(0, (n_act + 1) // 2, two_tiles, 0)

        @pl.when(n_act % 2 == 1)
        def _():
            for c in range(n_pairs):
                sa_ref[c] = sb_ref[c]

        use(sa_ref, n_act, lambda d: d >= 0.0)
        return [acc_ref[c] for c in range(n_pairs)]

    in_window = lambda d: (d >= 0.0) & (d < float(WINDOW))
    win_out = init
    for w in range(WINDOW // KT + 1):
        jw = j_diag - WINDOW // KT + w
        before_start = jnp.where(jw < 0, float(WINDOW + KT), 0.0)
        jc = jnp.maximum(jw, 0)
        win_out = consume(jc, scores(jc, pairs, kw_ref, LANES), win_out, vwt_ref, in_window, before_start)

    cidx = lax.broadcasted_iota(I32, (nc_rows, pw), 0)
    c_end = cidx * CMP_STRIDE + (CMP_BLOCK - 1)
    cmask = (c_end <= q0 + (lax.broadcasted_iota(I32, (nc_rows, pw), 1) & (QB - 1))) & (cidx < nc_rows - 1)

    o_cmp, q_sel = [], []
    any_sel = jnp.zeros((n_blk, QB), F32)
    for g in range(N_KV):
        p_sum = jnp.zeros((nc_rows, QB), F32)
        for pr in range(g * (HPG // 2), (g + 1) * (HPG // 2)):
            s = jnp.where(cmask, _dot(kc_ref[0, :, g * LANES:(g + 1) * LANES], pairs[pr]), NEG)
            e = jnp.where(cmask, jnp.exp(s - jnp.max(_halve_rows(s, jnp.maximum), axis=0, keepdims=True)), 0.0)
            p = e * _safe_inv(jnp.sum(_halve_rows(e, jnp.add), axis=0, keepdims=True))
            o_cmp.append(_dot(vct_ref[0, g * HEAD_DIM:(g + 1) * HEAD_DIM, :], p.astype(BF16)))
            p_sum = p_sum + p[:, :QB] + p[:, QB:]
        ph, plo = _split(p_sum)
        imp_t = _dot(ovt_ref[...], ph) + _dot(ovt_ref[...], plo)

        blk = lax.broadcasted_iota(I32, (n_blk, QB), 0)
        tl = q0 + lax.broadcasted_iota(I32, (n_blk, QB), 1)
        valid = blk * SEL_BLOCK <= tl
        cur = tl // SEL_BLOCK
        forced = (blk == 0) | (blk == cur) | (blk == cur - 1)
        score = jnp.where(valid, imp_t + jnp.where(forced, BIG, 0.0), -BIG)
        sel = jnp.zeros((n_blk, QB), jnp.bool_)
        for _ in range(min(N_SEL, n_blk)):
            mx = jnp.max(score, axis=0, keepdims=True)
            first = jnp.min(jnp.where(score == mx, blk, n_blk), axis=0, keepdims=True)
            pick = blk == first
            sel = sel | pick
            score = jnp.where(pick, -jnp.inf, score)
        any_sel = any_sel + jnp.where(sel & valid, 1.0, 0.0)
        unsel = jnp.where(sel & valid, 0.0, UNSELECTED).astype(BF16)
        bias = jnp.concatenate([unsel, jnp.zeros((2 * HEAD_DIM - n_blk, QB), BF16)], axis=0)
        for pr in range(g * (HPG // 2), (g + 1) * (HPG // 2)):
            q_sel.append(jnp.concatenate([pairs[pr], jnp.concatenate([bias, bias], axis=1)], axis=0))

    per_blk = jnp.sum(any_sel, axis=1, keepdims=True)
    n_act = jnp.int32(0)
    for jt in range(n_blk * SEL_BLOCK // KT):
        blocks = slice(jt * (KT // SEL_BLOCK), (jt + 1) * (KT // SEL_BLOCK))
        tiles_ref[n_act] = jnp.int32(jt)
        n_act = n_act + ((jnp.sum(per_blk[blocks]) > 0.0) & (jt < j_diag)).astype(I32)
    tiles_ref[n_act] = j_diag

    sel_acc = selected_branch(q_sel, n_act)

    for c in range(n_pairs):
        o_s = sel_acc[c][:HEAD_DIM] / sel_acc[c][HEAD_DIM:HEAD_DIM + 1]
        o_w = win_out[2 * c + 1][:HEAD_DIM] / win_out[2 * c + 1][HEAD_DIM:HEAD_DIM + 1]
        halves = []
        for k in range(2):
            g0 = 3 * (2 * c + k)
            lanes = slice(k * QB, (k + 1) * QB)
            halves.append(gtt[g0:g0 + 1, :] * o_cmp[c][:, lanes] + gtt[g0 + 1:g0 + 2, :] * o_s[:, lanes]
                          + gtt[g0 + 2:g0 + 3, :] * o_w[:, lanes])
        o_ref[:, c * LANES:(c + 1) * LANES] = jnp.concatenate(halves, axis=0).T.astype(BF16)


def _attention(qt, gtt, kc, vct, ks, vst, kw, vwt, ovt, b, s):
    n_q = s // QB
    per_q = lambda n: pl.BlockSpec((n, QB), lambda bi, i: (0, bi * n_q + i))
    per_b = lambda a: pl.BlockSpec((1,) + a.shape[1:], lambda bi, i: (bi, 0, 0))
    vt = pl.BlockSpec((s // KT, 2 * LANES, KT), lambda bi, i: (bi, 0, 0))
    return pl.pallas_call(
        _attn_kernel,
        grid=(b, n_q),
        in_specs=[per_q(Q_COLS), per_q(LANES), per_b(kc), per_b(vct), per_b(ks), vt, per_b(kw), vt,
                  pl.BlockSpec(ovt.shape, lambda bi, i: (0, 0))],
        out_specs=pl.BlockSpec((QB, Q_COLS), lambda bi, i: (bi * n_q + i, 0)),
        out_shape=jax.ShapeDtypeStruct((b * s, Q_COLS), BF16),
        scratch_shapes=[pltpu.SMEM((s // KT + 1,), I32),
                        pltpu.VMEM((N_HEADS // 2, KT, 2 * QB), F32), pltpu.VMEM((N_HEADS // 2, KT, 2 * QB), F32),
                        pltpu.VMEM((N_HEADS // 2, SUBLANES, 2 * QB), F32),
                        pltpu.VMEM((N_HEADS // 2, LANES, 2 * QB), F32)],
        compiler_params=_cparams(("parallel", "parallel")),
        name="nsa_attention",
    )(qt, gtt, kc, vct, ks, vst, kw, vwt, ovt)


def _conv_kernel(z_ref, w_ref, b_ref, g_ref, beta_ref, o_ref, buf, shifted):
    ts = z_ref.shape[0]
    chunk = 64

    @pl.when(pl.program_id(1) == 0)
    def _():
        buf[0:HALO, :] = jnp.zeros((HALO, CONV_CH), F32)

    buf[HALO:HALO + ts, :] = z_ref[...]
    span = ts + HALO - SUBLANES
    for r in range(1, SUBLANES):
        shifted[r - 1, 0:span, :] = buf[r:r + span, :]

    def tap_rows(k, c):
        start = HALO - (CONV_WIDTH - 1) + k
        r = start % SUBLANES
        src = buf if r == 0 else shifted.at[r - 1]
        return src[start - r + c * chunk:start - r + (c + 1) * chunk, :]

    for c in range(ts // chunk):
        acc = jnp.zeros((chunk, CONV_CH), F32) + b_ref[...]
        for k in range(CONV_WIDTH):
            acc = acc + tap_rows(k, c) * w_ref[k:k + 1, :]
        mu = jnp.mean(acc, axis=-1, keepdims=True)
        d = acc - mu
        var = jnp.mean(d * d, axis=-1, keepdims=True)
        y = d * lax.rsqrt(var + EPS) * g_ref[...] + beta_ref[...]
        o_ref[c * chunk:(c + 1) * chunk, :] = (y * jax.nn.sigmoid(y)).astype(BF16)
    buf[0:HALO, :] = buf[ts:ts + HALO, :]


def _conformer_conv(glu, w, bias, g, beta, b, s):
    ts = min(TS_CONV, s)
    full = lambda a: pl.BlockSpec(a.shape, lambda bi, i: (0, 0))
    row = pl.BlockSpec((ts, CONV_CH), lambda bi, i: (bi * (s // ts) + i, 0))
    return pl.pallas_call(
        _conv_kernel,
        grid=(b, s // ts),
        in_specs=[row, full(w), full(bias), full(g), full(beta)],
        out_specs=row,
        out_shape=jax.ShapeDtypeStruct((b * s, CONV_CH), BF16),
        scratch_shapes=[pltpu.VMEM((HALO + ts, CONV_CH), F32),
                        pltpu.VMEM((SUBLANES - 1, HALO + ts, CONV_CH), F32)],
        compiler_params=_cparams(("arbitrary", "arbitrary")),
        name="conformer_conv",
    )(glu, w, bias, g, beta)


def _merge_router_kernel(x_ref, a_ref, c_ref, mg_ref, wa_ref, wc_ref, wo_ref, fg_ref, wr_ref, br_ref,
                         x1_o, h2_o, rt_o, cnt_o):
    tm = x_ref.shape[0]
    y_a = _dot(a_ref[...], wa_ref[...])
    y_c = _dot(c_ref[...], wc_ref[...])
    mix = mg_ref[:, 0:D_MODEL].astype(F32) * y_a + mg_ref[:, D_MODEL:2 * D_MODEL].astype(F32) * y_c
    x1 = x_ref[...] + _dot(mix.astype(BF16), wo_ref[...])
    x1_o[...] = x1
    h2 = x1 * lax.rsqrt(jnp.mean(x1 * x1, axis=-1, keepdims=True) + EPS) * fg_ref[...]
    h2_o[...] = h2

    hh, hl = _split(h2)
    logits = _dot(hh, wr_ref[0]) + _dot(hl, wr_ref[0]) + _dot(hh, wr_ref[1]) + br_ref[...]
    lane = lax.broadcasted_iota(I32, (tm, LANES), 1)
    is_g = (lane >= GROUP_LANE0) & (lane < GROUP_LANE0 + N_GROUPS)
    gmax = jnp.max(jnp.where(is_g, logits, -jnp.inf), axis=-1, keepdims=True)
    gsel = jnp.min(jnp.where(is_g & (logits == gmax), lane - GROUP_LANE0, N_GROUPS), axis=-1, keepdims=True)
    pg_sel = 1.0 / jnp.sum(jnp.where(is_g, jnp.exp(logits - gmax), 0.0), axis=-1, keepdims=True)
    in_grp = (lane < N_EXPERTS) & (lane // EXP_PER_GROUP == gsel)
    emax = jnp.max(jnp.where(in_grp, logits, -jnp.inf), axis=-1, keepdims=True)
    ee = jnp.where(in_grp, jnp.exp(logits - emax), 0.0)
    pe = jnp.where(in_grp, ee / jnp.sum(ee, axis=-1, keepdims=True), -1.0)
    p1 = jnp.max(pe, axis=-1, keepdims=True)
    i1 = jnp.min(jnp.where(pe == p1, lane, LANES), axis=-1, keepdims=True)
    pe2 = jnp.where(lane == i1, -1.0, pe)
    p2 = jnp.max(pe2, axis=-1, keepdims=True)
    i2 = jnp.min(jnp.where(pe2 == p2, lane, LANES), axis=-1, keepdims=True)
    w1 = pg_sel * p1 / (p1 + p2)
    w2 = pg_sel * p2 / (p1 + p2)

    hot1 = lane == i1
    hot2 = lane == i2
    ind = jnp.where(hot1 | hot2, 1.0, 0.0)
    tri = (lax.broadcasted_iota(I32, (tm, tm), 1) < lax.broadcasted_iota(I32, (tm, tm), 0)).astype(BF16)
    before = _dot(tri, ind.astype(BF16))
    k1 = jnp.sum(jnp.where(hot1, before, 0.0), axis=-1, keepdims=True)
    k2 = jnp.sum(jnp.where(hot2, before, 0.0), axis=-1, keepdims=True)
    rec = jnp.zeros((tm, LANES), F32)
    for slot, val in ((R_E1, i1.astype(F32)), (R_E2, i2.astype(F32)), (R_W1, w1), (R_W2, w2), (R_K1, k1), (R_K2, k2)):
        rec = jnp.where(lane == slot, val, rec)
    rt_o[...] = rec
    cnt_o[0] = jnp.broadcast_to(jnp.sum(ind, axis=0, keepdims=True), (SUBLANES, LANES))


def _merge_router(x2, attn, cact, mg, wa, wc, wo, fg, wr, br):
    t = x2.shape[0]
    tm = TR
    row = lambda n: pl.BlockSpec((tm, n), lambda i: (i, 0))
    full = lambda a: pl.BlockSpec(a.shape, lambda i: (0,) * a.ndim)
    return pl.pallas_call(
        _merge_router_kernel,
        grid=(t // tm,),
        in_specs=[row(D_MODEL), row(Q_COLS), row(CONV_CH), row(2 * D_MODEL), full(wa), full(wc), full(wo),
                  full(fg), full(wr), full(br)],
        out_specs=[row(D_MODEL), row(D_MODEL), row(LANES), pl.BlockSpec((1, SUBLANES, LANES), lambda i: (i, 0, 0))],
        out_shape=[jax.ShapeDtypeStruct((t, D_MODEL), F32), jax.ShapeDtypeStruct((t, D_MODEL), F32),
                   jax.ShapeDtypeStruct((t, LANES), F32), jax.ShapeDtypeStruct((t // tm, SUBLANES, LANES), F32)],
        compiler_params=_cparams(("parallel",)),
        name="merge_router",
    )(x2, attn, cact, mg, wa, wc, wo, fg, wr, br)


def _start_chunks(dst8_ref, tile, n_chunks, make_copy):
    def issue(c, carry):
        far = pl.multiple_of(dst8_ref[tile * NCH + c] * CHUNK, CHUNK)
        make_copy(pl.multiple_of(c * CHUNK, CHUNK), far).start()
        return carry

    lax.fori_loop(0, n_chunks, issue, 0)


def _wait_chunks(n_chunks, make_copy):
    def wait_one(c, carry):
        make_copy(0, 0).wait()
        return carry

    lax.fori_loop(0, n_chunks, wait_one, 0)


def _dispatch_kernel(dst8_ref, nct_ref, pad8_ref, npad_ref, nused_ref, h_ref, rt_ref, lo_ref, xg_ref, slot_o, xs,
                     zbuf, sem, zsem):
    tr = h_ref.shape[0]

    @pl.when(pl.program_id(0) == 0)
    def _():
        zbuf[...] = jnp.zeros(zbuf.shape, F32)

        def zero_chunk(far):
            return pltpu.make_async_copy(zbuf.at[pl.ds(0, CHUNK), :], xg_ref.at[pl.ds(far, CHUNK), :], zsem)

        def zero_block(blk):
            return pltpu.make_async_copy(zbuf, xg_ref.at[pl.ds(pl.multiple_of(blk * BM, BM), BM), :], sem.at[0])

        def per_expert(e, carry, wait):
            def one(c, inner):
                if wait:
                    zero_chunk(0).wait()
                else:
                    zero_chunk(pl.multiple_of((pad8_ref[e] + c) * CHUNK, CHUNK)).start()
                return inner
            return lax.fori_loop(0, npad_ref[e], one, carry)

        def tail(blk, carry, wait):
            if wait:
                zero_block(0).wait()
            else:
                zero_block(blk).start()
            return carry

        n_blocks = xg_ref.shape[0] // BM
        lax.fori_loop(0, N_EXPERTS, functools.partial(per_expert, wait=False), 0)
        lax.fori_loop(nused_ref[0], n_blocks, functools.partial(tail, wait=False), 0)
        lax.fori_loop(0, N_EXPERTS, functools.partial(per_expert, wait=True), 0)
        lax.fori_loop(nused_ref[0], n_blocks, functools.partial(tail, wait=True), 0)

    rt = rt_ref[...]
    lane = lax.broadcasted_iota(I32, (tr, LANES), 1)
    lane_f = lane.astype(F32)
    lo_row = lo_ref[0, 0:1, :]
    slots = []
    for e_lane, k_lane in ((R_E1, R_K1), (R_E2, R_K2)):
        hot = lane_f == rt[:, e_lane:e_lane + 1]
        slots.append(jnp.sum(jnp.where(hot, lo_row, 0.0), axis=-1, keepdims=True) + rt[:, k_lane:k_lane + 1])
    rec = jnp.where(lane == 0, slots[0], jnp.where(lane == 1, slots[1], 0.0))
    slot_o[...] = rec
    rec_t = jnp.concatenate([rec[c * LANES:(c + 1) * LANES].T for c in range(tr // LANES)], axis=1)
    row_id = lax.broadcasted_iota(I32, (R_LOCAL, tr), 0).astype(F32)
    pick = (row_id == rec_t[0:1, :]) | (row_id == rec_t[1:2, :])
    staged = _dot(jnp.where(pick, 1.0, 0.0).astype(BF16), h_ref[...].astype(BF16))

    i = pl.program_id(0)
    last = pl.num_programs(0) - 1
    slot = i % 2

    def copies_from(s):
        return lambda near, far: pltpu.make_async_copy(
            xs.at[s, pl.ds(near, CHUNK), :], xg_ref.at[pl.ds(far, CHUNK), :], sem.at[s])

    @pl.when(i >= 2)
    def _():
        _wait_chunks(nct_ref[i - 2], copies_from(slot))

    xs[slot] = staged
    _start_chunks(dst8_ref, i, nct_ref[i], copies_from(slot))

    @pl.when(i == last)
    def _():
        @pl.when(i >= 1)
        def _():
            _wait_chunks(nct_ref[i - 1], copies_from(1 - slot))
        _wait_chunks(nct_ref[i], copies_from(slot))


def _dispatch(dst8, nct, pad8, npad, nused, h2, route, lo_rows, p_rows):
    t = h2.shape[0]
    return pl.pallas_call(
        _dispatch_kernel,
        grid_spec=pltpu.PrefetchScalarGridSpec(
            num_scalar_prefetch=5,
            grid=(t // TR,),
            in_specs=[pl.BlockSpec((TR, D_MODEL), lambda i, *_: (i, 0)),
                      pl.BlockSpec((TR, LANES), lambda i, *_: (i, 0)),
                      pl.BlockSpec((1, SUBLANES, LANES), lambda i, *_: (i, 0, 0))],
            out_specs=[pl.BlockSpec(memory_space=pl.ANY),
                       pl.BlockSpec((TR, LANES), lambda i, *_: (i, 0))],
            scratch_shapes=[pltpu.VMEM((2, R_LOCAL, D_MODEL), F32), pltpu.VMEM((BM, D_MODEL), F32),
                            pltpu.SemaphoreType.DMA((2,)), pltpu.SemaphoreType.DMA(())],
        ),
        out_shape=[jax.ShapeDtypeStruct((p_rows, D_MODEL), F32), jax.ShapeDtypeStruct((t, LANES), F32)],
        compiler_params=_cparams(("arbitrary",)),
        name="moe_dispatch",
    )(dst8, nct, pad8, npad, nused, h2, route, lo_rows)


def _expert_kernel(blk_e_ref, nused_ref, x_ref, wg_ref, wu_ref, wd_ref, y_ref, wg_b, wu_b, wd_b):
    i = pl.program_id(0)
    used = i < nused_ref[0]

    @pl.when((i == 0) | (blk_e_ref[i] != blk_e_ref[jnp.maximum(i - 1, 0)]))
    def _():
        wg_b[...] = wg_ref[0].astype(BF16)
        wu_b[...] = wu_ref[0].astype(BF16)
        wd_b[...] = wd_ref[0].astype(BF16)

    @pl.when(used)
    def _():
        xb = x_ref[...].astype(BF16)
        a = _dot(xb, wg_b[...])
        u = _dot(xb, wu_b[...])
        y_ref[...] = _dot((a * jax.nn.sigmoid(a) * u).astype(BF16), wd_b[...])

    @pl.when(jnp.logical_not(used))
    def _():
        y_ref[...] = jnp.zeros(y_ref.shape, F32)


def _experts(blk_e, nused, xg, wg, wu, wd):
    p_rows = xg.shape[0]
    x_blk = lambda i, be, nu: (jnp.minimum(i, nu[0] - 1), 0)
    return pl.pallas_call(
        _expert_kernel,
        grid_spec=pltpu.PrefetchScalarGridSpec(
            num_scalar_prefetch=2,
            grid=(p_rows // BM,),
            in_specs=[pl.BlockSpec((BM, D_MODEL), x_blk),
                      pl.BlockSpec((1, D_MODEL, EXPERT_FF), lambda i, be, nu: (be[i], 0, 0)),
                      pl.BlockSpec((1, D_MODEL, EXPERT_FF), lambda i, be, nu: (be[i], 0, 0)),
                      pl.BlockSpec((1, EXPERT_FF, D_MODEL), lambda i, be, nu: (be[i], 0, 0))],
            out_specs=pl.BlockSpec((BM, D_MODEL), lambda i, be, nu: (i, 0)),
            scratch_shapes=[pltpu.VMEM((D_MODEL, EXPERT_FF), BF16), pltpu.VMEM((D_MODEL, EXPERT_FF), BF16),
                            pltpu.VMEM((EXPERT_FF, D_MODEL), BF16)],
        ),
        out_shape=jax.ShapeDtypeStruct((p_rows, D_MODEL), F32),
        compiler_params=_cparams(("arbitrary",)),
        name="moe_experts",
    )(blk_e, nused, xg, wg, wu, wd)


def _combine_ple_kernel(dst8_ref, nct_ref, x1_ref, rt_ref, slot_ref, p_ref, pg_ref, wpg_ref, bpg_ref, wpp_ref,
                        yg_ref, o_ref, ys, sem):
    tr = x1_ref.shape[0]
    i = pl.program_id(0)
    slot = i % 2

    def copies_into(s):
        return lambda near, far: pltpu.make_async_copy(
            yg_ref.at[pl.ds(far, CHUNK), :], ys.at[s, pl.ds(near, CHUNK), :], sem.at[s])

    def fetch(tile, s):
        _start_chunks(dst8_ref, tile, nct_ref[tile], copies_into(s))

        def clear(c, carry):
            ys[s, pl.ds(pl.multiple_of(c * CHUNK, CHUNK), CHUNK), :] = jnp.zeros((CHUNK, D_MODEL), F32)
            return carry

        lax.fori_loop(nct_ref[tile], NCH, clear, 0)

    @pl.when(i == 0)
    def _():
        fetch(0, 0)

    @pl.when(i + 1 < pl.num_programs(0))
    def _():
        fetch(i + 1, 1 - slot)

    _wait_chunks(nct_ref[i], copies_into(slot))

    rt = rt_ref[...]
    sl = slot_ref[...]
    col_id = lax.broadcasted_iota(I32, (tr, R_LOCAL), 1).astype(F32)
    yb = ys[slot].astype(BF16)
    y0 = _dot(jnp.where(col_id == sl[:, 0:1], 1.0, 0.0).astype(BF16), yb)
    y1 = _dot(jnp.where(col_id == sl[:, 1:2], 1.0, 0.0).astype(BF16), yb)
    x2 = x1_ref[...] + rt[:, R_W1:R_W1 + 1] * y0 + rt[:, R_W2:R_W2 + 1] * y1
    h3 = (x2 * lax.rsqrt(jnp.mean(x2 * x2, axis=-1, keepdims=True) + EPS) * pg_ref[...]).astype(BF16)
    gate = jax.nn.sigmoid(_dot(h3, wpg_ref[...]) + bpg_ref[...])
    o_ref[...] = x2 + gate * _dot(p_ref[...].astype(BF16), wpp_ref[...])


def _combine_ple(dst8, nct, x1, route, slots, p2, pg, wpg, bpg, wpp, yg):
    t = x1.shape[0]
    row = lambda n: pl.BlockSpec((TR, n), lambda i, d, c: (i, 0))
    full = lambda a: pl.BlockSpec(a.shape, lambda i, d, c: (0,) * a.ndim)
    return pl.pallas_call(
        _combine_ple_kernel,
        grid_spec=pltpu.PrefetchScalarGridSpec(
            num_scalar_prefetch=2,
            grid=(t // TR,),
            in_specs=[row(D_MODEL), row(LANES), row(LANES), row(PLE_DIM), full(pg), full(wpg), full(bpg),
                      full(wpp), pl.BlockSpec(memory_space=pl.ANY)],
            out_specs=row(D_MODEL),
            scratch_shapes=[pltpu.VMEM((2, R_LOCAL, D_MODEL), F32), pltpu.SemaphoreType.DMA((2,))],
        ),
        out_shape=jax.ShapeDtypeStruct((t, D_MODEL), F32),
        compiler_params=_cparams(("arbitrary",)),
        name="moe_combine_ple",
    )(dst8, nct, x1, route, slots, p2, pg, wpg, bpg, wpp, yg)


def _seg_sum_matrix():
    seg = np.arange(LANES) // HEAD_DIM
    return jnp.asarray(seg[:, None] == seg[None, :], BF16)


def _overlap_t(nc_rows, n_blk):
    nc = np.arange(nc_rows)[None, :] * CMP_STRIDE
    sb = np.arange(n_blk)[:, None] * SEL_BLOCK
    ov = (nc < sb + SEL_BLOCK) & (nc + CMP_BLOCK > sb) & (np.arange(nc_rows)[None, :] < nc_rows - 1)
    return jnp.asarray(ov, BF16)


def _compress_weights(pos, w1, w2):
    half = CMP_BLOCK // 2
    pe = jnp.broadcast_to(pos[:, None, :], (CMP_BLOCK, N_KV, HEAD_DIM)).reshape(2, half * N_KV * HEAD_DIM)
    w1r = w1.reshape(2, half, HEAD_DIM, CMP_HIDDEN)
    eye = jnp.eye(N_KV, dtype=w1.dtype)
    wx = jnp.einsum('hldj,gk->hlgdkj', w1r, eye).reshape(2, half * N_KV * HEAD_DIM, N_KV * CMP_HIDDEN)
    w2x = jnp.einsum('jd,gk->gjkd', w2, eye).reshape(N_KV * CMP_HIDDEN, N_KV * HEAD_DIM)
    return pe, wx[0].astype(BF16), wx[1].astype(BF16), w2x.astype(BF16)


def _layer(x, p_i, prm):
    b, s, _ = x.shape
    t = b * s
    x2 = x.reshape(t, D_MODEL)
    bd = _seg_sum_matrix()

    w = prm["w_in"]
    kv = lambda j: w[:, OFF_KV + j * KV_COLS:OFF_KV + (j + 1) * KV_COLS]
    w_nat = jnp.concatenate([kv(0), kv(1), kv(2), kv(4), w[:, OFF_CONV:]], axis=1).astype(BF16)
    w_t = jnp.concatenate([w[:, :Q_COLS], kv(3), kv(5),
                           jnp.pad(w[:, OFF_GATE:OFF_CONV], ((0, 0), (0, LANES - 3 * N_HEADS)))], axis=1).T.astype(BF16)
    qg = jnp.broadcast_to(prm["q_norm_g"][:, None], (HEAD_DIM, TM_PROJ))
    kg = jnp.tile(prm["k_norm_g"], (1, LANES // HEAD_DIM))
    qt, kc_raw, vc_raw, ks, vst, kw, vwt, gtt, glu, mg = _in_proj(
        x2, prm["attn_norm_g"][None, :], w_nat, w_t, qg, kg, bd, prm["merge_b"][None, :], s)

    nrow = s // CMP_STRIDE
    cw = [_compress_weights(prm["cmp_pos"][j], prm["cmp_w1"][j], prm["cmp_w2"][j]) for j in range(2)]
    pe_x, wa, wb = (jnp.stack([cw[0][j], cw[1][j]]) for j in range(3))
    kc, vct = _compress(kc_raw.reshape(b, nrow, CMP_STRIDE * LANES), vc_raw.reshape(b, nrow, CMP_STRIDE * LANES),
                        pe_x, wa, wb, cw[0][3], cw[1][3].T, bd, kg)

    attn = _attention(qt, gtt, kc, vct, ks.reshape(b, s, 4 * LANES), vst, kw.reshape(b, s, 2 * LANES), vwt,
                      _overlap_t(nrow, s // SEL_BLOCK), b, s)

    cact = _conformer_conv(glu, prm["conv_w"], prm["conv_b"][None, :], prm["conv_norm_g"][None, :],
                           prm["conv_norm_b"][None, :], b, s)

    wr_full = jnp.zeros((D_MODEL, LANES), F32)
    wr_full = wr_full.at[:, :N_EXPERTS].set(prm["w_router_expert"])
    wr_full = wr_full.at[:, GROUP_LANE0:GROUP_LANE0 + N_GROUPS].set(prm["w_router_group"])
    wr_hi = wr_full.astype(BF16)
    wr = jnp.stack([wr_hi, (wr_full - wr_hi.astype(F32)).astype(BF16)])
    br = jnp.zeros((1, LANES), F32)
    br = br.at[0, :N_EXPERTS].set(prm["b_router_expert"])
    br = br.at[0, GROUP_LANE0:GROUP_LANE0 + N_GROUPS].set(prm["b_router_group"])
    x1, h2, route, cnt = _merge_router(
        x2, attn, cact, mg, prm["w_attn_out"].astype(BF16), prm["w_conv_out"].astype(BF16),
        prm["w_out"].astype(BF16), prm["ffn_norm_g"][None, :], wr, br)

    n_tiles = t // TR
    counts = cnt[:, 0, :N_EXPERTS].astype(I32)
    seg = (counts + CHUNK - 1) // CHUNK * CHUNK
    local_off = jnp.cumsum(seg, axis=1) - seg
    nct = (jnp.sum(seg, axis=1) // CHUNK).astype(I32)
    total = jnp.sum(seg, axis=0)
    padded = (total + BM - 1) // BM * BM
    pends = jnp.cumsum(padded)
    far_off = (pends - padded)[None, :] + jnp.cumsum(seg, axis=0) - seg
    chunk = jnp.arange(NCH, dtype=I32)[None, :, None]
    lo8 = (local_off // CHUNK)[:, None, :]
    in_seg = (chunk >= lo8) & (chunk < ((local_off + seg) // CHUNK)[:, None, :])
    dst8 = (jnp.sum(jnp.where(in_seg, (far_off // CHUNK)[:, None, :] - lo8, 0), axis=-1)
            + chunk[:, :, 0]).astype(I32).reshape(n_tiles * NCH)
    lo_rows = jnp.broadcast_to(jnp.pad(local_off.astype(F32), ((0, 0), (0, LANES - N_EXPERTS)))[:, None, :],
                               (n_tiles, SUBLANES, LANES))
    n_blk = -(-(2 * t + n_tiles * N_EXPERTS * (CHUNK - 1) + N_EXPERTS * (BM - 1)) // BM)
    blk_e = jnp.minimum(jnp.sum(jnp.arange(n_blk, dtype=I32)[:, None] * BM >= pends[None, :], axis=1),
                        N_EXPERTS - 1).astype(I32)
    nused = (pends[-1:] // BM).astype(I32)
    pad8 = ((pends - padded + total) // CHUNK).astype(I32)
    npad = ((padded - total) // CHUNK).astype(I32)

    xg, slots = _dispatch(dst8, nct, pad8, npad, nused, h2, route, lo_rows, n_blk * BM)
    yg = _experts(blk_e, nused, xg, prm["w_exp_gate"], prm["w_exp_up"], prm["w_exp_down"])
    out = _combine_ple(dst8, nct, x1, route, slots, p_i.reshape(t, PLE_DIM), prm["ple_norm_g"][None, :],
                       prm["w_ple_gate"].astype(BF16), prm["b_ple_gate"][None, :], prm["w_ple_proj"].astype(BF16), yg)
    return out.reshape(b, s, D_MODEL)


def kernel(x, p, attn_norm_g, w_in, q_norm_g, k_norm_g, cmp_pos, cmp_w1, cmp_w2, w_attn_out, conv_w, conv_b, conv_norm_g, conv_norm_b, w_conv_out, merge_b, w_out, ffn_norm_g, w_router_group, b_router_group, w_router_expert, b_router_expert, w_exp_gate, w_exp_up, w_exp_down, ple_norm_g, w_ple_gate, b_ple_gate, w_ple_proj):
    stacked = dict(attn_norm_g=attn_norm_g, w_in=w_in, q_norm_g=q_norm_g, k_norm_g=k_norm_g, cmp_pos=cmp_pos,
                   cmp_w1=cmp_w1, cmp_w2=cmp_w2, w_attn_out=w_attn_out, conv_w=conv_w, conv_b=conv_b,
                   conv_norm_g=conv_norm_g, conv_norm_b=conv_norm_b, w_conv_out=w_conv_out, merge_b=merge_b,
                   w_out=w_out, ffn_norm_g=ffn_norm_g, w_router_group=w_router_group,
                   b_router_group=b_router_group, w_router_expert=w_router_expert,
                   b_router_expert=b_router_expert, w_exp_gate=w_exp_gate, w_exp_up=w_exp_up,
                   w_exp_down=w_exp_down, ple_norm_g=ple_norm_g, w_ple_gate=w_ple_gate, b_ple_gate=b_ple_gate,
                   w_ple_proj=w_ple_proj)
    for i in range(w_in.shape[0]):
        x = _layer(x, p[i], {k: v[i] for k, v in stacked.items()})
    return x
```

```python
import functools

import numpy as np
import jax
import jax.numpy as jnp
from jax import lax
from jax.experimental import pallas as pl
from jax.experimental.pallas import tpu as pltpu

F32 = jnp.float32
BF16 = jnp.bfloat16
I32 = jnp.int32

D_MODEL = 1024
PLE_DIM = 256
N_HEADS = 8
N_KV = 2
HPG = N_HEADS // N_KV
HEAD_DIM = 64
CMP_BLOCK = 32
CMP_STRIDE = 16
CMP_HIDDEN = 128
SEL_BLOCK = 64
N_SEL = 8
WINDOW = 512
CONV_CH = 512
CONV_WIDTH = 31
N_GROUPS = 4
EXP_PER_GROUP = 8
N_EXPERTS = N_GROUPS * EXP_PER_GROUP
EXPERT_FF = 512
EPS = 1e-6
NEG = -1e30
BIG = 1e4
Q_COLS = N_HEADS * HEAD_DIM
KV_COLS = N_KV * HEAD_DIM
OFF_KV = Q_COLS
OFF_GATE = OFF_KV + 6 * KV_COLS
OFF_CONV = OFF_GATE + 3 * N_HEADS
OFF_MERGE = OFF_CONV + 2 * CONV_CH

LANES = 128
SUBLANES = 8
VMEM_LIMIT = 56 * 1024 * 1024

TM_PROJ = 256
QB = 256
KT = 256
TS_CONV = 512
HALO = 32
TR = 512
BM = 256
CHUNK = SUBLANES
R_LOCAL = 2 * TR + N_EXPERTS * CHUNK
NCH = R_LOCAL // CHUNK

N_KC = 0
N_VC = N_KC + LANES
N_KS = N_VC + LANES
N_KW = N_KS + LANES
N_CONV = N_KW + LANES
N_MERGE = N_CONV + 2 * CONV_CH
T_Q = 0
T_VS = T_Q + Q_COLS
T_VW = T_VS + LANES
T_GATE = T_VW + LANES

POS_HI_LANE = HEAD_DIM
POS_LO_LANE = HEAD_DIM + 1
UNSELECTED = -(2.0 ** 40)

R_E1, R_E2, R_W1, R_W2, R_K1, R_K2 = 0, 1, 2, 3, 4, 5
GROUP_LANE0 = N_EXPERTS


def _cparams(sem, vmem=VMEM_LIMIT):
    return pltpu.CompilerParams(dimension_semantics=sem, vmem_limit_bytes=vmem)


def _dot(a, b):
    return jnp.dot(a, b, preferred_element_type=F32)


def _dot_nt(a, b):
    return lax.dot_general(a, b, (((1,), (1,)), ((), ())), preferred_element_type=F32)


def _split(x):
    hi = x.astype(BF16)
    lo = (x - hi.astype(F32)).astype(BF16)
    return hi, lo


def _seg_rmsnorm(z, bd):
    hi, lo = _split(z * z)
    ss = _dot(hi, bd) + _dot(lo, bd)
    return z * lax.rsqrt(ss * (1.0 / HEAD_DIM) + EPS)


def _with_pos_lanes(kn, pos):
    lane = lax.broadcasted_iota(I32, kn.shape, 1)
    hi = pos // SEL_BLOCK * SEL_BLOCK
    pos_lanes = jnp.where(lane == POS_HI_LANE, hi.astype(F32),
                          jnp.where(lane == POS_LO_LANE, (pos - hi).astype(F32), 0.0))
    lo_half = lane < HEAD_DIM
    return (jnp.where(lo_half, kn, pos_lanes).astype(BF16),
            jnp.where(lo_half, pltpu.roll(kn, HEAD_DIM, 1), pos_lanes).astype(BF16))


def _in_proj_kernel(x_ref, g_ref, w_ref, wt_ref, qg_ref, kg_ref, bd_ref, mb_ref,
                    qt_o, kc_o, vc_o, ks_o, vst_o, kw_o, vwt_o, gtt_o, glu_o, mg_o, *, seq_len):
    tm = x_ref.shape[0]
    x = x_ref[...]
    h = (x * lax.rsqrt(jnp.mean(x * x, axis=-1, keepdims=True) + EPS) * g_ref[...]).astype(BF16)
    bd = bd_ref[...]
    lane = lax.broadcasted_iota(I32, (tm, LANES), 1)

    def proj(a, b):
        return _dot(h, w_ref[:, a:b])

    zt = _dot_nt(wt_ref[...], h)
    for hd in range(N_HEADS):
        zh = zt[T_Q + hd * HEAD_DIM:T_Q + (hd + 1) * HEAD_DIM, :]
        ms = jnp.sum(zh * zh, axis=0, keepdims=True) * (1.0 / HEAD_DIM)
        qt_o[hd * HEAD_DIM:(hd + 1) * HEAD_DIM, :] = (
            zh * lax.rsqrt(ms + EPS) * qg_ref[...] * (HEAD_DIM ** -0.5)).astype(BF16)
    ones = jnp.ones((HEAD_DIM, tm), F32)
    for off, v_o in ((T_VS, vst_o), (T_VW, vwt_o)):
        v_o[0] = jnp.concatenate([zt[off:off + HEAD_DIM, :], ones, zt[off + HEAD_DIM:off + LANES, :], ones],
                                 axis=0).astype(BF16)
    gtt_o[...] = jax.nn.sigmoid(zt[T_GATE:T_GATE + LANES, :])

    kc_o[...] = proj(N_KC, N_KC + LANES)
    vc_o[...] = proj(N_VC, N_VC + LANES)

    pos = (pl.program_id(0) * tm) % seq_len + lax.broadcasted_iota(I32, (tm, LANES), 0)
    onehot = jnp.where(lane == pos // SEL_BLOCK, 1.0, 0.0).astype(BF16)

    def key_tiles(zk, gain):
        return _with_pos_lanes(_seg_rmsnorm(zk, bd) * gain, pos)

    k0, k1 = key_tiles(proj(N_KS, N_KS + LANES), kg_ref[1:2, :])
    ks_o[...] = jnp.concatenate([k0, onehot, k1, onehot], axis=1)
    k0, k1 = key_tiles(proj(N_KW, N_KW + LANES), kg_ref[2:3, :])
    kw_o[...] = jnp.concatenate([k0, k1], axis=1)
    za = proj(N_CONV, N_CONV + CONV_CH)
    zg = proj(N_CONV + CONV_CH, N_CONV + 2 * CONV_CH)
    glu_o[...] = za * jax.nn.sigmoid(zg)
    for c in range(2):
        zm = proj(N_MERGE + c * D_MODEL, N_MERGE + (c + 1) * D_MODEL)
        mg_o[:, c * D_MODEL:(c + 1) * D_MODEL] = jax.nn.sigmoid(
            zm + mb_ref[:, c * D_MODEL:(c + 1) * D_MODEL]).astype(BF16)


def _in_proj(x2, norm_g, w_nat, w_t, qg, kg, bd, mb, seq_len):
    t = x2.shape[0]
    tm = TM_PROJ
    assert tm == KT and seq_len % tm == 0 and seq_len // SEL_BLOCK <= HEAD_DIM
    row = lambda n: pl.BlockSpec((tm, n), lambda i: (i, 0))
    col = lambda n: pl.BlockSpec((n, tm), lambda i: (0, i))
    vt = pl.BlockSpec((1, 2 * LANES, tm), lambda i: (i, 0, 0))
    full = lambda a: pl.BlockSpec(a.shape, lambda i: (0,) * a.ndim)
    sds = jax.ShapeDtypeStruct
    return pl.pallas_call(
        functools.partial(_in_proj_kernel, seq_len=seq_len),
        grid=(t // tm,),
        in_specs=[row(D_MODEL), full(norm_g), full(w_nat), full(w_t), full(qg), full(kg), full(bd), full(mb)],
        out_specs=[col(Q_COLS), row(LANES), row(LANES), row(4 * LANES), vt, row(2 * LANES), vt, col(LANES),
                   row(CONV_CH), row(2 * D_MODEL)],
        out_shape=[sds((Q_COLS, t), BF16), sds((t, LANES), F32), sds((t, LANES), F32), sds((t, 4 * LANES), BF16),
                   sds((t // tm, 2 * LANES, tm), BF16), sds((t, 2 * LANES), BF16), sds((t // tm, 2 * LANES, tm), BF16),
                   sds((LANES, t), F32), sds((t, CONV_CH), F32), sds((t, 2 * D_MODEL), BF16)],
        compiler_params=_cparams(("parallel",)),
        name="in_proj",
    )(x2, norm_g, w_nat, w_t, qg, kg, bd, mb)


def _compress_kernel(ck_ref, cv_ref, pe_ref, wa_ref, wb_ref, w2k_ref, w2vt_ref, bd_ref, kg_ref, kc_o, vct_o):
    nrow = ck_ref.shape[1]

    def hidden(idx, c_ref):
        c = c_ref[0]
        ya = _dot((c + pe_ref[idx, 0:1, :]).astype(BF16), wa_ref[idx])
        yb = _dot((c + pe_ref[idx, 1:2, :]).astype(BF16), wb_ref[idx])
        return jax.nn.gelu(ya + pltpu.roll(yb, nrow - 1, 0)).astype(BF16)

    kn = _seg_rmsnorm(_dot(hidden(0, ck_ref), w2k_ref[...]), bd_ref[...]) * kg_ref[0:1, :]
    c_end = lax.broadcasted_iota(I32, (nrow, LANES), 0) * CMP_STRIDE + (CMP_BLOCK - 1)
    kc_o[0] = jnp.concatenate(_with_pos_lanes(kn, c_end), axis=1)
    vct_o[0] = _dot_nt(w2vt_ref[...], hidden(1, cv_ref)).astype(BF16)


def _compress(kc_raw, vc_raw, pe_x, wa, wb, w2k, w2vt, bd, kg):
    b, nrow, width = kc_raw.shape
    batch = lambda r, n: pl.BlockSpec((1, r, n), lambda i: (i, 0, 0))
    full = lambda a: pl.BlockSpec(a.shape, lambda i: (0,) * a.ndim)
    return pl.pallas_call(
        _compress_kernel,
        grid=(b,),
        in_specs=[batch(nrow, width), batch(nrow, width), full(pe_x), full(wa), full(wb), full(w2k), full(w2vt),
                  full(bd), full(kg)],
        out_specs=[batch(nrow, 2 * LANES), batch(LANES, nrow)],
        out_shape=[jax.ShapeDtypeStruct((b, nrow, 2 * LANES), BF16), jax.ShapeDtypeStruct((b, LANES, nrow), BF16)],
        compiler_params=_cparams(("parallel",)),
        name="compress",
    )(kc_raw, vc_raw, pe_x, wa, wb, w2k, w2vt, bd, kg)


def _safe_inv(l):
    pos = l > 0.0
    return jnp.where(pos, 1.0 / jnp.where(pos, l, 1.0), 0.0)


def _halve_rows(x, op):
    rows = x.shape[0]
    while rows > SUBLANES:
        rows //= 2
        x = op(x[:rows], x[rows:])
    return x


def _attn_kernel(qt_ref, gtt_ref, kc_ref, vct_ref, ks_ref, vst_ref, kw_ref, vwt_ref, ovt_ref, o_ref,
                 tiles_ref, sa_ref, sb_ref, m_ref, acc_ref):
    i = pl.program_id(1)
    q0 = i * QB
    nc_rows = kc_ref.shape[1]
    n_blk = ovt_ref.shape[0]
    hpc = 2 * LANES // QB
    pw = hpc * QB
    n_pairs = N_HEADS // hpc
    group_of = lambda c: c * hpc // HPG
    rel = ((lax.broadcasted_iota(I32, (KT, pw), 1) & (QB - 1))
           - lax.broadcasted_iota(I32, (KT, pw), 0)).astype(F32)
    slope_rows = lax.broadcasted_iota(I32, (HEAD_DIM, QB), 0) < 2
    gtt = gtt_ref[...]

    pairs = []
    for pr in range(n_pairs):
        q_pos = []
        for hd in range(pr * hpc, (pr + 1) * hpc):
            slope = jnp.where(slope_rows, 2.0 ** -(hd + 1), 0.0).astype(BF16)
            q_pos.append(jnp.concatenate([qt_ref[hd * HEAD_DIM:(hd + 1) * HEAD_DIM, :], slope], axis=0))
        pairs.append(jnp.concatenate(q_pos, axis=1))

    def update(s, vt, m, acc):
        m_new = jnp.maximum(m, jnp.max(_halve_rows(s, jnp.maximum), axis=0, keepdims=True))
        p = jnp.exp(s - m_new).astype(BF16)
        return m_new, jnp.exp(m - m_new) * acc + _dot(vt, p)

    def scores(j, q_list, k_ref, k_width):
        k0 = pl.multiple_of(j * KT, KT)
        return tuple(_dot(k_ref[0, pl.ds(k0, KT), group_of(c) * k_width:(group_of(c) + 1) * k_width], q_list[c])
                     for c in range(n_pairs))

    def consume(j, s_list, state, vt_ref, mask_fn, extra_dist=0.0):
        keep = None if mask_fn is None else mask_fn(rel + ((q0 - j * KT).astype(F32) + extra_dist))
        out = []
        for c in range(n_pairs):
            g = group_of(c)
            s = s_list[c] if keep is None else jnp.where(keep, s_list[c], NEG)
            out += update(s, vt_ref[j, g * LANES:(g + 1) * LANES, :], *state[2 * c:2 * c + 2])
        return tuple(out)

    j_diag = (q0 + QB - 1) // KT
    init = (jnp.full((1, pw), NEG, F32), jnp.zeros((LANES, pw), F32)) * n_pairs

    def selected_branch(q_list, n_act):
        def fill(buf, n):
            for c, s in enumerate(scores(tiles_ref[n], q_list, ks_ref, 2 * LANES)):
                buf[c] = s

        def use(buf, n, mask_fn):
            state = []
            for c in range(n_pairs):
                state += [m_ref[c, 0:1, :], acc_ref[c]]
            out = consume(tiles_ref[n], [buf[c] for c in range(n_pairs)], state, vst_ref, mask_fn)
            for c in range(n_pairs):
                m_ref[c, 0:1, :] = out[2 * c]
                acc_ref[c] = out[2 * c + 1]

        for c in range(n_pairs):
            m_ref[c] = jnp.full((SUBLANES, pw), NEG, F32)
            acc_ref[c] = jnp.zeros((LANES, pw), F32)
        fill(sa_ref, 0)

        def two_tiles(h, carry):
            n = 2 * h
            fill(sb_ref, n + 1)
            use(sa_ref, n, None)

            @pl.when(n + 1 < n_act)
            def _():
                fill(sa_ref, n + 2)
                use(sb_ref, n + 1, None)
            return carry

        lax.fori_loop(0, (n_act + 1) // 2, two_tiles, 0)

        @pl.when(n_act % 2 == 1)
        def _():
            for c in range(n_pairs):
                sa_ref[c] = sb_ref[c]

        use(sa_ref, n_act, lambda d: d >= 0.0)
        return [acc_ref[c] for c in range(n_pairs)]

    assert WINDOW == 2 * KT and KT % QB == 0
    win_out = init
    for w in range(WINDOW // KT + 1):
        jw = j_diag - WINDOW // KT + w
        before_start = jnp.where(jw < 0, float(WINDOW + KT), 0.0)
        jc = jnp.maximum(jw, 0)
        mask_fn = ((lambda d: d < float(WINDOW)), (lambda d, jw=jw: jw >= 0), (lambda d: d >= 0.0))[w]
        win_out = consume(jc, scores(jc, pairs, kw_ref, LANES), win_out, vwt_ref, mask_fn, before_start)

    cidx = lax.broadcasted_iota(I32, (nc_rows, pw), 0)
    c_end = cidx * CMP_STRIDE + (CMP_BLOCK - 1)
    cmask = (c_end <= q0 + (lax.broadcasted_iota(I32, (nc_rows, pw), 1) & (QB - 1))) & (cidx < nc_rows - 1)

    o_cmp, imp = [], []
    for g in range(N_KV):
        p_sum = jnp.zeros((nc_rows, QB), F32)
        for pr in range(g * HPG // hpc, (g + 1) * HPG // hpc):
            s = jnp.where(cmask, _dot(kc_ref[0, :, g * LANES:(g + 1) * LANES], pairs[pr]), NEG)
            e = jnp.where(cmask, jnp.exp(s - jnp.max(_halve_rows(s, jnp.maximum), axis=0, keepdims=True)), 0.0)
            p = e * _safe_inv(jnp.sum(_halve_rows(e, jnp.add), axis=0, keepdims=True))
            o_cmp.append(_dot(vct_ref[0, g * HEAD_DIM:(g + 1) * HEAD_DIM, :], p.astype(BF16)))
            for k in range(hpc):
                p_sum = p_sum + p[:, k * QB:(k + 1) * QB]
        ph, plo = _split(p_sum)
        imp.append(_dot(ovt_ref[...], ph) + _dot(ovt_ref[...], plo))

    blk = lax.broadcasted_iota(I32, (n_blk, N_KV * QB), 0)
    tl = q0 + (lax.broadcasted_iota(I32, (n_blk, N_KV * QB), 1) & (QB - 1))
    valid = blk * SEL_BLOCK <= tl
    cur = tl // SEL_BLOCK
    forced = (blk == 0) | (blk == cur) | (blk == cur - 1)
    sel = forced & valid
    score = jnp.where(valid, jnp.where(sel, -jnp.inf, jnp.concatenate(imp, axis=1)), -BIG)
    for _ in range(min(N_SEL, n_blk) - 3):
        mx = jnp.max(score, axis=0, keepdims=True)
        first = jnp.min(jnp.where(score == mx, blk, n_blk), axis=0, keepdims=True)
        pick = blk == first
        sel = sel | pick
        score = jnp.where(pick, -jnp.inf, score)
    chosen = sel & valid
    any_sel = jnp.where(chosen[:, :QB] | chosen[:, QB:], 1.0, 0.0)
    unsel = jnp.where(chosen, 0.0, UNSELECTED).astype(BF16)
    bias = jnp.concatenate([unsel, jnp.zeros((2 * HEAD_DIM - n_blk, N_KV * QB), BF16)], axis=0)
    q_sel = []
    for c in range(n_pairs):
        g_bias = bias[:, group_of(c) * QB:(group_of(c) + 1) * QB]
        q_sel.append(jnp.concatenate([pairs[c], jnp.concatenate([g_bias] * hpc, axis=1)], axis=0))

    per_blk = jnp.sum(any_sel, axis=1, keepdims=True)
    n_act = jnp.int32(0)
    for jt in range(n_blk * SEL_BLOCK // KT):
        blocks = slice(jt * (KT // SEL_BLOCK), (jt + 1) * (KT // SEL_BLOCK))
        tiles_ref[n_act] = jnp.int32(jt)
        n_act = n_act + ((jnp.sum(per_blk[blocks]) > 0.0) & (jt < j_diag)).astype(I32)
    tiles_ref[n_act] = j_diag

    sel_acc = selected_branch(q_sel, n_act)

    o_sel = [a[:HEAD_DIM] / a[HEAD_DIM:HEAD_DIM + 1] for a in sel_acc]
    o_win = [a[:HEAD_DIM] / a[HEAD_DIM:HEAD_DIM + 1] for a in win_out[1::2]]
    mixed = []
    for hd in range(N_HEADS):
        c, lanes = hd // hpc, slice((hd % hpc) * QB, (hd % hpc + 1) * QB)
        mixed.append(gtt[3 * hd:3 * hd + 1, :] * o_cmp[c][:, lanes] + gtt[3 * hd + 1:3 * hd + 2, :] * o_sel[c][:, lanes]
                     + gtt[3 * hd + 2:3 * hd + 3, :] * o_win[c][:, lanes])
    for pr in range(N_HEADS // 2):
        two = jnp.concatenate(mixed[2 * pr:2 * pr + 2], axis=0)
        o_ref[:, pr * LANES:(pr + 1) * LANES] = jnp.concatenate(
            [two[:, k * LANES:(k + 1) * LANES].T for k in range(QB // LANES)], axis=0).astype(BF16)


def _attention(qt, gtt, kc, vct, ks, vst, kw, vwt, ovt, b, s):
    n_q = s // QB
    n_tiles = N_HEADS * QB // (2 * LANES)
    per_q = lambda n: pl.BlockSpec((n, QB), lambda bi, i: (0, bi * n_q + i))
    per_b = lambda a: pl.BlockSpec((1,) + a.shape[1:], lambda bi, i: (bi, 0, 0))
    vt = pl.BlockSpec((s // KT, 2 * LANES, KT), lambda bi, i: (bi, 0, 0))
    return pl.pallas_call(
        _attn_kernel,
        grid=(b, n_q),
        in_specs=[per_q(Q_COLS), per_q(LANES), per_b(kc), per_b(vct), per_b(ks), vt, per_b(kw), vt,
                  pl.BlockSpec(ovt.shape, lambda bi, i: (0, 0))],
        out_specs=pl.BlockSpec((QB, Q_COLS), lambda bi, i: (bi * n_q + i, 0)),
        out_shape=jax.ShapeDtypeStruct((b * s, Q_COLS), BF16),
        scratch_shapes=[pltpu.SMEM((s // KT + 1,), I32),
                        pltpu.VMEM((n_tiles, KT, 2 * LANES), F32), pltpu.VMEM((n_tiles, KT, 2 * LANES), F32),
                        pltpu.VMEM((n_tiles, SUBLANES, 2 * LANES), F32),
                        pltpu.VMEM((n_tiles, LANES, 2 * LANES), F32)],
        compiler_params=_cparams(("parallel", "parallel")),
        name="nsa_attention",
    )(qt, gtt, kc, vct, ks, vst, kw, vwt, ovt)


def _conv_kernel(z_ref, w_ref, b_ref, g_ref, beta_ref, o_ref, buf, shifted):
    ts = z_ref.shape[0]
    chunk = 64

    @pl.when(pl.program_id(1) == 0)
    def _():
        buf[0:HALO, :] = jnp.zeros((HALO, CONV_CH), F32)

    buf[HALO:HALO + ts, :] = z_ref[...]
    span = ts + HALO - SUBLANES
    for r in range(1, SUBLANES):
        shifted[r - 1, 0:span, :] = buf[r:r + span, :]

    def tap_rows(k, c):
        start = HALO - (CONV_WIDTH - 1) + k
        r = start % SUBLANES
        src = buf if r == 0 else shifted.at[r - 1]
        return src[start - r + c * chunk:start - r + (c + 1) * chunk, :]

    for c in range(ts // chunk):
        acc = jnp.zeros((chunk, CONV_CH), F32) + b_ref[...]
        for k in range(CONV_WIDTH):
            acc = acc + tap_rows(k, c) * w_ref[k:k + 1, :]
        mu = jnp.mean(acc, axis=-1, keepdims=True)
        d = acc - mu
        var = jnp.mean(d * d, axis=-1, keepdims=True)
        y = d * lax.rsqrt(var + EPS) * g_ref[...] + beta_ref[...]
        o_ref[c * chunk:(c + 1) * chunk, :] = (y * jax.nn.sigmoid(y)).astype(BF16)
    buf[0:HALO, :] = buf[ts:ts + HALO, :]


def _conformer_conv(glu, w, bias, g, beta, b, s):
    ts = min(TS_CONV, s)
    full = lambda a: pl.BlockSpec(a.shape, lambda bi, i: (0, 0))
    row = pl.BlockSpec((ts, CONV_CH), lambda bi, i: (bi * (s // ts) + i, 0))
    return pl.pallas_call(
        _conv_kernel,
        grid=(b, s // ts),
        in_specs=[row, full(w), full(bias), full(g), full(beta)],
        out_specs=row,
        out_shape=jax.ShapeDtypeStruct((b * s, CONV_CH), BF16),
        scratch_shapes=[pltpu.VMEM((HALO + ts, CONV_CH), F32),
                        pltpu.VMEM((SUBLANES - 1, HALO + ts, CONV_CH), F32)],
        compiler_params=_cparams(("arbitrary", "arbitrary")),
        name="conformer_conv",
    )(glu, w, bias, g, beta)


def _merge_router_kernel(x_ref, a_ref, c_ref, mg_ref, wa_ref, wc_ref, wo_ref, fg_ref, wr_ref, br_ref,
                         x1_o, h2_o, rt_o, cnt_o):
    tm = x_ref.shape[0]
    y_a = _dot(a_ref[...], wa_ref[...])
    y_c = _dot(c_ref[...], wc_ref[...])
    mix = mg_ref[:, 0:D_MODEL].astype(F32) * y_a + mg_ref[:, D_MODEL:2 * D_MODEL].astype(F32) * y_c
    x1 = x_ref[...] + _dot(mix.astype(BF16), wo_ref[...])
    x1_o[...] = x1
    h2 = x1 * lax.rsqrt(jnp.mean(x1 * x1, axis=-1, keepdims=True) + EPS) * fg_ref[...]
    h2_o[...] = h2

    hh, hl = _split(h2)
    logits = _dot(hh, wr_ref[0]) + _dot(hl, wr_ref[0]) + _dot(hh, wr_ref[1]) + br_ref[...]
    lane = lax.broadcasted_iota(I32, (tm, LANES), 1)
    is_g = (lane >= GROUP_LANE0) & (lane < GROUP_LANE0 + N_GROUPS)
    gmax = jnp.max(jnp.where(is_g, logits, -jnp.inf), axis=-1, keepdims=True)
    gsel = jnp.min(jnp.where(is_g & (logits == gmax), lane - GROUP_LANE0, N_GROUPS), axis=-1, keepdims=True)
    pg_sel = 1.0 / jnp.sum(jnp.where(is_g, jnp.exp(logits - gmax), 0.0), axis=-1, keepdims=True)
    in_grp = (lane < N_EXPERTS) & (lane // EXP_PER_GROUP == gsel)
    emax = jnp.max(jnp.where(in_grp, logits, -jnp.inf), axis=-1, keepdims=True)
    ee = jnp.where(in_grp, jnp.exp(logits - emax), 0.0)
    pe = jnp.where(in_grp, ee / jnp.sum(ee, axis=-1, keepdims=True), -1.0)
    p1 = jnp.max(pe, axis=-1, keepdims=True)
    i1 = jnp.min(jnp.where(pe == p1, lane, LANES), axis=-1, keepdims=True)
    pe2 = jnp.where(lane == i1, -1.0, pe)
    p2 = jnp.max(pe2, axis=-1, keepdims=True)
    i2 = jnp.min(jnp.where(pe2 == p2, lane, LANES), axis=-1, keepdims=True)
    w1 = pg_sel * p1 / (p1 + p2)
    w2 = pg_sel * p2 / (p1 + p2)

    hot1 = lane == i1
    hot2 = lane == i2
    ind = jnp.where(hot1 | hot2, 1.0, 0.0)
    tri = (lax.broadcasted_iota(I32, (tm, tm), 1) < lax.broadcasted_iota(I32, (tm, tm), 0)).astype(BF16)
    before = _dot(tri, ind.astype(BF16))
    k1 = jnp.sum(jnp.where(hot1, before, 0.0), axis=-1, keepdims=True)
    k2 = jnp.sum(jnp.where(hot2, before, 0.0), axis=-1, keepdims=True)
    rec = jnp.zeros((tm, LANES), F32)
    for slot, val in ((R_E1, i1.astype(F32)), (R_E2, i2.astype(F32)), (R_W1, w1), (R_W2, w2), (R_K1, k1), (R_K2, k2)):
        rec = jnp.where(lane == slot, val, rec)
    rt_o[...] = rec
    cnt_o[0] = jnp.broadcast_to(jnp.sum(ind, axis=0, keepdims=True), (SUBLANES, LANES))


def _merge_router(x2, attn, cact, mg, wa, wc, wo, fg, wr, br):
    t = x2.shape[0]
    tm = TR
    row = lambda n: pl.BlockSpec((tm, n), lambda i: (i, 0))
    full = lambda a: pl.BlockSpec(a.shape, lambda i: (0,) * a.ndim)
    return pl.pallas_call(
        _merge_router_kernel,
        grid=(t // tm,),
        in_specs=[row(D_MODEL), row(Q_COLS), row(CONV_CH), row(2 * D_MODEL), full(wa), full(wc), full(wo),
                  full(fg), full(wr), full(br)],
        out_specs=[row(D_MODEL), row(D_MODEL), row(LANES), pl.BlockSpec((1, SUBLANES, LANES), lambda i: (i, 0, 0))],
        out_shape=[jax.ShapeDtypeStruct((t, D_MODEL), F32), jax.ShapeDtypeStruct((t, D_MODEL), F32),
                   jax.ShapeDtypeStruct((t, LANES), F32), jax.ShapeDtypeStruct((t // tm, SUBLANES, LANES), F32)],
        compiler_params=_cparams(("parallel",)),
        name="merge_router",
    )(x2, attn, cact, mg, wa, wc, wo, fg, wr, br)


def _start_chunks(dst8_ref, tile, n_chunks, make_copy):
    def issue(c, carry):
        far = pl.multiple_of(dst8_ref[tile * NCH + c] * CHUNK, CHUNK)
        make_copy(pl.multiple_of(c * CHUNK, CHUNK), far).start()
        return carry

    lax.fori_loop(0, n_chunks, issue, 0)


def _wait_chunks(n_chunks, make_copy):
    def wait_one(c, carry):
        make_copy(0, 0).wait()
        return carry

    lax.fori_loop(0, n_chunks, wait_one, 0)


def _dispatch_kernel(dst8_ref, nct_ref, pad8_ref, npad_ref, nused_ref, h_ref, rt_ref, lo_ref, xg_ref, slot_o, xs,
                     zbuf, sem, zsem):
    tr = h_ref.shape[0]

    @pl.when(pl.program_id(0) == 0)
    def _():
        zbuf[...] = jnp.zeros(zbuf.shape, F32)

        def zero_chunk(far):
            return pltpu.make_async_copy(zbuf.at[pl.ds(0, CHUNK), :], xg_ref.at[pl.ds(far, CHUNK), :], zsem)

        def zero_block(blk):
            return pltpu.make_async_copy(zbuf, xg_ref.at[pl.ds(pl.multiple_of(blk * BM, BM), BM), :], sem.at[0])

        def per_expert(e, carry, wait):
            def one(c, inner):
                if wait:
                    zero_chunk(0).wait()
                else:
                    zero_chunk(pl.multiple_of((pad8_ref[e] + c) * CHUNK, CHUNK)).start()
                return inner
            return lax.fori_loop(0, npad_ref[e], one, carry)

        def tail(blk, carry, wait):
            if wait:
                zero_block(0).wait()
            else:
                zero_block(blk).start()
            return carry

        n_blocks = xg_ref.shape[0] // BM
        lax.fori_loop(0, N_EXPERTS, functools.partial(per_expert, wait=False), 0)
        lax.fori_loop(nused_ref[0], n_blocks, functools.partial(tail, wait=False), 0)
        lax.fori_loop(0, N_EXPERTS, functools.partial(per_expert, wait=True), 0)
        lax.fori_loop(nused_ref[0], n_blocks, functools.partial(tail, wait=True), 0)

    rt = rt_ref[...]
    lane = lax.broadcasted_iota(I32, (tr, LANES), 1)
    lane_f = lane.astype(F32)
    lo_row = lo_ref[0, 0:1, :]
    slots = []
    for e_lane, k_lane in ((R_E1, R_K1), (R_E2, R_K2)):
        hot = lane_f == rt[:, e_lane:e_lane + 1]
        slots.append(jnp.sum(jnp.where(hot, lo_row, 0.0), axis=-1, keepdims=True) + rt[:, k_lane:k_lane + 1])
    rec = jnp.where(lane == 0, slots[0], jnp.where(lane == 1, slots[1], 0.0))
    slot_o[...] = rec
    rec_t = jnp.concatenate([rec[c * LANES:(c + 1) * LANES].T for c in range(tr // LANES)], axis=1)
    row_id = lax.broadcasted_iota(I32, (R_LOCAL, tr), 0).astype(F32)
    pick = (row_id == rec_t[0:1, :]) | (row_id == rec_t[1:2, :])
    staged = _dot(jnp.where(pick, 1.0, 0.0).astype(BF16), h_ref[...].astype(BF16))

    i = pl.program_id(0)
    last = pl.num_programs(0) - 1
    slot = i % 2

    def copies_from(s):
        return lambda near, far: pltpu.make_async_copy(
            xs.at[s, pl.ds(near, CHUNK), :], xg_ref.at[pl.ds(far, CHUNK), :], sem.at[s])

    @pl.when(i >= 2)
    def _():
        _wait_chunks(nct_ref[i - 2], copies_from(slot))

    xs[slot] = staged
    _start_chunks(dst8_ref, i, nct_ref[i], copies_from(slot))

    @pl.when(i == last)
    def _():
        @pl.when(i >= 1)
        def _():
            _wait_chunks(nct_ref[i - 1], copies_from(1 - slot))
        _wait_chunks(nct_ref[i], copies_from(slot))


def _dispatch(dst8, nct, pad8, npad, nused, h2, route, lo_rows, p_rows):
    t = h2.shape[0]
    return pl.pallas_call(
        _dispatch_kernel,
        grid_spec=pltpu.PrefetchScalarGridSpec(
            num_scalar_prefetch=5,
            grid=(t // TR,),
            in_specs=[pl.BlockSpec((TR, D_MODEL), lambda i, *_: (i, 0)),
                      pl.BlockSpec((TR, LANES), lambda i, *_: (i, 0)),
                      pl.BlockSpec((1, SUBLANES, LANES), lambda i, *_: (i, 0, 0))],
            out_specs=[pl.BlockSpec(memory_space=pl.ANY),
                       pl.BlockSpec((TR, LANES), lambda i, *_: (i, 0))],
            scratch_shapes=[pltpu.VMEM((2, R_LOCAL, D_MODEL), F32), pltpu.VMEM((BM, D_MODEL), F32),
                            pltpu.SemaphoreType.DMA((2,)), pltpu.SemaphoreType.DMA(())],
        ),
        out_shape=[jax.ShapeDtypeStruct((p_rows, D_MODEL), F32), jax.ShapeDtypeStruct((t, LANES), F32)],
        compiler_params=_cparams(("arbitrary",)),
        name="moe_dispatch",
    )(dst8, nct, pad8, npad, nused, h2, route, lo_rows)


def _expert_kernel(blk_e_ref, nused_ref, x_ref, wg_ref, wu_ref, wd_ref, y_ref, wg_b, wu_b, wd_b):
    i = pl.program_id(0)
    used = i < nused_ref[0]

    @pl.when((i == 0) | (blk_e_ref[i] != blk_e_ref[jnp.maximum(i - 1, 0)]))
    def _():
        wg_b[...] = wg_ref[0].astype(BF16)
        wu_b[...] = wu_ref[0].astype(BF16)
        wd_b[...] = wd_ref[0].astype(BF16)

    @pl.when(used)
    def _():
        xb = x_ref[...].astype(BF16)
        a = _dot(xb, wg_b[...])
        u = _dot(xb, wu_b[...])
        y_ref[...] = _dot((a * jax.nn.sigmoid(a) * u).astype(BF16), wd_b[...])

    @pl.when(jnp.logical_not(used))
    def _():
        y_ref[...] = jnp.zeros(y_ref.shape, F32)


def _experts(blk_e, nused, xg, wg, wu, wd):
    p_rows = xg.shape[0]
    x_blk = lambda i, be, nu: (jnp.minimum(i, nu[0] - 1), 0)
    return pl.pallas_call(
        _expert_kernel,
        grid_spec=pltpu.PrefetchScalarGridSpec(
            num_scalar_prefetch=2,
            grid=(p_rows // BM,),
            in_specs=[pl.BlockSpec((BM, D_MODEL), x_blk),
                      pl.BlockSpec((1, D_MODEL, EXPERT_FF), lambda i, be, nu: (be[i], 0, 0)),
                      pl.BlockSpec((1, D_MODEL, EXPERT_FF), lambda i, be, nu: (be[i], 0, 0)),
                      pl.BlockSpec((1, EXPERT_FF, D_MODEL), lambda i, be, nu: (be[i], 0, 0))],
            out_specs=pl.BlockSpec((BM, D_MODEL), lambda i, be, nu: (i, 0)),
            scratch_shapes=[pltpu.VMEM((D_MODEL, EXPERT_FF), BF16), pltpu.VMEM((D_MODEL, EXPERT_FF), BF16),
                            pltpu.VMEM((EXPERT_FF, D_MODEL), BF16)],
        ),
        out_shape=jax.ShapeDtypeStruct((p_rows, D_MODEL), F32),
        compiler_params=_cparams(("arbitrary",)),
        name="moe_experts",
    )(blk_e, nused, xg, wg, wu, wd)


def _combine_ple_kernel(dst8_ref, nct_ref, x1_ref, rt_ref, slot_ref, p_ref, pg_ref, wpg_ref, bpg_ref, wpp_ref,
                        yg_ref, o_ref, ys, sem):
    tr = x1_ref.shape[0]
    i = pl.program_id(0)
    slot = i % 2

    def copies_into(s):
        return lambda near, far: pltpu.make_async_copy(
            yg_ref.at[pl.ds(far, CHUNK), :], ys.at[s, pl.ds(near, CHUNK), :], sem.at[s])

    def fetch(tile, s):
        _start_chunks(dst8_ref, tile, nct_ref[tile], copies_into(s))

        def clear(c, carry):
            ys[s, pl.ds(pl.multiple_of(c * CHUNK, CHUNK), CHUNK), :] = jnp.zeros((CHUNK, D_MODEL), F32)
            return carry

        lax.fori_loop(nct_ref[tile], NCH, clear, 0)

    @pl.when(i == 0)
    def _():
        fetch(0, 0)

    @pl.when(i + 1 < pl.num_programs(0))
    def _():
        fetch(i + 1, 1 - slot)

    _wait_chunks(nct_ref[i], copies_into(slot))

    rt = rt_ref[...]
    sl = slot_ref[...]
    col_id = lax.broadcasted_iota(I32, (tr, R_LOCAL), 1).astype(F32)
    yb = ys[slot].astype(BF16)
    y0 = _dot(jnp.where(col_id == sl[:, 0:1], 1.0, 0.0).astype(BF16), yb)
    y1 = _dot(jnp.where(col_id == sl[:, 1:2], 1.0, 0.0).astype(BF16), yb)
    x2 = x1_ref[...] + rt[:, R_W1:R_W1 + 1] * y0 + rt[:, R_W2:R_W2 + 1] * y1
    h3 = (x2 * lax.rsqrt(jnp.mean(x2 * x2, axis=-1, keepdims=True) + EPS) * pg_ref[...]).astype(BF16)
    gate = jax.nn.sigmoid(_dot(h3, wpg_ref[...]) + bpg_ref[...])
    o_ref[...] = x2 + gate * _dot(p_ref[...].astype(BF16), wpp_ref[...])


def _combine_ple(dst8, nct, x1, route, slots, p2, pg, wpg, bpg, wpp, yg):
    t = x1.shape[0]
    row = lambda n: pl.BlockSpec((TR, n), lambda i, d, c: (i, 0))
    full = lambda a: pl.BlockSpec(a.shape, lambda i, d, c: (0,) * a.ndim)
    return pl.pallas_call(
        _combine_ple_kernel,
        grid_spec=pltpu.PrefetchScalarGridSpec(
            num_scalar_prefetch=2,
            grid=(t // TR,),
            in_specs=[row(D_MODEL), row(LANES), row(LANES), row(PLE_DIM), full(pg), full(wpg), full(bpg),
                      full(wpp), pl.BlockSpec(memory_space=pl.ANY)],
            out_specs=row(D_MODEL),
            scratch_shapes=[pltpu.VMEM((2, R_LOCAL, D_MODEL), F32), pltpu.SemaphoreType.DMA((2,))],
        ),
        out_shape=jax.ShapeDtypeStruct((t, D_MODEL), F32),
        compiler_params=_cparams(("arbitrary",)),
        name="moe_combine_ple",
    )(dst8, nct, x1, route, slots, p2, pg, wpg, bpg, wpp, yg)


def _seg_sum_matrix():
    seg = np.arange(LANES) // HEAD_DIM
    return jnp.asarray(seg[:, None] == seg[None, :], BF16)


def _overlap_t(nc_rows, n_blk):
    nc = np.arange(nc_rows)[None, :] * CMP_STRIDE
    sb = np.arange(n_blk)[:, None] * SEL_BLOCK
    ov = (nc < sb + SEL_BLOCK) & (nc + CMP_BLOCK > sb) & (np.arange(nc_rows)[None, :] < nc_rows - 1)
    return jnp.asarray(ov, BF16)


def _compress_weights(pos, w1, w2):
    half = CMP_BLOCK // 2
    pe = jnp.broadcast_to(pos[:, None, :], (CMP_BLOCK, N_KV, HEAD_DIM)).reshape(2, half * N_KV * HEAD_DIM)
    w1r = w1.reshape(2, half, HEAD_DIM, CMP_HIDDEN)
    eye = jnp.eye(N_KV, dtype=w1.dtype)
    wx = jnp.einsum('hldj,gk->hlgdkj', w1r, eye).reshape(2, half * N_KV * HEAD_DIM, N_KV * CMP_HIDDEN)
    w2x = jnp.einsum('jd,gk->gjkd', w2, eye).reshape(N_KV * CMP_HIDDEN, N_KV * HEAD_DIM)
    return pe, wx[0].astype(BF16), wx[1].astype(BF16), w2x.astype(BF16)


def _layer(x, p_i, prm):
    b, s, _ = x.shape
    t = b * s
    x2 = x.reshape(t, D_MODEL)
    bd = _seg_sum_matrix()

    w = prm["w_in"]
    kv = lambda j: w[:, OFF_KV + j * KV_COLS:OFF_KV + (j + 1) * KV_COLS]
    w_nat = jnp.concatenate([kv(0), kv(1), kv(2), kv(4), w[:, OFF_CONV:]], axis=1).astype(BF16)
    w_t = jnp.concatenate([w[:, :Q_COLS], kv(3), kv(5),
                           jnp.pad(w[:, OFF_GATE:OFF_CONV], ((0, 0), (0, LANES - 3 * N_HEADS)))], axis=1).T.astype(BF16)
    qg = jnp.broadcast_to(prm["q_norm_g"][:, None], (HEAD_DIM, TM_PROJ))
    kg = jnp.tile(prm["k_norm_g"], (1, LANES // HEAD_DIM))
    qt, kc_raw, vc_raw, ks, vst, kw, vwt, gtt, glu, mg = _in_proj(
        x2, prm["attn_norm_g"][None, :], w_nat, w_t, qg, kg, bd, prm["merge_b"][None, :], s)

    nrow = s // CMP_STRIDE
    cw = [_compress_weights(prm["cmp_pos"][j], prm["cmp_w1"][j], prm["cmp_w2"][j]) for j in range(2)]
    pe_x, wa, wb = (jnp.stack([cw[0][j], cw[1][j]]) for j in range(3))
    kc, vct = _compress(kc_raw.reshape(b, nrow, CMP_STRIDE * LANES), vc_raw.reshape(b, nrow, CMP_STRIDE * LANES),
                        pe_x, wa, wb, cw[0][3], cw[1][3].T, bd, kg)

    attn = _attention(qt, gtt, kc, vct, ks.reshape(b, s, 4 * LANES), vst, kw.reshape(b, s, 2 * LANES), vwt,
                      _overlap_t(nrow, s // SEL_BLOCK), b, s)

    cact = _conformer_conv(glu, prm["conv_w"], prm["conv_b"][None, :], prm["conv_norm_g"][None, :],
                           prm["conv_norm_b"][None, :], b, s)

    wr_full = jnp.zeros((D_MODEL, LANES), F32)
    wr_full = wr_full.at[:, :N_EXPERTS].set(prm["w_router_expert"])
    wr_full = wr_full.at[:, GROUP_LANE0:GROUP_LANE0 + N_GROUPS].set(prm["w_router_group"])
    wr_hi = wr_full.astype(BF16)
    wr = jnp.stack([wr_hi, (wr_full - wr_hi.astype(F32)).astype(BF16)])
    br = jnp.zeros((1, LANES), F32)
    br = br.at[0, :N_EXPERTS].set(prm["b_router_expert"])
    br = br.at[0, GROUP_LANE0:GROUP_LANE0 + N_GROUPS].set(prm["b_router_group"])
    x1, h2, route, cnt = _merge_router(
        x2, attn, cact, mg, prm["w_attn_out"].astype(BF16), prm["w_conv_out"].astype(BF16),
        prm["w_out"].astype(BF16), prm["ffn_norm_g"][None, :], wr, br)

    n_tiles = t // TR
    counts = cnt[:, 0, :N_EXPERTS].astype(I32)
    seg = (counts + CHUNK - 1) // CHUNK * CHUNK
    local_off = jnp.cumsum(seg, axis=1) - seg
    nct = (jnp.sum(seg, axis=1) // CHUNK).astype(I32)
    total = jnp.sum(seg, axis=0)
    padded = (total + BM - 1) // BM * BM
    pends = jnp.cumsum(padded)
    far_off = (pends - padded)[None, :] + jnp.cumsum(seg, axis=0) - seg
    chunk = jnp.arange(NCH, dtype=I32)[None, :, None]
    lo8 = (local_off // CHUNK)[:, None, :]
    in_seg = (chunk >= lo8) & (chunk < ((local_off + seg) // CHUNK)[:, None, :])
    dst8 = (jnp.sum(jnp.where(in_seg, (far_off // CHUNK)[:, None, :] - lo8, 0), axis=-1)
            + chunk[:, :, 0]).astype(I32).reshape(n_tiles * NCH)
    lo_rows = jnp.broadcast_to(jnp.pad(local_off.astype(F32), ((0, 0), (0, LANES - N_EXPERTS)))[:, None, :],
                               (n_tiles, SUBLANES, LANES))
    n_blk = -(-(2 * t + n_tiles * N_EXPERTS * (CHUNK - 1) + N_EXPERTS * (BM - 1)) // BM)
    blk_e = jnp.minimum(jnp.sum(jnp.arange(n_blk, dtype=I32)[:, None] * BM >= pends[None, :], axis=1),
                        N_EXPERTS - 1).astype(I32)
    nused = (pends[-1:] // BM).astype(I32)
    pad8 = ((pends - padded + total) // CHUNK).astype(I32)
    npad = ((padded - total) // CHUNK).astype(I32)

    xg, slots = _dispatch(dst8, nct, pad8, npad, nused, h2, route, lo_rows, n_blk * BM)
    yg = _experts(blk_e, nused, xg, prm["w_exp_gate"], prm["w_exp_up"], prm["w_exp_down"])
    out = _combine_ple(dst8, nct, x1, route, slots, p_i.reshape(t, PLE_DIM), prm["ple_norm_g"][None, :],
                       prm["w_ple_gate"].astype(BF16), prm["b_ple_gate"][None, :], prm["w_ple_proj"].astype(BF16), yg)
    return out.reshape(b, s, D_MODEL)


def kernel(x, p, attn_norm_g, w_in, q_norm_g, k_norm_g, cmp_pos, cmp_w1, cmp_w2, w_attn_out, conv_w, conv_b, conv_norm_g, conv_norm_b, w_conv_out, merge_b, w_out, ffn_norm_g, w_router_group, b_router_group, w_router_expert, b_router_expert, w_exp_gate, w_exp_up, w_exp_down, ple_norm_g, w_ple_gate, b_ple_gate, w_ple_proj):
    stacked = dict(attn_norm_g=attn_norm_g, w_in=w_in, q_norm_g=q_norm_g, k_norm_g=k_norm_g, cmp_pos=cmp_pos,
                   cmp_w1=cmp_w1, cmp_w2=cmp_w2, w_attn_out=w_attn_out, conv_w=conv_w, conv_b=conv_b,
                   conv_norm_g=conv_norm_g, conv_norm_b=conv_norm_b, w_conv_out=w_conv_out, merge_b=merge_b,
                   w_out=w_out, ffn_norm_g=ffn_norm_g, w_router_group=w_router_group,
                   b_router_group=b_router_group, w_router_expert=w_router_expert,
                   b_router_expert=b_router_expert, w_exp_gate=w_exp_gate, w_exp_up=w_exp_up,
                   w_exp_down=w_exp_down, ple_norm_g=ple_norm_g, w_ple_gate=w_ple_gate, b_ple_gate=b_ple_gate,
                   w_ple_proj=w_ple_proj)
    for i in range(w_in.shape[0]):
        x = _layer(x, p[i], {k: v[i] for k, v in stacked.items()})
    return x
```

```python
import functools

import numpy as np
import jax
import jax.numpy as jnp
from jax import lax
from jax.experimental import pallas as pl
from jax.experimental.pallas import tpu as pltpu

F32 = jnp.float32
BF16 = jnp.bfloat16
I32 = jnp.int32

D_MODEL = 1024
PLE_DIM = 256
N_HEADS = 8
N_KV = 2
HPG = N_HEADS // N_KV
HEAD_DIM = 64
CMP_BLOCK = 32
CMP_STRIDE = 16
CMP_HIDDEN = 128
SEL_BLOCK = 64
N_SEL = 8
WINDOW = 512
CONV_CH = 512
CONV_WIDTH = 31
N_GROUPS = 4
EXP_PER_GROUP = 8
N_EXPERTS = N_GROUPS * EXP_PER_GROUP
EXPERT_FF = 512
EPS = 1e-6
NEG = -1e30
BIG = 1e4
Q_COLS = N_HEADS * HEAD_DIM
KV_COLS = N_KV * HEAD_DIM
OFF_KV = Q_COLS
OFF_GATE = OFF_KV + 6 * KV_COLS
OFF_CONV = OFF_GATE + 3 * N_HEADS
OFF_MERGE = OFF_CONV + 2 * CONV_CH

LANES = 128
SUBLANES = 8
VMEM_LIMIT = 56 * 1024 * 1024

TM_PROJ = 256
QB = 256
KT = 256
TS_CONV = 512
HALO = 32
TR = 512
BM = 256
CHUNK = SUBLANES
R_LOCAL = 2 * TR + N_EXPERTS * CHUNK
NCH = R_LOCAL // CHUNK

N_KC = 0
N_VC = N_KC + LANES
N_KS = N_VC + LANES
N_KW = N_KS + LANES
N_CONV = N_KW + LANES
N_MERGE = N_CONV + 2 * CONV_CH
T_Q = 0
T_VS = T_Q + Q_COLS
T_VW = T_VS + LANES
T_GATE = T_VW + LANES

POS_HI_LANE = HEAD_DIM
POS_LO_LANE = HEAD_DIM + 1
UNSELECTED = -(2.0 ** 40)

R_E1, R_E2, R_W1, R_W2, R_K1, R_K2 = 0, 1, 2, 3, 4, 5
GROUP_LANE0 = N_EXPERTS


def _cparams(sem, vmem=VMEM_LIMIT):
    return pltpu.CompilerParams(dimension_semantics=sem, vmem_limit_bytes=vmem)


def _dot(a, b):
    return jnp.dot(a, b, preferred_element_type=F32)


def _dot_nt(a, b):
    return lax.dot_general(a, b, (((1,), (1,)), ((), ())), preferred_element_type=F32)


def _split(x):
    hi = x.astype(BF16)
    lo = (x - hi.astype(F32)).astype(BF16)
    return hi, lo


def _seg_rmsnorm(z, bd):
    hi, lo = _split(z * z)
    ss = _dot(hi, bd) + _dot(lo, bd)
    return z * lax.rsqrt(ss * (1.0 / HEAD_DIM) + EPS)


def _with_pos_lanes(kn, pos):
    lane = lax.broadcasted_iota(I32, kn.shape, 1)
    hi = pos // SEL_BLOCK * SEL_BLOCK
    pos_lanes = jnp.where(lane == POS_HI_LANE, hi.astype(F32),
                          jnp.where(lane == POS_LO_LANE, (pos - hi).astype(F32), 0.0))
    lo_half = lane < HEAD_DIM
    return (jnp.where(lo_half, kn, pos_lanes).astype(BF16),
            jnp.where(lo_half, pltpu.roll(kn, HEAD_DIM, 1), pos_lanes).astype(BF16))


def _in_proj_kernel(x_ref, g_ref, w_ref, wt_ref, qg_ref, kg_ref, bd_ref, mb_ref,
                    qt_o, kc_o, vc_o, ks_o, vst_o, kw_o, vwt_o, gtt_o, glu_o, mg_o, *, seq_len):
    tm = x_ref.shape[0]
    x = x_ref[...]
    h = (x * lax.rsqrt(jnp.mean(x * x, axis=-1, keepdims=True) + EPS) * g_ref[...]).astype(BF16)
    bd = bd_ref[...]
    lane = lax.broadcasted_iota(I32, (tm, LANES), 1)

    def proj(a, b):
        return _dot(h, w_ref[:, a:b])

    zt = _dot_nt(wt_ref[...], h)
    for hd in range(N_HEADS):
        zh = zt[T_Q + hd * HEAD_DIM:T_Q + (hd + 1) * HEAD_DIM, :]
        ms = jnp.sum(zh * zh, axis=0, keepdims=True) * (1.0 / HEAD_DIM)
        qt_o[hd * HEAD_DIM:(hd + 1) * HEAD_DIM, :] = (
            zh * lax.rsqrt(ms + EPS) * qg_ref[...] * (HEAD_DIM ** -0.5)).astype(BF16)
    ones = jnp.ones((HEAD_DIM, tm), F32)
    for off, v_o in ((T_VS, vst_o), (T_VW, vwt_o)):
        v_o[0] = jnp.concatenate([zt[off:off + HEAD_DIM, :], ones, zt[off + HEAD_DIM:off + LANES, :], ones],
                                 axis=0).astype(BF16)
    gtt_o[...] = jax.nn.sigmoid(zt[T_GATE:T_GATE + LANES, :])

    kc_o[...] = proj(N_KC, N_KC + LANES)
    vc_o[...] = proj(N_VC, N_VC + LANES)

    pos = (pl.program_id(0) * tm) % seq_len + lax.broadcasted_iota(I32, (tm, LANES), 0)
    onehot = jnp.where(lane == pos // SEL_BLOCK, 1.0, 0.0).astype(BF16)

    def key_tiles(zk, gain):
        return _with_pos_lanes(_seg_rmsnorm(zk, bd) * gain, pos)

    k0, k1 = key_tiles(proj(N_KS, N_KS + LANES), kg_ref[1:2, :])
    ks_o[...] = jnp.concatenate([k0, onehot, k1, onehot], axis=1)
    k0, k1 = key_tiles(proj(N_KW, N_KW + LANES), kg_ref[2:3, :])
    kw_o[...] = jnp.concatenate([k0, k1], axis=1)
    za = proj(N_CONV, N_CONV + CONV_CH)
    zg = proj(N_CONV + CONV_CH, N_CONV + 2 * CONV_CH)
    glu_o[...] = za * jax.nn.sigmoid(zg)
    for c in range(2):
        zm = proj(N_MERGE + c * D_MODEL, N_MERGE + (c + 1) * D_MODEL)
        mg_o[:, c * D_MODEL:(c + 1) * D_MODEL] = jax.nn.sigmoid(
            zm + mb_ref[:, c * D_MODEL:(c + 1) * D_MODEL]).astype(BF16)


def _in_proj(x2, norm_g, w_nat, w_t, qg, kg, bd, mb, seq_len):
    t = x2.shape[0]
    tm = TM_PROJ
    assert tm == KT and seq_len % tm == 0 and seq_len // SEL_BLOCK <= HEAD_DIM
    row = lambda n: pl.BlockSpec((tm, n), lambda i: (i, 0))
    col = lambda n: pl.BlockSpec((n, tm), lambda i: (0, i))
    vt = pl.BlockSpec((1, 2 * LANES, tm), lambda i: (i, 0, 0))
    full = lambda a: pl.BlockSpec(a.shape, lambda i: (0,) * a.ndim)
    sds = jax.ShapeDtypeStruct
    return pl.pallas_call(
        functools.partial(_in_proj_kernel, seq_len=seq_len),
        grid=(t // tm,),
        in_specs=[row(D_MODEL), full(norm_g), full(w_nat), full(w_t), full(qg), full(kg), full(bd), full(mb)],
        out_specs=[col(Q_COLS), row(LANES), row(LANES), row(4 * LANES), vt, row(2 * LANES), vt, col(LANES),
                   row(CONV_CH), row(2 * D_MODEL)],
        out_shape=[sds((Q_COLS, t), BF16), sds((t, LANES), F32), sds((t, LANES), F32), sds((t, 4 * LANES), BF16),
                   sds((t // tm, 2 * LANES, tm), BF16), sds((t, 2 * LANES), BF16), sds((t // tm, 2 * LANES, tm), BF16),
                   sds((LANES, t), F32), sds((t, CONV_CH), F32), sds((t, 2 * D_MODEL), BF16)],
        compiler_params=_cparams(("parallel",)),
        name="in_proj",
    )(x2, norm_g, w_nat, w_t, qg, kg, bd, mb)


def _compress_kernel(ck_ref, cv_ref, pe_ref, wa_ref, wb_ref, w2k_ref, w2vt_ref, bd_ref, kg_ref, kc_o, vct_o):
    nrow = ck_ref.shape[1]

    def hidden(idx, c_ref):
        c = c_ref[0]
        ya = _dot((c + pe_ref[idx, 0:1, :]).astype(BF16), wa_ref[idx])
        yb = _dot((c + pe_ref[idx, 1:2, :]).astype(BF16), wb_ref[idx])
        return jax.nn.gelu(ya + pltpu.roll(yb, nrow - 1, 0)).astype(BF16)

    kn = _seg_rmsnorm(_dot(hidden(0, ck_ref), w2k_ref[...]), bd_ref[...]) * kg_ref[0:1, :]
    c_end = lax.broadcasted_iota(I32, (nrow, LANES), 0) * CMP_STRIDE + (CMP_BLOCK - 1)
    kc_o[0] = jnp.concatenate(_with_pos_lanes(kn, c_end), axis=1)
    vt = _dot_nt(w2vt_ref[...], hidden(1, cv_ref))
    ones = jnp.ones((HEAD_DIM, nrow), F32)
    vct_o[0] = jnp.concatenate([vt[:HEAD_DIM], ones, vt[HEAD_DIM:], ones], axis=0).astype(BF16)


def _compress(kc_raw, vc_raw, pe_x, wa, wb, w2k, w2vt, bd, kg):
    b, nrow, width = kc_raw.shape
    batch = lambda r, n: pl.BlockSpec((1, r, n), lambda i: (i, 0, 0))
    full = lambda a: pl.BlockSpec(a.shape, lambda i: (0,) * a.ndim)
    return pl.pallas_call(
        _compress_kernel,
        grid=(b,),
        in_specs=[batch(nrow, width), batch(nrow, width), full(pe_x), full(wa), full(wb), full(w2k), full(w2vt),
                  full(bd), full(kg)],
        out_specs=[batch(nrow, 2 * LANES), batch(2 * LANES, nrow)],
        out_shape=[jax.ShapeDtypeStruct((b, nrow, 2 * LANES), BF16),
                   jax.ShapeDtypeStruct((b, 2 * LANES, nrow), BF16)],
        compiler_params=_cparams(("parallel",)),
        name="compress",
    )(kc_raw, vc_raw, pe_x, wa, wb, w2k, w2vt, bd, kg)


def _halve_rows(x, op):
    rows = x.shape[0]
    while rows > SUBLANES:
        rows //= 2
        x = op(x[:rows], x[rows:])
    return x


def _attn_kernel(qt_ref, gtt_ref, kc_ref, vct_ref, ks_ref, vst_ref, kw_ref, vwt_ref, ovt_ref, o_ref,
                 tiles_ref, sa_ref, sb_ref, m_ref, acc_ref):
    i = pl.program_id(1)
    q0 = i * QB
    nc_rows = kc_ref.shape[1]
    n_blk = ovt_ref.shape[0]
    hpc = 2 * LANES // QB
    pw = hpc * QB
    n_pairs = N_HEADS // hpc
    group_of = lambda c: c * hpc // HPG
    rel = ((lax.broadcasted_iota(I32, (KT, pw), 1) & (QB - 1))
           - lax.broadcasted_iota(I32, (KT, pw), 0)).astype(F32)
    slope_rows = lax.broadcasted_iota(I32, (HEAD_DIM, QB), 0) < 2
    gtt = gtt_ref[...]

    pairs = []
    for pr in range(n_pairs):
        q_pos = []
        for hd in range(pr * hpc, (pr + 1) * hpc):
            slope = jnp.where(slope_rows, 2.0 ** -(hd + 1), 0.0).astype(BF16)
            q_pos.append(jnp.concatenate([qt_ref[hd * HEAD_DIM:(hd + 1) * HEAD_DIM, :], slope], axis=0))
        pairs.append(jnp.concatenate(q_pos, axis=1))

    def update(s, vt, m, acc):
        m_new = jnp.maximum(m, jnp.max(_halve_rows(s, jnp.maximum), axis=0, keepdims=True))
        p = jnp.exp(s - m_new).astype(BF16)
        return m_new, jnp.exp(m - m_new) * acc + _dot(vt, p)

    def scores(j, q_list, k_ref, k_width):
        k0 = pl.multiple_of(j * KT, KT)
        return tuple(_dot(k_ref[0, pl.ds(k0, KT), group_of(c) * k_width:(group_of(c) + 1) * k_width], q_list[c])
                     for c in range(n_pairs))

    def consume(j, s_list, state, vt_ref, mask_fn, extra_dist=0.0):
        keep = None if mask_fn is None else mask_fn(rel + ((q0 - j * KT).astype(F32) + extra_dist))
        out = []
        for c in range(n_pairs):
            g = group_of(c)
            s = s_list[c] if keep is None else jnp.where(keep, s_list[c], NEG)
            out += update(s, vt_ref[j, g * LANES:(g + 1) * LANES, :], *state[2 * c:2 * c + 2])
        return tuple(out)

    j_diag = (q0 + QB - 1) // KT
    init = (jnp.full((1, pw), NEG, F32), jnp.zeros((LANES, pw), F32)) * n_pairs

    def selected_branch(q_list, n_act):
        def fill(buf, n):
            for c, s in enumerate(scores(tiles_ref[n], q_list, ks_ref, 2 * LANES)):
                buf[c] = s

        def use(buf, n, mask_fn):
            state = []
            for c in range(n_pairs):
                state += [m_ref[c, 0:1, :], acc_ref[c]]
            out = consume(tiles_ref[n], [buf[c] for c in range(n_pairs)], state, vst_ref, mask_fn)
            for c in range(n_pairs):
                m_ref[c, 0:1, :] = out[2 * c]
                acc_ref[c] = out[2 * c + 1]

        for c in range(n_pairs):
            m_ref[c] = jnp.full((SUBLANES, pw), NEG, F32)
            acc_ref[c] = jnp.zeros((LANES, pw), F32)
        fill(sa_ref, 0)

        def two_tiles(h, carry):
            n = 2 * h
            fill(sb_ref, n + 1)
            use(sa_ref, n, None)

            @pl.when(n + 1 < n_act)
            def _():
                fill(sa_ref, n + 2)
                use(sb_ref, n + 1, None)
            return carry

        lax.fori_loop(0, (n_act + 1) // 2, two_tiles, 0)

        @pl.when(n_act % 2 == 1)
        def _():
            for c in range(n_pairs):
                sa_ref[c] = sb_ref[c]

        use(sa_ref, n_act, lambda d: d >= 0.0)
        return [acc_ref[c] for c in range(n_pairs)]

    assert WINDOW == 2 * KT and KT % QB == 0
    win_out = init
    for w in range(WINDOW // KT + 1):
        jw = j_diag - WINDOW // KT + w
        before_start = jnp.where(jw < 0, float(WINDOW + KT), 0.0)
        jc = jnp.maximum(jw, 0)
        mask_fn = ((lambda d: d < float(WINDOW)), (lambda d, jw=jw: jw >= 0), (lambda d: d >= 0.0))[w]
        win_out = consume(jc, scores(jc, pairs, kw_ref, LANES), win_out, vwt_ref, mask_fn, before_start)

    cidx = lax.broadcasted_iota(I32, (nc_rows, pw), 0)
    c_end = cidx * CMP_STRIDE + (CMP_BLOCK - 1)
    cmask = (c_end <= q0 + (lax.broadcasted_iota(I32, (nc_rows, pw), 1) & (QB - 1))) & (cidx < nc_rows - 1)

    sees_any = jnp.where(q0 + (lax.broadcasted_iota(I32, (1, pw), 1) & (QB - 1)) >= CMP_BLOCK - 1, 1.0, 0.0)
    o_cmp, imp = [], []
    for g in range(N_KV):
        imp_g = jnp.zeros((n_blk, QB), F32)
        for pr in range(g * HPG // hpc, (g + 1) * HPG // hpc):
            s = jnp.where(cmask, _dot(kc_ref[0, :, g * LANES:(g + 1) * LANES], pairs[pr]), NEG)
            e = jnp.exp(s - jnp.max(_halve_rows(s, jnp.maximum), axis=0, keepdims=True)).astype(BF16)
            acc = _dot(vct_ref[0, g * LANES:(g + 1) * LANES, :], e)
            inv = sees_any / acc[HEAD_DIM:HEAD_DIM + 1]
            o_cmp.append(acc[:HEAD_DIM] * inv)
            weighted = _dot(ovt_ref[...], e) * inv
            for k in range(hpc):
                imp_g = imp_g + weighted[:, k * QB:(k + 1) * QB]
        imp.append(imp_g)

    blk = lax.broadcasted_iota(I32, (n_blk, N_KV * QB), 0)
    tl = q0 + (lax.broadcasted_iota(I32, (n_blk, N_KV * QB), 1) & (QB - 1))
    valid = blk * SEL_BLOCK <= tl
    cur = tl // SEL_BLOCK
    forced = (blk == 0) | (blk == cur) | (blk == cur - 1)
    sel = forced & valid
    score = jnp.where(valid, jnp.where(sel, -jnp.inf, jnp.concatenate(imp, axis=1)), -BIG)
    for _ in range(min(N_SEL, n_blk) - 3):
        mx = jnp.max(score, axis=0, keepdims=True)
        first = jnp.min(jnp.where(score == mx, blk, n_blk), axis=0, keepdims=True)
        pick = blk == first
        sel = sel | pick
        score = jnp.where(pick, -jnp.inf, score)
    chosen = sel & valid
    any_sel = jnp.where(chosen[:, :QB] | chosen[:, QB:], 1.0, 0.0)
    unsel = jnp.where(chosen, 0.0, UNSELECTED).astype(BF16)
    bias = jnp.concatenate([unsel, jnp.zeros((2 * HEAD_DIM - n_blk, N_KV * QB), BF16)], axis=0)
    q_sel = []
    for c in range(n_pairs):
        g_bias = bias[:, group_of(c) * QB:(group_of(c) + 1) * QB]
        q_sel.append(jnp.concatenate([pairs[c], jnp.concatenate([g_bias] * hpc, axis=1)], axis=0))

    per_blk = jnp.sum(any_sel, axis=1, keepdims=True)
    n_act = jnp.int32(0)
    for jt in range(n_blk * SEL_BLOCK // KT):
        blocks = slice(jt * (KT // SEL_BLOCK), (jt + 1) * (KT // SEL_BLOCK))
        tiles_ref[n_act] = jnp.int32(jt)
        n_act = n_act + ((jnp.sum(per_blk[blocks]) > 0.0) & (jt < j_diag)).astype(I32)
    tiles_ref[n_act] = j_diag

    sel_acc = selected_branch(q_sel, n_act)

    o_sel = [a[:HEAD_DIM] / a[HEAD_DIM:HEAD_DIM + 1] for a in sel_acc]
    o_win = [a[:HEAD_DIM] / a[HEAD_DIM:HEAD_DIM + 1] for a in win_out[1::2]]
    mixed = []
    for hd in range(N_HEADS):
        c, lanes = hd // hpc, slice((hd % hpc) * QB, (hd % hpc + 1) * QB)
        mixed.append(gtt[3 * hd:3 * hd + 1, :] * o_cmp[c][:, lanes] + gtt[3 * hd + 1:3 * hd + 2, :] * o_sel[c][:, lanes]
                     + gtt[3 * hd + 2:3 * hd + 3, :] * o_win[c][:, lanes])
    for pr in range(N_HEADS // 2):
        two = jnp.concatenate(mixed[2 * pr:2 * pr + 2], axis=0)
        o_ref[:, pr * LANES:(pr + 1) * LANES] = jnp.concatenate(
            [two[:, k * LANES:(k + 1) * LANES].T for k in range(QB // LANES)], axis=0).astype(BF16)


def _attention(qt, gtt, kc, vct, ks, vst, kw, vwt, ovt, b, s):
    n_q = s // QB
    n_tiles = N_HEADS * QB // (2 * LANES)
    per_q = lambda n: pl.BlockSpec((n, QB), lambda bi, i: (0, bi * n_q + i))
    per_b = lambda a: pl.BlockSpec((1,) + a.shape[1:], lambda bi, i: (bi, 0, 0))
    vt = pl.BlockSpec((s // KT, 2 * LANES, KT), lambda bi, i: (bi, 0, 0))
    return pl.pallas_call(
        _attn_kernel,
        grid=(b, n_q),
        in_specs=[per_q(Q_COLS), per_q(LANES), per_b(kc), per_b(vct), per_b(ks), vt, per_b(kw), vt,
                  pl.BlockSpec(ovt.shape, lambda bi, i: (0, 0))],
        out_specs=pl.BlockSpec((QB, Q_COLS), lambda bi, i: (bi * n_q + i, 0)),
        out_shape=jax.ShapeDtypeStruct((b * s, Q_COLS), BF16),
        scratch_shapes=[pltpu.SMEM((s // KT + 1,), I32),
                        pltpu.VMEM((n_tiles, KT, 2 * LANES), F32), pltpu.VMEM((n_tiles, KT, 2 * LANES), F32),
                        pltpu.VMEM((n_tiles, SUBLANES, 2 * LANES), F32),
                        pltpu.VMEM((n_tiles, LANES, 2 * LANES), F32)],
        compiler_params=_cparams(("parallel", "parallel")),
        name="nsa_attention",
    )(qt, gtt, kc, vct, ks, vst, kw, vwt, ovt)


def _conv_kernel(z_ref, w_ref, b_ref, g_ref, beta_ref, o_ref, buf, shifted):
    ts = z_ref.shape[0]
    chunk = 64

    @pl.when(pl.program_id(1) == 0)
    def _():
        buf[0:HALO, :] = jnp.zeros((HALO, CONV_CH), F32)

    buf[HALO:HALO + ts, :] = z_ref[...]
    span = ts + HALO - SUBLANES
    for r in range(1, SUBLANES):
        shifted[r - 1, 0:span, :] = buf[r:r + span, :]

    def tap_rows(k, c):
        start = HALO - (CONV_WIDTH - 1) + k
        r = start % SUBLANES
        src = buf if r == 0 else shifted.at[r - 1]
        return src[start - r + c * chunk:start - r + (c + 1) * chunk, :]

    for c in range(ts // chunk):
        acc = jnp.zeros((chunk, CONV_CH), F32) + b_ref[...]
        for k in range(CONV_WIDTH):
            acc = acc + tap_rows(k, c) * w_ref[k:k + 1, :]
        mu = jnp.mean(acc, axis=-1, keepdims=True)
        d = acc - mu
        var = jnp.mean(d * d, axis=-1, keepdims=True)
        y = d * lax.rsqrt(var + EPS) * g_ref[...] + beta_ref[...]
        o_ref[c * chunk:(c + 1) * chunk, :] = (y * jax.nn.sigmoid(y)).astype(BF16)
    buf[0:HALO, :] = buf[ts:ts + HALO, :]


def _conformer_conv(glu, w, bias, g, beta, b, s):
    ts = min(TS_CONV, s)
    full = lambda a: pl.BlockSpec(a.shape, lambda bi, i: (0, 0))
    row = pl.BlockSpec((ts, CONV_CH), lambda bi, i: (bi * (s // ts) + i, 0))
    return pl.pallas_call(
        _conv_kernel,
        grid=(b, s // ts),
        in_specs=[row, full(w), full(bias), full(g), full(beta)],
        out_specs=row,
        out_shape=jax.ShapeDtypeStruct((b * s, CONV_CH), BF16),
        scratch_shapes=[pltpu.VMEM((HALO + ts, CONV_CH), F32),
                        pltpu.VMEM((SUBLANES - 1, HALO + ts, CONV_CH), F32)],
        compiler_params=_cparams(("arbitrary", "arbitrary")),
        name="conformer_conv",
    )(glu, w, bias, g, beta)


def _merge_router_kernel(x_ref, a_ref, c_ref, mg_ref, wa_ref, wc_ref, wo_ref, fg_ref, wr_ref, br_ref,
                         x1_o, h2_o, rt_o, cnt_o):
    tm = x_ref.shape[0]
    y_a = _dot(a_ref[...], wa_ref[...])
    y_c = _dot(c_ref[...], wc_ref[...])
    mix = mg_ref[:, 0:D_MODEL].astype(F32) * y_a + mg_ref[:, D_MODEL:2 * D_MODEL].astype(F32) * y_c
    x1 = x_ref[...] + _dot(mix.astype(BF16), wo_ref[...])
    x1_o[...] = x1
    h2 = x1 * lax.rsqrt(jnp.mean(x1 * x1, axis=-1, keepdims=True) + EPS) * fg_ref[...]
    h2_o[...] = h2

    hh, hl = _split(h2)
    logits = _dot(hh, wr_ref[0]) + _dot(hl, wr_ref[0]) + _dot(hh, wr_ref[1]) + br_ref[...]
    lane = lax.broadcasted_iota(I32, (tm, LANES), 1)
    is_g = (lane >= GROUP_LANE0) & (lane < GROUP_LANE0 + N_GROUPS)
    gmax = jnp.max(jnp.where(is_g, logits, -jnp.inf), axis=-1, keepdims=True)
    gsel = jnp.min(jnp.where(is_g & (logits == gmax), lane - GROUP_LANE0, N_GROUPS), axis=-1, keepdims=True)
    pg_sel = 1.0 / jnp.sum(jnp.where(is_g, jnp.exp(logits - gmax), 0.0), axis=-1, keepdims=True)
    in_grp = (lane < N_EXPERTS) & (lane // EXP_PER_GROUP == gsel)
    emax = jnp.max(jnp.where(in_grp, logits, -jnp.inf), axis=-1, keepdims=True)
    ee = jnp.where(in_grp, jnp.exp(logits - emax), 0.0)
    pe = jnp.where(in_grp, ee / jnp.sum(ee, axis=-1, keepdims=True), -1.0)
    p1 = jnp.max(pe, axis=-1, keepdims=True)
    i1 = jnp.min(jnp.where(pe == p1, lane, LANES), axis=-1, keepdims=True)
    pe2 = jnp.where(lane == i1, -1.0, pe)
    p2 = jnp.max(pe2, axis=-1, keepdims=True)
    i2 = jnp.min(jnp.where(pe2 == p2, lane, LANES), axis=-1, keepdims=True)
    w1 = pg_sel * p1 / (p1 + p2)
    w2 = pg_sel * p2 / (p1 + p2)

    hot1 = lane == i1
    hot2 = lane == i2
    ind = jnp.where(hot1 | hot2, 1.0, 0.0)
    tri = (lax.broadcasted_iota(I32, (tm, tm), 1) < lax.broadcasted_iota(I32, (tm, tm), 0)).astype(BF16)
    before = _dot(tri, ind.astype(BF16))
    k1 = jnp.sum(jnp.where(hot1, before, 0.0), axis=-1, keepdims=True)
    k2 = jnp.sum(jnp.where(hot2, before, 0.0), axis=-1, keepdims=True)
    rec = jnp.zeros((tm, LANES), F32)
    for slot, val in ((R_E1, i1.astype(F32)), (R_E2, i2.astype(F32)), (R_W1, w1), (R_W2, w2), (R_K1, k1), (R_K2, k2)):
        rec = jnp.where(lane == slot, val, rec)
    rt_o[...] = rec
    cnt_o[0] = jnp.broadcast_to(jnp.sum(ind, axis=0, keepdims=True), (SUBLANES, LANES))


def _merge_router(x2, attn, cact, mg, wa, wc, wo, fg, wr, br):
    t = x2.shape[0]
    tm = TR
    row = lambda n: pl.BlockSpec((tm, n), lambda i: (i, 0))
    full = lambda a: pl.BlockSpec(a.shape, lambda i: (0,) * a.ndim)
    return pl.pallas_call(
        _merge_router_kernel,
        grid=(t // tm,),
        in_specs=[row(D_MODEL), row(Q_COLS), row(CONV_CH), row(2 * D_MODEL), full(wa), full(wc), full(wo),
                  full(fg), full(wr), full(br)],
        out_specs=[row(D_MODEL), row(D_MODEL), row(LANES), pl.BlockSpec((1, SUBLANES, LANES), lambda i: (i, 0, 0))],
        out_shape=[jax.ShapeDtypeStruct((t, D_MODEL), F32), jax.ShapeDtypeStruct((t, D_MODEL), F32),
                   jax.ShapeDtypeStruct((t, LANES), F32), jax.ShapeDtypeStruct((t // tm, SUBLANES, LANES), F32)],
        compiler_params=_cparams(("parallel",)),
        name="merge_router",
    )(x2, attn, cact, mg, wa, wc, wo, fg, wr, br)


def _start_chunks(dst8_ref, tile, n_chunks, make_copy):
    def issue(c, carry):
        far = pl.multiple_of(dst8_ref[tile * NCH + c] * CHUNK, CHUNK)
        make_copy(pl.multiple_of(c * CHUNK, CHUNK), far).start()
        return carry

    lax.fori_loop(0, n_chunks, issue, 0)


def _wait_chunks(n_chunks, make_copy):
    def wait_one(c, carry):
        make_copy(0, 0).wait()
        return carry

    lax.fori_loop(0, n_chunks, wait_one, 0)


def _dispatch_kernel(dst8_ref, nct_ref, pad8_ref, npad_ref, nused_ref, h_ref, rt_ref, lo_ref, xg_ref, slot_o, xs,
                     zbuf, sem, zsem):
    tr = h_ref.shape[0]

    @pl.when(pl.program_id(0) == 0)
    def _():
        zbuf[...] = jnp.zeros(zbuf.shape, F32)

        def zero_chunk(far):
            return pltpu.make_async_copy(zbuf.at[pl.ds(0, CHUNK), :], xg_ref.at[pl.ds(far, CHUNK), :], zsem)

        def zero_block(blk):
            return pltpu.make_async_copy(zbuf, xg_ref.at[pl.ds(pl.multiple_of(blk * BM, BM), BM), :], sem.at[0])

        def per_expert(e, carry, wait):
            def one(c, inner):
                if wait:
                    zero_chunk(0).wait()
                else:
                    zero_chunk(pl.multiple_of((pad8_ref[e] + c) * CHUNK, CHUNK)).start()
                return inner
            return lax.fori_loop(0, npad_ref[e], one, carry)

        def tail(blk, carry, wait):
            if wait:
                zero_block(0).wait()
            else:
                zero_block(blk).start()
            return carry

        n_blocks = xg_ref.shape[0] // BM
        lax.fori_loop(0, N_EXPERTS, functools.partial(per_expert, wait=False), 0)
        lax.fori_loop(nused_ref[0], n_blocks, functools.partial(tail, wait=False), 0)
        lax.fori_loop(0, N_EXPERTS, functools.partial(per_expert, wait=True), 0)
        lax.fori_loop(nused_ref[0], n_blocks, functools.partial(tail, wait=True), 0)

    rt = rt_ref[...]
    lane = lax.broadcasted_iota(I32, (tr, LANES), 1)
    lane_f = lane.astype(F32)
    lo_row = lo_ref[0, 0:1, :]
    slots = []
    for e_lane, k_lane in ((R_E1, R_K1), (R_E2, R_K2)):
        hot = lane_f == rt[:, e_lane:e_lane + 1]
        slots.append(jnp.sum(jnp.where(hot, lo_row, 0.0), axis=-1, keepdims=True) + rt[:, k_lane:k_lane + 1])
    rec = jnp.where(lane == 0, slots[0], jnp.where(lane == 1, slots[1], 0.0))
    slot_o[...] = rec
    rec_t = jnp.concatenate([rec[c * LANES:(c + 1) * LANES].T for c in range(tr // LANES)], axis=1)
    row_id = lax.broadcasted_iota(I32, (R_LOCAL, tr), 0).astype(F32)
    pick = (row_id == rec_t[0:1, :]) | (row_id == rec_t[1:2, :])
    staged = _dot(jnp.where(pick, 1.0, 0.0).astype(BF16), h_ref[...].astype(BF16))

    i = pl.program_id(0)
    last = pl.num_programs(0) - 1
    slot = i % 2

    def copies_from(s):
        return lambda near, far: pltpu.make_async_copy(
            xs.at[s, pl.ds(near, CHUNK), :], xg_ref.at[pl.ds(far, CHUNK), :], sem.at[s])

    @pl.when(i >= 2)
    def _():
        _wait_chunks(nct_ref[i - 2], copies_from(slot))

    xs[slot] = staged
    _start_chunks(dst8_ref, i, nct_ref[i], copies_from(slot))

    @pl.when(i == last)
    def _():
        @pl.when(i >= 1)
        def _():
            _wait_chunks(nct_ref[i - 1], copies_from(1 - slot))
        _wait_chunks(nct_ref[i], copies_from(slot))


def _dispatch(dst8, nct, pad8, npad, nused, h2, route, lo_rows, p_rows):
    t = h2.shape[0]
    return pl.pallas_call(
        _dispatch_kernel,
        grid_spec=pltpu.PrefetchScalarGridSpec(
            num_scalar_prefetch=5,
            grid=(t // TR,),
            in_specs=[pl.BlockSpec((TR, D_MODEL), lambda i, *_: (i, 0)),
                      pl.BlockSpec((TR, LANES), lambda i, *_: (i, 0)),
                      pl.BlockSpec((1, SUBLANES, LANES), lambda i, *_: (i, 0, 0))],
            out_specs=[pl.BlockSpec(memory_space=pl.ANY),
                       pl.BlockSpec((TR, LANES), lambda i, *_: (i, 0))],
            scratch_shapes=[pltpu.VMEM((2, R_LOCAL, D_MODEL), F32), pltpu.VMEM((BM, D_MODEL), F32),
                            pltpu.SemaphoreType.DMA((2,)), pltpu.SemaphoreType.DMA(())],
        ),
        out_shape=[jax.ShapeDtypeStruct((p_rows, D_MODEL), F32), jax.ShapeDtypeStruct((t, LANES), F32)],
        compiler_params=_cparams(("arbitrary",)),
        name="moe_dispatch",
    )(dst8, nct, pad8, npad, nused, h2, route, lo_rows)


def _expert_kernel(blk_e_ref, nused_ref, x_ref, wg_ref, wu_ref, wd_ref, y_ref, wg_b, wu_b, wd_b):
    i = pl.program_id(0)
    used = i < nused_ref[0]

    @pl.when((i == 0) | (blk_e_ref[i] != blk_e_ref[jnp.maximum(i - 1, 0)]))
    def _():
        wg_b[...] = wg_ref[0].astype(BF16)
        wu_b[...] = wu_ref[0].astype(BF16)
        wd_b[...] = wd_ref[0].astype(BF16)

    @pl.when(used)
    def _():
        xb = x_ref[...].astype(BF16)
        a = _dot(xb, wg_b[...])
        u = _dot(xb, wu_b[...])
        y_ref[...] = _dot((a * jax.nn.sigmoid(a) * u).astype(BF16), wd_b[...])

    @pl.when(jnp.logical_not(used))
    def _():
        y_ref[...] = jnp.zeros(y_ref.shape, F32)


def _experts(blk_e, nused, xg, wg, wu, wd):
    p_rows = xg.shape[0]
    x_blk = lambda i, be, nu: (jnp.minimum(i, nu[0] - 1), 0)
    return pl.pallas_call(
        _expert_kernel,
        grid_spec=pltpu.PrefetchScalarGridSpec(
            num_scalar_prefetch=2,
            grid=(p_rows // BM,),
            in_specs=[pl.BlockSpec((BM, D_MODEL), x_blk),
                      pl.BlockSpec((1, D_MODEL, EXPERT_FF), lambda i, be, nu: (be[i], 0, 0)),
                      pl.BlockSpec((1, D_MODEL, EXPERT_FF), lambda i, be, nu: (be[i], 0, 0)),
                      pl.BlockSpec((1, EXPERT_FF, D_MODEL), lambda i, be, nu: (be[i], 0, 0))],
            out_specs=pl.BlockSpec((BM, D_MODEL), lambda i, be, nu: (i, 0)),
            scratch_shapes=[pltpu.VMEM((D_MODEL, EXPERT_FF), BF16), pltpu.VMEM((D_MODEL, EXPERT_FF), BF16),
                            pltpu.VMEM((EXPERT_FF, D_MODEL), BF16)],
        ),
        out_shape=jax.ShapeDtypeStruct((p_rows, D_MODEL), F32),
        compiler_params=_cparams(("arbitrary",)),
        name="moe_experts",
    )(blk_e, nused, xg, wg, wu, wd)


def _combine_ple_kernel(dst8_ref, nct_ref, x1_ref, rt_ref, slot_ref, p_ref, pg_ref, wpg_ref, bpg_ref, wpp_ref,
                        yg_ref, o_ref, ys, sem):
    tr = x1_ref.shape[0]
    i = pl.program_id(0)
    slot = i % 2

    def copies_into(s):
        return lambda near, far: pltpu.make_async_copy(
            yg_ref.at[pl.ds(far, CHUNK), :], ys.at[s, pl.ds(near, CHUNK), :], sem.at[s])

    def fetch(tile, s):
        _start_chunks(dst8_ref, tile, nct_ref[tile], copies_into(s))

        def clear(c, carry):
            ys[s, pl.ds(pl.multiple_of(c * CHUNK, CHUNK), CHUNK), :] = jnp.zeros((CHUNK, D_MODEL), F32)
            return carry

        lax.fori_loop(nct_ref[tile], NCH, clear, 0)

    @pl.when(i == 0)
    def _():
        fetch(0, 0)

    @pl.when(i + 1 < pl.num_programs(0))
    def _():
        fetch(i + 1, 1 - slot)

    _wait_chunks(nct_ref[i], copies_into(slot))

    rt = rt_ref[...]
    sl = slot_ref[...]
    col_id = lax.broadcasted_iota(I32, (tr, R_LOCAL), 1).astype(F32)
    yb = ys[slot].astype(BF16)
    gather_w = jnp.where(col_id == sl[:, 0:1], rt[:, R_W1:R_W1 + 1],
                         jnp.where(col_id == sl[:, 1:2], rt[:, R_W2:R_W2 + 1], 0.0)).astype(BF16)
    x2 = x1_ref[...] + _dot(gather_w, yb)
    h3 = (x2 * lax.rsqrt(jnp.mean(x2 * x2, axis=-1, keepdims=True) + EPS) * pg_ref[...]).astype(BF16)
    gate = jax.nn.sigmoid(_dot(h3, wpg_ref[...]) + bpg_ref[...])
    o_ref[...] = x2 + gate * _dot(p_ref[...].astype(BF16), wpp_ref[...])


def _combine_ple(dst8, nct, x1, route, slots, p2, pg, wpg, bpg, wpp, yg):
    t = x1.shape[0]
    row = lambda n: pl.BlockSpec((TR, n), lambda i, d, c: (i, 0))
    full = lambda a: pl.BlockSpec(a.shape, lambda i, d, c: (0,) * a.ndim)
    return pl.pallas_call(
        _combine_ple_kernel,
        grid_spec=pltpu.PrefetchScalarGridSpec(
            num_scalar_prefetch=2,
            grid=(t // TR,),
            in_specs=[row(D_MODEL), row(LANES), row(LANES), row(PLE_DIM), full(pg), full(wpg), full(bpg),
                      full(wpp), pl.BlockSpec(memory_space=pl.ANY)],
            out_specs=row(D_MODEL),
            scratch_shapes=[pltpu.VMEM((2, R_LOCAL, D_MODEL), F32), pltpu.SemaphoreType.DMA((2,))],
        ),
        out_shape=jax.ShapeDtypeStruct((t, D_MODEL), F32),
        compiler_params=_cparams(("arbitrary",)),
        name="moe_combine_ple",
    )(dst8, nct, x1, route, slots, p2, pg, wpg, bpg, wpp, yg)


def _seg_sum_matrix():
    seg = np.arange(LANES) // HEAD_DIM
    return jnp.asarray(seg[:, None] == seg[None, :], BF16)


def _overlap_t(nc_rows, n_blk):
    nc = np.arange(nc_rows)[None, :] * CMP_STRIDE
    sb = np.arange(n_blk)[:, None] * SEL_BLOCK
    ov = (nc < sb + SEL_BLOCK) & (nc + CMP_BLOCK > sb) & (np.arange(nc_rows)[None, :] < nc_rows - 1)
    return jnp.asarray(ov, BF16)


def _compress_weights(pos, w1, w2):
    half = CMP_BLOCK // 2
    pe = jnp.broadcast_to(pos[:, None, :], (CMP_BLOCK, N_KV, HEAD_DIM)).reshape(2, half * N_KV * HEAD_DIM)
    w1r = w1.reshape(2, half, HEAD_DIM, CMP_HIDDEN)
    eye = jnp.eye(N_KV, dtype=w1.dtype)
    wx = jnp.einsum('hldj,gk->hlgdkj', w1r, eye).reshape(2, half * N_KV * HEAD_DIM, N_KV * CMP_HIDDEN)
    w2x = jnp.einsum('jd,gk->gjkd', w2, eye).reshape(N_KV * CMP_HIDDEN, N_KV * HEAD_DIM)
    return pe, wx[0].astype(BF16), wx[1].astype(BF16), w2x.astype(BF16)


def _layer(x, p_i, prm):
    b, s, _ = x.shape
    t = b * s
    x2 = x.reshape(t, D_MODEL)
    bd = _seg_sum_matrix()

    w = prm["w_in"]
    kv = lambda j: w[:, OFF_KV + j * KV_COLS:OFF_KV + (j + 1) * KV_COLS]
    w_nat = jnp.concatenate([kv(0), kv(1), kv(2), kv(4), w[:, OFF_CONV:]], axis=1).astype(BF16)
    w_t = jnp.concatenate([w[:, :Q_COLS], kv(3), kv(5),
                           jnp.pad(w[:, OFF_GATE:OFF_CONV], ((0, 0), (0, LANES - 3 * N_HEADS)))], axis=1).T.astype(BF16)
    qg = jnp.broadcast_to(prm["q_norm_g"][:, None], (HEAD_DIM, TM_PROJ))
    kg = jnp.tile(prm["k_norm_g"], (1, LANES // HEAD_DIM))
    qt, kc_raw, vc_raw, ks, vst, kw, vwt, gtt, glu, mg = _in_proj(
        x2, prm["attn_norm_g"][None, :], w_nat, w_t, qg, kg, bd, prm["merge_b"][None, :], s)

    nrow = s // CMP_STRIDE
    cw = [_compress_weights(prm["cmp_pos"][j], prm["cmp_w1"][j], prm["cmp_w2"][j]) for j in range(2)]
    pe_x, wa, wb = (jnp.stack([cw[0][j], cw[1][j]]) for j in range(3))
    kc, vct = _compress(kc_raw.reshape(b, nrow, CMP_STRIDE * LANES), vc_raw.reshape(b, nrow, CMP_STRIDE * LANES),
                        pe_x, wa, wb, cw[0][3], cw[1][3].T, bd, kg)

    attn = _attention(qt, gtt, kc, vct, ks.reshape(b, s, 4 * LANES), vst, kw.reshape(b, s, 2 * LANES), vwt,
                      _overlap_t(nrow, s // SEL_BLOCK), b, s)

    cact = _conformer_conv(glu, prm["conv_w"], prm["conv_b"][None, :], prm["conv_norm_g"][None, :],
                           prm["conv_norm_b"][None, :], b, s)

    wr_full = jnp.zeros((D_MODEL, LANES), F32)
    wr_full = wr_full.at[:, :N_EXPERTS].set(prm["w_router_expert"])
    wr_full = wr_full.at[:, GROUP_LANE0:GROUP_LANE0 + N_GROUPS].set(prm["w_router_group"])
    wr_hi = wr_full.astype(BF16)
    wr = jnp.stack([wr_hi, (wr_full - wr_hi.astype(F32)).astype(BF16)])
    br = jnp.zeros((1, LANES), F32)
    br = br.at[0, :N_EXPERTS].set(prm["b_router_expert"])
    br = br.at[0, GROUP_LANE0:GROUP_LANE0 + N_GROUPS].set(prm["b_router_group"])
    x1, h2, route, cnt = _merge_router(
        x2, attn, cact, mg, prm["w_attn_out"].astype(BF16), prm["w_conv_out"].astype(BF16),
        prm["w_out"].astype(BF16), prm["ffn_norm_g"][None, :], wr, br)

    n_tiles = t // TR
    counts = cnt[:, 0, :N_EXPERTS].astype(I32)
    seg = (counts + CHUNK - 1) // CHUNK * CHUNK
    local_off = jnp.cumsum(seg, axis=1) - seg
    nct = (jnp.sum(seg, axis=1) // CHUNK).astype(I32)
    total = jnp.sum(seg, axis=0)
    padded = (total + BM - 1) // BM * BM
    pends = jnp.cumsum(padded)
    far_off = (pends - padded)[None, :] + jnp.cumsum(seg, axis=0) - seg
    chunk = jnp.arange(NCH, dtype=I32)[None, :, None]
    lo8 = (local_off // CHUNK)[:, None, :]
    in_seg = (chunk >= lo8) & (chunk < ((local_off + seg) // CHUNK)[:, None, :])
    dst8 = (jnp.sum(jnp.where(in_seg, (far_off // CHUNK)[:, None, :] - lo8, 0), axis=-1)
            + chunk[:, :, 0]).astype(I32).reshape(n_tiles * NCH)
    lo_rows = jnp.broadcast_to(jnp.pad(local_off.astype(F32), ((0, 0), (0, LANES - N_EXPERTS)))[:, None, :],
                               (n_tiles, SUBLANES, LANES))
    n_blk = -(-(2 * t + n_tiles * N_EXPERTS * (CHUNK - 1) + N_EXPERTS * (BM - 1)) // BM)
    blk_e = jnp.minimum(jnp.sum(jnp.arange(n_blk, dtype=I32)[:, None] * BM >= pends[None, :], axis=1),
                        N_EXPERTS - 1).astype(I32)
    nused = (pends[-1:] // BM).astype(I32)
    pad8 = ((pends - padded + total) // CHUNK).astype(I32)
    npad = ((padded - total) // CHUNK).astype(I32)

    xg, slots = _dispatch(dst8, nct, pad8, npad, nused, h2, route, lo_rows, n_blk * BM)
    yg = _experts(blk_e, nused, xg, prm["w_exp_gate"], prm["w_exp_up"], prm["w_exp_down"])
    out = _combine_ple(dst8, nct, x1, route, slots, p_i.reshape(t, PLE_DIM), prm["ple_norm_g"][None, :],
                       prm["w_ple_gate"].astype(BF16), prm["b_ple_gate"][None, :], prm["w_ple_proj"].astype(BF16), yg)
    return out.reshape(b, s, D_MODEL)


def kernel(x, p, attn_norm_g, w_in, q_norm_g, k_norm_g, cmp_pos, cmp_w1, cmp_w2, w_attn_out, conv_w, conv_b, conv_norm_g, conv_norm_b, w_conv_out, merge_b, w_out, ffn_norm_g, w_router_group, b_router_group, w_router_expert, b_router_expert, w_exp_gate, w_exp_up, w_exp_down, ple_norm_g, w_ple_gate, b_ple_gate, w_ple_proj):
    stacked = dict(attn_norm_g=attn_norm_g, w_in=w_in, q_norm_g=q_norm_g, k_norm_g=k_norm_g, cmp_pos=cmp_pos,
                   cmp_w1=cmp_w1, cmp_w2=cmp_w2, w_attn_out=w_attn_out, conv_w=conv_w, conv_b=conv_b,
                   conv_norm_g=conv_norm_g, conv_norm_b=conv_norm_b, w_conv_out=w_conv_out, merge_b=merge_b,
                   w_out=w_out, ffn_norm_g=ffn_norm_g, w_router_group=w_router_group,
                   b_router_group=b_router_group, w_router_expert=w_router_expert,
                   b_router_expert=b_router_expert, w_exp_gate=w_exp_gate, w_exp_up=w_exp_up,
                   w_exp_down=w_exp_down, ple_norm_g=ple_norm_g, w_ple_gate=w_ple_gate, b_ple_gate=b_ple_gate,
                   w_ple_proj=w_ple_proj)
    for i in range(w_in.shape[0]):
        x = _layer(x, p[i], {k: v[i] for k, v in stacked.items()})
    return x
```

```python
import functools

import numpy as np
import jax
import jax.numpy as jnp
from jax import lax
from jax.experimental import pallas as pl
from jax.experimental.pallas import tpu as pltpu

F32 = jnp.float32
BF16 = jnp.bfloat16
I32 = jnp.int32

D_MODEL = 1024
PLE_DIM = 256
N_HEADS = 8
N_KV = 2
HPG = N_HEADS // N_KV
HEAD_DIM = 64
CMP_BLOCK = 32
CMP_STRIDE = 16
CMP_HIDDEN = 128
SEL_BLOCK = 64
N_SEL = 8
WINDOW = 512
CONV_CH = 512
CONV_WIDTH = 31
N_GROUPS = 4
EXP_PER_GROUP = 8
N_EXPERTS = N_GROUPS * EXP_PER_GROUP
EXPERT_FF = 512
EPS = 1e-6
NEG = -1e30
BIG = 1e4
Q_COLS = N_HEADS * HEAD_DIM
KV_COLS = N_KV * HEAD_DIM
OFF_KV = Q_COLS
OFF_GATE = OFF_KV + 6 * KV_COLS
OFF_CONV = OFF_GATE + 3 * N_HEADS
OFF_MERGE = OFF_CONV + 2 * CONV_CH

LANES = 128
SUBLANES = 8
VMEM_LIMIT = 56 * 1024 * 1024

TM_PROJ = 256
QB = 256
KT = 256
TS_CONV = 512
HALO = 32
TR = 512
BM = 256
CHUNK = SUBLANES
R_LOCAL = 2 * TR + N_EXPERTS * CHUNK
NCH = R_LOCAL // CHUNK

N_KC = 0
N_VC = N_KC + LANES
N_KS = N_VC + LANES
N_KW = N_KS + LANES
N_CONV = N_KW + LANES
N_MERGE = N_CONV + 2 * CONV_CH
T_Q = 0
T_VS = T_Q + Q_COLS
T_VW = T_VS + LANES
T_GATE = T_VW + LANES

POS_HI_LANE = HEAD_DIM
POS_LO_LANE = HEAD_DIM + 1
UNSELECTED = -(2.0 ** 40)

R_E1, R_E2, R_W1, R_W2, R_K1, R_K2 = 0, 1, 2, 3, 4, 5
GROUP_LANE0 = N_EXPERTS


def _cparams(sem, vmem=VMEM_LIMIT):
    return pltpu.CompilerParams(dimension_semantics=sem, vmem_limit_bytes=vmem)


def _dot(a, b):
    return jnp.dot(a, b, preferred_element_type=F32)


def _dot_nt(a, b):
    return lax.dot_general(a, b, (((1,), (1,)), ((), ())), preferred_element_type=F32)


def _split(x):
    hi = x.astype(BF16)
    lo = (x - hi.astype(F32)).astype(BF16)
    return hi, lo


def _seg_rmsnorm(z, bd):
    hi, lo = _split(z * z)
    ss = _dot(hi, bd) + _dot(lo, bd)
    return z * lax.rsqrt(ss * (1.0 / HEAD_DIM) + EPS)


def _with_pos_lanes(kn, pos):
    lane = lax.broadcasted_iota(I32, kn.shape, 1)
    hi = pos // SEL_BLOCK * SEL_BLOCK
    pos_lanes = jnp.where(lane == POS_HI_LANE, hi.astype(F32),
                          jnp.where(lane == POS_LO_LANE, (pos - hi).astype(F32), 0.0))
    lo_half = lane < HEAD_DIM
    return (jnp.where(lo_half, kn, pos_lanes).astype(BF16),
            jnp.where(lo_half, pltpu.roll(kn, HEAD_DIM, 1), pos_lanes).astype(BF16))


def _in_proj_kernel(x_ref, g_ref, w_ref, wt_ref, qg_ref, kg_ref, bd_ref, mb_ref,
                    qt_o, kc_o, vc_o, ks_o, vst_o, kw_o, vwt_o, gtt_o, glu_o, mg_o, *, seq_len):
    tm = x_ref.shape[0]
    x = x_ref[...]
    h = (x * lax.rsqrt(jnp.mean(x * x, axis=-1, keepdims=True) + EPS) * g_ref[...]).astype(BF16)
    bd = bd_ref[...]
    lane = lax.broadcasted_iota(I32, (tm, LANES), 1)

    def proj(a, b):
        return _dot(h, w_ref[:, a:b])

    zt = _dot_nt(wt_ref[...], h)
    for hd in range(N_HEADS):
        zh = zt[T_Q + hd * HEAD_DIM:T_Q + (hd + 1) * HEAD_DIM, :]
        ms = jnp.sum(zh * zh, axis=0, keepdims=True) * (1.0 / HEAD_DIM)
        qt_o[hd * HEAD_DIM:(hd + 1) * HEAD_DIM, :] = (
            zh * lax.rsqrt(ms + EPS) * qg_ref[...] * (HEAD_DIM ** -0.5)).astype(BF16)
    ones = jnp.ones((HEAD_DIM, tm), F32)
    for off, v_o in ((T_VS, vst_o), (T_VW, vwt_o)):
        v_o[0] = jnp.concatenate([zt[off:off + HEAD_DIM, :], ones, zt[off + HEAD_DIM:off + LANES, :], ones],
                                 axis=0).astype(BF16)
    gtt_o[...] = jax.nn.sigmoid(zt[T_GATE:T_GATE + LANES, :])

    kc_o[...] = proj(N_KC, N_KC + LANES)
    vc_o[...] = proj(N_VC, N_VC + LANES)

    pos = (pl.program_id(0) * tm) % seq_len + lax.broadcasted_iota(I32, (tm, LANES), 0)
    onehot = jnp.where(lane == pos // SEL_BLOCK, 1.0, 0.0).astype(BF16)

    def key_tiles(zk, gain):
        return _with_pos_lanes(_seg_rmsnorm(zk, bd) * gain, pos)

    k0, k1 = key_tiles(proj(N_KS, N_KS + LANES), kg_ref[1:2, :])
    ks_o[...] = jnp.concatenate([k0, onehot, k1, onehot], axis=1)
    k0, k1 = key_tiles(proj(N_KW, N_KW + LANES), kg_ref[2:3, :])
    kw_o[...] = jnp.concatenate([k0, k1], axis=1)
    za = proj(N_CONV, N_CONV + CONV_CH)
    zg = proj(N_CONV + CONV_CH, N_CONV + 2 * CONV_CH)
    glu_o[...] = za * jax.nn.sigmoid(zg)
    for c in range(2):
        zm = proj(N_MERGE + c * D_MODEL, N_MERGE + (c + 1) * D_MODEL)
        mg_o[:, c * D_MODEL:(c + 1) * D_MODEL] = jax.nn.sigmoid(
            zm + mb_ref[:, c * D_MODEL:(c + 1) * D_MODEL]).astype(BF16)


def _in_proj(x2, norm_g, w_nat, w_t, qg, kg, bd, mb, seq_len):
    t = x2.shape[0]
    tm = TM_PROJ
    assert tm == KT and seq_len % tm == 0 and seq_len // SEL_BLOCK <= HEAD_DIM
    row = lambda n: pl.BlockSpec((tm, n), lambda i: (i, 0))
    col = lambda n: pl.BlockSpec((n, tm), lambda i: (0, i))
    vt = pl.BlockSpec((1, 2 * LANES, tm), lambda i: (i, 0, 0))
    full = lambda a: pl.BlockSpec(a.shape, lambda i: (0,) * a.ndim)
    sds = jax.ShapeDtypeStruct
    return pl.pallas_call(
        functools.partial(_in_proj_kernel, seq_len=seq_len),
        grid=(t // tm,),
        in_specs=[row(D_MODEL), full(norm_g), full(w_nat), full(w_t), full(qg), full(kg), full(bd), full(mb)],
        out_specs=[col(Q_COLS), row(LANES), row(LANES), row(4 * LANES), vt, row(2 * LANES), vt, col(LANES),
                   row(CONV_CH), row(2 * D_MODEL)],
        out_shape=[sds((Q_COLS, t), BF16), sds((t, LANES), F32), sds((t, LANES), F32), sds((t, 4 * LANES), BF16),
                   sds((t // tm, 2 * LANES, tm), BF16), sds((t, 2 * LANES), BF16), sds((t // tm, 2 * LANES, tm), BF16),
                   sds((LANES, t), F32), sds((t, CONV_CH), F32), sds((t, 2 * D_MODEL), BF16)],
        compiler_params=_cparams(("parallel",)),
        name="in_proj",
    )(x2, norm_g, w_nat, w_t, qg, kg, bd, mb)


def _compress_kernel(ck_ref, cv_ref, pe_ref, wa_ref, wb_ref, w2k_ref, w2vt_ref, bd_ref, kg_ref, kc_o, vct_o):
    nrow = ck_ref.shape[1]

    def hidden(idx, c_ref):
        c = c_ref[0]
        ya = _dot((c + pe_ref[idx, 0:1, :]).astype(BF16), wa_ref[idx])
        yb = _dot((c + pe_ref[idx, 1:2, :]).astype(BF16), wb_ref[idx])
        return jax.nn.gelu(ya + pltpu.roll(yb, nrow - 1, 0)).astype(BF16)

    kn = _seg_rmsnorm(_dot(hidden(0, ck_ref), w2k_ref[...]), bd_ref[...]) * kg_ref[0:1, :]
    c_end = lax.broadcasted_iota(I32, (nrow, LANES), 0) * CMP_STRIDE + (CMP_BLOCK - 1)
    kc_o[0] = jnp.concatenate(_with_pos_lanes(kn, c_end), axis=1)
    vt = _dot_nt(w2vt_ref[...], hidden(1, cv_ref))
    ones = jnp.ones((HEAD_DIM, nrow), F32)
    vct_o[0] = jnp.concatenate([vt[:HEAD_DIM], ones, vt[HEAD_DIM:], ones], axis=0).astype(BF16)


def _compress(kc_raw, vc_raw, pe_x, wa, wb, w2k, w2vt, bd, kg):
    b, nrow, width = kc_raw.shape
    batch = lambda r, n: pl.BlockSpec((1, r, n), lambda i: (i, 0, 0))
    full = lambda a: pl.BlockSpec(a.shape, lambda i: (0,) * a.ndim)
    return pl.pallas_call(
        _compress_kernel,
        grid=(b,),
        in_specs=[batch(nrow, width), batch(nrow, width), full(pe_x), full(wa), full(wb), full(w2k), full(w2vt),
                  full(bd), full(kg)],
        out_specs=[batch(nrow, 2 * LANES), batch(2 * LANES, nrow)],
        out_shape=[jax.ShapeDtypeStruct((b, nrow, 2 * LANES), BF16),
                   jax.ShapeDtypeStruct((b, 2 * LANES, nrow), BF16)],
        compiler_params=_cparams(("parallel",)),
        name="compress",
    )(kc_raw, vc_raw, pe_x, wa, wb, w2k, w2vt, bd, kg)


def _halve_rows(x, op):
    rows = x.shape[0]
    while rows > SUBLANES:
        rows //= 2
        x = op(x[:rows], x[rows:])
    return x


def _attn_kernel(qt_ref, gtt_ref, kc_ref, vct_ref, ks_ref, vst_ref, kw_ref, vwt_ref, ovt_ref, o_ref,
                 tiles_ref, sa_ref, sb_ref, m_ref, acc_ref):
    i = pl.program_id(1)
    q0 = i * QB
    nc_rows = kc_ref.shape[1]
    n_blk = ovt_ref.shape[0]
    hpc = 2 * LANES // QB
    pw = hpc * QB
    n_pairs = N_HEADS // hpc
    group_of = lambda c: c * hpc // HPG
    rel = ((lax.broadcasted_iota(I32, (KT, pw), 1) & (QB - 1))
           - lax.broadcasted_iota(I32, (KT, pw), 0)).astype(F32)
    slope_rows = lax.broadcasted_iota(I32, (HEAD_DIM, QB), 0) < 2
    gtt = gtt_ref[...]

    pairs = []
    for pr in range(n_pairs):
        q_pos = []
        for hd in range(pr * hpc, (pr + 1) * hpc):
            slope = jnp.where(slope_rows, 2.0 ** -(hd + 1), 0.0).astype(BF16)
            q_pos.append(jnp.concatenate([qt_ref[hd * HEAD_DIM:(hd + 1) * HEAD_DIM, :], slope], axis=0))
        pairs.append(jnp.concatenate(q_pos, axis=1))

    def update(s, vt, m, acc):
        m_new = jnp.maximum(m, jnp.max(_halve_rows(s, jnp.maximum), axis=0, keepdims=True))
        p = jnp.exp(s - m_new).astype(BF16)
        return m_new, jnp.exp(m - m_new) * acc + _dot(vt, p)

    def scores(j, q_list, k_ref, k_width):
        k0 = pl.multiple_of(j * KT, KT)
        return tuple(_dot(k_ref[0, pl.ds(k0, KT), group_of(c) * k_width:(group_of(c) + 1) * k_width], q_list[c])
                     for c in range(n_pairs))

    def consume(j, s_list, state, vt_ref, mask_fn, extra_dist=0.0):
        keep = None if mask_fn is None else mask_fn(rel + ((q0 - j * KT).astype(F32) + extra_dist))
        out = []
        for c in range(n_pairs):
            g = group_of(c)
            s = s_list[c] if keep is None else jnp.where(keep, s_list[c], NEG)
            out += update(s, vt_ref[j, g * LANES:(g + 1) * LANES, :], *state[2 * c:2 * c + 2])
        return tuple(out)

    j_diag = (q0 + QB - 1) // KT
    init = (jnp.full((1, pw), NEG, F32), jnp.zeros((LANES, pw), F32)) * n_pairs

    def selected_branch(q_list, n_act):
        def fill(buf, n):
            for c, s in enumerate(scores(tiles_ref[n], q_list, ks_ref, 2 * LANES)):
                buf[c] = s

        def use(buf, n, mask_fn):
            state = []
            for c in range(n_pairs):
                state += [m_ref[c, 0:1, :], acc_ref[c]]
            out = consume(tiles_ref[n], [buf[c] for c in range(n_pairs)], state, vst_ref, mask_fn)
            for c in range(n_pairs):
                m_ref[c, 0:1, :] = out[2 * c]
                acc_ref[c] = out[2 * c + 1]

        for c in range(n_pairs):
            m_ref[c] = jnp.full((SUBLANES, pw), NEG, F32)
            acc_ref[c] = jnp.zeros((LANES, pw), F32)
        fill(sa_ref, 0)

        def two_tiles(h, carry):
            n = 2 * h
            fill(sb_ref, n + 1)
            use(sa_ref, n, None)

            @pl.when(n + 1 < n_act)
            def _():
                fill(sa_ref, n + 2)
                use(sb_ref, n + 1, None)
            return carry

        lax.fori_loop(0, (n_act + 1) // 2, two_tiles, 0)

        @pl.when(n_act % 2 == 1)
        def _():
            for c in range(n_pairs):
                sa_ref[c] = sb_ref[c]

        use(sa_ref, n_act, lambda d: d >= 0.0)
        return [acc_ref[c] for c in range(n_pairs)]

    assert WINDOW == 2 * KT and KT % QB == 0
    win_out = init
    for w in range(WINDOW // KT + 1):
        jw = j_diag - WINDOW // KT + w
        before_start = jnp.where(jw < 0, float(WINDOW + KT), 0.0)
        jc = jnp.maximum(jw, 0)
        mask_fn = ((lambda d: d < float(WINDOW)), (lambda d, jw=jw: jw >= 0), (lambda d: d >= 0.0))[w]
        win_out = consume(jc, scores(jc, pairs, kw_ref, LANES), win_out, vwt_ref, mask_fn, before_start)

    cidx = lax.broadcasted_iota(I32, (nc_rows, pw), 0)
    c_end = cidx * CMP_STRIDE + (CMP_BLOCK - 1)
    cmask = (c_end <= q0 + (lax.broadcasted_iota(I32, (nc_rows, pw), 1) & (QB - 1))) & (cidx < nc_rows - 1)

    sees_any = jnp.where(q0 + (lax.broadcasted_iota(I32, (1, pw), 1) & (QB - 1)) >= CMP_BLOCK - 1, 1.0, 0.0)
    s_cmp = [_dot(kc_ref[0, :, group_of(c) * LANES:(group_of(c) + 1) * LANES], pairs[c]) for c in range(n_pairs)]
    o_cmp, imp = [], []
    for g in range(N_KV):
        imp_g = jnp.zeros((n_blk, QB), F32)
        for pr in range(g * HPG // hpc, (g + 1) * HPG // hpc):
            s = jnp.where(cmask, s_cmp[pr], NEG)
            e = jnp.exp(s - jnp.max(_halve_rows(s, jnp.maximum), axis=0, keepdims=True)).astype(BF16)
            acc = _dot(vct_ref[0, g * LANES:(g + 1) * LANES, :], e)
            inv = sees_any / acc[HEAD_DIM:HEAD_DIM + 1]
            o_cmp.append(acc[:HEAD_DIM] * inv)
            weighted = _dot(ovt_ref[...], e) * inv
            for k in range(hpc):
                imp_g = imp_g + weighted[:, k * QB:(k + 1) * QB]
        imp.append(imp_g)

    blk = lax.broadcasted_iota(I32, (n_blk, N_KV * QB), 0)
    tl = q0 + (lax.broadcasted_iota(I32, (n_blk, N_KV * QB), 1) & (QB - 1))
    valid = blk * SEL_BLOCK <= tl
    cur = tl // SEL_BLOCK
    forced = (blk == 0) | (blk == cur) | (blk == cur - 1)
    sel = forced & valid
    score = jnp.where(valid, jnp.where(sel, -jnp.inf, jnp.concatenate(imp, axis=1)), -BIG)
    for _ in range(min(N_SEL, n_blk) - 3):
        mx = jnp.max(score, axis=0, keepdims=True)
        first = jnp.min(jnp.where(score == mx, blk, n_blk), axis=0, keepdims=True)
        pick = blk == first
        sel = sel | pick
        score = jnp.where(pick, -jnp.inf, score)
    chosen = sel & valid
    any_sel = jnp.where(chosen[:, :QB] | chosen[:, QB:], 1.0, 0.0)
    unsel = jnp.where(chosen, 0.0, UNSELECTED).astype(BF16)
    bias = jnp.concatenate([unsel, jnp.zeros((2 * HEAD_DIM - n_blk, N_KV * QB), BF16)], axis=0)
    q_sel = []
    for c in range(n_pairs):
        g_bias = bias[:, group_of(c) * QB:(group_of(c) + 1) * QB]
        q_sel.append(jnp.concatenate([pairs[c], jnp.concatenate([g_bias] * hpc, axis=1)], axis=0))

    per_blk = jnp.sum(any_sel, axis=1, keepdims=True)
    n_act = jnp.int32(0)
    for jt in range(n_blk * SEL_BLOCK // KT):
        blocks = slice(jt * (KT // SEL_BLOCK), (jt + 1) * (KT // SEL_BLOCK))
        tiles_ref[n_act] = jnp.int32(jt)
        n_act = n_act + ((jnp.sum(per_blk[blocks]) > 0.0) & (jt < j_diag)).astype(I32)
    tiles_ref[n_act] = j_diag

    sel_acc = selected_branch(q_sel, n_act)

    o_sel = [a[:HEAD_DIM] / a[HEAD_DIM:HEAD_DIM + 1] for a in sel_acc]
    o_win = [a[:HEAD_DIM] / a[HEAD_DIM:HEAD_DIM + 1] for a in win_out[1::2]]
    mixed = []
    for hd in range(N_HEADS):
        c, lanes = hd // hpc, slice((hd % hpc) * QB, (hd % hpc + 1) * QB)
        mixed.append(gtt[3 * hd:3 * hd + 1, :] * o_cmp[c][:, lanes] + gtt[3 * hd + 1:3 * hd + 2, :] * o_sel[c][:, lanes]
                     + gtt[3 * hd + 2:3 * hd + 3, :] * o_win[c][:, lanes])
    for pr in range(N_HEADS // 2):
        two = jnp.concatenate(mixed[2 * pr:2 * pr + 2], axis=0)
        o_ref[:, pr * LANES:(pr + 1) * LANES] = jnp.concatenate(
            [two[:, k * LANES:(k + 1) * LANES].T for k in range(QB // LANES)], axis=0).astype(BF16)


def _attention(qt, gtt, kc, vct, ks, vst, kw, vwt, ovt, b, s):
    n_q = s // QB
    n_tiles = N_HEADS * QB // (2 * LANES)
    per_q = lambda n: pl.BlockSpec((n, QB), lambda bi, i: (0, bi * n_q + i))
    per_b = lambda a: pl.BlockSpec((1,) + a.shape[1:], lambda bi, i: (bi, 0, 0))
    vt = pl.BlockSpec((s // KT, 2 * LANES, KT), lambda bi, i: (bi, 0, 0))
    return pl.pallas_call(
        _attn_kernel,
        grid=(b, n_q),
        in_specs=[per_q(Q_COLS), per_q(LANES), per_b(kc), per_b(vct), per_b(ks), vt, per_b(kw), vt,
                  pl.BlockSpec(ovt.shape, lambda bi, i: (0, 0))],
        out_specs=pl.BlockSpec((QB, Q_COLS), lambda bi, i: (bi * n_q + i, 0)),
        out_shape=jax.ShapeDtypeStruct((b * s, Q_COLS), BF16),
        scratch_shapes=[pltpu.SMEM((s // KT + 1,), I32),
                        pltpu.VMEM((n_tiles, KT, 2 * LANES), F32), pltpu.VMEM((n_tiles, KT, 2 * LANES), F32),
                        pltpu.VMEM((n_tiles, SUBLANES, 2 * LANES), F32),
                        pltpu.VMEM((n_tiles, LANES, 2 * LANES), F32)],
        compiler_params=_cparams(("parallel", "parallel")),
        name="nsa_attention",
    )(qt, gtt, kc, vct, ks, vst, kw, vwt, ovt)


def _conv_kernel(z_ref, w_ref, b_ref, g_ref, beta_ref, o_ref, buf, shifted):
    ts = z_ref.shape[0]
    chunk = 64

    @pl.when(pl.program_id(1) == 0)
    def _():
        buf[0:HALO, :] = jnp.zeros((HALO, CONV_CH), F32)

    buf[HALO:HALO + ts, :] = z_ref[...]
    span = ts + HALO - SUBLANES
    for r in range(1, SUBLANES):
        shifted[r - 1, 0:span, :] = buf[r:r + span, :]

    def tap_rows(k, c):
        start = HALO - (CONV_WIDTH - 1) + k
        r = start % SUBLANES
        src = buf if r == 0 else shifted.at[r - 1]
        return src[start - r + c * chunk:start - r + (c + 1) * chunk, :]

    for c in range(ts // chunk):
        acc = jnp.zeros((chunk, CONV_CH), F32) + b_ref[...]
        for k in range(CONV_WIDTH):
            acc = acc + tap_rows(k, c) * w_ref[k:k + 1, :]
        mu = jnp.mean(acc, axis=-1, keepdims=True)
        d = acc - mu
        var = jnp.mean(d * d, axis=-1, keepdims=True)
        y = d * lax.rsqrt(var + EPS) * g_ref[...] + beta_ref[...]
        o_ref[c * chunk:(c + 1) * chunk, :] = (y * jax.nn.sigmoid(y)).astype(BF16)
    buf[0:HALO, :] = buf[ts:ts + HALO, :]


def _conformer_conv(glu, w, bias, g, beta, b, s):
    ts = min(TS_CONV, s)
    full = lambda a: pl.BlockSpec(a.shape, lambda bi, i: (0, 0))
    row = pl.BlockSpec((ts, CONV_CH), lambda bi, i: (bi * (s // ts) + i, 0))
    return pl.pallas_call(
        _conv_kernel,
        grid=(b, s // ts),
        in_specs=[row, full(w), full(bias), full(g), full(beta)],
        out_specs=row,
        out_shape=jax.ShapeDtypeStruct((b * s, CONV_CH), BF16),
        scratch_shapes=[pltpu.VMEM((HALO + ts, CONV_CH), F32),
                        pltpu.VMEM((SUBLANES - 1, HALO + ts, CONV_CH), F32)],
        compiler_params=_cparams(("arbitrary", "arbitrary")),
        name="conformer_conv",
    )(glu, w, bias, g, beta)


def _merge_router_kernel(x_ref, a_ref, c_ref, mg_ref, wa_ref, wc_ref, wo_ref, fg_ref, wr_ref, br_ref,
                         x1_o, h2_o, rt_o, cnt_o):
    tm = x_ref.shape[0]
    y_a = _dot(a_ref[...], wa_ref[...])
    y_c = _dot(c_ref[...], wc_ref[...])
    mix = mg_ref[:, 0:D_MODEL].astype(F32) * y_a + mg_ref[:, D_MODEL:2 * D_MODEL].astype(F32) * y_c
    x1 = x_ref[...] + _dot(mix.astype(BF16), wo_ref[...])
    x1_o[...] = x1
    h2 = x1 * lax.rsqrt(jnp.mean(x1 * x1, axis=-1, keepdims=True) + EPS) * fg_ref[...]
    h2_o[...] = h2

    hh, hl = _split(h2)
    logits = _dot(hh, wr_ref[0]) + _dot(hl, wr_ref[0]) + _dot(hh, wr_ref[1]) + br_ref[...]
    lane = lax.broadcasted_iota(I32, (tm, LANES), 1)
    is_g = (lane >= GROUP_LANE0) & (lane < GROUP_LANE0 + N_GROUPS)
    gmax = jnp.max(jnp.where(is_g, logits, -jnp.inf), axis=-1, keepdims=True)
    gsel = jnp.min(jnp.where(is_g & (logits == gmax), lane - GROUP_LANE0, N_GROUPS), axis=-1, keepdims=True)
    pg_sel = 1.0 / jnp.sum(jnp.where(is_g, jnp.exp(logits - gmax), 0.0), axis=-1, keepdims=True)
    in_grp = (lane < N_EXPERTS) & (lane // EXP_PER_GROUP == gsel)
    emax = jnp.max(jnp.where(in_grp, logits, -jnp.inf), axis=-1, keepdims=True)
    ee = jnp.where(in_grp, jnp.exp(logits - emax), 0.0)
    pe = jnp.where(in_grp, ee / jnp.sum(ee, axis=-1, keepdims=True), -1.0)
    p1 = jnp.max(pe, axis=-1, keepdims=True)
    i1 = jnp.min(jnp.where(pe == p1, lane, LANES), axis=-1, keepdims=True)
    pe2 = jnp.where(lane == i1, -1.0, pe)
    p2 = jnp.max(pe2, axis=-1, keepdims=True)
    i2 = jnp.min(jnp.where(pe2 == p2, lane, LANES), axis=-1, keepdims=True)
    w1 = pg_sel * p1 / (p1 + p2)
    w2 = pg_sel * p2 / (p1 + p2)

    hot1 = lane == i1
    hot2 = lane == i2
    ind = jnp.where(hot1 | hot2, 1.0, 0.0)
    tri = (lax.broadcasted_iota(I32, (tm, tm), 1) < lax.broadcasted_iota(I32, (tm, tm), 0)).astype(BF16)
    before = _dot(tri, ind.astype(BF16))
    k1 = jnp.sum(jnp.where(hot1, before, 0.0), axis=-1, keepdims=True)
    k2 = jnp.sum(jnp.where(hot2, before, 0.0), axis=-1, keepdims=True)
    rec = jnp.zeros((tm, LANES), F32)
    for slot, val in ((R_E1, i1.astype(F32)), (R_E2, i2.astype(F32)), (R_W1, w1), (R_W2, w2), (R_K1, k1), (R_K2, k2)):
        rec = jnp.where(lane == slot, val, rec)
    rt_o[...] = rec
    cnt_o[0] = jnp.broadcast_to(jnp.sum(ind, axis=0, keepdims=True), (SUBLANES, LANES))


def _merge_router(x2, attn, cact, mg, wa, wc, wo, fg, wr, br):
    t = x2.shape[0]
    tm = TR
    row = lambda n: pl.BlockSpec((tm, n), lambda i: (i, 0))
    full = lambda a: pl.BlockSpec(a.shape, lambda i: (0,) * a.ndim)
    return pl.pallas_call(
        _merge_router_kernel,
        grid=(t // tm,),
        in_specs=[row(D_MODEL), row(Q_COLS), row(CONV_CH), row(2 * D_MODEL), full(wa), full(wc), full(wo),
                  full(fg), full(wr), full(br)],
        out_specs=[row(D_MODEL), row(D_MODEL), row(LANES), pl.BlockSpec((1, SUBLANES, LANES), lambda i: (i, 0, 0))],
        out_shape=[jax.ShapeDtypeStruct((t, D_MODEL), F32), jax.ShapeDtypeStruct((t, D_MODEL), F32),
                   jax.ShapeDtypeStruct((t, LANES), F32), jax.ShapeDtypeStruct((t // tm, SUBLANES, LANES), F32)],
        compiler_params=_cparams(("parallel",)),
        name="merge_router",
    )(x2, attn, cact, mg, wa, wc, wo, fg, wr, br)


def _start_chunks(dst8_ref, tile, n_chunks, make_copy):
    def issue(c, carry):
        far = pl.multiple_of(dst8_ref[tile * NCH + c] * CHUNK, CHUNK)
        make_copy(pl.multiple_of(c * CHUNK, CHUNK), far).start()
        return carry

    lax.fori_loop(0, n_chunks, issue, 0)


def _wait_chunks(n_chunks, make_copy):
    def wait_one(c, carry):
        make_copy(0, 0).wait()
        return carry

    lax.fori_loop(0, n_chunks, wait_one, 0)


def _dispatch_kernel(dst8_ref, nct_ref, pad8_ref, npad_ref, nused_ref, h_ref, rt_ref, lo_ref, xg_ref, slot_o, xs,
                     zbuf, sem, zsem):
    tr = h_ref.shape[0]

    @pl.when(pl.program_id(0) == 0)
    def _():
        zbuf[...] = jnp.zeros(zbuf.shape, F32)

        def zero_chunk(far):
            return pltpu.make_async_copy(zbuf.at[pl.ds(0, CHUNK), :], xg_ref.at[pl.ds(far, CHUNK), :], zsem)

        def zero_block(blk):
            return pltpu.make_async_copy(zbuf, xg_ref.at[pl.ds(pl.multiple_of(blk * BM, BM), BM), :], sem.at[0])

        def per_expert(e, carry, wait):
            def one(c, inner):
                if wait:
                    zero_chunk(0).wait()
                else:
                    zero_chunk(pl.multiple_of((pad8_ref[e] + c) * CHUNK, CHUNK)).start()
                return inner
            return lax.fori_loop(0, npad_ref[e], one, carry)

        def tail(blk, carry, wait):
            if wait:
                zero_block(0).wait()
            else:
                zero_block(blk).start()
            return carry

        n_blocks = xg_ref.shape[0] // BM
        lax.fori_loop(0, N_EXPERTS, functools.partial(per_expert, wait=False), 0)
        lax.fori_loop(nused_ref[0], n_blocks, functools.partial(tail, wait=False), 0)
        lax.fori_loop(0, N_EXPERTS, functools.partial(per_expert, wait=True), 0)
        lax.fori_loop(nused_ref[0], n_blocks, functools.partial(tail, wait=True), 0)

    rt = rt_ref[...]
    lane = lax.broadcasted_iota(I32, (tr, LANES), 1)
    lane_f = lane.astype(F32)
    lo_row = lo_ref[0, 0:1, :]
    slots = []
    for e_lane, k_lane in ((R_E1, R_K1), (R_E2, R_K2)):
        hot = lane_f == rt[:, e_lane:e_lane + 1]
        slots.append(jnp.sum(jnp.where(hot, lo_row, 0.0), axis=-1, keepdims=True) + rt[:, k_lane:k_lane + 1])
    rec = jnp.where(lane == 0, slots[0], jnp.where(lane == 1, slots[1], 0.0))
    slot_o[...] = rec
    rec_t = jnp.concatenate([rec[c * LANES:(c + 1) * LANES].T for c in range(tr // LANES)], axis=1)
    row_id = lax.broadcasted_iota(I32, (R_LOCAL, tr), 0).astype(F32)
    pick = (row_id == rec_t[0:1, :]) | (row_id == rec_t[1:2, :])
    staged = _dot(jnp.where(pick, 1.0, 0.0).astype(BF16), h_ref[...].astype(BF16))

    i = pl.program_id(0)
    last = pl.num_programs(0) - 1
    slot = i % 2

    def copies_from(s):
        return lambda near, far: pltpu.make_async_copy(
            xs.at[s, pl.ds(near, CHUNK), :], xg_ref.at[pl.ds(far, CHUNK), :], sem.at[s])

    @pl.when(i >= 2)
    def _():
        _wait_chunks(nct_ref[i - 2], copies_from(slot))

    xs[slot] = staged
    _start_chunks(dst8_ref, i, nct_ref[i], copies_from(slot))

    @pl.when(i == last)
    def _():
        @pl.when(i >= 1)
        def _():
            _wait_chunks(nct_ref[i - 1], copies_from(1 - slot))
        _wait_chunks(nct_ref[i], copies_from(slot))


def _dispatch(dst8, nct, pad8, npad, nused, h2, route, lo_rows, p_rows):
    t = h2.shape[0]
    return pl.pallas_call(
        _dispatch_kernel,
        grid_spec=pltpu.PrefetchScalarGridSpec(
            num_scalar_prefetch=5,
            grid=(t // TR,),
            in_specs=[pl.BlockSpec((TR, D_MODEL), lambda i, *_: (i, 0)),
                      pl.BlockSpec((TR, LANES), lambda i, *_: (i, 0)),
                      pl.BlockSpec((1, SUBLANES, LANES), lambda i, *_: (i, 0, 0))],
            out_specs=[pl.BlockSpec(memory_space=pl.ANY),
                       pl.BlockSpec((TR, LANES), lambda i, *_: (i, 0))],
            scratch_shapes=[pltpu.VMEM((2, R_LOCAL, D_MODEL), F32), pltpu.VMEM((BM, D_MODEL), F32),
                            pltpu.SemaphoreType.DMA((2,)), pltpu.SemaphoreType.DMA(())],
        ),
        out_shape=[jax.ShapeDtypeStruct((p_rows, D_MODEL), F32), jax.ShapeDtypeStruct((t, LANES), F32)],
        compiler_params=_cparams(("arbitrary",)),
        name="moe_dispatch",
    )(dst8, nct, pad8, npad, nused, h2, route, lo_rows)


def _expert_kernel(blk_e_ref, nused_ref, x_ref, wg_ref, wu_ref, wd_ref, y_ref, wg_b, wu_b, wd_b):
    i = pl.program_id(0)
    used = i < nused_ref[0]

    @pl.when((i == 0) | (blk_e_ref[i] != blk_e_ref[jnp.maximum(i - 1, 0)]))
    def _():
        wg_b[...] = wg_ref[0].astype(BF16)
        wu_b[...] = wu_ref[0].astype(BF16)
        wd_b[...] = wd_ref[0].astype(BF16)

    @pl.when(used)
    def _():
        xb = x_ref[...].astype(BF16)
        a = _dot(xb, wg_b[...])
        u = _dot(xb, wu_b[...])
        y_ref[...] = _dot((a * jax.nn.sigmoid(a) * u).astype(BF16), wd_b[...])

    @pl.when(jnp.logical_not(used))
    def _():
        y_ref[...] = jnp.zeros(y_ref.shape, F32)


def _experts(blk_e, nused, xg, wg, wu, wd):
    p_rows = xg.shape[0]
    x_blk = lambda i, be, nu: (jnp.minimum(i, nu[0] - 1), 0)
    return pl.pallas_call(
        _expert_kernel,
        grid_spec=pltpu.PrefetchScalarGridSpec(
            num_scalar_prefetch=2,
            grid=(p_rows // BM,),
            in_specs=[pl.BlockSpec((BM, D_MODEL), x_blk),
                      pl.BlockSpec((1, D_MODEL, EXPERT_FF), lambda i, be, nu: (be[i], 0, 0)),
                      pl.BlockSpec((1, D_MODEL, EXPERT_FF), lambda i, be, nu: (be[i], 0, 0)),
                      pl.BlockSpec((1, EXPERT_FF, D_MODEL), lambda i, be, nu: (be[i], 0, 0))],
            out_specs=pl.BlockSpec((BM, D_MODEL), lambda i, be, nu: (i, 0)),
            scratch_shapes=[pltpu.VMEM((D_MODEL, EXPERT_FF), BF16), pltpu.VMEM((D_MODEL, EXPERT_FF), BF16),
                            pltpu.VMEM((EXPERT_FF, D_MODEL), BF16)],
        ),
        out_shape=jax.ShapeDtypeStruct((p_rows, D_MODEL), F32),
        compiler_params=_cparams(("arbitrary",)),
        name="moe_experts",
    )(blk_e, nused, xg, wg, wu, wd)


def _combine_ple_kernel(dst8_ref, nct_ref, x1_ref, rt_ref, slot_ref, p_ref, pg_ref, wpg_ref, bpg_ref, wpp_ref,
                        yg_ref, o_ref, ys, sem):
    tr = x1_ref.shape[0]
    i = pl.program_id(0)
    slot = i % 2

    def copies_into(s):
        return lambda near, far: pltpu.make_async_copy(
            yg_ref.at[pl.ds(far, CHUNK), :], ys.at[s, pl.ds(near, CHUNK), :], sem.at[s])

    def fetch(tile, s):
        _start_chunks(dst8_ref, tile, nct_ref[tile], copies_into(s))

        def clear(c, carry):
            ys[s, pl.ds(pl.multiple_of(c * CHUNK, CHUNK), CHUNK), :] = jnp.zeros((CHUNK, D_MODEL), F32)
            return carry

        lax.fori_loop(nct_ref[tile], NCH, clear, 0)

    @pl.when(i == 0)
    def _():
        fetch(0, 0)

    @pl.when(i + 1 < pl.num_programs(0))
    def _():
        fetch(i + 1, 1 - slot)

    _wait_chunks(nct_ref[i], copies_into(slot))

    rt = rt_ref[...]
    sl = slot_ref[...]
    col_id = lax.broadcasted_iota(I32, (tr, R_LOCAL), 1).astype(F32)
    yb = ys[slot].astype(BF16)
    gather_w = jnp.where(col_id == sl[:, 0:1], rt[:, R_W1:R_W1 + 1],
                         jnp.where(col_id == sl[:, 1:2], rt[:, R_W2:R_W2 + 1], 0.0)).astype(BF16)
    x2 = x1_ref[...] + _dot(gather_w, yb)
    h3 = (x2 * lax.rsqrt(jnp.mean(x2 * x2, axis=-1, keepdims=True) + EPS) * pg_ref[...]).astype(BF16)
    gate = jax.nn.sigmoid(_dot(h3, wpg_ref[...]) + bpg_ref[...])
    o_ref[...] = x2 + gate * _dot(p_ref[...].astype(BF16), wpp_ref[...])


def _combine_ple(dst8, nct, x1, route, slots, p2, pg, wpg, bpg, wpp, yg):
    t = x1.shape[0]
    row = lambda n: pl.BlockSpec((TR, n), lambda i, d, c: (i, 0))
    full = lambda a: pl.BlockSpec(a.shape, lambda i, d, c: (0,) * a.ndim)
    return pl.pallas_call(
        _combine_ple_kernel,
        grid_spec=pltpu.PrefetchScalarGridSpec(
            num_scalar_prefetch=2,
            grid=(t // TR,),
            in_specs=[row(D_MODEL), row(LANES), row(LANES), row(PLE_DIM), full(pg), full(wpg), full(bpg),
                      full(wpp), pl.BlockSpec(memory_space=pl.ANY)],
            out_specs=row(D_MODEL),
            scratch_shapes=[pltpu.VMEM((2, R_LOCAL, D_MODEL), F32), pltpu.SemaphoreType.DMA((2,))],
        ),
        out_shape=jax.ShapeDtypeStruct((t, D_MODEL), F32),
        compiler_params=_cparams(("arbitrary",)),
        name="moe_combine_ple",
    )(dst8, nct, x1, route, slots, p2, pg, wpg, bpg, wpp, yg)


def _seg_sum_matrix():
    seg = np.arange(LANES) // HEAD_DIM
    return jnp.asarray(seg[:, None] == seg[None, :], BF16)


def _overlap_t(nc_rows, n_blk):
    nc = np.arange(nc_rows)[None, :] * CMP_STRIDE
    sb = np.arange(n_blk)[:, None] * SEL_BLOCK
    ov = (nc < sb + SEL_BLOCK) & (nc + CMP_BLOCK > sb) & (np.arange(nc_rows)[None, :] < nc_rows - 1)
    return jnp.asarray(ov, BF16)


def _compress_weights(pos, w1, w2):
    half = CMP_BLOCK // 2
    pe = jnp.broadcast_to(pos[:, None, :], (CMP_BLOCK, N_KV, HEAD_DIM)).reshape(2, half * N_KV * HEAD_DIM)
    w1r = w1.reshape(2, half, HEAD_DIM, CMP_HIDDEN)
    eye = jnp.eye(N_KV, dtype=w1.dtype)
    wx = jnp.einsum('hldj,gk->hlgdkj', w1r, eye).reshape(2, half * N_KV * HEAD_DIM, N_KV * CMP_HIDDEN)
    w2x = jnp.einsum('jd,gk->gjkd', w2, eye).reshape(N_KV * CMP_HIDDEN, N_KV * HEAD_DIM)
    return pe, wx[0].astype(BF16), wx[1].astype(BF16), w2x.astype(BF16)


def _layer(x, p_i, prm):
    b, s, _ = x.shape
    t = b * s
    x2 = x.reshape(t, D_MODEL)
    bd = _seg_sum_matrix()

    w = prm["w_in"]
    kv = lambda j: w[:, OFF_KV + j * KV_COLS:OFF_KV + (j + 1) * KV_COLS]
    w_nat = jnp.concatenate([kv(0), kv(1), kv(2), kv(4), w[:, OFF_CONV:]], axis=1).astype(BF16)
    w_t = jnp.concatenate([w[:, :Q_COLS], kv(3), kv(5),
                           jnp.pad(w[:, OFF_GATE:OFF_CONV], ((0, 0), (0, LANES - 3 * N_HEADS)))], axis=1).T.astype(BF16)
    qg = jnp.broadcast_to(prm["q_norm_g"][:, None], (HEAD_DIM, TM_PROJ))
    kg = jnp.tile(prm["k_norm_g"], (1, LANES // HEAD_DIM))
    qt, kc_raw, vc_raw, ks, vst, kw, vwt, gtt, glu, mg = _in_proj(
        x2, prm["attn_norm_g"][None, :], w_nat, w_t, qg, kg, bd, prm["merge_b"][None, :], s)

    nrow = s // CMP_STRIDE
    cw = [_compress_weights(prm["cmp_pos"][j], prm["cmp_w1"][j], prm["cmp_w2"][j]) for j in range(2)]
    pe_x, wa, wb = (jnp.stack([cw[0][j], cw[1][j]]) for j in range(3))
    kc, vct = _compress(kc_raw.reshape(b, nrow, CMP_STRIDE * LANES), vc_raw.reshape(b, nrow, CMP_STRIDE * LANES),
                        pe_x, wa, wb, cw[0][3], cw[1][3].T, bd, kg)

    attn = _attention(qt, gtt, kc, vct, ks.reshape(b, s, 4 * LANES), vst, kw.reshape(b, s, 2 * LANES), vwt,
                      _overlap_t(nrow, s // SEL_BLOCK), b, s)

    cact = _conformer_conv(glu, prm["conv_w"], prm["conv_b"][None, :], prm["conv_norm_g"][None, :],
                           prm["conv_norm_b"][None, :], b, s)

    wr_full = jnp.zeros((D_MODEL, LANES), F32)
    wr_full = wr_full.at[:, :N_EXPERTS].set(prm["w_router_expert"])
    wr_full = wr_full.at[:, GROUP_LANE0:GROUP_LANE0 + N_GROUPS].set(prm["w_router_group"])
    wr_hi = wr_full.astype(BF16)
    wr = jnp.stack([wr_hi, (wr_full - wr_hi.astype(F32)).astype(BF16)])
    br = jnp.zeros((1, LANES), F32)
    br = br.at[0, :N_EXPERTS].set(prm["b_router_expert"])
    br = br.at[0, GROUP_LANE0:GROUP_LANE0 + N_GROUPS].set(prm["b_router_group"])
    x1, h2, route, cnt = _merge_router(
        x2, attn, cact, mg, prm["w_attn_out"].astype(BF16), prm["w_conv_out"].astype(BF16),
        prm["w_out"].astype(BF16), prm["ffn_norm_g"][None, :], wr, br)

    n_tiles = t // TR
    counts = cnt[:, 0, :N_EXPERTS].astype(I32)
    seg = (counts + CHUNK - 1) // CHUNK * CHUNK
    local_off = jnp.cumsum(seg, axis=1) - seg
    nct = (jnp.sum(seg, axis=1) // CHUNK).astype(I32)
    total = jnp.sum(seg, axis=0)
    padded = (total + BM - 1) // BM * BM
    pends = jnp.cumsum(padded)
    far_off = (pends - padded)[None, :] + jnp.cumsum(seg, axis=0) - seg
    chunk = jnp.arange(NCH, dtype=I32)[None, :, None]
    lo8 = (local_off // CHUNK)[:, None, :]
    in_seg = (chunk >= lo8) & (chunk < ((local_off + seg) // CHUNK)[:, None, :])
    dst8 = (jnp.sum(jnp.where(in_seg, (far_off // CHUNK)[:, None, :] - lo8, 0), axis=-1)
            + chunk[:, :, 0]).astype(I32).reshape(n_tiles * NCH)
    lo_rows = jnp.broadcast_to(jnp.pad(local_off.astype(F32), ((0, 0), (0, LANES - N_EXPERTS)))[:, None, :],
                               (n_tiles, SUBLANES, LANES))
    n_blk = -(-(2 * t + n_tiles * N_EXPERTS * (CHUNK - 1) + N_EXPERTS * (BM - 1)) // BM)
    blk_e = jnp.minimum(jnp.sum(jnp.arange(n_blk, dtype=I32)[:, None] * BM >= pends[None, :], axis=1),
                        N_EXPERTS - 1).astype(I32)
    nused = (pends[-1:] // BM).astype(I32)
    pad8 = ((pends - padded + total) // CHUNK).astype(I32)
    npad = ((padded - total) // CHUNK).astype(I32)

    xg, slots = _dispatch(dst8, nct, pad8, npad, nused, h2, route, lo_rows, n_blk * BM)
    yg = _experts(blk_e, nused, xg, prm["w_exp_gate"], prm["w_exp_up"], prm["w_exp_down"])
    out = _combine_ple(dst8, nct, x1, route, slots, p_i.reshape(t, PLE_DIM), prm["ple_norm_g"][None, :],
                       prm["w_ple_gate"].astype(BF16), prm["b_ple_gate"][None, :], prm["w_ple_proj"].astype(BF16), yg)
    return out.reshape(b, s, D_MODEL)


def kernel(x, p, attn_norm_g, w_in, q_norm_g, k_norm_g, cmp_pos, cmp_w1, cmp_w2, w_attn_out, conv_w, conv_b, conv_norm_g, conv_norm_b, w_conv_out, merge_b, w_out, ffn_norm_g, w_router_group, b_router_group, w_router_expert, b_router_expert, w_exp_gate, w_exp_up, w_exp_down, ple_norm_g, w_ple_gate, b_ple_gate, w_ple_proj):
    stacked = dict(attn_norm_g=attn_norm_g, w_in=w_in, q_norm_g=q_norm_g, k_norm_g=k_norm_g, cmp_pos=cmp_pos,
                   cmp_w1=cmp_w1, cmp_w2=cmp_w2, w_attn_out=w_attn_out, conv_w=conv_w, conv_b=conv_b,
                   conv_norm_g=conv_norm_g, conv_norm_b=conv_norm_b, w_conv_out=w_conv_out, merge_b=merge_b,
                   w_out=w_out, ffn_norm_g=ffn_norm_g, w_router_group=w_router_group,
                   b_router_group=b_router_group, w_router_expert=w_router_expert,
                   b_router_expert=b_router_expert, w_exp_gate=w_exp_gate, w_exp_up=w_exp_up,
                   w_exp_down=w_exp_down, ple_norm_g=ple_norm_g, w_ple_gate=w_ple_gate, b_ple_gate=b_ple_gate,
                   w_ple_proj=w_ple_proj)
    for i in range(w_in.shape[0]):
        x = _layer(x, p[i], {k: v[i] for k, v in stacked.items()})
    return x
```

```python
import functools

import numpy as np
import jax
import jax.numpy as jnp
from jax import lax
from jax.experimental import pallas as pl
from jax.experimental.pallas import tpu as pltpu

F32 = jnp.float32
BF16 = jnp.bfloat16
I32 = jnp.int32

D_MODEL = 1024
PLE_DIM = 256
N_HEADS = 8
N_KV = 2
HPG = N_HEADS // N_KV
HEAD_DIM = 64
CMP_BLOCK = 32
CMP_STRIDE = 16
CMP_HIDDEN = 128
SEL_BLOCK = 64
N_SEL = 8
WINDOW = 512
CONV_CH = 512
CONV_WIDTH = 31
N_GROUPS = 4
EXP_PER_GROUP = 8
N_EXPERTS = N_GROUPS * EXP_PER_GROUP
EXPERT_FF = 512
EPS = 1e-6
NEG = -1e30
BIG = 1e4
Q_COLS = N_HEADS * HEAD_DIM
KV_COLS = N_KV * HEAD_DIM
OFF_KV = Q_COLS
OFF_GATE = OFF_KV + 6 * KV_COLS
OFF_CONV = OFF_GATE + 3 * N_HEADS
OFF_MERGE = OFF_CONV + 2 * CONV_CH

LANES = 128
SUBLANES = 8
VMEM_LIMIT = 56 * 1024 * 1024

TM_PROJ = 256
QB = 256
KT = 256
TS_CONV = 512
HALO = 32
TR = 512
BM = 512
CHUNK = SUBLANES
R_LOCAL = 2 * TR + N_EXPERTS * CHUNK
NCH = R_LOCAL // CHUNK

N_KC = 0
N_VC = N_KC + LANES
N_KS = N_VC + LANES
N_KW = N_KS + LANES
N_CONV = N_KW + LANES
N_MERGE = N_CONV + 2 * CONV_CH
T_Q = 0
T_VS = T_Q + Q_COLS
T_VW = T_VS + LANES
T_GATE = T_VW + LANES

POS_HI_LANE = HEAD_DIM
POS_LO_LANE = HEAD_DIM + 1
UNSELECTED = -(2.0 ** 40)

R_E1, R_E2, R_W1, R_W2, R_K1, R_K2 = 0, 1, 2, 3, 4, 5
GROUP_LANE0 = N_EXPERTS


def _cparams(sem, vmem=VMEM_LIMIT):
    return pltpu.CompilerParams(dimension_semantics=sem, vmem_limit_bytes=vmem)


def _dot(a, b):
    return jnp.dot(a, b, preferred_element_type=F32)


def _dot_nt(a, b):
    return lax.dot_general(a, b, (((1,), (1,)), ((), ())), preferred_element_type=F32)


def _split(x):
    hi = x.astype(BF16)
    lo = (x - hi.astype(F32)).astype(BF16)
    return hi, lo


def _seg_rmsnorm(z, bd):
    hi, lo = _split(z * z)
    ss = _dot(hi, bd) + _dot(lo, bd)
    return z * lax.rsqrt(ss * (1.0 / HEAD_DIM) + EPS)


def _with_pos_lanes(kn, pos):
    lane = lax.broadcasted_iota(I32, kn.shape, 1)
    hi = pos // SEL_BLOCK * SEL_BLOCK
    pos_lanes = jnp.where(lane == POS_HI_LANE, hi.astype(F32),
                          jnp.where(lane == POS_LO_LANE, (pos - hi).astype(F32), 0.0))
    lo_half = lane < HEAD_DIM
    return (jnp.where(lo_half, kn, pos_lanes).astype(BF16),
            jnp.where(lo_half, pltpu.roll(kn, HEAD_DIM, 1), pos_lanes).astype(BF16))


def _in_proj_kernel(x_ref, g_ref, w_ref, wt_ref, qg_ref, kg_ref, bd_ref, mb_ref,
                    qt_o, kc_o, vc_o, ks_o, vst_o, kw_o, vwt_o, gtt_o, glu_o, mg_o, *, seq_len):
    tm = x_ref.shape[0]
    x = x_ref[...]
    h = (x * lax.rsqrt(jnp.mean(x * x, axis=-1, keepdims=True) + EPS) * g_ref[...]).astype(BF16)
    bd = bd_ref[...]
    lane = lax.broadcasted_iota(I32, (tm, LANES), 1)

    def proj(a, b):
        return _dot(h, w_ref[:, a:b])

    zt = _dot_nt(wt_ref[...], h)
    for hd in range(N_HEADS):
        zh = zt[T_Q + hd * HEAD_DIM:T_Q + (hd + 1) * HEAD_DIM, :]
        ms = jnp.sum(zh * zh, axis=0, keepdims=True) * (1.0 / HEAD_DIM)
        qt_o[hd * HEAD_DIM:(hd + 1) * HEAD_DIM, :] = (
            zh * lax.rsqrt(ms + EPS) * qg_ref[...] * (HEAD_DIM ** -0.5)).astype(BF16)
    ones = jnp.ones((HEAD_DIM, tm), F32)
    for off, v_o in ((T_VS, vst_o), (T_VW, vwt_o)):
        v_o[0] = jnp.concatenate([zt[off:off + HEAD_DIM, :], ones, zt[off + HEAD_DIM:off + LANES, :], ones],
                                 axis=0).astype(BF16)
    gtt_o[...] = jax.nn.sigmoid(zt[T_GATE:T_GATE + LANES, :])

    kc_o[...] = proj(N_KC, N_KC + LANES)
    vc_o[...] = proj(N_VC, N_VC + LANES)

    pos = (pl.program_id(0) * tm) % seq_len + lax.broadcasted_iota(I32, (tm, LANES), 0)
    onehot = jnp.where(lane == pos // SEL_BLOCK, 1.0, 0.0).astype(BF16)

    def key_tiles(zk, gain):
        return _with_pos_lanes(_seg_rmsnorm(zk, bd) * gain, pos)

    k0, k1 = key_tiles(proj(N_KS, N_KS + LANES), kg_ref[1:2, :])
    ks_o[...] = jnp.concatenate([k0, onehot, k1, onehot], axis=1)
    k0, k1 = key_tiles(proj(N_KW, N_KW + LANES), kg_ref[2:3, :])
    kw_o[...] = jnp.concatenate([k0, k1], axis=1)
    za = proj(N_CONV, N_CONV + CONV_CH)
    zg = proj(N_CONV + CONV_CH, N_CONV + 2 * CONV_CH)
    glu_o[...] = za * jax.nn.sigmoid(zg)
    for c in range(2):
        zm = proj(N_MERGE + c * D_MODEL, N_MERGE + (c + 1) * D_MODEL)
        mg_o[:, c * D_MODEL:(c + 1) * D_MODEL] = jax.nn.sigmoid(
            zm + mb_ref[:, c * D_MODEL:(c + 1) * D_MODEL]).astype(BF16)


def _in_proj(x2, norm_g, w_nat, w_t, qg, kg, bd, mb, seq_len):
    t = x2.shape[0]
    tm = TM_PROJ
    assert tm == KT and seq_len % tm == 0 and seq_len // SEL_BLOCK <= HEAD_DIM
    row = lambda n: pl.BlockSpec((tm, n), lambda i: (i, 0))
    col = lambda n: pl.BlockSpec((n, tm), lambda i: (0, i))
    vt = pl.BlockSpec((1, 2 * LANES, tm), lambda i: (i, 0, 0))
    full = lambda a: pl.BlockSpec(a.shape, lambda i: (0,) * a.ndim)
    sds = jax.ShapeDtypeStruct
    return pl.pallas_call(
        functools.partial(_in_proj_kernel, seq_len=seq_len),
        grid=(t // tm,),
        in_specs=[row(D_MODEL), full(norm_g), full(w_nat), full(w_t), full(qg), full(kg), full(bd), full(mb)],
        out_specs=[col(Q_COLS), row(LANES), row(LANES), row(4 * LANES), vt, row(2 * LANES), vt, col(LANES),
                   row(CONV_CH), row(2 * D_MODEL)],
        out_shape=[sds((Q_COLS, t), BF16), sds((t, LANES), F32), sds((t, LANES), F32), sds((t, 4 * LANES), BF16),
                   sds((t // tm, 2 * LANES, tm), BF16), sds((t, 2 * LANES), BF16), sds((t // tm, 2 * LANES, tm), BF16),
                   sds((LANES, t), F32), sds((t, CONV_CH), F32), sds((t, 2 * D_MODEL), BF16)],
        compiler_params=_cparams(("parallel",)),
        name="in_proj",
    )(x2, norm_g, w_nat, w_t, qg, kg, bd, mb)


def _compress_kernel(ck_ref, cv_ref, pe_ref, wa_ref, wb_ref, w2k_ref, w2vt_ref, bd_ref, kg_ref, kc_o, vct_o):
    nrow = ck_ref.shape[1]

    def hidden(idx, c_ref):
        c = c_ref[0]
        ya = _dot((c + pe_ref[idx, 0:1, :]).astype(BF16), wa_ref[idx])
        yb = _dot((c + pe_ref[idx, 1:2, :]).astype(BF16), wb_ref[idx])
        return jax.nn.gelu(ya + pltpu.roll(yb, nrow - 1, 0)).astype(BF16)

    kn = _seg_rmsnorm(_dot(hidden(0, ck_ref), w2k_ref[...]), bd_ref[...]) * kg_ref[0:1, :]
    c_end = lax.broadcasted_iota(I32, (nrow, LANES), 0) * CMP_STRIDE + (CMP_BLOCK - 1)
    kc_o[0] = jnp.concatenate(_with_pos_lanes(kn, c_end), axis=1)
    vt = _dot_nt(w2vt_ref[...], hidden(1, cv_ref))
    ones = jnp.ones((HEAD_DIM, nrow), F32)
    vct_o[0] = jnp.concatenate([vt[:HEAD_DIM], ones, vt[HEAD_DIM:], ones], axis=0).astype(BF16)


def _compress(kc_raw, vc_raw, pe_x, wa, wb, w2k, w2vt, bd, kg):
    b, nrow, width = kc_raw.shape
    batch = lambda r, n: pl.BlockSpec((1, r, n), lambda i: (i, 0, 0))
    full = lambda a: pl.BlockSpec(a.shape, lambda i: (0,) * a.ndim)
    return pl.pallas_call(
        _compress_kernel,
        grid=(b,),
        in_specs=[batch(nrow, width), batch(nrow, width), full(pe_x), full(wa), full(wb), full(w2k), full(w2vt),
                  full(bd), full(kg)],
        out_specs=[batch(nrow, 2 * LANES), batch(2 * LANES, nrow)],
        out_shape=[jax.ShapeDtypeStruct((b, nrow, 2 * LANES), BF16),
                   jax.ShapeDtypeStruct((b, 2 * LANES, nrow), BF16)],
        compiler_params=_cparams(("parallel",)),
        name="compress",
    )(kc_raw, vc_raw, pe_x, wa, wb, w2k, w2vt, bd, kg)


def _halve_rows(x, op):
    rows = x.shape[0]
    while rows > SUBLANES:
        rows //= 2
        x = op(x[:rows], x[rows:])
    return x


def _attn_kernel(qt_ref, gtt_ref, kc_ref, vct_ref, ks_ref, vst_ref, kw_ref, vwt_ref, ovt_ref, o_ref,
                 tiles_ref, sa_ref, sb_ref, m_ref, acc_ref):
    i = pl.program_id(1)
    q0 = i * QB
    nc_rows = kc_ref.shape[1]
    n_blk = ovt_ref.shape[0]
    hpc = 2 * LANES // QB
    pw = hpc * QB
    n_pairs = N_HEADS // hpc
    group_of = lambda c: c * hpc // HPG
    rel = ((lax.broadcasted_iota(I32, (KT, pw), 1) & (QB - 1))
           - lax.broadcasted_iota(I32, (KT, pw), 0)).astype(F32)
    slope_rows = lax.broadcasted_iota(I32, (HEAD_DIM, QB), 0) < 2
    gtt = gtt_ref[...]

    pairs = []
    for pr in range(n_pairs):
        q_pos = []
        for hd in range(pr * hpc, (pr + 1) * hpc):
            slope = jnp.where(slope_rows, 2.0 ** -(hd + 1), 0.0).astype(BF16)
            q_pos.append(jnp.concatenate([qt_ref[hd * HEAD_DIM:(hd + 1) * HEAD_DIM, :], slope], axis=0))
        pairs.append(jnp.concatenate(q_pos, axis=1))

    def update(s, vt, m, acc):
        m_new = jnp.maximum(m, jnp.max(_halve_rows(s, jnp.maximum), axis=0, keepdims=True))
        p = jnp.exp(s - m_new).astype(BF16)
        return m_new, jnp.exp(m - m_new) * acc + _dot(vt, p)

    def scores(j, q_list, k_ref, k_width):
        k0 = pl.multiple_of(j * KT, KT)
        return tuple(_dot(k_ref[0, pl.ds(k0, KT), group_of(c) * k_width:(group_of(c) + 1) * k_width], q_list[c])
                     for c in range(n_pairs))

    def consume(j, s_list, state, vt_ref, mask_fn, extra_dist=0.0):
        keep = None if mask_fn is None else mask_fn(rel + ((q0 - j * KT).astype(F32) + extra_dist))
        out = []
        for c in range(n_pairs):
            g = group_of(c)
            s = s_list[c] if keep is None else jnp.where(keep, s_list[c], NEG)
            out += update(s, vt_ref[j, g * LANES:(g + 1) * LANES, :], *state[2 * c:2 * c + 2])
        return tuple(out)

    j_diag = (q0 + QB - 1) // KT
    init = (jnp.full((1, pw), NEG, F32), jnp.zeros((LANES, pw), F32)) * n_pairs

    def selected_branch(q_list, n_act):
        def fill(buf, n):
            for c, s in enumerate(scores(tiles_ref[n], q_list, ks_ref, 2 * LANES)):
                buf[c] = s

        def use(buf, n, mask_fn):
            state = []
            for c in range(n_pairs):
                state += [m_ref[c, 0:1, :], acc_ref[c]]
            out = consume(tiles_ref[n], [buf[c] for c in range(n_pairs)], state, vst_ref, mask_fn)
            for c in range(n_pairs):
                m_ref[c, 0:1, :] = out[2 * c]
                acc_ref[c] = out[2 * c + 1]

        for c in range(n_pairs):
            m_ref[c] = jnp.full((SUBLANES, pw), NEG, F32)
            acc_ref[c] = jnp.zeros((LANES, pw), F32)
        fill(sa_ref, 0)

        def two_tiles(h, carry):
            n = 2 * h
            fill(sb_ref, n + 1)
            use(sa_ref, n, None)

            @pl.when(n + 1 < n_act)
            def _():
                fill(sa_ref, n + 2)
                use(sb_ref, n + 1, None)
            return carry

        lax.fori_loop(0, (n_act + 1) // 2, two_tiles, 0)

        @pl.when(n_act % 2 == 1)
        def _():
            for c in range(n_pairs):
                sa_ref[c] = sb_ref[c]

        use(sa_ref, n_act, lambda d: d >= 0.0)
        return [acc_ref[c] for c in range(n_pairs)]

    cidx = lax.broadcasted_iota(I32, (nc_rows, pw), 0)
    c_end = cidx * CMP_STRIDE + (CMP_BLOCK - 1)
    cmask = (c_end <= q0 + (lax.broadcasted_iota(I32, (nc_rows, pw), 1) & (QB - 1))) & (cidx < nc_rows - 1)

    sees_any = jnp.where(q0 + (lax.broadcasted_iota(I32, (1, pw), 1) & (QB - 1)) >= CMP_BLOCK - 1, 1.0, 0.0)
    s_cmp = [_dot(kc_ref[0, :, group_of(c) * LANES:(group_of(c) + 1) * LANES], pairs[c]) for c in range(n_pairs)]
    o_cmp, imp = [], []
    for g in range(N_KV):
        imp_g = jnp.zeros((n_blk, QB), F32)
        for pr in range(g * HPG // hpc, (g + 1) * HPG // hpc):
            s = jnp.where(cmask, s_cmp[pr], NEG)
            e = jnp.exp(s - jnp.max(_halve_rows(s, jnp.maximum), axis=0, keepdims=True)).astype(BF16)
            acc = _dot(vct_ref[0, g * LANES:(g + 1) * LANES, :], e)
            inv = sees_any / acc[HEAD_DIM:HEAD_DIM + 1]
            o_cmp.append(acc[:HEAD_DIM] * inv)
            weighted = _dot(ovt_ref[...], e) * inv
            for k in range(hpc):
                imp_g = imp_g + weighted[:, k * QB:(k + 1) * QB]
        imp.append(imp_g)

    blk = lax.broadcasted_iota(I32, (n_blk, N_KV * QB), 0)
    tl = q0 + (lax.broadcasted_iota(I32, (n_blk, N_KV * QB), 1) & (QB - 1))
    valid = blk * SEL_BLOCK <= tl
    cur = tl // SEL_BLOCK
    forced = (blk == 0) | (blk == cur) | (blk == cur - 1)
    sel = forced & valid
    score = jnp.where(valid, jnp.where(sel, -jnp.inf, jnp.concatenate(imp, axis=1)), -BIG)
    for _ in range(min(N_SEL, n_blk) - 3):
        mx = jnp.max(score, axis=0, keepdims=True)
        first = jnp.min(jnp.where(score == mx, blk, n_blk), axis=0, keepdims=True)
        pick = blk == first
        sel = sel | pick
        score = jnp.where(pick, -jnp.inf, score)
    chosen = sel & valid
    any_sel = jnp.where(chosen[:, :QB] | chosen[:, QB:], 1.0, 0.0)
    unsel = jnp.where(chosen, 0.0, UNSELECTED).astype(BF16)
    bias = jnp.concatenate([unsel, jnp.zeros((2 * HEAD_DIM - n_blk, N_KV * QB), BF16)], axis=0)
    q_sel = []
    for c in range(n_pairs):
        g_bias = bias[:, group_of(c) * QB:(group_of(c) + 1) * QB]
        q_sel.append(jnp.concatenate([pairs[c], jnp.concatenate([g_bias] * hpc, axis=1)], axis=0))

    per_blk = jnp.sum(any_sel, axis=1, keepdims=True)
    n_act = jnp.int32(0)
    for jt in range(n_blk * SEL_BLOCK // KT):
        blocks = slice(jt * (KT // SEL_BLOCK), (jt + 1) * (KT // SEL_BLOCK))
        tiles_ref[n_act] = jnp.int32(jt)
        n_act = n_act + ((jnp.sum(per_blk[blocks]) > 0.0) & (jt < j_diag)).astype(I32)
    tiles_ref[n_act] = j_diag

    assert WINDOW == 2 * KT and KT % QB == 0
    win_out = init
    for w in range(WINDOW // KT + 1):
        jw = j_diag - WINDOW // KT + w
        before_start = jnp.where(jw < 0, float(WINDOW + KT), 0.0)
        jc = jnp.maximum(jw, 0)
        mask_fn = ((lambda d: d < float(WINDOW)), (lambda d, jw=jw: jw >= 0), (lambda d: d >= 0.0))[w]
        win_out = consume(jc, scores(jc, pairs, kw_ref, LANES), win_out, vwt_ref, mask_fn, before_start)

    sel_acc = selected_branch(q_sel, n_act)

    o_sel = [a[:HEAD_DIM] / a[HEAD_DIM:HEAD_DIM + 1] for a in sel_acc]
    o_win = [a[:HEAD_DIM] / a[HEAD_DIM:HEAD_DIM + 1] for a in win_out[1::2]]
    mixed = []
    for hd in range(N_HEADS):
        c, lanes = hd // hpc, slice((hd % hpc) * QB, (hd % hpc + 1) * QB)
        mixed.append(gtt[3 * hd:3 * hd + 1, :] * o_cmp[c][:, lanes] + gtt[3 * hd + 1:3 * hd + 2, :] * o_sel[c][:, lanes]
                     + gtt[3 * hd + 2:3 * hd + 3, :] * o_win[c][:, lanes])
    for pr in range(N_HEADS // 2):
        two = jnp.concatenate(mixed[2 * pr:2 * pr + 2], axis=0)
        o_ref[:, pr * LANES:(pr + 1) * LANES] = jnp.concatenate(
            [two[:, k * LANES:(k + 1) * LANES].T for k in range(QB // LANES)], axis=0).astype(BF16)


def _attention(qt, gtt, kc, vct, ks, vst, kw, vwt, ovt, b, s):
    n_q = s // QB
    n_tiles = N_HEADS * QB // (2 * LANES)
    per_q = lambda n: pl.BlockSpec((n, QB), lambda bi, i: (0, bi * n_q + i))
    per_b = lambda a: pl.BlockSpec((1,) + a.shape[1:], lambda bi, i: (bi, 0, 0))
    vt = pl.BlockSpec((s // KT, 2 * LANES, KT), lambda bi, i: (bi, 0, 0))
    return pl.pallas_call(
        _attn_kernel,
        grid=(b, n_q),
        in_specs=[per_q(Q_COLS), per_q(LANES), per_b(kc), per_b(vct), per_b(ks), vt, per_b(kw), vt,
                  pl.BlockSpec(ovt.shape, lambda bi, i: (0, 0))],
        out_specs=pl.BlockSpec((QB, Q_COLS), lambda bi, i: (bi * n_q + i, 0)),
        out_shape=jax.ShapeDtypeStruct((b * s, Q_COLS), BF16),
        scratch_shapes=[pltpu.SMEM((s // KT + 1,), I32),
                        pltpu.VMEM((n_tiles, KT, 2 * LANES), F32), pltpu.VMEM((n_tiles, KT, 2 * LANES), F32),
                        pltpu.VMEM((n_tiles, SUBLANES, 2 * LANES), F32),
                        pltpu.VMEM((n_tiles, LANES, 2 * LANES), F32)],
        compiler_params=_cparams(("parallel", "parallel")),
        name="nsa_attention",
    )(qt, gtt, kc, vct, ks, vst, kw, vwt, ovt)


def _conv_kernel(z_ref, w_ref, b_ref, g_ref, beta_ref, o_ref, buf, shifted):
    ts = z_ref.shape[0]
    chunk = 64

    @pl.when(pl.program_id(1) == 0)
    def _():
        buf[0:HALO, :] = jnp.zeros((HALO, CONV_CH), F32)

    buf[HALO:HALO + ts, :] = z_ref[...]
    span = ts + HALO - SUBLANES
    for r in range(1, SUBLANES):
        shifted[r - 1, 0:span, :] = buf[r:r + span, :]

    def tap_rows(k, c):
        start = HALO - (CONV_WIDTH - 1) + k
        r = start % SUBLANES
        src = buf if r == 0 else shifted.at[r - 1]
        return src[start - r + c * chunk:start - r + (c + 1) * chunk, :]

    for c in range(ts // chunk):
        acc = jnp.zeros((chunk, CONV_CH), F32) + b_ref[...]
        for k in range(CONV_WIDTH):
            acc = acc + tap_rows(k, c) * w_ref[k:k + 1, :]
        mu = jnp.mean(acc, axis=-1, keepdims=True)
        d = acc - mu
        var = jnp.mean(d * d, axis=-1, keepdims=True)
        y = d * lax.rsqrt(var + EPS) * g_ref[...] + beta_ref[...]
        o_ref[c * chunk:(c + 1) * chunk, :] = (y * jax.nn.sigmoid(y)).astype(BF16)
    buf[0:HALO, :] = buf[ts:ts + HALO, :]


def _conformer_conv(glu, w, bias, g, beta, b, s):
    ts = min(TS_CONV, s)
    full = lambda a: pl.BlockSpec(a.shape, lambda bi, i: (0, 0))
    row = pl.BlockSpec((ts, CONV_CH), lambda bi, i: (bi * (s // ts) + i, 0))
    return pl.pallas_call(
        _conv_kernel,
        grid=(b, s // ts),
        in_specs=[row, full(w), full(bias), full(g), full(beta)],
        out_specs=row,
        out_shape=jax.ShapeDtypeStruct((b * s, CONV_CH), BF16),
        scratch_shapes=[pltpu.VMEM((HALO + ts, CONV_CH), F32),
                        pltpu.VMEM((SUBLANES - 1, HALO + ts, CONV_CH), F32)],
        compiler_params=_cparams(("arbitrary", "arbitrary")),
        name="conformer_conv",
    )(glu, w, bias, g, beta)


def _merge_router_kernel(x_ref, a_ref, c_ref, mg_ref, wa_ref, wc_ref, wo_ref, fg_ref, wr_ref, br_ref,
                         x1_o, h2_o, rt_o, cnt_o):
    tm = x_ref.shape[0]
    y_a = _dot(a_ref[...], wa_ref[...])
    y_c = _dot(c_ref[...], wc_ref[...])
    mix = mg_ref[:, 0:D_MODEL].astype(F32) * y_a + mg_ref[:, D_MODEL:2 * D_MODEL].astype(F32) * y_c
    x1 = x_ref[...] + _dot(mix.astype(BF16), wo_ref[...])
    x1_o[...] = x1
    h2 = x1 * lax.rsqrt(jnp.mean(x1 * x1, axis=-1, keepdims=True) + EPS) * fg_ref[...]
    h2_o[...] = h2

    hh, hl = _split(h2)
    logits = _dot(hh, wr_ref[0]) + _dot(hl, wr_ref[0]) + _dot(hh, wr_ref[1]) + br_ref[...]
    lane = lax.broadcasted_iota(I32, (tm, LANES), 1)
    is_g = (lane >= GROUP_LANE0) & (lane < GROUP_LANE0 + N_GROUPS)
    gmax = jnp.max(jnp.where(is_g, logits, -jnp.inf), axis=-1, keepdims=True)
    gsel = jnp.min(jnp.where(is_g & (logits == gmax), lane - GROUP_LANE0, N_GROUPS), axis=-1, keepdims=True)
    pg_sel = 1.0 / jnp.sum(jnp.where(is_g, jnp.exp(logits - gmax), 0.0), axis=-1, keepdims=True)
    in_grp = (lane < N_EXPERTS) & (lane // EXP_PER_GROUP == gsel)
    emax = jnp.max(jnp.where(in_grp, logits, -jnp.inf), axis=-1, keepdims=True)
    ee = jnp.where(in_grp, jnp.exp(logits - emax), 0.0)
    pe = jnp.where(in_grp, ee / jnp.sum(ee, axis=-1, keepdims=True), -1.0)
    p1 = jnp.max(pe, axis=-1, keepdims=True)
    i1 = jnp.min(jnp.where(pe == p1, lane, LANES), axis=-1, keepdims=True)
    pe2 = jnp.where(lane == i1, -1.0, pe)
    p2 = jnp.max(pe2, axis=-1, keepdims=True)
    i2 = jnp.min(jnp.where(pe2 == p2, lane, LANES), axis=-1, keepdims=True)
    w1 = pg_sel * p1 / (p1 + p2)
    w2 = pg_sel * p2 / (p1 + p2)

    hot1 = lane == i1
    hot2 = lane == i2
    ind = jnp.where(hot1 | hot2, 1.0, 0.0)
    tri = (lax.broadcasted_iota(I32, (tm, tm), 1) < lax.broadcasted_iota(I32, (tm, tm), 0)).astype(BF16)
    before = _dot(tri, ind.astype(BF16))
    k1 = jnp.sum(jnp.where(hot1, before, 0.0), axis=-1, keepdims=True)
    k2 = jnp.sum(jnp.where(hot2, before, 0.0), axis=-1, keepdims=True)
    rec = jnp.zeros((tm, LANES), F32)
    for slot, val in ((R_E1, i1.astype(F32)), (R_E2, i2.astype(F32)), (R_W1, w1), (R_W2, w2), (R_K1, k1), (R_K2, k2)):
        rec = jnp.where(lane == slot, val, rec)
    rt_o[...] = rec
    cnt_o[0] = jnp.broadcast_to(jnp.sum(ind, axis=0, keepdims=True), (SUBLANES, LANES))


def _merge_router(x2, attn, cact, mg, wa, wc, wo, fg, wr, br):
    t = x2.shape[0]
    tm = TR
    row = lambda n: pl.BlockSpec((tm, n), lambda i: (i, 0))
    full = lambda a: pl.BlockSpec(a.shape, lambda i: (0,) * a.ndim)
    return pl.pallas_call(
        _merge_router_kernel,
        grid=(t // tm,),
        in_specs=[row(D_MODEL), row(Q_COLS), row(CONV_CH), row(2 * D_MODEL), full(wa), full(wc), full(wo),
                  full(fg), full(wr), full(br)],
        out_specs=[row(D_MODEL), row(D_MODEL), row(LANES), pl.BlockSpec((1, SUBLANES, LANES), lambda i: (i, 0, 0))],
        out_shape=[jax.ShapeDtypeStruct((t, D_MODEL), F32), jax.ShapeDtypeStruct((t, D_MODEL), F32),
                   jax.ShapeDtypeStruct((t, LANES), F32), jax.ShapeDtypeStruct((t // tm, SUBLANES, LANES), F32)],
        compiler_params=_cparams(("parallel",)),
        name="merge_router",
    )(x2, attn, cact, mg, wa, wc, wo, fg, wr, br)


def _start_chunks(dst8_ref, tile, n_chunks, make_copy):
    def issue(c, carry):
        far = pl.multiple_of(dst8_ref[tile * NCH + c] * CHUNK, CHUNK)
        make_copy(pl.multiple_of(c * CHUNK, CHUNK), far).start()
        return carry

    lax.fori_loop(0, n_chunks, issue, 0)


def _wait_chunks(n_chunks, make_copy):
    def wait_one(c, carry):
        make_copy(0, 0).wait()
        return carry

    lax.fori_loop(0, n_chunks, wait_one, 0)


def _dispatch_kernel(dst8_ref, nct_ref, pad8_ref, npad_ref, nused_ref, h_ref, rt_ref, lo_ref, xg_ref, slot_o, xs,
                     zbuf, sem, zsem):
    tr = h_ref.shape[0]

    @pl.when(pl.program_id(0) == 0)
    def _():
        zbuf[...] = jnp.zeros(zbuf.shape, F32)

        def zero_chunk(far):
            return pltpu.make_async_copy(zbuf.at[pl.ds(0, CHUNK), :], xg_ref.at[pl.ds(far, CHUNK), :], zsem)

        def zero_block(blk):
            return pltpu.make_async_copy(zbuf, xg_ref.at[pl.ds(pl.multiple_of(blk * BM, BM), BM), :], sem.at[0])

        def per_expert(e, carry, wait):
            def one(c, inner):
                if wait:
                    zero_chunk(0).wait()
                else:
                    zero_chunk(pl.multiple_of((pad8_ref[e] + c) * CHUNK, CHUNK)).start()
                return inner
            return lax.fori_loop(0, npad_ref[e], one, carry)

        def tail(blk, carry, wait):
            if wait:
                zero_block(0).wait()
            else:
                zero_block(blk).start()
            return carry

        n_blocks = xg_ref.shape[0] // BM
        lax.fori_loop(0, N_EXPERTS, functools.partial(per_expert, wait=False), 0)
        lax.fori_loop(nused_ref[0], n_blocks, functools.partial(tail, wait=False), 0)
        lax.fori_loop(0, N_EXPERTS, functools.partial(per_expert, wait=True), 0)
        lax.fori_loop(nused_ref[0], n_blocks, functools.partial(tail, wait=True), 0)

    rt = rt_ref[...]
    lane = lax.broadcasted_iota(I32, (tr, LANES), 1)
    lane_f = lane.astype(F32)
    lo_row = lo_ref[0, 0:1, :]
    slots = []
    for e_lane, k_lane in ((R_E1, R_K1), (R_E2, R_K2)):
        hot = lane_f == rt[:, e_lane:e_lane + 1]
        slots.append(jnp.sum(jnp.where(hot, lo_row, 0.0), axis=-1, keepdims=True) + rt[:, k_lane:k_lane + 1])
    rec = jnp.where(lane == 0, slots[0], jnp.where(lane == 1, slots[1], 0.0))
    slot_o[...] = rec
    rec_t = jnp.concatenate([rec[c * LANES:(c + 1) * LANES].T for c in range(tr // LANES)], axis=1)
    row_id = lax.broadcasted_iota(I32, (R_LOCAL, tr), 0).astype(F32)
    pick = (row_id == rec_t[0:1, :]) | (row_id == rec_t[1:2, :])
    staged = _dot(jnp.where(pick, 1.0, 0.0).astype(BF16), h_ref[...].astype(BF16))

    i = pl.program_id(0)
    last = pl.num_programs(0) - 1
    slot = i % 2

    def copies_from(s):
        return lambda near, far: pltpu.make_async_copy(
            xs.at[s, pl.ds(near, CHUNK), :], xg_ref.at[pl.ds(far, CHUNK), :], sem.at[s])

    @pl.when(i >= 2)
    def _():
        _wait_chunks(nct_ref[i - 2], copies_from(slot))

    xs[slot] = staged
    _start_chunks(dst8_ref, i, nct_ref[i], copies_from(slot))

    @pl.when(i == last)
    def _():
        @pl.when(i >= 1)
        def _():
            _wait_chunks(nct_ref[i - 1], copies_from(1 - slot))
        _wait_chunks(nct_ref[i], copies_from(slot))


def _dispatch(dst8, nct, pad8, npad, nused, h2, route, lo_rows, p_rows):
    t = h2.shape[0]
    return pl.pallas_call(
        _dispatch_kernel,
        grid_spec=pltpu.PrefetchScalarGridSpec(
            num_scalar_prefetch=5,
            grid=(t // TR,),
            in_specs=[pl.BlockSpec((TR, D_MODEL), lambda i, *_: (i, 0)),
                      pl.BlockSpec((TR, LANES), lambda i, *_: (i, 0)),
                      pl.BlockSpec((1, SUBLANES, LANES), lambda i, *_: (i, 0, 0))],
            out_specs=[pl.BlockSpec(memory_space=pl.ANY),
                       pl.BlockSpec((TR, LANES), lambda i, *_: (i, 0))],
            scratch_shapes=[pltpu.VMEM((2, R_LOCAL, D_MODEL), F32), pltpu.VMEM((BM, D_MODEL), F32),
                            pltpu.SemaphoreType.DMA((2,)), pltpu.SemaphoreType.DMA(())],
        ),
        out_shape=[jax.ShapeDtypeStruct((p_rows, D_MODEL), F32), jax.ShapeDtypeStruct((t, LANES), F32)],
        compiler_params=_cparams(("arbitrary",)),
        name="moe_dispatch",
    )(dst8, nct, pad8, npad, nused, h2, route, lo_rows)


def _expert_kernel(blk_e_ref, nused_ref, x_ref, wg_ref, wu_ref, wd_ref, y_ref, wg_b, wu_b, wd_b):
    i = pl.program_id(0)
    used = i < nused_ref[0]

    @pl.when((i == 0) | (blk_e_ref[i] != blk_e_ref[jnp.maximum(i - 1, 0)]))
    def _():
        wg_b[...] = wg_ref[0].astype(BF16)
        wu_b[...] = wu_ref[0].astype(BF16)
        wd_b[...] = wd_ref[0].astype(BF16)

    @pl.when(used)
    def _():
        xb = x_ref[...].astype(BF16)
        a = _dot(xb, wg_b[...])
        u = _dot(xb, wu_b[...])
        y_ref[...] = _dot((a * jax.nn.sigmoid(a) * u).astype(BF16), wd_b[...])

    @pl.when(jnp.logical_not(used))
    def _():
        y_ref[...] = jnp.zeros(y_ref.shape, F32)


def _experts(blk_e, nused, xg, wg, wu, wd):
    p_rows = xg.shape[0]
    x_blk = lambda i, be, nu: (jnp.minimum(i, nu[0] - 1), 0)
    return pl.pallas_call(
        _expert_kernel,
        grid_spec=pltpu.PrefetchScalarGridSpec(
            num_scalar_prefetch=2,
            grid=(p_rows // BM,),
            in_specs=[pl.BlockSpec((BM, D_MODEL), x_blk),
                      pl.BlockSpec((1, D_MODEL, EXPERT_FF), lambda i, be, nu: (be[i], 0, 0)),
                      pl.BlockSpec((1, D_MODEL, EXPERT_FF), lambda i, be, nu: (be[i], 0, 0)),
                      pl.BlockSpec((1, EXPERT_FF, D_MODEL), lambda i, be, nu: (be[i], 0, 0))],
            out_specs=pl.BlockSpec((BM, D_MODEL), lambda i, be, nu: (i, 0)),
            scratch_shapes=[pltpu.VMEM((D_MODEL, EXPERT_FF), BF16), pltpu.VMEM((D_MODEL, EXPERT_FF), BF16),
                            pltpu.VMEM((EXPERT_FF, D_MODEL), BF16)],
        ),
        out_shape=jax.ShapeDtypeStruct((p_rows, D_MODEL), F32),
        compiler_params=_cparams(("arbitrary",)),
        name="moe_experts",
    )(blk_e, nused, xg, wg, wu, wd)


def _combine_ple_kernel(dst8_ref, nct_ref, x1_ref, rt_ref, slot_ref, p_ref, pg_ref, wpg_ref, bpg_ref, wpp_ref,
                        yg_ref, o_ref, ys, sem):
    tr = x1_ref.shape[0]
    i = pl.program_id(0)
    slot = i % 2

    def copies_into(s):
        return lambda near, far: pltpu.make_async_copy(
            yg_ref.at[pl.ds(far, CHUNK), :], ys.at[s, pl.ds(near, CHUNK), :], sem.at[s])

    def fetch(tile, s):
        _start_chunks(dst8_ref, tile, nct_ref[tile], copies_into(s))

        def clear(c, carry):
            ys[s, pl.ds(pl.multiple_of(c * CHUNK, CHUNK), CHUNK), :] = jnp.zeros((CHUNK, D_MODEL), F32)
            return carry

        lax.fori_loop(nct_ref[tile], NCH, clear, 0)

    @pl.when(i == 0)
    def _():
        fetch(0, 0)

    @pl.when(i + 1 < pl.num_programs(0))
    def _():
        fetch(i + 1, 1 - slot)

    _wait_chunks(nct_ref[i], copies_into(slot))

    rt = rt_ref[...]
    sl = slot_ref[...]
    col_id = lax.broadcasted_iota(I32, (tr, R_LOCAL), 1).astype(F32)
    yb = ys[slot].astype(BF16)
    gather_w = jnp.where(col_id == sl[:, 0:1], rt[:, R_W1:R_W1 + 1],
                         jnp.where(col_id == sl[:, 1:2], rt[:, R_W2:R_W2 + 1], 0.0)).astype(BF16)
    x2 = x1_ref[...] + _dot(gather_w, yb)
    h3 = (x2 * lax.rsqrt(jnp.mean(x2 * x2, axis=-1, keepdims=True) + EPS) * pg_ref[...]).astype(BF16)
    gate = jax.nn.sigmoid(_dot(h3, wpg_ref[...]) + bpg_ref[...])
    o_ref[...] = x2 + gate * _dot(p_ref[...].astype(BF16), wpp_ref[...])


def _combine_ple(dst8, nct, x1, route, slots, p2, pg, wpg, bpg, wpp, yg):
    t = x1.shape[0]
    row = lambda n: pl.BlockSpec((TR, n), lambda i, d, c: (i, 0))
    full = lambda a: pl.BlockSpec(a.shape, lambda i, d, c: (0,) * a.ndim)
    return pl.pallas_call(
        _combine_ple_kernel,
        grid_spec=pltpu.PrefetchScalarGridSpec(
            num_scalar_prefetch=2,
            grid=(t // TR,),
            in_specs=[row(D_MODEL), row(LANES), row(LANES), row(PLE_DIM), full(pg), full(wpg), full(bpg),
                      full(wpp), pl.BlockSpec(memory_space=pl.ANY)],
            out_specs=row(D_MODEL),
            scratch_shapes=[pltpu.VMEM((2, R_LOCAL, D_MODEL), F32), pltpu.SemaphoreType.DMA((2,))],
        ),
        out_shape=jax.ShapeDtypeStruct((t, D_MODEL), F32),
        compiler_params=_cparams(("arbitrary",)),
        name="moe_combine_ple",
    )(dst8, nct, x1, route, slots, p2, pg, wpg, bpg, wpp, yg)


def _seg_sum_matrix():
    seg = np.arange(LANES) // HEAD_DIM
    return jnp.asarray(seg[:, None] == seg[None, :], BF16)


def _overlap_t(nc_rows, n_blk):
    nc = np.arange(nc_rows)[None, :] * CMP_STRIDE
    sb = np.arange(n_blk)[:, None] * SEL_BLOCK
    ov = (nc < sb + SEL_BLOCK) & (nc + CMP_BLOCK > sb) & (np.arange(nc_rows)[None, :] < nc_rows - 1)
    return jnp.asarray(ov, BF16)


def _compress_weights(pos, w1, w2):
    half = CMP_BLOCK // 2
    pe = jnp.broadcast_to(pos[:, None, :], (CMP_BLOCK, N_KV, HEAD_DIM)).reshape(2, half * N_KV * HEAD_DIM)
    w1r = w1.reshape(2, half, HEAD_DIM, CMP_HIDDEN)
    eye = jnp.eye(N_KV, dtype=w1.dtype)
    wx = jnp.einsum('hldj,gk->hlgdkj', w1r, eye).reshape(2, half * N_KV * HEAD_DIM, N_KV * CMP_HIDDEN)
    w2x = jnp.einsum('jd,gk->gjkd', w2, eye).reshape(N_KV * CMP_HIDDEN, N_KV * HEAD_DIM)
    return pe, wx[0].astype(BF16), wx[1].astype(BF16), w2x.astype(BF16)


def _layer(x, p_i, prm):
    b, s, _ = x.shape
    t = b * s
    x2 = x.reshape(t, D_MODEL)
    bd = _seg_sum_matrix()

    w = prm["w_in"]
    kv = lambda j: w[:, OFF_KV + j * KV_COLS:OFF_KV + (j + 1) * KV_COLS]
    w_nat = jnp.concatenate([kv(0), kv(1), kv(2), kv(4), w[:, OFF_CONV:]], axis=1).astype(BF16)
    w_t = jnp.concatenate([w[:, :Q_COLS], kv(3), kv(5),
                           jnp.pad(w[:, OFF_GATE:OFF_CONV], ((0, 0), (0, LANES - 3 * N_HEADS)))], axis=1).T.astype(BF16)
    qg = jnp.broadcast_to(prm["q_norm_g"][:, None], (HEAD_DIM, TM_PROJ))
    kg = jnp.tile(prm["k_norm_g"], (1, LANES // HEAD_DIM))
    qt, kc_raw, vc_raw, ks, vst, kw, vwt, gtt, glu, mg = _in_proj(
        x2, prm["attn_norm_g"][None, :], w_nat, w_t, qg, kg, bd, prm["merge_b"][None, :], s)

    nrow = s // CMP_STRIDE
    cw = [_compress_weights(prm["cmp_pos"][j], prm["cmp_w1"][j], prm["cmp_w2"][j]) for j in range(2)]
    pe_x, wa, wb = (jnp.stack([cw[0][j], cw[1][j]]) for j in range(3))
    kc, vct = _compress(kc_raw.reshape(b, nrow, CMP_STRIDE * LANES), vc_raw.reshape(b, nrow, CMP_STRIDE * LANES),
                        pe_x, wa, wb, cw[0][3], cw[1][3].T, bd, kg)

    attn = _attention(qt, gtt, kc, vct, ks.reshape(b, s, 4 * LANES), vst, kw.reshape(b, s, 2 * LANES), vwt,
                      _overlap_t(nrow, s // SEL_BLOCK), b, s)

    cact = _conformer_conv(glu, prm["conv_w"], prm["conv_b"][None, :], prm["conv_norm_g"][None, :],
                           prm["conv_norm_b"][None, :], b, s)

    wr_full = jnp.zeros((D_MODEL, LANES), F32)
    wr_full = wr_full.at[:, :N_EXPERTS].set(prm["w_router_expert"])
    wr_full = wr_full.at[:, GROUP_LANE0:GROUP_LANE0 + N_GROUPS].set(prm["w_router_group"])
    wr_hi = wr_full.astype(BF16)
    wr = jnp.stack([wr_hi, (wr_full - wr_hi.astype(F32)).astype(BF16)])
    br = jnp.zeros((1, LANES), F32)
    br = br.at[0, :N_EXPERTS].set(prm["b_router_expert"])
    br = br.at[0, GROUP_LANE0:GROUP_LANE0 + N_GROUPS].set(prm["b_router_group"])
    x1, h2, route, cnt = _merge_router(
        x2, attn, cact, mg, prm["w_attn_out"].astype(BF16), prm["w_conv_out"].astype(BF16),
        prm["w_out"].astype(BF16), prm["ffn_norm_g"][None, :], wr, br)

    n_tiles = t // TR
    counts = cnt[:, 0, :N_EXPERTS].astype(I32)
    seg = (counts + CHUNK - 1) // CHUNK * CHUNK
    local_off = jnp.cumsum(seg, axis=1) - seg
    nct = (jnp.sum(seg, axis=1) // CHUNK).astype(I32)
    total = jnp.sum(seg, axis=0)
    padded = (total + BM - 1) // BM * BM
    pends = jnp.cumsum(padded)
    far_off = (pends - padded)[None, :] + jnp.cumsum(seg, axis=0) - seg
    chunk = jnp.arange(NCH, dtype=I32)[None, :, None]
    lo8 = (local_off // CHUNK)[:, None, :]
    in_seg = (chunk >= lo8) & (chunk < ((local_off + seg) // CHUNK)[:, None, :])
    dst8 = (jnp.sum(jnp.where(in_seg, (far_off // CHUNK)[:, None, :] - lo8, 0), axis=-1)
            + chunk[:, :, 0]).astype(I32).reshape(n_tiles * NCH)
    lo_rows = jnp.broadcast_to(jnp.pad(local_off.astype(F32), ((0, 0), (0, LANES - N_EXPERTS)))[:, None, :],
                               (n_tiles, SUBLANES, LANES))
    n_blk = -(-(2 * t + n_tiles * N_EXPERTS * (CHUNK - 1) + N_EXPERTS * (BM - 1)) // BM)
    blk_e = jnp.minimum(jnp.sum(jnp.arange(n_blk, dtype=I32)[:, None] * BM >= pends[None, :], axis=1),
                        N_EXPERTS - 1).astype(I32)
    nused = (pends[-1:] // BM).astype(I32)
    pad8 = ((pends - padded + total) // CHUNK).astype(I32)
    npad = ((padded - total) // CHUNK).astype(I32)

    xg, slots = _dispatch(dst8, nct, pad8, npad, nused, h2, route, lo_rows, n_blk * BM)
    yg = _experts(blk_e, nused, xg, prm["w_exp_gate"], prm["w_exp_up"], prm["w_exp_down"])
    out = _combine_ple(dst8, nct, x1, route, slots, p_i.reshape(t, PLE_DIM), prm["ple_norm_g"][None, :],
                       prm["w_ple_gate"].astype(BF16), prm["b_ple_gate"][None, :], prm["w_ple_proj"].astype(BF16), yg)
    return out.reshape(b, s, D_MODEL)


def kernel(x, p, attn_norm_g, w_in, q_norm_g, k_norm_g, cmp_pos, cmp_w1, cmp_w2, w_attn_out, conv_w, conv_b, conv_norm_g, conv_norm_b, w_conv_out, merge_b, w_out, ffn_norm_g, w_router_group, b_router_group, w_router_expert, b_router_expert, w_exp_gate, w_exp_up, w_exp_down, ple_norm_g, w_ple_gate, b_ple_gate, w_ple_proj):
    stacked = dict(attn_norm_g=attn_norm_g, w_in=w_in, q_norm_g=q_norm_g, k_norm_g=k_norm_g, cmp_pos=cmp_pos,
                   cmp_w1=cmp_w1, cmp_w2=cmp_w2, w_attn_out=w_attn_out, conv_w=conv_w, conv_b=conv_b,
                   conv_norm_g=conv_norm_g, conv_norm_b=conv_norm_b, w_conv_out=w_conv_out, merge_b=merge_b,
                   w_out=w_out, ffn_norm_g=ffn_norm_g, w_router_group=w_router_group,
                   b_router_group=b_router_group, w_router_expert=w_router_expert,
                   b_router_expert=b_router_expert, w_exp_gate=w_exp_gate, w_exp_up=w_exp_up,
                   w_exp_down=w_exp_down, ple_norm_g=ple_norm_g, w_ple_gate=w_ple_gate, b_ple_gate=b_ple_gate,
                   w_ple_proj=w_ple_proj)
    for i in range(w_in.shape[0]):
        x = _layer(x, p[i], {k: v[i] for k, v in stacked.items()})
    return x
```

```python
import functools

import numpy as np
import jax
import jax.numpy as jnp
from jax import lax
from jax.experimental import pallas as pl
from jax.experimental.pallas import tpu as pltpu

F32 = jnp.float32
BF16 = jnp.bfloat16
I32 = jnp.int32

D_MODEL = 1024
PLE_DIM = 256
N_HEADS = 8
N_KV = 2
HPG = N_HEADS // N_KV
HEAD_DIM = 64
CMP_BLOCK = 32
CMP_STRIDE = 16
CMP_HIDDEN = 128
SEL_BLOCK = 64
N_SEL = 8
WINDOW = 512
CONV_CH = 512
CONV_WIDTH = 31
N_GROUPS = 4
EXP_PER_GROUP = 8
N_EXPERTS = N_GROUPS * EXP_PER_GROUP
EXPERT_FF = 512
EPS = 1e-6
NEG = -1e30
BIG = 1e4
Q_COLS = N_HEADS * HEAD_DIM
KV_COLS = N_KV * HEAD_DIM
OFF_KV = Q_COLS
OFF_GATE = OFF_KV + 6 * KV_COLS
OFF_CONV = OFF_GATE + 3 * N_HEADS
N_FORCED = 3

LANES = 128
SUBLANES = 8
VMEM_LIMIT = 56 * 1024 * 1024

TM_PROJ = 256
QB = 256
KT = 256
TS_CONV = 512
CONV_ROWS = 64
HALO = 32
TR = 512
BM = 512
CHUNK = SUBLANES
R_LOCAL = 2 * TR + N_EXPERTS * CHUNK
NCH = R_LOCAL // CHUNK

N_KC = 0
N_VC = N_KC + LANES
N_KS = N_VC + LANES
N_KW = N_KS + LANES
N_CONV = N_KW + LANES
N_MERGE = N_CONV + 2 * CONV_CH
T_Q = 0
T_VS = T_Q + Q_COLS
T_VW = T_VS + LANES
T_GATE = T_VW + LANES

POS_HI_LANE = HEAD_DIM
POS_LO_LANE = HEAD_DIM + 1
UNSELECTED = -(2.0 ** 40)

R_E1, R_E2, R_W1, R_W2, R_K1, R_K2 = 0, 1, 2, 3, 4, 5
GROUP_LANE0 = N_EXPERTS


def _cparams(sem, vmem=VMEM_LIMIT):
    return pltpu.CompilerParams(dimension_semantics=sem, vmem_limit_bytes=vmem)


def _dot(a, b):
    return jnp.dot(a, b, preferred_element_type=F32)


def _dot_nt(a, b):
    return lax.dot_general(a, b, (((1,), (1,)), ((), ())), preferred_element_type=F32)


def _split(x):
    hi = x.astype(BF16)
    lo = (x - hi.astype(F32)).astype(BF16)
    return hi, lo


def _seg_rmsnorm(z, bd):
    hi, lo = _split(z * z)
    ss = _dot(hi, bd) + _dot(lo, bd)
    return z * lax.rsqrt(ss * (1.0 / HEAD_DIM) + EPS)


def _with_pos_lanes(kn, pos):
    lane = lax.broadcasted_iota(I32, kn.shape, 1)
    hi = pos // SEL_BLOCK * SEL_BLOCK
    pos_lanes = jnp.where(lane == POS_HI_LANE, hi.astype(F32),
                          jnp.where(lane == POS_LO_LANE, (pos - hi).astype(F32), 0.0))
    lo_half = lane < HEAD_DIM
    return (jnp.where(lo_half, kn, pos_lanes).astype(BF16),
            jnp.where(lo_half, pltpu.roll(kn, HEAD_DIM, 1), pos_lanes).astype(BF16))


def _in_proj_kernel(x_ref, g_ref, w_ref, wt_ref, qg_ref, kg_ref, bd_ref, mb_ref,
                    qt_o, kc_o, vc_o, ks_o, vst_o, kw_o, vwt_o, gtt_o, glu_o, mg_o, *, seq_len):
    tm = x_ref.shape[0]
    x = x_ref[...]
    h = (x * lax.rsqrt(jnp.mean(x * x, axis=-1, keepdims=True) + EPS) * g_ref[...]).astype(BF16)
    bd = bd_ref[...]
    lane = lax.broadcasted_iota(I32, (tm, LANES), 1)

    def proj(a, b):
        return _dot(h, w_ref[:, a:b])

    zt = _dot_nt(wt_ref[...], h)
    for hd in range(N_HEADS):
        zh = zt[T_Q + hd * HEAD_DIM:T_Q + (hd + 1) * HEAD_DIM, :]
        ms = jnp.sum(zh * zh, axis=0, keepdims=True) * (1.0 / HEAD_DIM)
        qt_o[hd * HEAD_DIM:(hd + 1) * HEAD_DIM, :] = (
            zh * lax.rsqrt(ms + EPS) * qg_ref[...] * (HEAD_DIM ** -0.5)).astype(BF16)
    ones = jnp.ones((HEAD_DIM, tm), F32)
    for off, v_o in ((T_VS, vst_o), (T_VW, vwt_o)):
        v_o[0] = jnp.concatenate([zt[off:off + HEAD_DIM, :], ones, zt[off + HEAD_DIM:off + LANES, :], ones],
                                 axis=0).astype(BF16)
    gtt_o[...] = jax.nn.sigmoid(zt[T_GATE:T_GATE + LANES, :])

    kc_o[...] = proj(N_KC, N_KC + LANES)
    vc_o[...] = proj(N_VC, N_VC + LANES)

    pos = (pl.program_id(0) * tm) % seq_len + lax.broadcasted_iota(I32, (tm, LANES), 0)
    onehot = jnp.where(lane == pos // SEL_BLOCK, 1.0, 0.0).astype(BF16)

    def key_tiles(zk, gain):
        return _with_pos_lanes(_seg_rmsnorm(zk, bd) * gain, pos)

    k0, k1 = key_tiles(proj(N_KS, N_KS + LANES), kg_ref[1:2, :])
    ks_o[...] = jnp.concatenate([k0, onehot, k1, onehot], axis=1)
    k0, k1 = key_tiles(proj(N_KW, N_KW + LANES), kg_ref[2:3, :])
    kw_o[...] = jnp.concatenate([k0, k1], axis=1)
    za = proj(N_CONV, N_CONV + CONV_CH)
    zg = proj(N_CONV + CONV_CH, N_CONV + 2 * CONV_CH)
    glu_o[...] = za * jax.nn.sigmoid(zg)
    for c in range(2):
        zm = proj(N_MERGE + c * D_MODEL, N_MERGE + (c + 1) * D_MODEL)
        mg_o[:, c * D_MODEL:(c + 1) * D_MODEL] = jax.nn.sigmoid(
            zm + mb_ref[:, c * D_MODEL:(c + 1) * D_MODEL]).astype(BF16)


def _in_proj(x2, norm_g, w_nat, w_t, qg, kg, bd, mb, seq_len):
    t = x2.shape[0]
    tm = TM_PROJ
    assert tm == KT and seq_len % tm == 0 and seq_len // SEL_BLOCK <= HEAD_DIM
    row = lambda n: pl.BlockSpec((tm, n), lambda i: (i, 0))
    col = lambda n: pl.BlockSpec((n, tm), lambda i: (0, i))
    vt = pl.BlockSpec((1, 2 * LANES, tm), lambda i: (i, 0, 0))
    full = lambda a: pl.BlockSpec(a.shape, lambda i: (0,) * a.ndim)
    sds = jax.ShapeDtypeStruct
    return pl.pallas_call(
        functools.partial(_in_proj_kernel, seq_len=seq_len),
        grid=(t // tm,),
        in_specs=[row(D_MODEL), full(norm_g), full(w_nat), full(w_t), full(qg), full(kg), full(bd), full(mb)],
        out_specs=[col(Q_COLS), row(LANES), row(LANES), row(4 * LANES), vt, row(2 * LANES), vt, col(LANES),
                   row(CONV_CH), row(2 * D_MODEL)],
        out_shape=[sds((Q_COLS, t), BF16), sds((t, LANES), F32), sds((t, LANES), F32), sds((t, 4 * LANES), BF16),
                   sds((t // tm, 2 * LANES, tm), BF16), sds((t, 2 * LANES), BF16), sds((t // tm, 2 * LANES, tm), BF16),
                   sds((LANES, t), F32), sds((t, CONV_CH), F32), sds((t, 2 * D_MODEL), BF16)],
        compiler_params=_cparams(("parallel",)),
        name="in_proj",
    )(x2, norm_g, w_nat, w_t, qg, kg, bd, mb)


def _compress_kernel(ck_ref, cv_ref, pe_ref, wa_ref, wb_ref, w2k_ref, w2vt_ref, bd_ref, kg_ref, kc_o, vct_o):
    nrow = ck_ref.shape[1] // CMP_STRIDE

    def hidden(idx, c_ref):
        ya = jnp.zeros((nrow, N_KV * CMP_HIDDEN), F32)
        yb = jnp.zeros((nrow, N_KV * CMP_HIDDEN), F32)
        for l in range(CMP_STRIDE):
            tok = c_ref[0, pl.ds(l, nrow, stride=CMP_STRIDE), :]
            cols = slice(l * LANES, (l + 1) * LANES)
            ya = ya + _dot((tok + pe_ref[idx, 0:1, cols]).astype(BF16), wa_ref[idx, cols, :])
            yb = yb + _dot((tok + pe_ref[idx, 1:2, cols]).astype(BF16), wb_ref[idx, cols, :])
        return jax.nn.gelu(ya + pltpu.roll(yb, nrow - 1, 0)).astype(BF16)

    kn = _seg_rmsnorm(_dot(hidden(0, ck_ref), w2k_ref[...]), bd_ref[...]) * kg_ref[0:1, :]
    c_end = lax.broadcasted_iota(I32, (nrow, LANES), 0) * CMP_STRIDE + (CMP_BLOCK - 1)
    kc_o[0] = jnp.concatenate(_with_pos_lanes(kn, c_end), axis=1)
    vt = _dot_nt(w2vt_ref[...], hidden(1, cv_ref))
    ones = jnp.ones((HEAD_DIM, nrow), F32)
    vct_o[0] = jnp.concatenate([vt[:HEAD_DIM], ones, vt[HEAD_DIM:], ones], axis=0).astype(BF16)


def _compress(kc_raw, vc_raw, pe_x, wa, wb, w2k, w2vt, bd, kg):
    b, s, width = kc_raw.shape
    nrow = s // CMP_STRIDE
    batch = lambda r, n: pl.BlockSpec((1, r, n), lambda i: (i, 0, 0))
    full = lambda a: pl.BlockSpec(a.shape, lambda i: (0,) * a.ndim)
    return pl.pallas_call(
        _compress_kernel,
        grid=(b,),
        in_specs=[batch(s, width), batch(s, width), full(pe_x), full(wa), full(wb), full(w2k), full(w2vt),
                  full(bd), full(kg)],
        out_specs=[batch(nrow, 2 * LANES), batch(2 * LANES, nrow)],
        out_shape=[jax.ShapeDtypeStruct((b, nrow, 2 * LANES), BF16),
                   jax.ShapeDtypeStruct((b, 2 * LANES, nrow), BF16)],
        compiler_params=_cparams(("parallel",)),
        name="compress",
    )(kc_raw, vc_raw, pe_x, wa, wb, w2k, w2vt, bd, kg)


def _halve_rows(x, op):
    rows = x.shape[0]
    while rows > SUBLANES:
        rows //= 2
        x = op(x[:rows], x[rows:])
    return x


def _attn_kernel(qt_ref, gtt_ref, kc_ref, vct_ref, ks_ref, vst_ref, kw_ref, vwt_ref, ovt_ref, o_ref,
                 tiles_ref, sa_ref, sb_ref, m_ref, acc_ref):
    i = pl.program_id(1)
    q0 = i * QB
    nc_rows = kc_ref.shape[1]
    n_blk = ovt_ref.shape[0]
    hpc = 2 * LANES // QB
    pw = hpc * QB
    n_pairs = N_HEADS // hpc
    group_of = lambda c: c * hpc // HPG
    rel = ((lax.broadcasted_iota(I32, (KT, pw), 1) & (QB - 1))
           - lax.broadcasted_iota(I32, (KT, pw), 0)).astype(F32)
    slope_rows = lax.broadcasted_iota(I32, (HEAD_DIM, QB), 0) < 2
    gtt = gtt_ref[...]

    pairs = []
    for pr in range(n_pairs):
        q_pos = []
        for hd in range(pr * hpc, (pr + 1) * hpc):
            slope = jnp.where(slope_rows, 2.0 ** -(hd + 1), 0.0).astype(BF16)
            q_pos.append(jnp.concatenate([qt_ref[hd * HEAD_DIM:(hd + 1) * HEAD_DIM, :], slope], axis=0))
        pairs.append(jnp.concatenate(q_pos, axis=1))

    def update(s, vt, m, acc):
        m_new = jnp.maximum(m, jnp.max(_halve_rows(s, jnp.maximum), axis=0, keepdims=True))
        p = jnp.exp(s - m_new).astype(BF16)
        return m_new, jnp.exp(m - m_new) * acc + _dot(vt, p)

    def scores(j, q_list, k_ref, k_width):
        k0 = pl.multiple_of(j * KT, KT)
        return tuple(_dot(k_ref[0, pl.ds(k0, KT), group_of(c) * k_width:(group_of(c) + 1) * k_width], q_list[c])
                     for c in range(n_pairs))

    def consume(j, s_list, state, vt_ref, mask_fn, extra_dist=0.0):
        keep = None if mask_fn is None else mask_fn(rel + ((q0 - j * KT).astype(F32) + extra_dist))
        out = []
        for c in range(n_pairs):
            g = group_of(c)
            s = s_list[c] if keep is None else jnp.where(keep, s_list[c], NEG)
            out += update(s, vt_ref[j, g * LANES:(g + 1) * LANES, :], *state[2 * c:2 * c + 2])
        return tuple(out)

    j_diag = (q0 + QB - 1) // KT
    init = (jnp.full((1, pw), NEG, F32), jnp.zeros((LANES, pw), F32)) * n_pairs

    def selected_branch(q_list, n_act):
        def fill(buf, n):
            for c, s in enumerate(scores(tiles_ref[n], q_list, ks_ref, 2 * LANES)):
                buf[c] = s

        def use(buf, n, mask_fn):
            state = []
            for c in range(n_pairs):
                state += [m_ref[c, 0:1, :], acc_ref[c]]
            out = consume(tiles_ref[n], [buf[c] for c in range(n_pairs)], state, vst_ref, mask_fn)
            for c in range(n_pairs):
                m_ref[c, 0:1, :] = out[2 * c]
                acc_ref[c] = out[2 * c + 1]

        for c in range(n_pairs):
            m_ref[c] = jnp.full((SUBLANES, pw), NEG, F32)
            acc_ref[c] = jnp.zeros((LANES, pw), F32)
        fill(sa_ref, 0)

        def two_tiles(h, carry):
            n = 2 * h
            fill(sb_ref, n + 1)
            use(sa_ref, n, None)

            @pl.when(n + 1 < n_act)
            def _():
                fill(sa_ref, n + 2)
                use(sb_ref, n + 1, None)
            return carry

        lax.fori_loop(0, (n_act + 1) // 2, two_tiles, 0)

        @pl.when(n_act % 2 == 1)
        def _():
            for c in range(n_pairs):
                sa_ref[c] = sb_ref[c]

        use(sa_ref, n_act, lambda d: d >= 0.0)
        return [acc_ref[c] for c in range(n_pairs)]

    cidx = lax.broadcasted_iota(I32, (nc_rows, pw), 0)
    c_end = cidx * CMP_STRIDE + (CMP_BLOCK - 1)
    cmask = (c_end <= q0 + (lax.broadcasted_iota(I32, (nc_rows, pw), 1) & (QB - 1))) & (cidx < nc_rows - 1)

    sees_any = jnp.where(q0 + (lax.broadcasted_iota(I32, (1, pw), 1) & (QB - 1)) >= CMP_BLOCK - 1, 1.0, 0.0)
    s_cmp = [_dot(kc_ref[0, :, group_of(c) * LANES:(group_of(c) + 1) * LANES], pairs[c]) for c in range(n_pairs)]
    o_cmp, imp = [], []
    for g in range(N_KV):
        imp_g = jnp.zeros((n_blk, QB), F32)
        for pr in range(g * HPG // hpc, (g + 1) * HPG // hpc):
            s = jnp.where(cmask, s_cmp[pr], NEG)
            e = jnp.exp(s - jnp.max(_halve_rows(s, jnp.maximum), axis=0, keepdims=True)).astype(BF16)
            acc = _dot(vct_ref[0, g * LANES:(g + 1) * LANES, :], e)
            inv = sees_any / acc[HEAD_DIM:HEAD_DIM + 1]
            o_cmp.append(acc[:HEAD_DIM] * inv)
            weighted = _dot(ovt_ref[...], e) * inv
            for k in range(hpc):
                imp_g = imp_g + weighted[:, k * QB:(k + 1) * QB]
        imp.append(imp_g)

    blk = lax.broadcasted_iota(I32, (n_blk, N_KV * QB), 0)
    tl = q0 + (lax.broadcasted_iota(I32, (n_blk, N_KV * QB), 1) & (QB - 1))
    valid = blk * SEL_BLOCK <= tl
    cur = tl // SEL_BLOCK
    forced = (blk == 0) | (blk == cur) | (blk == cur - 1)
    sel = forced & valid
    score = jnp.where(valid, jnp.where(sel, -jnp.inf, jnp.concatenate(imp, axis=1)), -BIG)
    for _ in range(min(N_SEL, n_blk) - N_FORCED):
        mx = jnp.max(score, axis=0, keepdims=True)
        first = jnp.min(jnp.where(score == mx, blk, n_blk), axis=0, keepdims=True)
        pick = blk == first
        sel = sel | pick
        score = jnp.where(pick, -jnp.inf, score)
    chosen = sel & valid
    any_sel = jnp.where(chosen[:, :QB] | chosen[:, QB:], 1.0, 0.0)
    unsel = jnp.where(chosen, 0.0, UNSELECTED).astype(BF16)
    bias = jnp.concatenate([unsel, jnp.zeros((2 * HEAD_DIM - n_blk, N_KV * QB), BF16)], axis=0)
    q_sel = []
    for c in range(n_pairs):
        g_bias = bias[:, group_of(c) * QB:(group_of(c) + 1) * QB]
        q_sel.append(jnp.concatenate([pairs[c], jnp.concatenate([g_bias] * hpc, axis=1)], axis=0))

    per_blk = jnp.sum(any_sel, axis=1, keepdims=True)
    n_act = jnp.int32(0)
    for jt in range(n_blk * SEL_BLOCK // KT):
        blocks = slice(jt * (KT // SEL_BLOCK), (jt + 1) * (KT // SEL_BLOCK))
        tiles_ref[n_act] = jnp.int32(jt)
        n_act = n_act + ((jnp.sum(per_blk[blocks]) > 0.0) & (jt < j_diag)).astype(I32)
    tiles_ref[n_act] = j_diag

    assert WINDOW == 2 * KT and KT % QB == 0
    win_out = init
    for w in range(WINDOW // KT + 1):
        jw = j_diag - WINDOW // KT + w
        before_start = jnp.where(jw < 0, float(WINDOW + KT), 0.0)
        jc = jnp.maximum(jw, 0)
        mask_fn = ((lambda d: d < float(WINDOW)), (lambda d, jw=jw: jw >= 0), (lambda d: d >= 0.0))[w]
        win_out = consume(jc, scores(jc, pairs, kw_ref, LANES), win_out, vwt_ref, mask_fn, before_start)

    sel_acc = selected_branch(q_sel, n_act)

    o_sel = [a[:HEAD_DIM] / a[HEAD_DIM:HEAD_DIM + 1] for a in sel_acc]
    o_win = [a[:HEAD_DIM] / a[HEAD_DIM:HEAD_DIM + 1] for a in win_out[1::2]]
    mixed = []
    for hd in range(N_HEADS):
        c, lanes = hd // hpc, slice((hd % hpc) * QB, (hd % hpc + 1) * QB)
        mixed.append(gtt[3 * hd:3 * hd + 1, :] * o_cmp[c][:, lanes] + gtt[3 * hd + 1:3 * hd + 2, :] * o_sel[c][:, lanes]
                     + gtt[3 * hd + 2:3 * hd + 3, :] * o_win[c][:, lanes])
    for pr in range(N_HEADS // 2):
        two = jnp.concatenate(mixed[2 * pr:2 * pr + 2], axis=0)
        o_ref[:, pr * LANES:(pr + 1) * LANES] = jnp.concatenate(
            [two[:, k * LANES:(k + 1) * LANES].T for k in range(QB // LANES)], axis=0).astype(BF16)


def _attention(qt, gtt, kc, vct, ks, vst, kw, vwt, ovt, b, s):
    n_q = s // QB
    n_tiles = N_HEADS * QB // (2 * LANES)
    per_q = lambda n: pl.BlockSpec((n, QB), lambda bi, i: (0, bi * n_q + i))
    per_b = lambda a: pl.BlockSpec((1,) + a.shape[1:], lambda bi, i: (bi, 0, 0))
    vt = pl.BlockSpec((s // KT, 2 * LANES, KT), lambda bi, i: (bi, 0, 0))
    return pl.pallas_call(
        _attn_kernel,
        grid=(b, n_q),
        in_specs=[per_q(Q_COLS), per_q(LANES), per_b(kc), per_b(vct), per_b(ks), vt, per_b(kw), vt,
                  pl.BlockSpec(ovt.shape, lambda bi, i: (0, 0))],
        out_specs=pl.BlockSpec((QB, Q_COLS), lambda bi, i: (bi * n_q + i, 0)),
        out_shape=jax.ShapeDtypeStruct((b * s, Q_COLS), BF16),
        scratch_shapes=[pltpu.SMEM((s // KT + 1,), I32),
                        pltpu.VMEM((n_tiles, KT, 2 * LANES), F32), pltpu.VMEM((n_tiles, KT, 2 * LANES), F32),
                        pltpu.VMEM((n_tiles, SUBLANES, 2 * LANES), F32),
                        pltpu.VMEM((n_tiles, LANES, 2 * LANES), F32)],
        compiler_params=_cparams(("parallel", "parallel")),
        name="nsa_attention",
    )(qt, gtt, kc, vct, ks, vst, kw, vwt, ovt)


def _conv_kernel(z_ref, w_ref, b_ref, g_ref, beta_ref, o_ref, buf, shifted):
    ts = z_ref.shape[0]
    chunk = CONV_ROWS

    @pl.when(pl.program_id(1) == 0)
    def _():
        buf[0:HALO, :] = jnp.zeros((HALO, CONV_CH), F32)

    buf[HALO:HALO + ts, :] = z_ref[...]
    span = ts + HALO - SUBLANES
    for r in range(1, SUBLANES):
        shifted[r - 1, 0:span, :] = buf[r:r + span, :]

    def tap_rows(k, c):
        start = HALO - (CONV_WIDTH - 1) + k
        r = start % SUBLANES
        src = buf if r == 0 else shifted.at[r - 1]
        return src[start - r + c * chunk:start - r + (c + 1) * chunk, :]

    for c in range(ts // chunk):
        acc = jnp.zeros((chunk, CONV_CH), F32) + b_ref[...]
        for k in range(CONV_WIDTH):
            acc = acc + tap_rows(k, c) * w_ref[k:k + 1, :]
        mu = jnp.mean(acc, axis=-1, keepdims=True)
        d = acc - mu
        var = jnp.mean(d * d, axis=-1, keepdims=True)
        y = d * lax.rsqrt(var + EPS) * g_ref[...] + beta_ref[...]
        o_ref[c * chunk:(c + 1) * chunk, :] = (y * jax.nn.sigmoid(y)).astype(BF16)
    buf[0:HALO, :] = buf[ts:ts + HALO, :]


def _conformer_conv(glu, w, bias, g, beta, b, s):
    ts = min(TS_CONV, s)
    full = lambda a: pl.BlockSpec(a.shape, lambda bi, i: (0, 0))
    row = pl.BlockSpec((ts, CONV_CH), lambda bi, i: (bi * (s // ts) + i, 0))
    return pl.pallas_call(
        _conv_kernel,
        grid=(b, s // ts),
        in_specs=[row, full(w), full(bias), full(g), full(beta)],
        out_specs=row,
        out_shape=jax.ShapeDtypeStruct((b * s, CONV_CH), BF16),
        scratch_shapes=[pltpu.VMEM((HALO + ts, CONV_CH), F32),
                        pltpu.VMEM((SUBLANES - 1, HALO + ts, CONV_CH), F32)],
        compiler_params=_cparams(("arbitrary", "arbitrary")),
        name="conformer_conv",
    )(glu, w, bias, g, beta)


def _merge_router_kernel(x_ref, a_ref, c_ref, mg_ref, wa_ref, wc_ref, wo_ref, fg_ref, wr_ref, br_ref,
                         x1_o, h2_o, rt_o, cnt_o):
    tm = x_ref.shape[0]
    y_a = _dot(a_ref[...], wa_ref[...])
    y_c = _dot(c_ref[...], wc_ref[...])
    mix = mg_ref[:, 0:D_MODEL].astype(F32) * y_a + mg_ref[:, D_MODEL:2 * D_MODEL].astype(F32) * y_c
    x1 = x_ref[...] + _dot(mix.astype(BF16), wo_ref[...])
    x1_o[...] = x1
    h2 = x1 * lax.rsqrt(jnp.mean(x1 * x1, axis=-1, keepdims=True) + EPS) * fg_ref[...]
    h2_o[...] = h2

    hh, hl = _split(h2)
    logits = _dot(hh, wr_ref[0]) + _dot(hl, wr_ref[0]) + _dot(hh, wr_ref[1]) + br_ref[...]
    lane = lax.broadcasted_iota(I32, (tm, LANES), 1)
    is_g = (lane >= GROUP_LANE0) & (lane < GROUP_LANE0 + N_GROUPS)
    gmax = jnp.max(jnp.where(is_g, logits, -jnp.inf), axis=-1, keepdims=True)
    gsel = jnp.min(jnp.where(is_g & (logits == gmax), lane - GROUP_LANE0, N_GROUPS), axis=-1, keepdims=True)
    pg_sel = 1.0 / jnp.sum(jnp.where(is_g, jnp.exp(logits - gmax), 0.0), axis=-1, keepdims=True)
    in_grp = (lane < N_EXPERTS) & (lane // EXP_PER_GROUP == gsel)
    emax = jnp.max(jnp.where(in_grp, logits, -jnp.inf), axis=-1, keepdims=True)
    ee = jnp.where(in_grp, jnp.exp(logits - emax), 0.0)
    pe = jnp.where(in_grp, ee / jnp.sum(ee, axis=-1, keepdims=True), -1.0)
    p1 = jnp.max(pe, axis=-1, keepdims=True)
    i1 = jnp.min(jnp.where(pe == p1, lane, LANES), axis=-1, keepdims=True)
    pe2 = jnp.where(lane == i1, -1.0, pe)
    p2 = jnp.max(pe2, axis=-1, keepdims=True)
    i2 = jnp.min(jnp.where(pe2 == p2, lane, LANES), axis=-1, keepdims=True)
    w1 = pg_sel * p1 / (p1 + p2)
    w2 = pg_sel * p2 / (p1 + p2)

    hot1 = lane == i1
    hot2 = lane == i2
    ind = jnp.where(hot1 | hot2, 1.0, 0.0)
    tri = (lax.broadcasted_iota(I32, (tm, tm), 1) < lax.broadcasted_iota(I32, (tm, tm), 0)).astype(BF16)
    before = _dot(tri, ind.astype(BF16))
    k1 = jnp.sum(jnp.where(hot1, before, 0.0), axis=-1, keepdims=True)
    k2 = jnp.sum(jnp.where(hot2, before, 0.0), axis=-1, keepdims=True)
    rec = jnp.zeros((tm, LANES), F32)
    for slot, val in ((R_E1, i1.astype(F32)), (R_E2, i2.astype(F32)), (R_W1, w1), (R_W2, w2), (R_K1, k1), (R_K2, k2)):
        rec = jnp.where(lane == slot, val, rec)
    rt_o[...] = rec
    cnt_o[0] = jnp.broadcast_to(jnp.sum(ind, axis=0, keepdims=True), (SUBLANES, LANES))


def _merge_router(x2, attn, cact, mg, wa, wc, wo, fg, wr, br):
    t = x2.shape[0]
    tm = TR
    row = lambda n: pl.BlockSpec((tm, n), lambda i: (i, 0))
    full = lambda a: pl.BlockSpec(a.shape, lambda i: (0,) * a.ndim)
    return pl.pallas_call(
        _merge_router_kernel,
        grid=(t // tm,),
        in_specs=[row(D_MODEL), row(Q_COLS), row(CONV_CH), row(2 * D_MODEL), full(wa), full(wc), full(wo),
                  full(fg), full(wr), full(br)],
        out_specs=[row(D_MODEL), row(D_MODEL), row(LANES), pl.BlockSpec((1, SUBLANES, LANES), lambda i: (i, 0, 0))],
        out_shape=[jax.ShapeDtypeStruct((t, D_MODEL), F32), jax.ShapeDtypeStruct((t, D_MODEL), F32),
                   jax.ShapeDtypeStruct((t, LANES), F32), jax.ShapeDtypeStruct((t // tm, SUBLANES, LANES), F32)],
        compiler_params=_cparams(("parallel",)),
        name="merge_router",
    )(x2, attn, cact, mg, wa, wc, wo, fg, wr, br)


def _start_chunks(dst8_ref, tile, n_chunks, make_copy):
    def issue(c, carry):
        far = pl.multiple_of(dst8_ref[tile * NCH + c] * CHUNK, CHUNK)
        make_copy(pl.multiple_of(c * CHUNK, CHUNK), far).start()
        return carry

    lax.fori_loop(0, n_chunks, issue, 0)


def _wait_chunks(n_chunks, make_copy):
    def wait_one(c, carry):
        make_copy(0, 0).wait()
        return carry

    lax.fori_loop(0, n_chunks, wait_one, 0)


def _dispatch_kernel(dst8_ref, nct_ref, pad8_ref, npad_ref, nused_ref, h_ref, rt_ref, lo_ref, xg_ref, slot_o, xs,
                     zbuf, sem, zsem):
    tr = h_ref.shape[0]

    @pl.when(pl.program_id(0) == 0)
    def _():
        zbuf[...] = jnp.zeros(zbuf.shape, F32)

        def zero_chunk(far):
            return pltpu.make_async_copy(zbuf.at[pl.ds(0, CHUNK), :], xg_ref.at[pl.ds(far, CHUNK), :], zsem)

        def zero_block(blk):
            return pltpu.make_async_copy(zbuf, xg_ref.at[pl.ds(pl.multiple_of(blk * BM, BM), BM), :], sem.at[0])

        def per_expert(e, carry, wait):
            def one(c, inner):
                if wait:
                    zero_chunk(0).wait()
                else:
                    zero_chunk(pl.multiple_of((pad8_ref[e] + c) * CHUNK, CHUNK)).start()
                return inner
            return lax.fori_loop(0, npad_ref[e], one, carry)

        def tail(blk, carry, wait):
            if wait:
                zero_block(0).wait()
            else:
                zero_block(blk).start()
            return carry

        n_blocks = xg_ref.shape[0] // BM
        lax.fori_loop(0, N_EXPERTS, functools.partial(per_expert, wait=False), 0)
        lax.fori_loop(nused_ref[0], n_blocks, functools.partial(tail, wait=False), 0)
        lax.fori_loop(0, N_EXPERTS, functools.partial(per_expert, wait=True), 0)
        lax.fori_loop(nused_ref[0], n_blocks, functools.partial(tail, wait=True), 0)

    rt = rt_ref[...]
    lane = lax.broadcasted_iota(I32, (tr, LANES), 1)
    lane_f = lane.astype(F32)
    lo_row = lo_ref[0, 0:1, :]
    slots = []
    for e_lane, k_lane in ((R_E1, R_K1), (R_E2, R_K2)):
        hot = lane_f == rt[:, e_lane:e_lane + 1]
        slots.append(jnp.sum(jnp.where(hot, lo_row, 0.0), axis=-1, keepdims=True) + rt[:, k_lane:k_lane + 1])
    rec = jnp.where(lane == 0, slots[0], jnp.where(lane == 1, slots[1], 0.0))
    slot_o[...] = rec
    rec_t = jnp.concatenate([rec[c * LANES:(c + 1) * LANES].T for c in range(tr // LANES)], axis=1)
    row_id = lax.broadcasted_iota(I32, (R_LOCAL, tr), 0).astype(F32)
    pick = (row_id == rec_t[0:1, :]) | (row_id == rec_t[1:2, :])
    staged = _dot(jnp.where(pick, 1.0, 0.0).astype(BF16), h_ref[...].astype(BF16))

    i = pl.program_id(0)
    last = pl.num_programs(0) - 1
    slot = i % 2

    def copies_from(s):
        return lambda near, far: pltpu.make_async_copy(
            xs.at[s, pl.ds(near, CHUNK), :], xg_ref.at[pl.ds(far, CHUNK), :], sem.at[s])

    @pl.when(i >= 2)
    def _():
        _wait_chunks(nct_ref[i - 2], copies_from(slot))

    xs[slot] = staged
    _start_chunks(dst8_ref, i, nct_ref[i], copies_from(slot))

    @pl.when(i == last)
    def _():
        @pl.when(i >= 1)
        def _():
            _wait_chunks(nct_ref[i - 1], copies_from(1 - slot))
        _wait_chunks(nct_ref[i], copies_from(slot))


def _dispatch(dst8, nct, pad8, npad, nused, h2, route, lo_rows, p_rows):
    t = h2.shape[0]
    return pl.pallas_call(
        _dispatch_kernel,
        grid_spec=pltpu.PrefetchScalarGridSpec(
            num_scalar_prefetch=5,
            grid=(t // TR,),
            in_specs=[pl.BlockSpec((TR, D_MODEL), lambda i, *_: (i, 0)),
                      pl.BlockSpec((TR, LANES), lambda i, *_: (i, 0)),
                      pl.BlockSpec((1, SUBLANES, LANES), lambda i, *_: (i, 0, 0))],
            out_specs=[pl.BlockSpec(memory_space=pl.ANY),
                       pl.BlockSpec((TR, LANES), lambda i, *_: (i, 0))],
            scratch_shapes=[pltpu.VMEM((2, R_LOCAL, D_MODEL), F32), pltpu.VMEM((BM, D_MODEL), F32),
                            pltpu.SemaphoreType.DMA((2,)), pltpu.SemaphoreType.DMA(())],
        ),
        out_shape=[jax.ShapeDtypeStruct((p_rows, D_MODEL), F32), jax.ShapeDtypeStruct((t, LANES), F32)],
        compiler_params=_cparams(("arbitrary",)),
        name="moe_dispatch",
    )(dst8, nct, pad8, npad, nused, h2, route, lo_rows)


def _expert_kernel(blk_e_ref, nused_ref, x_ref, wg_ref, wu_ref, wd_ref, y_ref, wg_b, wu_b, wd_b):
    i = pl.program_id(0)
    used = i < nused_ref[0]

    @pl.when((i == 0) | (blk_e_ref[i] != blk_e_ref[jnp.maximum(i - 1, 0)]))
    def _():
        wg_b[...] = wg_ref[0].astype(BF16)
        wu_b[...] = wu_ref[0].astype(BF16)
        wd_b[...] = wd_ref[0].astype(BF16)

    @pl.when(used)
    def _():
        xb = x_ref[...].astype(BF16)
        a = _dot(xb, wg_b[...])
        u = _dot(xb, wu_b[...])
        y_ref[...] = _dot((a * jax.nn.sigmoid(a) * u).astype(BF16), wd_b[...])

    @pl.when(jnp.logical_not(used))
    def _():
        y_ref[...] = jnp.zeros(y_ref.shape, F32)


def _experts(blk_e, nused, xg, wg, wu, wd):
    p_rows = xg.shape[0]
    x_blk = lambda i, be, nu: (jnp.minimum(i, nu[0] - 1), 0)
    return pl.pallas_call(
        _expert_kernel,
        grid_spec=pltpu.PrefetchScalarGridSpec(
            num_scalar_prefetch=2,
            grid=(p_rows // BM,),
            in_specs=[pl.BlockSpec((BM, D_MODEL), x_blk),
                      pl.BlockSpec((1, D_MODEL, EXPERT_FF), lambda i, be, nu: (be[i], 0, 0)),
                      pl.BlockSpec((1, D_MODEL, EXPERT_FF), lambda i, be, nu: (be[i], 0, 0)),
                      pl.BlockSpec((1, EXPERT_FF, D_MODEL), lambda i, be, nu: (be[i], 0, 0))],
            out_specs=pl.BlockSpec((BM, D_MODEL), lambda i, be, nu: (i, 0)),
            scratch_shapes=[pltpu.VMEM((D_MODEL, EXPERT_FF), BF16), pltpu.VMEM((D_MODEL, EXPERT_FF), BF16),
                            pltpu.VMEM((EXPERT_FF, D_MODEL), BF16)],
        ),
        out_shape=jax.ShapeDtypeStruct((p_rows, D_MODEL), F32),
        compiler_params=_cparams(("arbitrary",)),
        name="moe_experts",
    )(blk_e, nused, xg, wg, wu, wd)


def _combine_ple_kernel(dst8_ref, nct_ref, x1_ref, rt_ref, slot_ref, p_ref, pg_ref, wpg_ref, bpg_ref, wpp_ref,
                        yg_ref, o_ref, ys, sem):
    tr = x1_ref.shape[0]
    i = pl.program_id(0)
    slot = i % 2

    def copies_into(s):
        return lambda near, far: pltpu.make_async_copy(
            yg_ref.at[pl.ds(far, CHUNK), :], ys.at[s, pl.ds(near, CHUNK), :], sem.at[s])

    def fetch(tile, s):
        _start_chunks(dst8_ref, tile, nct_ref[tile], copies_into(s))

        def clear(c, carry):
            ys[s, pl.ds(pl.multiple_of(c * CHUNK, CHUNK), CHUNK), :] = jnp.zeros((CHUNK, D_MODEL), F32)
            return carry

        lax.fori_loop(nct_ref[tile], NCH, clear, 0)

    @pl.when(i == 0)
    def _():
        fetch(0, 0)

    @pl.when(i + 1 < pl.num_programs(0))
    def _():
        fetch(i + 1, 1 - slot)

    _wait_chunks(nct_ref[i], copies_into(slot))

    rt = rt_ref[...]
    sl = slot_ref[...]
    col_id = lax.broadcasted_iota(I32, (tr, R_LOCAL), 1).astype(F32)
    yb = ys[slot].astype(BF16)
    gather_w = jnp.where(col_id == sl[:, 0:1], rt[:, R_W1:R_W1 + 1],
                         jnp.where(col_id == sl[:, 1:2], rt[:, R_W2:R_W2 + 1], 0.0)).astype(BF16)
    x2 = x1_ref[...] + _dot(gather_w, yb)
    h3 = (x2 * lax.rsqrt(jnp.mean(x2 * x2, axis=-1, keepdims=True) + EPS) * pg_ref[...]).astype(BF16)
    gate = jax.nn.sigmoid(_dot(h3, wpg_ref[...]) + bpg_ref[...])
    o_ref[...] = x2 + gate * _dot(p_ref[...].astype(BF16), wpp_ref[...])


def _combine_ple(dst8, nct, x1, route, slots, p2, pg, wpg, bpg, wpp, yg):
    t = x1.shape[0]
    row = lambda n: pl.BlockSpec((TR, n), lambda i, d, c: (i, 0))
    full = lambda a: pl.BlockSpec(a.shape, lambda i, d, c: (0,) * a.ndim)
    return pl.pallas_call(
        _combine_ple_kernel,
        grid_spec=pltpu.PrefetchScalarGridSpec(
            num_scalar_prefetch=2,
            grid=(t // TR,),
            in_specs=[row(D_MODEL), row(LANES), row(LANES), row(PLE_DIM), full(pg), full(wpg), full(bpg),
                      full(wpp), pl.BlockSpec(memory_space=pl.ANY)],
            out_specs=row(D_MODEL),
            scratch_shapes=[pltpu.VMEM((2, R_LOCAL, D_MODEL), F32), pltpu.SemaphoreType.DMA((2,))],
        ),
        out_shape=jax.ShapeDtypeStruct((t, D_MODEL), F32),
        compiler_params=_cparams(("arbitrary",)),
        name="moe_combine_ple",
    )(dst8, nct, x1, route, slots, p2, pg, wpg, bpg, wpp, yg)


def _seg_sum_matrix():
    seg = np.arange(LANES) // HEAD_DIM
    return jnp.asarray(seg[:, None] == seg[None, :], BF16)


def _overlap_t(nc_rows, n_blk):
    nc = np.arange(nc_rows)[None, :] * CMP_STRIDE
    sb = np.arange(n_blk)[:, None] * SEL_BLOCK
    ov = (nc < sb + SEL_BLOCK) & (nc + CMP_BLOCK > sb) & (np.arange(nc_rows)[None, :] < nc_rows - 1)
    return jnp.asarray(ov, BF16)


def _compress_weights(pos, w1, w2):
    half = CMP_BLOCK // 2
    pe = jnp.broadcast_to(pos[:, None, :], (CMP_BLOCK, N_KV, HEAD_DIM)).reshape(2, half * N_KV * HEAD_DIM)
    w1r = w1.reshape(2, half, HEAD_DIM, CMP_HIDDEN)
    eye = jnp.eye(N_KV, dtype=w1.dtype)
    wx = jnp.einsum('hldj,gk->hlgdkj', w1r, eye).reshape(2, half * N_KV * HEAD_DIM, N_KV * CMP_HIDDEN)
    w2x = jnp.einsum('jd,gk->gjkd', w2, eye).reshape(N_KV * CMP_HIDDEN, N_KV * HEAD_DIM)
    return pe, wx[0].astype(BF16), wx[1].astype(BF16), w2x.astype(BF16)


def _layer(x, p_i, prm):
    b, s, _ = x.shape
    t = b * s
    x2 = x.reshape(t, D_MODEL)
    bd = _seg_sum_matrix()

    w = prm["w_in"]
    kv = lambda j: w[:, OFF_KV + j * KV_COLS:OFF_KV + (j + 1) * KV_COLS]
    w_nat = jnp.concatenate([kv(0), kv(1), kv(2), kv(4), w[:, OFF_CONV:]], axis=1).astype(BF16)
    w_t = jnp.concatenate([w[:, :Q_COLS], kv(3), kv(5),
                           jnp.pad(w[:, OFF_GATE:OFF_CONV], ((0, 0), (0, LANES - 3 * N_HEADS)))], axis=1).T.astype(BF16)
    qg = jnp.broadcast_to(prm["q_norm_g"][:, None], (HEAD_DIM, TM_PROJ))
    kg = jnp.tile(prm["k_norm_g"], (1, LANES // HEAD_DIM))
    qt, kc_raw, vc_raw, ks, vst, kw, vwt, gtt, glu, mg = _in_proj(
        x2, prm["attn_norm_g"][None, :], w_nat, w_t, qg, kg, bd, prm["merge_b"][None, :], s)

    nrow = s // CMP_STRIDE
    cw = [_compress_weights(prm["cmp_pos"][j], prm["cmp_w1"][j], prm["cmp_w2"][j]) for j in range(2)]
    pe_x, wa, wb = (jnp.stack([cw[0][j], cw[1][j]]) for j in range(3))
    kc, vct = _compress(kc_raw.reshape(b, s, LANES), vc_raw.reshape(b, s, LANES), pe_x, wa, wb, cw[0][3],
                        cw[1][3].T, bd, kg)

    attn = _attention(qt, gtt, kc, vct, ks.reshape(b, s, 4 * LANES), vst, kw.reshape(b, s, 2 * LANES), vwt,
                      _overlap_t(nrow, s // SEL_BLOCK), b, s)

    cact = _conformer_conv(glu, prm["conv_w"], prm["conv_b"][None, :], prm["conv_norm_g"][None, :],
                           prm["conv_norm_b"][None, :], b, s)

    wr_full = jnp.zeros((D_MODEL, LANES), F32)
    wr_full = wr_full.at[:, :N_EXPERTS].set(prm["w_router_expert"])
    wr_full = wr_full.at[:, GROUP_LANE0:GROUP_LANE0 + N_GROUPS].set(prm["w_router_group"])
    wr_hi = wr_full.astype(BF16)
    wr = jnp.stack([wr_hi, (wr_full - wr_hi.astype(F32)).astype(BF16)])
    br = jnp.zeros((1, LANES), F32)
    br = br.at[0, :N_EXPERTS].set(prm["b_router_expert"])
    br = br.at[0, GROUP_LANE0:GROUP_LANE0 + N_GROUPS].set(prm["b_router_group"])
    x1, h2, route, cnt = _merge_router(
        x2, attn, cact, mg, prm["w_attn_out"].astype(BF16), prm["w_conv_out"].astype(BF16),
        prm["w_out"].astype(BF16), prm["ffn_norm_g"][None, :], wr, br)

    n_tiles = t // TR
    counts = cnt[:, 0, :N_EXPERTS].astype(I32)
    seg = (counts + CHUNK - 1) // CHUNK * CHUNK
    local_off = jnp.cumsum(seg, axis=1) - seg
    nct = (jnp.sum(seg, axis=1) // CHUNK).astype(I32)
    total = jnp.sum(seg, axis=0)
    padded = (total + BM - 1) // BM * BM
    pends = jnp.cumsum(padded)
    far_off = (pends - padded)[None, :] + jnp.cumsum(seg, axis=0) - seg
    chunk = jnp.arange(NCH, dtype=I32)[None, :, None]
    lo8 = (local_off // CHUNK)[:, None, :]
    in_seg = (chunk >= lo8) & (chunk < ((local_off + seg) // CHUNK)[:, None, :])
    dst8 = (jnp.sum(jnp.where(in_seg, (far_off // CHUNK)[:, None, :] - lo8, 0), axis=-1)
            + chunk[:, :, 0]).astype(I32).reshape(n_tiles * NCH)
    lo_rows = jnp.broadcast_to(jnp.pad(local_off.astype(F32), ((0, 0), (0, LANES - N_EXPERTS)))[:, None, :],
                               (n_tiles, SUBLANES, LANES))
    n_blk = -(-(2 * t + n_tiles * N_EXPERTS * (CHUNK - 1) + N_EXPERTS * (BM - 1)) // BM)
    blk_e = jnp.minimum(jnp.sum(jnp.arange(n_blk, dtype=I32)[:, None] * BM >= pends[None, :], axis=1),
                        N_EXPERTS - 1).astype(I32)
    nused = (pends[-1:] // BM).astype(I32)
    pad8 = ((pends - padded + total) // CHUNK).astype(I32)
    npad = ((padded - total) // CHUNK).astype(I32)

    xg, slots = _dispatch(dst8, nct, pad8, npad, nused, h2, route, lo_rows, n_blk * BM)
    yg = _experts(blk_e, nused, xg, prm["w_exp_gate"], prm["w_exp_up"], prm["w_exp_down"])
    out = _combine_ple(dst8, nct, x1, route, slots, p_i.reshape(t, PLE_DIM), prm["ple_norm_g"][None, :],
                       prm["w_ple_gate"].astype(BF16), prm["b_ple_gate"][None, :], prm["w_ple_proj"].astype(BF16), yg)
    return out.reshape(b, s, D_MODEL)


def kernel(x, p, attn_norm_g, w_in, q_norm_g, k_norm_g, cmp_pos, cmp_w1, cmp_w2, w_attn_out, conv_w, conv_b, conv_norm_g, conv_norm_b, w_conv_out, merge_b, w_out, ffn_norm_g, w_router_group, b_router_group, w_router_expert, b_router_expert, w_exp_gate, w_exp_up, w_exp_down, ple_norm_g, w_ple_gate, b_ple_gate, w_ple_proj):
    stacked = dict(attn_norm_g=attn_norm_g, w_in=w_in, q_norm_g=q_norm_g, k_norm_g=k_norm_g, cmp_pos=cmp_pos,
                   cmp_w1=cmp_w1, cmp_w2=cmp_w2, w_attn_out=w_attn_out, conv_w=conv_w, conv_b=conv_b,
                   conv_norm_g=conv_norm_g, conv_norm_b=conv_norm_b, w_conv_out=w_conv_out, merge_b=merge_b,
                   w_out=w_out, ffn_norm_g=ffn_norm_g, w_router_group=w_router_group,
                   b_router_group=b_router_group, w_router_expert=w_router_expert,
                   b_router_expert=b_router_expert, w_exp_gate=w_exp_gate, w_exp_up=w_exp_up,
                   w_exp_down=w_exp_down, ple_norm_g=ple_norm_g, w_ple_gate=w_ple_gate, b_ple_gate=b_ple_gate,
                   w_ple_proj=w_ple_proj)
    for i in range(w_in.shape[0]):
        x = _layer(x, p[i], {k: v[i] for k, v in stacked.items()})
    return x
```

```python
import functools

import numpy as np
import jax
import jax.numpy as jnp
from jax import lax
from jax.experimental import pallas as pl
from jax.experimental.pallas import tpu as pltpu

F32 = jnp.float32
BF16 = jnp.bfloat16
I32 = jnp.int32

D_MODEL = 1024
PLE_DIM = 256
N_HEADS = 8
N_KV = 2
HPG = N_HEADS // N_KV
HEAD_DIM = 64
CMP_BLOCK = 32
CMP_STRIDE = 16
CMP_HIDDEN = 128
SEL_BLOCK = 64
N_SEL = 8
WINDOW = 512
CONV_CH = 512
CONV_WIDTH = 31
N_GROUPS = 4
EXP_PER_GROUP = 8
N_EXPERTS = N_GROUPS * EXP_PER_GROUP
EXPERT_FF = 512
EPS = 1e-6
NEG = -1e30
BIG = 1e4
Q_COLS = N_HEADS * HEAD_DIM
KV_COLS = N_KV * HEAD_DIM
OFF_KV = Q_COLS
OFF_GATE = OFF_KV + 6 * KV_COLS
OFF_CONV = OFF_GATE + 3 * N_HEADS
N_FORCED = 3

LANES = 128
SUBLANES = 8
VMEM_LIMIT = 56 * 1024 * 1024

TM_PROJ = 256
QB = 256
KT = 256
TS_CONV = 512
CONV_ROWS = 64
HALO = 32
TR = 512
BM = 512
CHUNK = SUBLANES
R_LOCAL = 2 * TR + N_EXPERTS * CHUNK
NCH = R_LOCAL // CHUNK

N_KC = 0
N_VC = N_KC + LANES
N_KS = N_VC + LANES
N_KW = N_KS + LANES
N_CONV = N_KW + LANES
N_MERGE = N_CONV + 2 * CONV_CH
T_Q = 0
T_VS = T_Q + Q_COLS
T_VW = T_VS + LANES
T_GATE = T_VW + LANES

POS_HI_LANE = HEAD_DIM
POS_LO_LANE = HEAD_DIM + 1
UNSELECTED = -(2.0 ** 40)

R_E1, R_E2, R_W1, R_W2, R_K1, R_K2 = 0, 1, 2, 3, 4, 5
GROUP_LANE0 = N_EXPERTS


def _cparams(sem, vmem=VMEM_LIMIT):
    return pltpu.CompilerParams(dimension_semantics=sem, vmem_limit_bytes=vmem)


def _dot(a, b):
    return jnp.dot(a, b, preferred_element_type=F32)


def _dot_nt(a, b):
    return lax.dot_general(a, b, (((1,), (1,)), ((), ())), preferred_element_type=F32)


def _split(x):
    hi = x.astype(BF16)
    lo = (x - hi.astype(F32)).astype(BF16)
    return hi, lo


def _seg_rmsnorm(z, bd):
    hi, lo = _split(z * z)
    ss = _dot(hi, bd) + _dot(lo, bd)
    return z * lax.rsqrt(ss * (1.0 / HEAD_DIM) + EPS)


def _with_pos_lanes(kn, pos):
    lane = lax.broadcasted_iota(I32, kn.shape, 1)
    hi = pos // SEL_BLOCK * SEL_BLOCK
    pos_lanes = jnp.where(lane == POS_HI_LANE, hi.astype(F32),
                          jnp.where(lane == POS_LO_LANE, (pos - hi).astype(F32), 0.0))
    lo_half = lane < HEAD_DIM
    return (jnp.where(lo_half, kn, pos_lanes).astype(BF16),
            jnp.where(lo_half, pltpu.roll(kn, HEAD_DIM, 1), pos_lanes).astype(BF16))


def _in_proj_kernel(x_ref, g_ref, w_ref, wt_ref, qg_ref, kg_ref, bd_ref, mb_ref,
                    qt_o, kc_o, vc_o, ks_o, vst_o, kw_o, vwt_o, gtt_o, glu_o, mg_o, *, seq_len):
    tm = x_ref.shape[0]
    x = x_ref[...]
    h = (x * lax.rsqrt(jnp.mean(x * x, axis=-1, keepdims=True) + EPS) * g_ref[...]).astype(BF16)
    bd = bd_ref[...]
    lane = lax.broadcasted_iota(I32, (tm, LANES), 1)

    def proj(a, b):
        return _dot(h, w_ref[:, a:b])

    zt = _dot_nt(wt_ref[...], h)
    for hd in range(N_HEADS):
        zh = zt[T_Q + hd * HEAD_DIM:T_Q + (hd + 1) * HEAD_DIM, :]
        ms = jnp.sum(zh * zh, axis=0, keepdims=True) * (1.0 / HEAD_DIM)
        qt_o[hd * HEAD_DIM:(hd + 1) * HEAD_DIM, :] = (
            zh * lax.rsqrt(ms + EPS) * qg_ref[...] * (HEAD_DIM ** -0.5)).astype(BF16)
    ones = jnp.ones((HEAD_DIM, tm), F32)
    for off, v_o in ((T_VS, vst_o), (T_VW, vwt_o)):
        v_o[0] = jnp.concatenate([zt[off:off + HEAD_DIM, :], ones, zt[off + HEAD_DIM:off + LANES, :], ones],
                                 axis=0).astype(BF16)
    gtt_o[...] = jax.nn.sigmoid(zt[T_GATE:T_GATE + LANES, :])

    kc_o[...] = proj(N_KC, N_KC + LANES)
    vc_o[...] = proj(N_VC, N_VC + LANES)

    pos = (pl.program_id(0) * tm) % seq_len + lax.broadcasted_iota(I32, (tm, LANES), 0)
    onehot = jnp.where(lane == pos // SEL_BLOCK, 1.0, 0.0).astype(BF16)

    def key_tiles(zk, gain):
        return _with_pos_lanes(_seg_rmsnorm(zk, bd) * gain, pos)

    k0, k1 = key_tiles(proj(N_KS, N_KS + LANES), kg_ref[1:2, :])
    ks_o[...] = jnp.concatenate([k0, onehot, k1, onehot], axis=1)
    k0, k1 = key_tiles(proj(N_KW, N_KW + LANES), kg_ref[2:3, :])
    kw_o[...] = jnp.concatenate([k0, k1], axis=1)
    za = proj(N_CONV, N_CONV + CONV_CH)
    zg = proj(N_CONV + CONV_CH, N_CONV + 2 * CONV_CH)
    glu_o[...] = za * jax.nn.sigmoid(zg)
    for c in range(2):
        zm = proj(N_MERGE + c * D_MODEL, N_MERGE + (c + 1) * D_MODEL)
        mg_o[:, c * D_MODEL:(c + 1) * D_MODEL] = jax.nn.sigmoid(
            zm + mb_ref[:, c * D_MODEL:(c + 1) * D_MODEL]).astype(BF16)


def _in_proj(x2, norm_g, w_nat, w_t, qg, kg, bd, mb, seq_len):
    t = x2.shape[0]
    tm = TM_PROJ
    assert tm == KT and seq_len % tm == 0 and seq_len // SEL_BLOCK <= HEAD_DIM
    row = lambda n: pl.BlockSpec((tm, n), lambda i: (i, 0))
    col = lambda n: pl.BlockSpec((n, tm), lambda i: (0, i))
    vt = pl.BlockSpec((1, 2 * LANES, tm), lambda i: (i, 0, 0))
    full = lambda a: pl.BlockSpec(a.shape, lambda i: (0,) * a.ndim)
    sds = jax.ShapeDtypeStruct
    return pl.pallas_call(
        functools.partial(_in_proj_kernel, seq_len=seq_len),
        grid=(t // tm,),
        in_specs=[row(D_MODEL), full(norm_g), full(w_nat), full(w_t), full(qg), full(kg), full(bd), full(mb)],
        out_specs=[col(Q_COLS), row(LANES), row(LANES), row(4 * LANES), vt, row(2 * LANES), vt, col(LANES),
                   row(CONV_CH), row(2 * D_MODEL)],
        out_shape=[sds((Q_COLS, t), BF16), sds((t, LANES), F32), sds((t, LANES), F32), sds((t, 4 * LANES), BF16),
                   sds((t // tm, 2 * LANES, tm), BF16), sds((t, 2 * LANES), BF16), sds((t // tm, 2 * LANES, tm), BF16),
                   sds((LANES, t), F32), sds((t, CONV_CH), F32), sds((t, 2 * D_MODEL), BF16)],
        compiler_params=_cparams(("parallel",)),
        name="in_proj",
    )(x2, norm_g, w_nat, w_t, qg, kg, bd, mb)


def _compress_kernel(ck_ref, cv_ref, pe_ref, wa_ref, wb_ref, w2k_ref, w2vt_ref, bd_ref, kg_ref, kc_o, vct_o):
    nrow = ck_ref.shape[1] // CMP_STRIDE

    def hidden(idx, c_ref):
        ya = jnp.zeros((nrow, N_KV * CMP_HIDDEN), F32)
        yb = jnp.zeros((nrow, N_KV * CMP_HIDDEN), F32)
        for l in range(CMP_STRIDE):
            tok = c_ref[0, pl.ds(l, nrow, stride=CMP_STRIDE), :]
            cols = slice(l * LANES, (l + 1) * LANES)
            ya = ya + _dot((tok + pe_ref[idx, 0:1, cols]).astype(BF16), wa_ref[idx, cols, :])
            yb = yb + _dot((tok + pe_ref[idx, 1:2, cols]).astype(BF16), wb_ref[idx, cols, :])
        return jax.nn.gelu(ya + pltpu.roll(yb, nrow - 1, 0)).astype(BF16)

    kn = _seg_rmsnorm(_dot(hidden(0, ck_ref), w2k_ref[...]), bd_ref[...]) * kg_ref[0:1, :]
    c_end = lax.broadcasted_iota(I32, (nrow, LANES), 0) * CMP_STRIDE + (CMP_BLOCK - 1)
    kc_o[0] = jnp.concatenate(_with_pos_lanes(kn, c_end), axis=1)
    vt = _dot_nt(w2vt_ref[...], hidden(1, cv_ref))
    ones = jnp.ones((HEAD_DIM, nrow), F32)
    vct_o[0] = jnp.concatenate([vt[:HEAD_DIM], ones, vt[HEAD_DIM:], ones], axis=0).astype(BF16)


def _compress(kc_raw, vc_raw, pe_x, wa, wb, w2k, w2vt, bd, kg):
    b, s, width = kc_raw.shape
    nrow = s // CMP_STRIDE
    batch = lambda r, n: pl.BlockSpec((1, r, n), lambda i: (i, 0, 0))
    full = lambda a: pl.BlockSpec(a.shape, lambda i: (0,) * a.ndim)
    return pl.pallas_call(
        _compress_kernel,
        grid=(b,),
        in_specs=[batch(s, width), batch(s, width), full(pe_x), full(wa), full(wb), full(w2k), full(w2vt),
                  full(bd), full(kg)],
        out_specs=[batch(nrow, 2 * LANES), batch(2 * LANES, nrow)],
        out_shape=[jax.ShapeDtypeStruct((b, nrow, 2 * LANES), BF16),
                   jax.ShapeDtypeStruct((b, 2 * LANES, nrow), BF16)],
        compiler_params=_cparams(("parallel",)),
        name="compress",
    )(kc_raw, vc_raw, pe_x, wa, wb, w2k, w2vt, bd, kg)


def _halve_rows(x, op):
    rows = x.shape[0]
    while rows > SUBLANES:
        rows //= 2
        x = op(x[:rows], x[rows:])
    return x


def _attn_kernel(qt_ref, gtt_ref, kc_ref, vct_ref, ks_ref, vst_ref, kw_ref, vwt_ref, ovt_ref, o_ref,
                 tiles_ref, sa_ref, sb_ref, m_ref, acc_ref):
    i = pl.program_id(1)
    q0 = i * QB
    nc_rows = kc_ref.shape[1]
    n_blk = ovt_ref.shape[0]
    hpc = 2 * LANES // QB
    pw = hpc * QB
    n_pairs = N_HEADS // hpc
    group_of = lambda c: c * hpc // HPG
    rel = ((lax.broadcasted_iota(I32, (KT, pw), 1) & (QB - 1))
           - lax.broadcasted_iota(I32, (KT, pw), 0)).astype(F32)
    slope_rows = lax.broadcasted_iota(I32, (HEAD_DIM, QB), 0) < 2
    gtt = gtt_ref[...]

    pairs = []
    for pr in range(n_pairs):
        q_pos = []
        for hd in range(pr * hpc, (pr + 1) * hpc):
            slope = jnp.where(slope_rows, 2.0 ** -(hd + 1), 0.0).astype(BF16)
            q_pos.append(jnp.concatenate([qt_ref[hd * HEAD_DIM:(hd + 1) * HEAD_DIM, :], slope], axis=0))
        pairs.append(jnp.concatenate(q_pos, axis=1))

    def update(s, vt, m, acc):
        m_new = jnp.maximum(m, jnp.max(_halve_rows(s, jnp.maximum), axis=0, keepdims=True))
        p = jnp.exp(s - m_new).astype(BF16)
        return m_new, jnp.exp(m - m_new) * acc + _dot(vt, p)

    def scores(j, q_list, k_ref, k_width):
        k0 = pl.multiple_of(j * KT, KT)
        return tuple(_dot(k_ref[0, pl.ds(k0, KT), group_of(c) * k_width:(group_of(c) + 1) * k_width], q_list[c])
                     for c in range(n_pairs))

    def consume(j, s_list, state, vt_ref, mask_fn, extra_dist=0.0):
        keep = None if mask_fn is None else mask_fn(rel + ((q0 - j * KT).astype(F32) + extra_dist))
        out = []
        for c in range(n_pairs):
            g = group_of(c)
            s = s_list[c] if keep is None else jnp.where(keep, s_list[c], NEG)
            out += update(s, vt_ref[j, g * LANES:(g + 1) * LANES, :], *state[2 * c:2 * c + 2])
        return tuple(out)

    j_diag = (q0 + QB - 1) // KT
    init = (jnp.full((1, pw), NEG, F32), jnp.zeros((LANES, pw), F32)) * n_pairs

    def selected_branch(q_list, n_act):
        def fill(buf, n):
            for c, s in enumerate(scores(tiles_ref[n], q_list, ks_ref, 2 * LANES)):
                buf[c] = s

        def use(buf, n, mask_fn):
            state = []
            for c in range(n_pairs):
                state += [m_ref[c, 0:1, :], acc_ref[c]]
            out = consume(tiles_ref[n], [buf[c] for c in range(n_pairs)], state, vst_ref, mask_fn)
            for c in range(n_pairs):
                m_ref[c, 0:1, :] = out[2 * c]
                acc_ref[c] = out[2 * c + 1]

        for c in range(n_pairs):
            m_ref[c] = jnp.full((SUBLANES, pw), NEG, F32)
            acc_ref[c] = jnp.zeros((LANES, pw), F32)
        fill(sa_ref, 0)

        def two_tiles(h, carry):
            n = 2 * h
            fill(sb_ref, n + 1)
            use(sa_ref, n, None)

            @pl.when(n + 1 < n_act)
            def _():
                fill(sa_ref, n + 2)
                use(sb_ref, n + 1, None)
            return carry

        lax.fori_loop(0, (n_act + 1) // 2, two_tiles, 0)

        @pl.when(n_act % 2 == 1)
        def _():
            for c in range(n_pairs):
                sa_ref[c] = sb_ref[c]

        use(sa_ref, n_act, lambda d: d >= 0.0)
        return [acc_ref[c] for c in range(n_pairs)]

    cidx = lax.broadcasted_iota(I32, (nc_rows, pw), 0)
    c_end = cidx * CMP_STRIDE + (CMP_BLOCK - 1)
    cmask = (c_end <= q0 + (lax.broadcasted_iota(I32, (nc_rows, pw), 1) & (QB - 1))) & (cidx < nc_rows - 1)

    sees_any = jnp.where(q0 + (lax.broadcasted_iota(I32, (1, pw), 1) & (QB - 1)) >= CMP_BLOCK - 1, 1.0, 0.0)
    s_cmp = [_dot(kc_ref[0, :, group_of(c) * LANES:(group_of(c) + 1) * LANES], pairs[c]) for c in range(n_pairs)]
    o_cmp, imp = [], []
    for g in range(N_KV):
        imp_g = jnp.zeros((n_blk, QB), F32)
        for pr in range(g * HPG // hpc, (g + 1) * HPG // hpc):
            s = jnp.where(cmask, s_cmp[pr], NEG)
            e = jnp.exp(s - jnp.max(_halve_rows(s, jnp.maximum), axis=0, keepdims=True)).astype(BF16)
            acc = _dot(vct_ref[0, g * LANES:(g + 1) * LANES, :], e)
            inv = sees_any / acc[HEAD_DIM:HEAD_DIM + 1]
            o_cmp.append(acc[:HEAD_DIM] * inv)
            weighted = _dot(ovt_ref[...], e) * inv
            for k in range(hpc):
                imp_g = imp_g + weighted[:, k * QB:(k + 1) * QB]
        imp.append(imp_g)

    blk = lax.broadcasted_iota(I32, (n_blk, N_KV * QB), 0)
    tl = q0 + (lax.broadcasted_iota(I32, (n_blk, N_KV * QB), 1) & (QB - 1))
    valid = blk * SEL_BLOCK <= tl
    cur = tl // SEL_BLOCK
    forced = (blk == 0) | (blk == cur) | (blk == cur - 1)
    sel = forced & valid
    score = jnp.where(valid, jnp.where(sel, -jnp.inf, jnp.concatenate(imp, axis=1)), -BIG)
    for _ in range(min(N_SEL, n_blk) - N_FORCED):
        mx = jnp.max(score, axis=0, keepdims=True)
        first = jnp.min(jnp.where(score == mx, blk, n_blk), axis=0, keepdims=True)
        pick = blk == first
        sel = sel | pick
        score = jnp.where(pick, -jnp.inf, score)
    chosen = sel & valid
    any_sel = jnp.where(chosen[:, :QB] | chosen[:, QB:], 1.0, 0.0)
    unsel = jnp.where(chosen, 0.0, UNSELECTED).astype(BF16)
    bias = jnp.concatenate([unsel, jnp.zeros((2 * HEAD_DIM - n_blk, N_KV * QB), BF16)], axis=0)
    q_sel = []
    for c in range(n_pairs):
        g_bias = bias[:, group_of(c) * QB:(group_of(c) + 1) * QB]
        q_sel.append(jnp.concatenate([pairs[c], jnp.concatenate([g_bias] * hpc, axis=1)], axis=0))

    per_blk = jnp.sum(any_sel, axis=1, keepdims=True)
    n_act = jnp.int32(0)
    for jt in range(n_blk * SEL_BLOCK // KT):
        blocks = slice(jt * (KT // SEL_BLOCK), (jt + 1) * (KT // SEL_BLOCK))
        tiles_ref[n_act] = jnp.int32(jt)
        n_act = n_act + ((jnp.sum(per_blk[blocks]) > 0.0) & (jt < j_diag)).astype(I32)
    tiles_ref[n_act] = j_diag

    assert WINDOW == 2 * KT and KT % QB == 0
    win_out = init
    for w in range(WINDOW // KT + 1):
        jw = j_diag - WINDOW // KT + w
        before_start = jnp.where(jw < 0, float(WINDOW + KT), 0.0)
        jc = jnp.maximum(jw, 0)
        mask_fn = ((lambda d: d < float(WINDOW)), (lambda d, jw=jw: jw >= 0), (lambda d: d >= 0.0))[w]
        win_out = consume(jc, scores(jc, pairs, kw_ref, LANES), win_out, vwt_ref, mask_fn, before_start)

    sel_acc = selected_branch(q_sel, n_act)

    o_sel = [a[:HEAD_DIM] / a[HEAD_DIM:HEAD_DIM + 1] for a in sel_acc]
    o_win = [a[:HEAD_DIM] / a[HEAD_DIM:HEAD_DIM + 1] for a in win_out[1::2]]
    mixed = []
    for hd in range(N_HEADS):
        c, lanes = hd // hpc, slice((hd % hpc) * QB, (hd % hpc + 1) * QB)
        mixed.append(gtt[3 * hd:3 * hd + 1, :] * o_cmp[c][:, lanes] + gtt[3 * hd + 1:3 * hd + 2, :] * o_sel[c][:, lanes]
                     + gtt[3 * hd + 2:3 * hd + 3, :] * o_win[c][:, lanes])
    for pr in range(N_HEADS // 2):
        two = jnp.concatenate(mixed[2 * pr:2 * pr + 2], axis=0)
        o_ref[:, pr * LANES:(pr + 1) * LANES] = jnp.concatenate(
            [two[:, k * LANES:(k + 1) * LANES].T for k in range(QB // LANES)], axis=0).astype(BF16)


def _attention(qt, gtt, kc, vct, ks, vst, kw, vwt, ovt, b, s):
    n_q = s // QB
    n_tiles = N_HEADS * QB // (2 * LANES)
    per_q = lambda n: pl.BlockSpec((n, QB), lambda bi, i: (0, bi * n_q + i))
    per_b = lambda a: pl.BlockSpec((1,) + a.shape[1:], lambda bi, i: (bi, 0, 0))
    vt = pl.BlockSpec((s // KT, 2 * LANES, KT), lambda bi, i: (bi, 0, 0))
    return pl.pallas_call(
        _attn_kernel,
        grid=(b, n_q),
        in_specs=[per_q(Q_COLS), per_q(LANES), per_b(kc), per_b(vct), per_b(ks), vt, per_b(kw), vt,
                  pl.BlockSpec(ovt.shape, lambda bi, i: (0, 0))],
        out_specs=pl.BlockSpec((QB, Q_COLS), lambda bi, i: (bi * n_q + i, 0)),
        out_shape=jax.ShapeDtypeStruct((b * s, Q_COLS), BF16),
        scratch_shapes=[pltpu.SMEM((s // KT + 1,), I32),
                        pltpu.VMEM((n_tiles, KT, 2 * LANES), F32), pltpu.VMEM((n_tiles, KT, 2 * LANES), F32),
                        pltpu.VMEM((n_tiles, SUBLANES, 2 * LANES), F32),
                        pltpu.VMEM((n_tiles, LANES, 2 * LANES), F32)],
        compiler_params=_cparams(("parallel", "parallel")),
        name="nsa_attention",
    )(qt, gtt, kc, vct, ks, vst, kw, vwt, ovt)


def _conv_kernel(z_ref, w_ref, b_ref, g_ref, beta_ref, o_ref, buf, shifted):
    ts = z_ref.shape[0]
    chunk = CONV_ROWS

    @pl.when(pl.program_id(1) == 0)
    def _():
        buf[0:HALO, :] = jnp.zeros((HALO, CONV_CH), F32)

    buf[HALO:HALO + ts, :] = z_ref[...]
    span = ts + HALO - SUBLANES
    for r in range(1, SUBLANES):
        shifted[r - 1, 0:span, :] = buf[r:r + span, :]

    def tap_rows(k, c):
        start = HALO - (CONV_WIDTH - 1) + k
        r = start % SUBLANES
        src = buf if r == 0 else shifted.at[r - 1]
        return src[start - r + c * chunk:start - r + (c + 1) * chunk, :]

    for c in range(ts // chunk):
        acc = jnp.zeros((chunk, CONV_CH), F32) + b_ref[...]
        for k in range(CONV_WIDTH):
            acc = acc + tap_rows(k, c) * w_ref[k:k + 1, :]
        mu = jnp.mean(acc, axis=-1, keepdims=True)
        d = acc - mu
        var = jnp.mean(d * d, axis=-1, keepdims=True)
        y = d * lax.rsqrt(var + EPS) * g_ref[...] + beta_ref[...]
        o_ref[c * chunk:(c + 1) * chunk, :] = (y * jax.nn.sigmoid(y)).astype(BF16)
    buf[0:HALO, :] = buf[ts:ts + HALO, :]


def _conformer_conv(glu, w, bias, g, beta, b, s):
    ts = min(TS_CONV, s)
    full = lambda a: pl.BlockSpec(a.shape, lambda bi, i: (0, 0))
    row = pl.BlockSpec((ts, CONV_CH), lambda bi, i: (bi * (s // ts) + i, 0))
    return pl.pallas_call(
        _conv_kernel,
        grid=(b, s // ts),
        in_specs=[row, full(w), full(bias), full(g), full(beta)],
        out_specs=row,
        out_shape=jax.ShapeDtypeStruct((b * s, CONV_CH), BF16),
        scratch_shapes=[pltpu.VMEM((HALO + ts, CONV_CH), F32),
                        pltpu.VMEM((SUBLANES - 1, HALO + ts, CONV_CH), F32)],
        compiler_params=_cparams(("arbitrary", "arbitrary")),
        name="conformer_conv",
    )(glu, w, bias, g, beta)


def _merge_router_kernel(x_ref, a_ref, c_ref, mg_ref, wa_ref, wc_ref, wo_ref, fg_ref, wr_ref, br_ref,
                         x1_o, h2_o, rt_o, cnt_o):
    tm = x_ref.shape[0]
    y_a = _dot(a_ref[...], wa_ref[...])
    y_c = _dot(c_ref[...], wc_ref[...])
    mix = mg_ref[:, 0:D_MODEL].astype(F32) * y_a + mg_ref[:, D_MODEL:2 * D_MODEL].astype(F32) * y_c
    x1 = x_ref[...] + _dot(mix.astype(BF16), wo_ref[...])
    x1_o[...] = x1
    h2 = x1 * lax.rsqrt(jnp.mean(x1 * x1, axis=-1, keepdims=True) + EPS) * fg_ref[...]
    h2_o[...] = h2

    hh, hl = _split(h2)
    logits = _dot_nt(wr_ref[0], hh) + _dot_nt(wr_ref[0], hl) + _dot_nt(wr_ref[1], hh) + br_ref[...]
    row = lax.broadcasted_iota(I32, (LANES, tm), 0)
    is_g = (row >= GROUP_LANE0) & (row < GROUP_LANE0 + N_GROUPS)
    gmax = jnp.max(jnp.where(is_g, logits, -jnp.inf), axis=0, keepdims=True)
    gsel = jnp.min(jnp.where(is_g & (logits == gmax), row - GROUP_LANE0, N_GROUPS), axis=0, keepdims=True)
    pg_sel = 1.0 / jnp.sum(jnp.where(is_g, jnp.exp(logits - gmax), 0.0), axis=0, keepdims=True)
    in_grp = (row < N_EXPERTS) & (row // EXP_PER_GROUP == gsel)
    emax = jnp.max(jnp.where(in_grp, logits, -jnp.inf), axis=0, keepdims=True)
    ee = jnp.where(in_grp, jnp.exp(logits - emax), 0.0)
    pe = jnp.where(in_grp, ee / jnp.sum(ee, axis=0, keepdims=True), -1.0)
    p1 = jnp.max(pe, axis=0, keepdims=True)
    i1 = jnp.min(jnp.where(pe == p1, row, LANES), axis=0, keepdims=True)
    pe2 = jnp.where(row == i1, -1.0, pe)
    p2 = jnp.max(pe2, axis=0, keepdims=True)
    i2 = jnp.min(jnp.where(pe2 == p2, row, LANES), axis=0, keepdims=True)
    w1 = pg_sel * p1 / (p1 + p2)
    w2 = pg_sel * p2 / (p1 + p2)

    hot1 = row == i1
    hot2 = row == i2
    ind = jnp.where(hot1 | hot2, 1.0, 0.0).astype(BF16)
    earlier = (lax.broadcasted_iota(I32, (tm, tm), 0) < lax.broadcasted_iota(I32, (tm, tm), 1)).astype(BF16)
    before = _dot(ind, earlier)
    k1 = jnp.sum(jnp.where(hot1, before, 0.0), axis=0, keepdims=True)
    k2 = jnp.sum(jnp.where(hot2, before, 0.0), axis=0, keepdims=True)
    rec_t = jnp.zeros((LANES, tm), F32)
    for slot, val in ((R_E1, i1.astype(F32)), (R_E2, i2.astype(F32)), (R_W1, w1), (R_W2, w2), (R_K1, k1), (R_K2, k2)):
        rec_t = jnp.where(row == slot, val, rec_t)
    for c in range(tm // LANES):
        rt_o[c * LANES:(c + 1) * LANES, :] = rec_t[:, c * LANES:(c + 1) * LANES].T
    cnt_o[0] = _dot_nt(jnp.ones((SUBLANES, tm), BF16), ind)


def _merge_router(x2, attn, cact, mg, wa, wc, wo, fg, wr, br):
    t = x2.shape[0]
    tm = TR
    row = lambda n: pl.BlockSpec((tm, n), lambda i: (i, 0))
    full = lambda a: pl.BlockSpec(a.shape, lambda i: (0,) * a.ndim)
    return pl.pallas_call(
        _merge_router_kernel,
        grid=(t // tm,),
        in_specs=[row(D_MODEL), row(Q_COLS), row(CONV_CH), row(2 * D_MODEL), full(wa), full(wc), full(wo),
                  full(fg), full(wr), full(br)],
        out_specs=[row(D_MODEL), row(D_MODEL), row(LANES), pl.BlockSpec((1, SUBLANES, LANES), lambda i: (i, 0, 0))],
        out_shape=[jax.ShapeDtypeStruct((t, D_MODEL), F32), jax.ShapeDtypeStruct((t, D_MODEL), F32),
                   jax.ShapeDtypeStruct((t, LANES), F32), jax.ShapeDtypeStruct((t // tm, SUBLANES, LANES), F32)],
        compiler_params=_cparams(("parallel",)),
        name="merge_router",
    )(x2, attn, cact, mg, wa, wc, wo, fg, wr, br)


def _start_chunks(dst8_ref, tile, n_chunks, make_copy):
    def issue(c, carry):
        far = pl.multiple_of(dst8_ref[tile * NCH + c] * CHUNK, CHUNK)
        make_copy(pl.multiple_of(c * CHUNK, CHUNK), far).start()
        return carry

    lax.fori_loop(0, n_chunks, issue, 0)


def _wait_chunks(n_chunks, make_copy):
    def wait_one(c, carry):
        make_copy(0, 0).wait()
        return carry

    lax.fori_loop(0, n_chunks, wait_one, 0)


def _dispatch_kernel(dst8_ref, nct_ref, pad8_ref, npad_ref, nused_ref, h_ref, rt_ref, lo_ref, xg_ref, slot_o, xs,
                     zbuf, sem, zsem):
    tr = h_ref.shape[0]

    @pl.when(pl.program_id(0) == 0)
    def _():
        zbuf[...] = jnp.zeros(zbuf.shape, F32)

        def zero_chunk(far):
            return pltpu.make_async_copy(zbuf.at[pl.ds(0, CHUNK), :], xg_ref.at[pl.ds(far, CHUNK), :], zsem)

        def zero_block(blk):
            return pltpu.make_async_copy(zbuf, xg_ref.at[pl.ds(pl.multiple_of(blk * BM, BM), BM), :], sem.at[0])

        def per_expert(e, carry, wait):
            def one(c, inner):
                if wait:
                    zero_chunk(0).wait()
                else:
                    zero_chunk(pl.multiple_of((pad8_ref[e] + c) * CHUNK, CHUNK)).start()
                return inner
            return lax.fori_loop(0, npad_ref[e], one, carry)

        def tail(blk, carry, wait):
            if wait:
                zero_block(0).wait()
            else:
                zero_block(blk).start()
            return carry

        n_blocks = xg_ref.shape[0] // BM
        lax.fori_loop(0, N_EXPERTS, functools.partial(per_expert, wait=False), 0)
        lax.fori_loop(nused_ref[0], n_blocks, functools.partial(tail, wait=False), 0)
        lax.fori_loop(0, N_EXPERTS, functools.partial(per_expert, wait=True), 0)
        lax.fori_loop(nused_ref[0], n_blocks, functools.partial(tail, wait=True), 0)

    rt = rt_ref[...]
    lane = lax.broadcasted_iota(I32, (tr, LANES), 1)
    lane_f = lane.astype(F32)
    lo_row = lo_ref[0, 0:1, :]
    slots = []
    for e_lane, k_lane in ((R_E1, R_K1), (R_E2, R_K2)):
        hot = lane_f == rt[:, e_lane:e_lane + 1]
        slots.append(jnp.sum(jnp.where(hot, lo_row, 0.0), axis=-1, keepdims=True) + rt[:, k_lane:k_lane + 1])
    rec = jnp.where(lane == 0, slots[0], jnp.where(lane == 1, slots[1], 0.0))
    slot_o[...] = rec
    rec_t = jnp.concatenate([rec[c * LANES:(c + 1) * LANES].T for c in range(tr // LANES)], axis=1)
    row_id = lax.broadcasted_iota(I32, (R_LOCAL, tr), 0).astype(F32)
    pick = (row_id == rec_t[0:1, :]) | (row_id == rec_t[1:2, :])
    staged = _dot(jnp.where(pick, 1.0, 0.0).astype(BF16), h_ref[...].astype(BF16))

    i = pl.program_id(0)
    last = pl.num_programs(0) - 1
    slot = i % 2

    def copies_from(s):
        return lambda near, far: pltpu.make_async_copy(
            xs.at[s, pl.ds(near, CHUNK), :], xg_ref.at[pl.ds(far, CHUNK), :], sem.at[s])

    @pl.when(i >= 2)
    def _():
        _wait_chunks(nct_ref[i - 2], copies_from(slot))

    xs[slot] = staged
    _start_chunks(dst8_ref, i, nct_ref[i], copies_from(slot))

    @pl.when(i == last)
    def _():
        @pl.when(i >= 1)
        def _():
            _wait_chunks(nct_ref[i - 1], copies_from(1 - slot))
        _wait_chunks(nct_ref[i], copies_from(slot))


def _dispatch(dst8, nct, pad8, npad, nused, h2, route, lo_rows, p_rows):
    t = h2.shape[0]
    return pl.pallas_call(
        _dispatch_kernel,
        grid_spec=pltpu.PrefetchScalarGridSpec(
            num_scalar_prefetch=5,
            grid=(t // TR,),
            in_specs=[pl.BlockSpec((TR, D_MODEL), lambda i, *_: (i, 0)),
                      pl.BlockSpec((TR, LANES), lambda i, *_: (i, 0)),
                      pl.BlockSpec((1, SUBLANES, LANES), lambda i, *_: (i, 0, 0))],
            out_specs=[pl.BlockSpec(memory_space=pl.ANY),
                       pl.BlockSpec((TR, LANES), lambda i, *_: (i, 0))],
            scratch_shapes=[pltpu.VMEM((2, R_LOCAL, D_MODEL), F32), pltpu.VMEM((BM, D_MODEL), F32),
                            pltpu.SemaphoreType.DMA((2,)), pltpu.SemaphoreType.DMA(())],
        ),
        out_shape=[jax.ShapeDtypeStruct((p_rows, D_MODEL), F32), jax.ShapeDtypeStruct((t, LANES), F32)],
        compiler_params=_cparams(("arbitrary",)),
        name="moe_dispatch",
    )(dst8, nct, pad8, npad, nused, h2, route, lo_rows)


def _expert_kernel(blk_e_ref, nused_ref, x_ref, wg_ref, wu_ref, wd_ref, y_ref, wg_b, wu_b, wd_b):
    i = pl.program_id(0)
    used = i < nused_ref[0]

    @pl.when((i == 0) | (blk_e_ref[i] != blk_e_ref[jnp.maximum(i - 1, 0)]))
    def _():
        wg_b[...] = wg_ref[0].astype(BF16)
        wu_b[...] = wu_ref[0].astype(BF16)
        wd_b[...] = wd_ref[0].astype(BF16)

    @pl.when(used)
    def _():
        xb = x_ref[...].astype(BF16)
        a = _dot(xb, wg_b[...])
        u = _dot(xb, wu_b[...])
        y_ref[...] = _dot((a * jax.nn.sigmoid(a) * u).astype(BF16), wd_b[...])

    @pl.when(jnp.logical_not(used))
    def _():
        y_ref[...] = jnp.zeros(y_ref.shape, F32)


def _experts(blk_e, nused, xg, wg, wu, wd):
    p_rows = xg.shape[0]
    x_blk = lambda i, be, nu: (jnp.minimum(i, nu[0] - 1), 0)
    return pl.pallas_call(
        _expert_kernel,
        grid_spec=pltpu.PrefetchScalarGridSpec(
            num_scalar_prefetch=2,
            grid=(p_rows // BM,),
            in_specs=[pl.BlockSpec((BM, D_MODEL), x_blk),
                      pl.BlockSpec((1, D_MODEL, EXPERT_FF), lambda i, be, nu: (be[i], 0, 0)),
                      pl.BlockSpec((1, D_MODEL, EXPERT_FF), lambda i, be, nu: (be[i], 0, 0)),
                      pl.BlockSpec((1, EXPERT_FF, D_MODEL), lambda i, be, nu: (be[i], 0, 0))],
            out_specs=pl.BlockSpec((BM, D_MODEL), lambda i, be, nu: (i, 0)),
            scratch_shapes=[pltpu.VMEM((D_MODEL, EXPERT_FF), BF16), pltpu.VMEM((D_MODEL, EXPERT_FF), BF16),
                            pltpu.VMEM((EXPERT_FF, D_MODEL), BF16)],
        ),
        out_shape=jax.ShapeDtypeStruct((p_rows, D_MODEL), F32),
        compiler_params=_cparams(("arbitrary",)),
        name="moe_experts",
    )(blk_e, nused, xg, wg, wu, wd)


def _combine_ple_kernel(dst8_ref, nct_ref, x1_ref, rt_ref, slot_ref, p_ref, pg_ref, wpg_ref, bpg_ref, wpp_ref,
                        yg_ref, o_ref, ys, sem):
    tr = x1_ref.shape[0]
    i = pl.program_id(0)
    slot = i % 2

    def copies_into(s):
        return lambda near, far: pltpu.make_async_copy(
            yg_ref.at[pl.ds(far, CHUNK), :], ys.at[s, pl.ds(near, CHUNK), :], sem.at[s])

    def fetch(tile, s):
        _start_chunks(dst8_ref, tile, nct_ref[tile], copies_into(s))

        def clear(c, carry):
            ys[s, pl.ds(pl.multiple_of(c * CHUNK, CHUNK), CHUNK), :] = jnp.zeros((CHUNK, D_MODEL), F32)
            return carry

        lax.fori_loop(nct_ref[tile], NCH, clear, 0)

    @pl.when(i == 0)
    def _():
        fetch(0, 0)

    @pl.when(i + 1 < pl.num_programs(0))
    def _():
        fetch(i + 1, 1 - slot)

    _wait_chunks(nct_ref[i], copies_into(slot))

    rt = rt_ref[...]
    sl = slot_ref[...]
    col_id = lax.broadcasted_iota(I32, (tr, R_LOCAL), 1).astype(F32)
    yb = ys[slot].astype(BF16)
    gather_w = jnp.where(col_id == sl[:, 0:1], rt[:, R_W1:R_W1 + 1],
                         jnp.where(col_id == sl[:, 1:2], rt[:, R_W2:R_W2 + 1], 0.0)).astype(BF16)
    x2 = x1_ref[...] + _dot(gather_w, yb)
    h3 = (x2 * lax.rsqrt(jnp.mean(x2 * x2, axis=-1, keepdims=True) + EPS) * pg_ref[...]).astype(BF16)
    gate = jax.nn.sigmoid(_dot(h3, wpg_ref[...]) + bpg_ref[...])
    o_ref[...] = x2 + gate * _dot(p_ref[...].astype(BF16), wpp_ref[...])


def _combine_ple(dst8, nct, x1, route, slots, p2, pg, wpg, bpg, wpp, yg):
    t = x1.shape[0]
    row = lambda n: pl.BlockSpec((TR, n), lambda i, d, c: (i, 0))
    full = lambda a: pl.BlockSpec(a.shape, lambda i, d, c: (0,) * a.ndim)
    return pl.pallas_call(
        _combine_ple_kernel,
        grid_spec=pltpu.PrefetchScalarGridSpec(
            num_scalar_prefetch=2,
            grid=(t // TR,),
            in_specs=[row(D_MODEL), row(LANES), row(LANES), row(PLE_DIM), full(pg), full(wpg), full(bpg),
                      full(wpp), pl.BlockSpec(memory_space=pl.ANY)],
            out_specs=row(D_MODEL),
            scratch_shapes=[pltpu.VMEM((2, R_LOCAL, D_MODEL), F32), pltpu.SemaphoreType.DMA((2,))],
        ),
        out_shape=jax.ShapeDtypeStruct((t, D_MODEL), F32),
        compiler_params=_cparams(("arbitrary",)),
        name="moe_combine_ple",
    )(dst8, nct, x1, route, slots, p2, pg, wpg, bpg, wpp, yg)


def _seg_sum_matrix():
    seg = np.arange(LANES) // HEAD_DIM
    return jnp.asarray(seg[:, None] == seg[None, :], BF16)


def _overlap_t(nc_rows, n_blk):
    nc = np.arange(nc_rows)[None, :] * CMP_STRIDE
    sb = np.arange(n_blk)[:, None] * SEL_BLOCK
    ov = (nc < sb + SEL_BLOCK) & (nc + CMP_BLOCK > sb) & (np.arange(nc_rows)[None, :] < nc_rows - 1)
    return jnp.asarray(ov, BF16)


def _compress_weights(pos, w1, w2):
    half = CMP_BLOCK // 2
    pe = jnp.broadcast_to(pos[:, None, :], (CMP_BLOCK, N_KV, HEAD_DIM)).reshape(2, half * N_KV * HEAD_DIM)
    w1r = w1.reshape(2, half, HEAD_DIM, CMP_HIDDEN)
    eye = jnp.eye(N_KV, dtype=w1.dtype)
    wx = jnp.einsum('hldj,gk->hlgdkj', w1r, eye).reshape(2, half * N_KV * HEAD_DIM, N_KV * CMP_HIDDEN)
    w2x = jnp.einsum('jd,gk->gjkd', w2, eye).reshape(N_KV * CMP_HIDDEN, N_KV * HEAD_DIM)
    return pe, wx[0].astype(BF16), wx[1].astype(BF16), w2x.astype(BF16)


def _layer(x, p_i, prm):
    b, s, _ = x.shape
    t = b * s
    x2 = x.reshape(t, D_MODEL)
    bd = _seg_sum_matrix()

    w = prm["w_in"]
    kv = lambda j: w[:, OFF_KV + j * KV_COLS:OFF_KV + (j + 1) * KV_COLS]
    w_nat = jnp.concatenate([kv(0), kv(1), kv(2), kv(4), w[:, OFF_CONV:]], axis=1).astype(BF16)
    w_t = jnp.concatenate([w[:, :Q_COLS], kv(3), kv(5),
                           jnp.pad(w[:, OFF_GATE:OFF_CONV], ((0, 0), (0, LANES - 3 * N_HEADS)))], axis=1).T.astype(BF16)
    qg = jnp.broadcast_to(prm["q_norm_g"][:, None], (HEAD_DIM, TM_PROJ))
    kg = jnp.tile(prm["k_norm_g"], (1, LANES // HEAD_DIM))
    qt, kc_raw, vc_raw, ks, vst, kw, vwt, gtt, glu, mg = _in_proj(
        x2, prm["attn_norm_g"][None, :], w_nat, w_t, qg, kg, bd, prm["merge_b"][None, :], s)

    nrow = s // CMP_STRIDE
    cw = [_compress_weights(prm["cmp_pos"][j], prm["cmp_w1"][j], prm["cmp_w2"][j]) for j in range(2)]
    pe_x, wa, wb = (jnp.stack([cw[0][j], cw[1][j]]) for j in range(3))
    kc, vct = _compress(kc_raw.reshape(b, s, LANES), vc_raw.reshape(b, s, LANES), pe_x, wa, wb, cw[0][3],
                        cw[1][3].T, bd, kg)

    attn = _attention(qt, gtt, kc, vct, ks.reshape(b, s, 4 * LANES), vst, kw.reshape(b, s, 2 * LANES), vwt,
                      _overlap_t(nrow, s // SEL_BLOCK), b, s)

    cact = _conformer_conv(glu, prm["conv_w"], prm["conv_b"][None, :], prm["conv_norm_g"][None, :],
                           prm["conv_norm_b"][None, :], b, s)

    wr_full = jnp.zeros((D_MODEL, LANES), F32)
    wr_full = wr_full.at[:, :N_EXPERTS].set(prm["w_router_expert"])
    wr_full = wr_full.at[:, GROUP_LANE0:GROUP_LANE0 + N_GROUPS].set(prm["w_router_group"])
    wr_hi = wr_full.astype(BF16)
    wr = jnp.stack([wr_hi.T, (wr_full - wr_hi.astype(F32)).astype(BF16).T])
    br = jnp.zeros((LANES,), F32)
    br = br.at[:N_EXPERTS].set(prm["b_router_expert"])
    br = br.at[GROUP_LANE0:GROUP_LANE0 + N_GROUPS].set(prm["b_router_group"])
    br = jnp.broadcast_to(br[:, None], (LANES, TR))
    x1, h2, route, cnt = _merge_router(
        x2, attn, cact, mg, prm["w_attn_out"].astype(BF16), prm["w_conv_out"].astype(BF16),
        prm["w_out"].astype(BF16), prm["ffn_norm_g"][None, :], wr, br)

    n_tiles = t // TR
    counts = cnt[:, 0, :N_EXPERTS].astype(I32)
    seg = (counts + CHUNK - 1) // CHUNK * CHUNK
    local_off = jnp.cumsum(seg, axis=1) - seg
    nct = (jnp.sum(seg, axis=1) // CHUNK).astype(I32)
    total = jnp.sum(seg, axis=0)
    padded = (total + BM - 1) // BM * BM
    pends = jnp.cumsum(padded)
    far_off = (pends - padded)[None, :] + jnp.cumsum(seg, axis=0) - seg
    chunk = jnp.arange(NCH, dtype=I32)[None, :, None]
    lo8 = (local_off // CHUNK)[:, None, :]
    in_seg = (chunk >= lo8) & (chunk < ((local_off + seg) // CHUNK)[:, None, :])
    dst8 = (jnp.sum(jnp.where(in_seg, (far_off // CHUNK)[:, None, :] - lo8, 0), axis=-1)
            + chunk[:, :, 0]).astype(I32).reshape(n_tiles * NCH)
    lo_rows = jnp.broadcast_to(jnp.pad(local_off.astype(F32), ((0, 0), (0, LANES - N_EXPERTS)))[:, None, :],
                               (n_tiles, SUBLANES, LANES))
    n_blk = -(-(2 * t + n_tiles * N_EXPERTS * (CHUNK - 1) + N_EXPERTS * (BM - 1)) // BM)
    blk_e = jnp.minimum(jnp.sum(jnp.arange(n_blk, dtype=I32)[:, None] * BM >= pends[None, :], axis=1),
                        N_EXPERTS - 1).astype(I32)
    nused = (pends[-1:] // BM).astype(I32)
    pad8 = ((pends - padded + total) // CHUNK).astype(I32)
    npad = ((padded - total) // CHUNK).astype(I32)

    xg, slots = _dispatch(dst8, nct, pad8, npad, nused, h2, route, lo_rows, n_blk * BM)
    yg = _experts(blk_e, nused, xg, prm["w_exp_gate"], prm["w_exp_up"], prm["w_exp_down"])
    out = _combine_ple(dst8, nct, x1, route, slots, p_i.reshape(t, PLE_DIM), prm["ple_norm_g"][None, :],
                       prm["w_ple_gate"].astype(BF16), prm["b_ple_gate"][None, :], prm["w_ple_proj"].astype(BF16), yg)
    return out.reshape(b, s, D_MODEL)


def kernel(x, p, attn_norm_g, w_in, q_norm_g, k_norm_g, cmp_pos, cmp_w1, cmp_w2, w_attn_out, conv_w, conv_b, conv_norm_g, conv_norm_b, w_conv_out, merge_b, w_out, ffn_norm_g, w_router_group, b_router_group, w_router_expert, b_router_expert, w_exp_gate, w_exp_up, w_exp_down, ple_norm_g, w_ple_gate, b_ple_gate, w_ple_proj):
    stacked = dict(attn_norm_g=attn_norm_g, w_in=w_in, q_norm_g=q_norm_g, k_norm_g=k_norm_g, cmp_pos=cmp_pos,
                   cmp_w1=cmp_w1, cmp_w2=cmp_w2, w_attn_out=w_attn_out, conv_w=conv_w, conv_b=conv_b,
                   conv_norm_g=conv_norm_g, conv_norm_b=conv_norm_b, w_conv_out=w_conv_out, merge_b=merge_b,
                   w_out=w_out, ffn_norm_g=ffn_norm_g, w_router_group=w_router_group,
                   b_router_group=b_router_group, w_router_expert=w_router_expert,
                   b_router_expert=b_router_expert, w_exp_gate=w_exp_gate, w_exp_up=w_exp_up,
                   w_exp_down=w_exp_down, ple_norm_g=ple_norm_g, w_ple_gate=w_ple_gate, b_ple_gate=b_ple_gate,
                   w_ple_proj=w_ple_proj)
    for i in range(w_in.shape[0]):
        x = _layer(x, p[i], {k: v[i] for k, v in stacked.items()})
    return x
```

```python
import functools

import numpy as np
import jax
import jax.numpy as jnp
from jax import lax
from jax.experimental import pallas as pl
from jax.experimental.pallas import tpu as pltpu

F32 = jnp.float32
BF16 = jnp.bfloat16
I32 = jnp.int32

D_MODEL = 1024
PLE_DIM = 256
N_HEADS = 8
N_KV = 2
HPG = N_HEADS // N_KV
HEAD_DIM = 64
CMP_BLOCK = 32
CMP_STRIDE = 16
CMP_HIDDEN = 128
SEL_BLOCK = 64
N_SEL = 8
WINDOW = 512
CONV_CH = 512
CONV_WIDTH = 31
N_GROUPS = 4
EXP_PER_GROUP = 8
N_EXPERTS = N_GROUPS * EXP_PER_GROUP
EXPERT_FF = 512
EPS = 1e-6
NEG = -1e30
BIG = 1e4
Q_COLS = N_HEADS * HEAD_DIM
KV_COLS = N_KV * HEAD_DIM
OFF_KV = Q_COLS
OFF_GATE = OFF_KV + 6 * KV_COLS
OFF_CONV = OFF_GATE + 3 * N_HEADS
N_FORCED = 3

LANES = 128
SUBLANES = 8
VMEM_LIMIT = 56 * 1024 * 1024

TM_PROJ = 256
QB = 256
KT = 256
TS_CONV = 512
CONV_ROWS = 64
HALO = 32
TR = 512
BM = 512
CHUNK = SUBLANES
R_LOCAL = 2 * TR + N_EXPERTS * CHUNK
NCH = R_LOCAL // CHUNK

N_KC = 0
N_VC = N_KC + LANES
N_KS = N_VC + LANES
N_KW = N_KS + LANES
N_CONV = N_KW + LANES
N_MERGE = N_CONV + 2 * CONV_CH
T_Q = 0
T_VS = T_Q + Q_COLS
T_VW = T_VS + LANES
T_GATE = T_VW + LANES

POS_HI_LANE = HEAD_DIM
POS_LO_LANE = HEAD_DIM + 1
UNSELECTED = -(2.0 ** 40)

R_E1, R_E2, R_W1, R_W2, R_K1, R_K2 = 0, 1, 2, 3, 4, 5
GROUP_LANE0 = N_EXPERTS


def _cparams(sem, vmem=VMEM_LIMIT):
    return pltpu.CompilerParams(dimension_semantics=sem, vmem_limit_bytes=vmem)


def _dot(a, b):
    return jnp.dot(a, b, preferred_element_type=F32)


def _dot_nt(a, b):
    return lax.dot_general(a, b, (((1,), (1,)), ((), ())), preferred_element_type=F32)


def _split(x):
    hi = x.astype(BF16)
    lo = (x - hi.astype(F32)).astype(BF16)
    return hi, lo


def _seg_rmsnorm(z, bd):
    hi, lo = _split(z * z)
    ss = _dot(hi, bd) + _dot(lo, bd)
    return z * lax.rsqrt(ss * (1.0 / HEAD_DIM) + EPS)


def _with_pos_lanes(kn, pos):
    lane = lax.broadcasted_iota(I32, kn.shape, 1)
    hi = pos // SEL_BLOCK * SEL_BLOCK
    pos_lanes = jnp.where(lane == POS_HI_LANE, hi.astype(F32),
                          jnp.where(lane == POS_LO_LANE, (pos - hi).astype(F32), 0.0))
    lo_half = lane < HEAD_DIM
    return (jnp.where(lo_half, kn, pos_lanes).astype(BF16),
            jnp.where(lo_half, pltpu.roll(kn, HEAD_DIM, 1), pos_lanes).astype(BF16))


def _in_proj_kernel(x_ref, g_ref, w_ref, wt_ref, qg_ref, kg_ref, bd_ref, mb_ref,
                    qt_o, kc_o, vc_o, ks_o, vst_o, kw_o, vwt_o, gtt_o, glu_o, mg_o, *, seq_len):
    tm = x_ref.shape[0]
    x = x_ref[...]
    h = (x * lax.rsqrt(jnp.mean(x * x, axis=-1, keepdims=True) + EPS) * g_ref[...]).astype(BF16)
    bd = bd_ref[...]
    lane = lax.broadcasted_iota(I32, (tm, LANES), 1)

    def proj(a, b):
        return _dot(h, w_ref[:, a:b])

    zt = _dot_nt(wt_ref[...], h)
    for hd in range(N_HEADS):
        zh = zt[T_Q + hd * HEAD_DIM:T_Q + (hd + 1) * HEAD_DIM, :]
        ms = jnp.sum(zh * zh, axis=0, keepdims=True) * (1.0 / HEAD_DIM)
        qt_o[hd * HEAD_DIM:(hd + 1) * HEAD_DIM, :] = (
            zh * lax.rsqrt(ms + EPS) * qg_ref[...] * (HEAD_DIM ** -0.5)).astype(BF16)
    ones = jnp.ones((HEAD_DIM, tm), F32)
    for off, v_o in ((T_VS, vst_o), (T_VW, vwt_o)):
        v_o[0] = jnp.concatenate([zt[off:off + HEAD_DIM, :], ones, zt[off + HEAD_DIM:off + LANES, :], ones],
                                 axis=0).astype(BF16)
    gtt_o[...] = jax.nn.sigmoid(zt[T_GATE:T_GATE + LANES, :])

    kc_o[...] = proj(N_KC, N_KC + LANES)
    vc_o[...] = proj(N_VC, N_VC + LANES)

    pos = (pl.program_id(0) * tm) % seq_len + lax.broadcasted_iota(I32, (tm, LANES), 0)
    onehot = jnp.where(lane == pos // SEL_BLOCK, 1.0, 0.0).astype(BF16)

    def key_tiles(zk, gain):
        return _with_pos_lanes(_seg_rmsnorm(zk, bd) * gain, pos)

    k0, k1 = key_tiles(proj(N_KS, N_KS + LANES), kg_ref[1:2, :])
    ks_o[...] = jnp.concatenate([k0, onehot, k1, onehot], axis=1)
    k0, k1 = key_tiles(proj(N_KW, N_KW + LANES), kg_ref[2:3, :])
    kw_o[...] = jnp.concatenate([k0, k1], axis=1)
    za = proj(N_CONV, N_CONV + CONV_CH)
    zg = proj(N_CONV + CONV_CH, N_CONV + 2 * CONV_CH)
    glu_o[...] = za * jax.nn.sigmoid(zg)
    for c in range(2):
        zm = proj(N_MERGE + c * D_MODEL, N_MERGE + (c + 1) * D_MODEL)
        mg_o[:, c * D_MODEL:(c + 1) * D_MODEL] = jax.nn.sigmoid(
            zm + mb_ref[:, c * D_MODEL:(c + 1) * D_MODEL]).astype(BF16)


def _in_proj(x2, norm_g, w_nat, w_t, qg, kg, bd, mb, seq_len):
    t = x2.shape[0]
    tm = TM_PROJ
    assert tm == KT and seq_len % tm == 0 and seq_len // SEL_BLOCK <= HEAD_DIM
    row = lambda n: pl.BlockSpec((tm, n), lambda i: (i, 0))
    col = lambda n: pl.BlockSpec((n, tm), lambda i: (0, i))
    vt = pl.BlockSpec((1, 2 * LANES, tm), lambda i: (i, 0, 0))
    full = lambda a: pl.BlockSpec(a.shape, lambda i: (0,) * a.ndim)
    sds = jax.ShapeDtypeStruct
    return pl.pallas_call(
        functools.partial(_in_proj_kernel, seq_len=seq_len),
        grid=(t // tm,),
        in_specs=[row(D_MODEL), full(norm_g), full(w_nat), full(w_t), full(qg), full(kg), full(bd), full(mb)],
        out_specs=[col(Q_COLS), row(LANES), row(LANES), row(4 * LANES), vt, row(2 * LANES), vt, col(LANES),
                   row(CONV_CH), row(2 * D_MODEL)],
        out_shape=[sds((Q_COLS, t), BF16), sds((t, LANES), F32), sds((t, LANES), F32), sds((t, 4 * LANES), BF16),
                   sds((t // tm, 2 * LANES, tm), BF16), sds((t, 2 * LANES), BF16), sds((t // tm, 2 * LANES, tm), BF16),
                   sds((LANES, t), F32), sds((t, CONV_CH), F32), sds((t, 2 * D_MODEL), BF16)],
        compiler_params=_cparams(("parallel",)),
        name="in_proj",
    )(x2, norm_g, w_nat, w_t, qg, kg, bd, mb)


def _compress_kernel(ck_ref, cv_ref, pe_ref, wa_ref, wb_ref, w2k_ref, w2vt_ref, bd_ref, kg_ref, kc_o, vct_o):
    nrow = ck_ref.shape[1] // CMP_STRIDE

    def hidden(idx, c_ref):
        ya = jnp.zeros((nrow, N_KV * CMP_HIDDEN), F32)
        yb = jnp.zeros((nrow, N_KV * CMP_HIDDEN), F32)
        for l in range(CMP_STRIDE):
            tok = c_ref[0, pl.ds(l, nrow, stride=CMP_STRIDE), :]
            cols = slice(l * LANES, (l + 1) * LANES)
            ya = ya + _dot((tok + pe_ref[idx, 0:1, cols]).astype(BF16), wa_ref[idx, cols, :])
            yb = yb + _dot((tok + pe_ref[idx, 1:2, cols]).astype(BF16), wb_ref[idx, cols, :])
        return jax.nn.gelu(ya + pltpu.roll(yb, nrow - 1, 0)).astype(BF16)

    kn = _seg_rmsnorm(_dot(hidden(0, ck_ref), w2k_ref[...]), bd_ref[...]) * kg_ref[0:1, :]
    c_end = lax.broadcasted_iota(I32, (nrow, LANES), 0) * CMP_STRIDE + (CMP_BLOCK - 1)
    kc_o[0] = jnp.concatenate(_with_pos_lanes(kn, c_end), axis=1)
    vt = _dot_nt(w2vt_ref[...], hidden(1, cv_ref))
    ones = jnp.ones((HEAD_DIM, nrow), F32)
    vct_o[0] = jnp.concatenate([vt[:HEAD_DIM], ones, vt[HEAD_DIM:], ones], axis=0).astype(BF16)


def _compress(kc_raw, vc_raw, pe_x, wa, wb, w2k, w2vt, bd, kg):
    b, s, width = kc_raw.shape
    nrow = s // CMP_STRIDE
    batch = lambda r, n: pl.BlockSpec((1, r, n), lambda i: (i, 0, 0))
    full = lambda a: pl.BlockSpec(a.shape, lambda i: (0,) * a.ndim)
    return pl.pallas_call(
        _compress_kernel,
        grid=(b,),
        in_specs=[batch(s, width), batch(s, width), full(pe_x), full(wa), full(wb), full(w2k), full(w2vt),
                  full(bd), full(kg)],
        out_specs=[batch(nrow, 2 * LANES), batch(2 * LANES, nrow)],
        out_shape=[jax.ShapeDtypeStruct((b, nrow, 2 * LANES), BF16),
                   jax.ShapeDtypeStruct((b, 2 * LANES, nrow), BF16)],
        compiler_params=_cparams(("parallel",)),
        name="compress",
    )(kc_raw, vc_raw, pe_x, wa, wb, w2k, w2vt, bd, kg)


def _halve_rows(x, op):
    rows = x.shape[0]
    while rows > SUBLANES:
        rows //= 2
        x = op(x[:rows], x[rows:])
    return x


def _attn_kernel(qt_ref, gtt_ref, kc_ref, vct_ref, ks_ref, vst_ref, kw_ref, vwt_ref, ovt_ref, o_ref,
                 tiles_ref, sa_ref, sb_ref, m_ref, acc_ref):
    i = pl.program_id(1)
    q0 = i * QB
    nc_rows = kc_ref.shape[1]
    n_blk = ovt_ref.shape[0]
    hpc = 2 * LANES // QB
    pw = hpc * QB
    n_pairs = N_HEADS // hpc
    group_of = lambda c: c * hpc // HPG
    rel = ((lax.broadcasted_iota(I32, (KT, pw), 1) & (QB - 1))
           - lax.broadcasted_iota(I32, (KT, pw), 0)).astype(F32)
    slope_rows = lax.broadcasted_iota(I32, (HEAD_DIM, QB), 0) < 2
    gtt = gtt_ref[...]

    pairs = []
    for pr in range(n_pairs):
        q_pos = []
        for hd in range(pr * hpc, (pr + 1) * hpc):
            slope = jnp.where(slope_rows, 2.0 ** -(hd + 1), 0.0).astype(BF16)
            q_pos.append(jnp.concatenate([qt_ref[hd * HEAD_DIM:(hd + 1) * HEAD_DIM, :], slope], axis=0))
        pairs.append(jnp.concatenate(q_pos, axis=1))

    def update(s, vt, m, acc):
        m_new = jnp.maximum(m, jnp.max(_halve_rows(s, jnp.maximum), axis=0, keepdims=True))
        p = jnp.exp(s - m_new).astype(BF16)
        return m_new, jnp.exp(m - m_new) * acc + _dot(vt, p)

    def scores(j, q_list, k_ref, k_width):
        k0 = pl.multiple_of(j * KT, KT)
        return tuple(_dot(k_ref[0, pl.ds(k0, KT), group_of(c) * k_width:(group_of(c) + 1) * k_width], q_list[c])
                     for c in range(n_pairs))

    def consume(j, s_list, state, vt_ref, mask_fn, extra_dist=0.0):
        keep = None if mask_fn is None else mask_fn(rel + ((q0 - j * KT).astype(F32) + extra_dist))
        out = []
        for c in range(n_pairs):
            g = group_of(c)
            s = s_list[c] if keep is None else jnp.where(keep, s_list[c], NEG)
            out += update(s, vt_ref[j, g * LANES:(g + 1) * LANES, :], *state[2 * c:2 * c + 2])
        return tuple(out)

    j_diag = (q0 + QB - 1) // KT
    init = (jnp.full((1, pw), NEG, F32), jnp.zeros((LANES, pw), F32)) * n_pairs

    def selected_branch(q_list, n_act):
        def fill(buf, n):
            for c, s in enumerate(scores(tiles_ref[n], q_list, ks_ref, 2 * LANES)):
                buf[c] = s

        def use(buf, n, mask_fn):
            state = []
            for c in range(n_pairs):
                state += [m_ref[c, 0:1, :], acc_ref[c]]
            out = consume(tiles_ref[n], [buf[c] for c in range(n_pairs)], state, vst_ref, mask_fn)
            for c in range(n_pairs):
                m_ref[c, 0:1, :] = out[2 * c]
                acc_ref[c] = out[2 * c + 1]

        for c in range(n_pairs):
            m_ref[c] = jnp.full((SUBLANES, pw), NEG, F32)
            acc_ref[c] = jnp.zeros((LANES, pw), F32)
        fill(sa_ref, 0)

        def two_tiles(h, carry):
            n = 2 * h
            fill(sb_ref, n + 1)
            use(sa_ref, n, None)

            @pl.when(n + 1 < n_act)
            def _():
                fill(sa_ref, n + 2)
                use(sb_ref, n + 1, None)
            return carry

        lax.fori_loop(0, (n_act + 1) // 2, two_tiles, 0)

        @pl.when(n_act % 2 == 1)
        def _():
            for c in range(n_pairs):
                sa_ref[c] = sb_ref[c]

        use(sa_ref, n_act, lambda d: d >= 0.0)
        return [acc_ref[c] for c in range(n_pairs)]

    cidx = lax.broadcasted_iota(I32, (nc_rows, pw), 0)
    c_end = cidx * CMP_STRIDE + (CMP_BLOCK - 1)
    cmask = (c_end <= q0 + (lax.broadcasted_iota(I32, (nc_rows, pw), 1) & (QB - 1))) & (cidx < nc_rows - 1)

    sees_any = jnp.where(q0 + (lax.broadcasted_iota(I32, (1, pw), 1) & (QB - 1)) >= CMP_BLOCK - 1, 1.0, 0.0)
    s_cmp = [_dot(kc_ref[0, :, group_of(c) * LANES:(group_of(c) + 1) * LANES], pairs[c]) for c in range(n_pairs)]
    o_cmp, imp = [], []
    for g in range(N_KV):
        imp_g = jnp.zeros((n_blk, QB), F32)
        for pr in range(g * HPG // hpc, (g + 1) * HPG // hpc):
            s = jnp.where(cmask, s_cmp[pr], NEG)
            e = jnp.exp(s - jnp.max(_halve_rows(s, jnp.maximum), axis=0, keepdims=True)).astype(BF16)
            acc = _dot(vct_ref[0, g * LANES:(g + 1) * LANES, :], e)
            inv = sees_any / acc[HEAD_DIM:HEAD_DIM + 1]
            o_cmp.append(acc[:HEAD_DIM] * inv)
            weighted = _dot(ovt_ref[...], e) * inv
            for k in range(hpc):
                imp_g = imp_g + weighted[:, k * QB:(k + 1) * QB]
        imp.append(imp_g)

    blk = lax.broadcasted_iota(I32, (n_blk, N_KV * QB), 0)
    tl = q0 + (lax.broadcasted_iota(I32, (n_blk, N_KV * QB), 1) & (QB - 1))
    valid = blk * SEL_BLOCK <= tl
    cur = tl // SEL_BLOCK
    forced = (blk == 0) | (blk == cur) | (blk == cur - 1)
    sel = forced & valid
    score = jnp.where(valid, jnp.where(sel, -jnp.inf, jnp.concatenate(imp, axis=1)), -BIG)
    for _ in range(min(N_SEL, n_blk) - N_FORCED):
        mx = jnp.max(score, axis=0, keepdims=True)
        first = jnp.min(jnp.where(score == mx, blk, n_blk), axis=0, keepdims=True)
        pick = blk == first
        sel = sel | pick
        score = jnp.where(pick, -jnp.inf, score)
    chosen = sel & valid
    any_sel = jnp.where(chosen[:, :QB] | chosen[:, QB:], 1.0, 0.0)
    unsel = jnp.where(chosen, 0.0, UNSELECTED).astype(BF16)
    bias = jnp.concatenate([unsel, jnp.zeros((2 * HEAD_DIM - n_blk, N_KV * QB), BF16)], axis=0)
    q_sel = []
    for c in range(n_pairs):
        g_bias = bias[:, group_of(c) * QB:(group_of(c) + 1) * QB]
        q_sel.append(jnp.concatenate([pairs[c], jnp.concatenate([g_bias] * hpc, axis=1)], axis=0))

    per_blk = jnp.sum(any_sel, axis=1, keepdims=True)
    n_act = jnp.int32(0)
    for jt in range(n_blk * SEL_BLOCK // KT):
        blocks = slice(jt * (KT // SEL_BLOCK), (jt + 1) * (KT // SEL_BLOCK))
        tiles_ref[n_act] = jnp.int32(jt)
        n_act = n_act + ((jnp.sum(per_blk[blocks]) > 0.0) & (jt < j_diag)).astype(I32)
    tiles_ref[n_act] = j_diag

    assert WINDOW == 2 * KT and KT % QB == 0
    win_out = init
    for w in range(WINDOW // KT + 1):
        jw = j_diag - WINDOW // KT + w
        before_start = jnp.where(jw < 0, float(WINDOW + KT), 0.0)
        jc = jnp.maximum(jw, 0)
        mask_fn = ((lambda d: d < float(WINDOW)), (lambda d, jw=jw: jw >= 0), (lambda d: d >= 0.0))[w]
        win_out = consume(jc, scores(jc, pairs, kw_ref, LANES), win_out, vwt_ref, mask_fn, before_start)

    sel_acc = selected_branch(q_sel, n_act)

    o_sel = [a[:HEAD_DIM] / a[HEAD_DIM:HEAD_DIM + 1] for a in sel_acc]
    o_win = [a[:HEAD_DIM] / a[HEAD_DIM:HEAD_DIM + 1] for a in win_out[1::2]]
    mixed = []
    for hd in range(N_HEADS):
        c, lanes = hd // hpc, slice((hd % hpc) * QB, (hd % hpc + 1) * QB)
        mixed.append(gtt[3 * hd:3 * hd + 1, :] * o_cmp[c][:, lanes] + gtt[3 * hd + 1:3 * hd + 2, :] * o_sel[c][:, lanes]
                     + gtt[3 * hd + 2:3 * hd + 3, :] * o_win[c][:, lanes])
    for pr in range(N_HEADS // 2):
        two = jnp.concatenate(mixed[2 * pr:2 * pr + 2], axis=0)
        o_ref[:, pr * LANES:(pr + 1) * LANES] = jnp.concatenate(
            [two[:, k * LANES:(k + 1) * LANES].T for k in range(QB // LANES)], axis=0).astype(BF16)


def _attention(qt, gtt, kc, vct, ks, vst, kw, vwt, ovt, b, s):
    n_q = s // QB
    n_tiles = N_HEADS * QB // (2 * LANES)
    per_q = lambda n: pl.BlockSpec((n, QB), lambda bi, i: (0, bi * n_q + i))
    per_b = lambda a: pl.BlockSpec((1,) + a.shape[1:], lambda bi, i: (bi, 0, 0))
    vt = pl.BlockSpec((s // KT, 2 * LANES, KT), lambda bi, i: (bi, 0, 0))
    return pl.pallas_call(
        _attn_kernel,
        grid=(b, n_q),
        in_specs=[per_q(Q_COLS), per_q(LANES), per_b(kc), per_b(vct), per_b(ks), vt, per_b(kw), vt,
                  pl.BlockSpec(ovt.shape, lambda bi, i: (0, 0))],
        out_specs=pl.BlockSpec((QB, Q_COLS), lambda bi, i: (bi * n_q + i, 0)),
        out_shape=jax.ShapeDtypeStruct((b * s, Q_COLS), BF16),
        scratch_shapes=[pltpu.SMEM((s // KT + 1,), I32),
                        pltpu.VMEM((n_tiles, KT, 2 * LANES), F32), pltpu.VMEM((n_tiles, KT, 2 * LANES), F32),
                        pltpu.VMEM((n_tiles, SUBLANES, 2 * LANES), F32),
                        pltpu.VMEM((n_tiles, LANES, 2 * LANES), F32)],
        compiler_params=_cparams(("parallel", "parallel")),
        name="nsa_attention",
    )(qt, gtt, kc, vct, ks, vst, kw, vwt, ovt)


def _conv_kernel(z_ref, w_ref, b_ref, g_ref, beta_ref, o_ref, buf, shifted):
    ts = z_ref.shape[0]
    chunk = CONV_ROWS

    @pl.when(pl.program_id(1) == 0)
    def _():
        buf[0:HALO, :] = jnp.zeros((HALO, CONV_CH), F32)

    buf[HALO:HALO + ts, :] = z_ref[...]
    span = ts + HALO - SUBLANES
    for r in range(1, SUBLANES):
        shifted[r - 1, 0:span, :] = buf[r:r + span, :]

    def tap_rows(k, c):
        start = HALO - (CONV_WIDTH - 1) + k
        r = start % SUBLANES
        src = buf if r == 0 else shifted.at[r - 1]
        return src[start - r + c * chunk:start - r + (c + 1) * chunk, :]

    for c in range(ts // chunk):
        acc = jnp.zeros((chunk, CONV_CH), F32) + b_ref[...]
        for k in range(CONV_WIDTH):
            acc = acc + tap_rows(k, c) * w_ref[k:k + 1, :]
        mu = jnp.mean(acc, axis=-1, keepdims=True)
        d = acc - mu
        var = jnp.mean(d * d, axis=-1, keepdims=True)
        y = d * lax.rsqrt(var + EPS) * g_ref[...] + beta_ref[...]
        o_ref[c * chunk:(c + 1) * chunk, :] = (y * jax.nn.sigmoid(y)).astype(BF16)
    buf[0:HALO, :] = buf[ts:ts + HALO, :]


def _conformer_conv(glu, w, bias, g, beta, b, s):
    ts = min(TS_CONV, s)
    full = lambda a: pl.BlockSpec(a.shape, lambda bi, i: (0, 0))
    row = pl.BlockSpec((ts, CONV_CH), lambda bi, i: (bi * (s // ts) + i, 0))
    return pl.pallas_call(
        _conv_kernel,
        grid=(b, s // ts),
        in_specs=[row, full(w), full(bias), full(g), full(beta)],
        out_specs=row,
        out_shape=jax.ShapeDtypeStruct((b * s, CONV_CH), BF16),
        scratch_shapes=[pltpu.VMEM((HALO + ts, CONV_CH), F32),
                        pltpu.VMEM((SUBLANES - 1, HALO + ts, CONV_CH), F32)],
        compiler_params=_cparams(("arbitrary", "arbitrary")),
        name="conformer_conv",
    )(glu, w, bias, g, beta)


def _merge_router_kernel(x_ref, a_ref, c_ref, mg_ref, wa_ref, wc_ref, wo_ref, fg_ref, wr_ref, br_ref,
                         x1_o, h2_o, rt_o, cnt_o):
    tm = x_ref.shape[0]
    y_a = _dot(a_ref[...], wa_ref[...])
    y_c = _dot(c_ref[...], wc_ref[...])
    mix = mg_ref[:, 0:D_MODEL].astype(F32) * y_a + mg_ref[:, D_MODEL:2 * D_MODEL].astype(F32) * y_c
    x1 = x_ref[...] + _dot(mix.astype(BF16), wo_ref[...])
    x1_o[...] = x1
    h2 = x1 * lax.rsqrt(jnp.mean(x1 * x1, axis=-1, keepdims=True) + EPS) * fg_ref[...]
    h2_o[...] = h2

    hh, hl = _split(h2)
    logits = _dot_nt(wr_ref[0], hh) + _dot_nt(wr_ref[0], hl) + _dot_nt(wr_ref[1], hh) + br_ref[...]
    row = lax.broadcasted_iota(I32, (LANES, tm), 0)
    is_g = (row >= GROUP_LANE0) & (row < GROUP_LANE0 + N_GROUPS)
    gmax = jnp.max(jnp.where(is_g, logits, -jnp.inf), axis=0, keepdims=True)
    gsel = jnp.min(jnp.where(is_g & (logits == gmax), row - GROUP_LANE0, N_GROUPS), axis=0, keepdims=True)
    pg_sel = 1.0 / jnp.sum(jnp.where(is_g, jnp.exp(logits - gmax), 0.0), axis=0, keepdims=True)
    in_grp = (row < N_EXPERTS) & (row // EXP_PER_GROUP == gsel)
    emax = jnp.max(jnp.where(in_grp, logits, -jnp.inf), axis=0, keepdims=True)
    ee = jnp.where(in_grp, jnp.exp(logits - emax), 0.0)
    pe = jnp.where(in_grp, ee / jnp.sum(ee, axis=0, keepdims=True), -1.0)
    p1 = jnp.max(pe, axis=0, keepdims=True)
    i1 = jnp.min(jnp.where(pe == p1, row, LANES), axis=0, keepdims=True)
    pe2 = jnp.where(row == i1, -1.0, pe)
    p2 = jnp.max(pe2, axis=0, keepdims=True)
    i2 = jnp.min(jnp.where(pe2 == p2, row, LANES), axis=0, keepdims=True)
    w1 = pg_sel * p1 / (p1 + p2)
    w2 = pg_sel * p2 / (p1 + p2)

    hot1 = row == i1
    hot2 = row == i2
    ind = jnp.where(hot1 | hot2, 1.0, 0.0).astype(BF16)
    earlier = (lax.broadcasted_iota(I32, (tm, tm), 0) < lax.broadcasted_iota(I32, (tm, tm), 1)).astype(BF16)
    before = _dot(ind, earlier)
    k1 = jnp.sum(jnp.where(hot1, before, 0.0), axis=0, keepdims=True)
    k2 = jnp.sum(jnp.where(hot2, before, 0.0), axis=0, keepdims=True)
    rec_t = jnp.zeros((LANES, tm), F32)
    for slot, val in ((R_E1, i1.astype(F32)), (R_E2, i2.astype(F32)), (R_W1, w1), (R_W2, w2), (R_K1, k1), (R_K2, k2)):
        rec_t = jnp.where(row == slot, val, rec_t)
    for c in range(tm // LANES):
        rt_o[c * LANES:(c + 1) * LANES, :] = rec_t[:, c * LANES:(c + 1) * LANES].T
    cnt_o[0] = _dot_nt(jnp.ones((SUBLANES, tm), BF16), ind)


def _merge_router(x2, attn, cact, mg, wa, wc, wo, fg, wr, br):
    t = x2.shape[0]
    tm = TR
    row = lambda n: pl.BlockSpec((tm, n), lambda i: (i, 0))
    full = lambda a: pl.BlockSpec(a.shape, lambda i: (0,) * a.ndim)
    return pl.pallas_call(
        _merge_router_kernel,
        grid=(t // tm,),
        in_specs=[row(D_MODEL), row(Q_COLS), row(CONV_CH), row(2 * D_MODEL), full(wa), full(wc), full(wo),
                  full(fg), full(wr), full(br)],
        out_specs=[row(D_MODEL), row(D_MODEL), row(LANES), pl.BlockSpec((1, SUBLANES, LANES), lambda i: (i, 0, 0))],
        out_shape=[jax.ShapeDtypeStruct((t, D_MODEL), F32), jax.ShapeDtypeStruct((t, D_MODEL), F32),
                   jax.ShapeDtypeStruct((t, LANES), F32), jax.ShapeDtypeStruct((t // tm, SUBLANES, LANES), F32)],
        compiler_params=_cparams(("parallel",)),
        name="merge_router",
    )(x2, attn, cact, mg, wa, wc, wo, fg, wr, br)


def _start_chunks(dst8_ref, tile, n_chunks, make_copy):
    def issue(c, priority):
        far = pl.multiple_of(dst8_ref[tile * NCH + c] * CHUNK, CHUNK)
        make_copy(pl.multiple_of(c * CHUNK, CHUNK), far).start(priority=priority)

    def two(p, carry):
        issue(2 * p, 0)

        @pl.when(2 * p + 1 < n_chunks)
        def _():
            issue(2 * p + 1, 1)
        return carry

    lax.fori_loop(0, (n_chunks + 1) // 2, two, 0)


def _wait_chunks(n_chunks, make_copy):
    def wait_one(c, carry):
        make_copy(0, 0).wait()
        return carry

    lax.fori_loop(0, n_chunks, wait_one, 0)


def _dispatch_kernel(dst8_ref, nct_ref, pad8_ref, npad_ref, nused_ref, h_ref, rt_ref, lo_ref, xg_ref, slot_o, xs,
                     zbuf, sem, zsem):
    tr = h_ref.shape[0]

    @pl.when(pl.program_id(0) == 0)
    def _():
        zbuf[...] = jnp.zeros(zbuf.shape, F32)

        def zero_chunk(far):
            return pltpu.make_async_copy(zbuf.at[pl.ds(0, CHUNK), :], xg_ref.at[pl.ds(far, CHUNK), :], zsem)

        def zero_block(blk):
            return pltpu.make_async_copy(zbuf, xg_ref.at[pl.ds(pl.multiple_of(blk * BM, BM), BM), :], sem.at[0])

        def per_expert(e, carry, wait):
            def one(c, inner):
                if wait:
                    zero_chunk(0).wait()
                else:
                    zero_chunk(pl.multiple_of((pad8_ref[e] + c) * CHUNK, CHUNK)).start()
                return inner
            return lax.fori_loop(0, npad_ref[e], one, carry)

        def tail(blk, carry, wait):
            if wait:
                zero_block(0).wait()
            else:
                zero_block(blk).start()
            return carry

        n_blocks = xg_ref.shape[0] // BM
        lax.fori_loop(0, N_EXPERTS, functools.partial(per_expert, wait=False), 0)
        lax.fori_loop(nused_ref[0], n_blocks, functools.partial(tail, wait=False), 0)
        lax.fori_loop(0, N_EXPERTS, functools.partial(per_expert, wait=True), 0)
        lax.fori_loop(nused_ref[0], n_blocks, functools.partial(tail, wait=True), 0)

    rt = rt_ref[...]
    lane = lax.broadcasted_iota(I32, (tr, LANES), 1)
    lane_f = lane.astype(F32)
    lo_row = lo_ref[0, 0:1, :]
    slots = []
    for e_lane, k_lane in ((R_E1, R_K1), (R_E2, R_K2)):
        hot = lane_f == rt[:, e_lane:e_lane + 1]
        slots.append(jnp.sum(jnp.where(hot, lo_row, 0.0), axis=-1, keepdims=True) + rt[:, k_lane:k_lane + 1])
    rec = jnp.where(lane == 0, slots[0], jnp.where(lane == 1, slots[1], 0.0))
    slot_o[...] = rec
    rec_t = jnp.concatenate([rec[c * LANES:(c + 1) * LANES].T for c in range(tr // LANES)], axis=1)
    row_id = lax.broadcasted_iota(I32, (R_LOCAL, tr), 0).astype(F32)
    pick = (row_id == rec_t[0:1, :]) | (row_id == rec_t[1:2, :])
    staged = _dot(jnp.where(pick, 1.0, 0.0).astype(BF16), h_ref[...].astype(BF16))

    i = pl.program_id(0)
    last = pl.num_programs(0) - 1
    slot = i % 2

    def copies_from(s):
        return lambda near, far: pltpu.make_async_copy(
            xs.at[s, pl.ds(near, CHUNK), :], xg_ref.at[pl.ds(far, CHUNK), :], sem.at[s])

    @pl.when(i >= 2)
    def _():
        _wait_chunks(nct_ref[i - 2], copies_from(slot))

    xs[slot] = staged
    _start_chunks(dst8_ref, i, nct_ref[i], copies_from(slot))

    @pl.when(i == last)
    def _():
        @pl.when(i >= 1)
        def _():
            _wait_chunks(nct_ref[i - 1], copies_from(1 - slot))
        _wait_chunks(nct_ref[i], copies_from(slot))


def _dispatch(dst8, nct, pad8, npad, nused, h2, route, lo_rows, p_rows):
    t = h2.shape[0]
    return pl.pallas_call(
        _dispatch_kernel,
        grid_spec=pltpu.PrefetchScalarGridSpec(
            num_scalar_prefetch=5,
            grid=(t // TR,),
            in_specs=[pl.BlockSpec((TR, D_MODEL), lambda i, *_: (i, 0)),
                      pl.BlockSpec((TR, LANES), lambda i, *_: (i, 0)),
                      pl.BlockSpec((1, SUBLANES, LANES), lambda i, *_: (i, 0, 0))],
            out_specs=[pl.BlockSpec(memory_space=pl.ANY),
                       pl.BlockSpec((TR, LANES), lambda i, *_: (i, 0))],
            scratch_shapes=[pltpu.VMEM((2, R_LOCAL, D_MODEL), F32), pltpu.VMEM((BM, D_MODEL), F32),
                            pltpu.SemaphoreType.DMA((2,)), pltpu.SemaphoreType.DMA(())],
        ),
        out_shape=[jax.ShapeDtypeStruct((p_rows, D_MODEL), F32), jax.ShapeDtypeStruct((t, LANES), F32)],
        compiler_params=_cparams(("arbitrary",)),
        name="moe_dispatch",
    )(dst8, nct, pad8, npad, nused, h2, route, lo_rows)


def _expert_kernel(blk_e_ref, nused_ref, x_ref, wg_ref, wu_ref, wd_ref, y_ref, wg_b, wu_b, wd_b):
    i = pl.program_id(0)
    used = i < nused_ref[0]

    @pl.when((i == 0) | (blk_e_ref[i] != blk_e_ref[jnp.maximum(i - 1, 0)]))
    def _():
        wg_b[...] = wg_ref[0].astype(BF16)
        wu_b[...] = wu_ref[0].astype(BF16)
        wd_b[...] = wd_ref[0].astype(BF16)

    @pl.when(used)
    def _():
        xb = x_ref[...].astype(BF16)
        a = _dot(xb, wg_b[...])
        u = _dot(xb, wu_b[...])
        y_ref[...] = _dot((a * jax.nn.sigmoid(a) * u).astype(BF16), wd_b[...])

    @pl.when(jnp.logical_not(used))
    def _():
        y_ref[...] = jnp.zeros(y_ref.shape, F32)


def _experts(blk_e, nused, xg, wg, wu, wd):
    p_rows = xg.shape[0]
    x_blk = lambda i, be, nu: (jnp.minimum(i, nu[0] - 1), 0)
    return pl.pallas_call(
        _expert_kernel,
        grid_spec=pltpu.PrefetchScalarGridSpec(
            num_scalar_prefetch=2,
            grid=(p_rows // BM,),
            in_specs=[pl.BlockSpec((BM, D_MODEL), x_blk),
                      pl.BlockSpec((1, D_MODEL, EXPERT_FF), lambda i, be, nu: (be[i], 0, 0)),
                      pl.BlockSpec((1, D_MODEL, EXPERT_FF), lambda i, be, nu: (be[i], 0, 0)),
                      pl.BlockSpec((1, EXPERT_FF, D_MODEL), lambda i, be, nu: (be[i], 0, 0))],
            out_specs=pl.BlockSpec((BM, D_MODEL), lambda i, be, nu: (i, 0)),
            scratch_shapes=[pltpu.VMEM((D_MODEL, EXPERT_FF), BF16), pltpu.VMEM((D_MODEL, EXPERT_FF), BF16),
                            pltpu.VMEM((EXPERT_FF, D_MODEL), BF16)],
        ),
        out_shape=jax.ShapeDtypeStruct((p_rows, D_MODEL), F32),
        compiler_params=_cparams(("arbitrary",)),
        name="moe_experts",
    )(blk_e, nused, xg, wg, wu, wd)


def _combine_ple_kernel(dst8_ref, nct_ref, x1_ref, rt_ref, slot_ref, p_ref, pg_ref, wpg_ref, bpg_ref, wpp_ref,
                        yg_ref, o_ref, ys, sem):
    tr = x1_ref.shape[0]
    i = pl.program_id(0)
    slot = i % 2

    def copies_into(s):
        return lambda near, far: pltpu.make_async_copy(
            yg_ref.at[pl.ds(far, CHUNK), :], ys.at[s, pl.ds(near, CHUNK), :], sem.at[s])

    def fetch(tile, s):
        _start_chunks(dst8_ref, tile, nct_ref[tile], copies_into(s))

        def clear(c, carry):
            ys[s, pl.ds(pl.multiple_of(c * CHUNK, CHUNK), CHUNK), :] = jnp.zeros((CHUNK, D_MODEL), F32)
            return carry

        lax.fori_loop(nct_ref[tile], NCH, clear, 0)

    @pl.when(i == 0)
    def _():
        fetch(0, 0)

    @pl.when(i + 1 < pl.num_programs(0))
    def _():
        fetch(i + 1, 1 - slot)

    _wait_chunks(nct_ref[i], copies_into(slot))

    rt = rt_ref[...]
    sl = slot_ref[...]
    col_id = lax.broadcasted_iota(I32, (tr, R_LOCAL), 1).astype(F32)
    yb = ys[slot].astype(BF16)
    gather_w = jnp.where(col_id == sl[:, 0:1], rt[:, R_W1:R_W1 + 1],
                         jnp.where(col_id == sl[:, 1:2], rt[:, R_W2:R_W2 + 1], 0.0)).astype(BF16)
    x2 = x1_ref[...] + _dot(gather_w, yb)
    h3 = (x2 * lax.rsqrt(jnp.mean(x2 * x2, axis=-1, keepdims=True) + EPS) * pg_ref[...]).astype(BF16)
    gate = jax.nn.sigmoid(_dot(h3, wpg_ref[...]) + bpg_ref[...])
    o_ref[...] = x2 + gate * _dot(p_ref[...].astype(BF16), wpp_ref[...])


def _combine_ple(dst8, nct, x1, route, slots, p2, pg, wpg, bpg, wpp, yg):
    t = x1.shape[0]
    row = lambda n: pl.BlockSpec((TR, n), lambda i, d, c: (i, 0))
    full = lambda a: pl.BlockSpec(a.shape, lambda i, d, c: (0,) * a.ndim)
    return pl.pallas_call(
        _combine_ple_kernel,
        grid_spec=pltpu.PrefetchScalarGridSpec(
            num_scalar_prefetch=2,
            grid=(t // TR,),
            in_specs=[row(D_MODEL), row(LANES), row(LANES), row(PLE_DIM), full(pg), full(wpg), full(bpg),
                      full(wpp), pl.BlockSpec(memory_space=pl.ANY)],
            out_specs=row(D_MODEL),
            scratch_shapes=[pltpu.VMEM((2, R_LOCAL, D_MODEL), F32), pltpu.SemaphoreType.DMA((2,))],
        ),
        out_shape=jax.ShapeDtypeStruct((t, D_MODEL), F32),
        compiler_params=_cparams(("arbitrary",)),
        name="moe_combine_ple",
    )(dst8, nct, x1, route, slots, p2, pg, wpg, bpg, wpp, yg)


def _seg_sum_matrix():
    seg = np.arange(LANES) // HEAD_DIM
    return jnp.asarray(seg[:, None] == seg[None, :], BF16)


def _overlap_t(nc_rows, n_blk):
    nc = np.arange(nc_rows)[None, :] * CMP_STRIDE
    sb = np.arange(n_blk)[:, None] * SEL_BLOCK
    ov = (nc < sb + SEL_BLOCK) & (nc + CMP_BLOCK > sb) & (np.arange(nc_rows)[None, :] < nc_rows - 1)
    return jnp.asarray(ov, BF16)


def _compress_weights(pos, w1, w2):
    half = CMP_BLOCK // 2
    pe = jnp.broadcast_to(pos[:, None, :], (CMP_BLOCK, N_KV, HEAD_DIM)).reshape(2, half * N_KV * HEAD_DIM)
    w1r = w1.reshape(2, half, HEAD_DIM, CMP_HIDDEN)
    eye = jnp.eye(N_KV, dtype=w1.dtype)
    wx = jnp.einsum('hldj,gk->hlgdkj', w1r, eye).reshape(2, half * N_KV * HEAD_DIM, N_KV * CMP_HIDDEN)
    w2x = jnp.einsum('jd,gk->gjkd', w2, eye).reshape(N_KV * CMP_HIDDEN, N_KV * HEAD_DIM)
    return pe, wx[0].astype(BF16), wx[1].astype(BF16), w2x.astype(BF16)


def _layer(x, p_i, prm):
    b, s, _ = x.shape
    t = b * s
    x2 = x.reshape(t, D_MODEL)
    bd = _seg_sum_matrix()

    w = prm["w_in"]
    kv = lambda j: w[:, OFF_KV + j * KV_COLS:OFF_KV + (j + 1) * KV_COLS]
    w_nat = jnp.concatenate([kv(0), kv(1), kv(2), kv(4), w[:, OFF_CONV:]], axis=1).astype(BF16)
    w_t = jnp.concatenate([w[:, :Q_COLS], kv(3), kv(5),
                           jnp.pad(w[:, OFF_GATE:OFF_CONV], ((0, 0), (0, LANES - 3 * N_HEADS)))], axis=1).T.astype(BF16)
    qg = jnp.broadcast_to(prm["q_norm_g"][:, None], (HEAD_DIM, TM_PROJ))
    kg = jnp.tile(prm["k_norm_g"], (1, LANES // HEAD_DIM))
    qt, kc_raw, vc_raw, ks, vst, kw, vwt, gtt, glu, mg = _in_proj(
        x2, prm["attn_norm_g"][None, :], w_nat, w_t, qg, kg, bd, prm["merge_b"][None, :], s)

    nrow = s // CMP_STRIDE
    cw = [_compress_weights(prm["cmp_pos"][j], prm["cmp_w1"][j], prm["cmp_w2"][j]) for j in range(2)]
    pe_x, wa, wb = (jnp.stack([cw[0][j], cw[1][j]]) for j in range(3))
    kc, vct = _compress(kc_raw.reshape(b, s, LANES), vc_raw.reshape(b, s, LANES), pe_x, wa, wb, cw[0][3],
                        cw[1][3].T, bd, kg)

    attn = _attention(qt, gtt, kc, vct, ks.reshape(b, s, 4 * LANES), vst, kw.reshape(b, s, 2 * LANES), vwt,
                      _overlap_t(nrow, s // SEL_BLOCK), b, s)

    cact = _conformer_conv(glu, prm["conv_w"], prm["conv_b"][None, :], prm["conv_norm_g"][None, :],
                           prm["conv_norm_b"][None, :], b, s)

    wr_full = jnp.zeros((D_MODEL, LANES), F32)
    wr_full = wr_full.at[:, :N_EXPERTS].set(prm["w_router_expert"])
    wr_full = wr_full.at[:, GROUP_LANE0:GROUP_LANE0 + N_GROUPS].set(prm["w_router_group"])
    wr_hi = wr_full.astype(BF16)
    wr = jnp.stack([wr_hi.T, (wr_full - wr_hi.astype(F32)).astype(BF16).T])
    br = jnp.zeros((LANES,), F32)
    br = br.at[:N_EXPERTS].set(prm["b_router_expert"])
    br = br.at[GROUP_LANE0:GROUP_LANE0 + N_GROUPS].set(prm["b_router_group"])
    br = jnp.broadcast_to(br[:, None], (LANES, TR))
    x1, h2, route, cnt = _merge_router(
        x2, attn, cact, mg, prm["w_attn_out"].astype(BF16), prm["w_conv_out"].astype(BF16),
        prm["w_out"].astype(BF16), prm["ffn_norm_g"][None, :], wr, br)

    n_tiles = t // TR
    counts = cnt[:, 0, :N_EXPERTS].astype(I32)
    seg = (counts + CHUNK - 1) // CHUNK * CHUNK
    local_off = jnp.cumsum(seg, axis=1) - seg
    nct = (jnp.sum(seg, axis=1) // CHUNK).astype(I32)
    total = jnp.sum(seg, axis=0)
    padded = (total + BM - 1) // BM * BM
    pends = jnp.cumsum(padded)
    far_off = (pends - padded)[None, :] + jnp.cumsum(seg, axis=0) - seg
    chunk = jnp.arange(NCH, dtype=I32)[None, :, None]
    lo8 = (local_off // CHUNK)[:, None, :]
    in_seg = (chunk >= lo8) & (chunk < ((local_off + seg) // CHUNK)[:, None, :])
    dst8 = (jnp.sum(jnp.where(in_seg, (far_off // CHUNK)[:, None, :] - lo8, 0), axis=-1)
            + chunk[:, :, 0]).astype(I32).reshape(n_tiles * NCH)
    lo_rows = jnp.broadcast_to(jnp.pad(local_off.astype(F32), ((0, 0), (0, LANES - N_EXPERTS)))[:, None, :],
                               (n_tiles, SUBLANES, LANES))
    n_blk = -(-(2 * t + n_tiles * N_EXPERTS * (CHUNK - 1) + N_EXPERTS * (BM - 1)) // BM)
    blk_e = jnp.minimum(jnp.sum(jnp.arange(n_blk, dtype=I32)[:, None] * BM >= pends[None, :], axis=1),
                        N_EXPERTS - 1).astype(I32)
    nused = (pends[-1:] // BM).astype(I32)
    pad8 = ((pends - padded + total) // CHUNK).astype(I32)
    npad = ((padded - total) // CHUNK).astype(I32)

    xg, slots = _dispatch(dst8, nct, pad8, npad, nused, h2, route, lo_rows, n_blk * BM)
    yg = _experts(blk_e, nused, xg, prm["w_exp_gate"], prm["w_exp_up"], prm["w_exp_down"])
    out = _combine_ple(dst8, nct, x1, route, slots, p_i.reshape(t, PLE_DIM), prm["ple_norm_g"][None, :],
                       prm["w_ple_gate"].astype(BF16), prm["b_ple_gate"][None, :], prm["w_ple_proj"].astype(BF16), yg)
    return out.reshape(b, s, D_MODEL)


def kernel(x, p, attn_norm_g, w_in, q_norm_g, k_norm_g, cmp_pos, cmp_w1, cmp_w2, w_attn_out, conv_w, conv_b, conv_norm_g, conv_norm_b, w_conv_out, merge_b, w_out, ffn_norm_g, w_router_group, b_router_group, w_router_expert, b_router_expert, w_exp_gate, w_exp_up, w_exp_down, ple_norm_g, w_ple_gate, b_ple_gate, w_ple_proj):
    stacked = dict(attn_norm_g=attn_norm_g, w_in=w_in, q_norm_g=q_norm_g, k_norm_g=k_norm_g, cmp_pos=cmp_pos,
                   cmp_w1=cmp_w1, cmp_w2=cmp_w2, w_attn_out=w_attn_out, conv_w=conv_w, conv_b=conv_b,
                   conv_norm_g=conv_norm_g, conv_norm_b=conv_norm_b, w_conv_out=w_conv_out, merge_b=merge_b,
                   w_out=w_out, ffn_norm_g=ffn_norm_g, w_router_group=w_router_group,
                   b_router_group=b_router_group, w_router_expert=w_router_expert,
                   b_router_expert=b_router_expert, w_exp_gate=w_exp_gate, w_exp_up=w_exp_up,
                   w_exp_down=w_exp_down, ple_norm_g=ple_norm_g, w_ple_gate=w_ple_gate, b_ple_gate=b_ple_gate,
                   w_ple_proj=w_ple_proj)
    for i in range(w_in.shape[0]):
        x = _layer(x, p[i], {k: v[i] for k, v in stacked.items()})
    return x
```
